```python
import jax, jax.numpy as jnp
from jax import lax
import numpy as np

D_MODEL = 2048
BATCH = 4
SEQ = 2048
DEPTH = 1
DEC_BATCH = 128
DEC_SEQ = 1
PAST_LEN = 16384
PAGE_SIZE = 128

D_POOL = D_MODEL // 2
POOL_WINDOWS = (2, 4, 8, 16)
POOL_GROUPS = len(POOL_WINDOWS)
POOL_GROUP_DIM = D_POOL // POOL_GROUPS
POOL_BUF = max(POOL_WINDOWS) - 1
N_HEADS = 8
HEAD_DK = D_MODEL // 16
HEAD_DV = D_MODEL // 8
D_QK = N_HEADS * HEAD_DK
D_V = N_HEADS * HEAD_DV
CHUNK = 128
ROPE_BASE = 10000.0
EPS = 1e-6
D_IN = 2 * D_POOL + 2 * D_QK + 2 * D_V
SPLITS = (D_POOL, 2 * D_POOL, 2 * D_POOL + D_QK, 2 * D_POOL + 2 * D_QK, 2 * D_POOL + 2 * D_QK + D_V)

kernel_name = 'pool_retention_gated_hybrid_step'


def rmsnorm(x, g):
    xf = x.astype(jnp.float32)
    xf = xf * lax.rsqrt(jnp.mean(xf * xf, axis=-1, keepdims=True) + EPS)
    return xf * g.astype(jnp.float32)


def rotary(x, start):
    T = x.shape[1]
    half = HEAD_DK // 2
    inv = ROPE_BASE ** (-jnp.arange(half, dtype=jnp.float32) / half)
    pos = start + jnp.arange(T, dtype=jnp.float32)
    ang = pos[:, None] * inv[None, :]
    cos = jnp.cos(ang)[None, :, None, :]
    sin = jnp.sin(ang)[None, :, None, :]
    x1, x2 = x[..., :half], x[..., half:]
    return jnp.concatenate([x1 * cos - x2 * sin, x1 * sin + x2 * cos], axis=-1)


def pool_mixer(xa, buf, start, pool_w, pool_scale):
    B, T, _ = xa.shape
    P = POOL_BUF
    ext = jnp.concatenate([buf.astype(jnp.float32), xa.astype(jnp.float32)], axis=1)
    cs = jnp.concatenate([jnp.zeros((B, 1, D_POOL), jnp.float32), jnp.cumsum(ext, axis=1)], axis=1)
    pos = start + jnp.arange(T)
    outs = []
    for g, w in enumerate(POOL_WINDOWS):
        sl = slice(g * POOL_GROUP_DIM, (g + 1) * POOL_GROUP_DIM)
        win = cs[:, P + 1:P + 1 + T, sl] - cs[:, P + 1 - w:P + 1 - w + T, sl]
        cnt = jnp.minimum(pos + 1, w).astype(jnp.float32)[None, :, None]
        outs.append(win / cnt - ext[:, P:, sl])
    pooled = jnp.stack(outs, axis=2)
    mixed = jnp.einsum('btgc,gcd->btgd', pooled, pool_w).reshape(B, T, D_POOL) * pool_scale
    return mixed, ext[:, -POOL_BUF:]


def retention(q, k, v, S0):
    B, H, T, _ = q.shape
    C = min(CHUNK, T)
    NC = T // C
    lg = jnp.log1p(-jnp.power(2.0, -5.0 - jnp.arange(N_HEADS, dtype=jnp.float32)))
    idx = jnp.arange(C, dtype=jnp.float32)
    diff = idx[:, None] - idx[None, :]
    dmask = jnp.where(diff[None] >= 0, jnp.exp(jnp.maximum(diff, 0.0)[None] * lg[:, None, None]), 0.0)
    q_dec = jnp.exp((idx + 1.0)[None, :] * lg[:, None])[..., None]
    k_dec = jnp.exp((C - 1.0 - idx)[None, :] * lg[:, None])[..., None]
    chunk_dec = jnp.exp(C * lg)[:, None, None]

    def to_chunks(a):
        return jnp.moveaxis(a.reshape(B, H, NC, C, a.shape[-1]), 2, 0)

    def step(S, inp):
        qc, kc, vc = inp
        scores = jnp.einsum('bhld,bhmd->bhlm', qc, kc) * dmask
        o = jnp.einsum('bhlm,bhmv->bhlv', scores, vc) + jnp.einsum('bhld,bhdv->bhlv', qc * q_dec, S)
        S = chunk_dec * S + jnp.einsum('bhld,bhlv->bhdv', kc * k_dec, vc)
        return S, o

    S, o = lax.scan(step, S0, (to_chunks(q), to_chunks(k), to_chunks(v)))
    o = jnp.moveaxis(o, 0, 2).reshape(B, H, T, HEAD_DV)
    return o, S


def mixer_layer(x, c, buf, S0, start, ada_w, ada_b, g_pre, g_post, w_in, pool_w, pool_scale,
                gn_g, w_a_proj, w_b_proj, w_merge, b_merge, w_out):
    B, T, _ = x.shape
    mod = jax.nn.silu(c.astype(jnp.float32)) @ ada_w + ada_b
    shift, scale, gate = jnp.split(mod, 3, axis=-1)
    h = rmsnorm(x, g_pre) * (1.0 + scale[:, None]) + shift[:, None]
    z = h @ w_in
    a_x, a_g, q, k, v, b_g = jnp.split(z, SPLITS, axis=-1)
    y_a, new_buf = pool_mixer(a_x, buf, start, pool_w, pool_scale)
    y_a = y_a * jax.nn.silu(a_g)
    q = rotary(q.reshape(B, T, N_HEADS, HEAD_DK), start)
    k = rotary(k.reshape(B, T, N_HEADS, HEAD_DK), start) * (HEAD_DK ** -0.5)
    v = v.reshape(B, T, N_HEADS, HEAD_DV)
    o, S = retention(q.transpose(0, 2, 1, 3), k.transpose(0, 2, 1, 3), v.transpose(0, 2, 1, 3),
                     S0.astype(jnp.float32))
    o = o.transpose(0, 2, 1, 3)
    mu = jnp.mean(o, axis=-1, keepdims=True)
    var = jnp.mean(jnp.square(o - mu), axis=-1, keepdims=True)
    o = (o - mu) * lax.rsqrt(var + EPS) * gn_g
    y_b = o.reshape(B, T, D_V) * jax.nn.silu(b_g)
    ya = y_a @ w_a_proj
    yb = y_b @ w_b_proj
    gates = jax.nn.sigmoid(h @ w_merge + b_merge)
    g_a, g_b = jnp.split(gates, 2, axis=-1)
    out = rmsnorm((g_a * ya + g_b * yb) @ w_out, g_post)
    y = x.astype(jnp.float32) + gate[:, None] * out
    return y.astype(x.dtype), new_buf.astype(buf.dtype), S.astype(S0.dtype)


def setup_inputs(seed: int = 0) -> dict:
    key = jax.random.key(seed)
    ks = jax.random.split(key, 20)
    f32 = jnp.float32
    nrm = lambda k, shape, s: jax.random.normal(k, shape, f32) * s
    D = D_MODEL
    return {
        'x_prompt': nrm(ks[0], (BATCH, SEQ, D), 1.0),
        'x_sample': nrm(ks[1], (DEC_BATCH, DEC_SEQ, D), 1.0),
        'state_pool': nrm(ks[2], (DEPTH, DEC_BATCH, POOL_BUF, D_POOL), 1.0),
        'state_ret': nrm(ks[3], (DEPTH, DEC_BATCH, N_HEADS, HEAD_DK, HEAD_DV), 0.5),
        'c_prompt': nrm(ks[4], (BATCH, D), 1.0),
        'c_sample': nrm(ks[5], (DEC_BATCH, D), 1.0),
        'ada_w': nrm(ks[6], (DEPTH, D, 3 * D), 0.5 * D ** -0.5),
        'ada_b': nrm(ks[7], (DEPTH, 3 * D), 0.02),
        'g_pre': 1.0 + nrm(ks[8], (DEPTH, D), 0.02),
        'g_post': 1.0 + nrm(ks[9], (DEPTH, D), 0.02),
        'w_in': nrm(ks[10], (DEPTH, D, D_IN), D ** -0.5),
        'pool_w': nrm(ks[11], (DEPTH, POOL_GROUPS, POOL_GROUP_DIM, POOL_GROUP_DIM), POOL_GROUP_DIM ** -0.5),
        'pool_scale': 1.0 + nrm(ks[12], (DEPTH, D_POOL), 0.02),
        'gn_g': 1.0 + nrm(ks[13], (DEPTH, N_HEADS, HEAD_DV), 0.02),
        'w_a_proj': nrm(ks[14], (DEPTH, D_POOL, D), D_POOL ** -0.5),
        'w_b_proj': nrm(ks[15], (DEPTH, D_V, D), D_V ** -0.5),
        'w_merge': nrm(ks[16], (DEPTH, D, 2 * D), D ** -0.5),
        'b_merge': nrm(ks[17], (DEPTH, 2 * D), 0.02),
        'w_out': nrm(ks[18], (DEPTH, D, D), D ** -0.5),
    }


def reference(x_prompt, x_sample, state_pool, state_ret, c_prompt, c_sample, ada_w, ada_b, g_pre, g_post,
              w_in, pool_w, pool_scale, gn_g, w_a_proj, w_b_proj, w_merge, b_merge, w_out):
    xp, xs = x_prompt, x_sample
    pool_p, ret_p, pool_s, ret_s = [], [], [], []
    for l in range(DEPTH):
        params = (ada_w[l], ada_b[l], g_pre[l], g_post[l], w_in[l], pool_w[l], pool_scale[l], gn_g[l],
                  w_a_proj[l], w_b_proj[l], w_merge[l], b_merge[l], w_out[l])
        buf0 = jnp.zeros((xp.shape[0], POOL_BUF, D_POOL), state_pool.dtype)
        S00 = jnp.zeros((xp.shape[0], N_HEADS, HEAD_DK, HEAD_DV), state_ret.dtype)
        xp, bp, sp = mixer_layer(xp, c_prompt, buf0, S00, 0, *params)
        xs, bs, ss = mixer_layer(xs, c_sample, state_pool[l], state_ret[l], PAST_LEN, *params)
        pool_p.append(bp)
        ret_p.append(sp)
        pool_s.append(bs)
        ret_s.append(ss)
    new_pool_prompt = jnp.stack(pool_p, axis=0)
    new_ret_prompt = jnp.stack(ret_p, axis=0)
    new_pool_sample = jnp.stack(pool_s, axis=0)
    new_ret_sample = jnp.stack(ret_s, axis=0)
    return (xp, xs, new_pool_prompt, new_ret_prompt, new_pool_sample, new_ret_sample)
```

```python
import functools

import jax
import jax.numpy as jnp
from jax import lax
from jax.experimental import pallas as pl
from jax.experimental.pallas import tpu as pltpu

F32 = jnp.float32
BF16 = jnp.bfloat16

D_MODEL = 2048
PAST_LEN = 16384
D_POOL = D_MODEL // 2
POOL_WINDOWS = (2, 4, 8, 16)
POOL_GROUP_DIM = D_POOL // len(POOL_WINDOWS)
POOL_BUF = max(POOL_WINDOWS) - 1
N_HEADS = 8
HEAD_DK = D_MODEL // 16
HEAD_DV = D_MODEL // 8
D_QK = N_HEADS * HEAD_DK
D_V = N_HEADS * HEAD_DV
CHUNK = 128
ROPE_BASE = 10000.0
EPS = 1e-6
D_IN = 2 * D_POOL + 2 * D_QK + 2 * D_V
D_Z = D_IN + 2 * D_MODEL
K_SCALE = HEAD_DK ** -0.5

VMEM_LIMIT_BYTES = 56 * 1024 * 1024
HALO = 16


def _params(*semantics):
    return pltpu.CompilerParams(dimension_semantics=semantics, vmem_limit_bytes=VMEM_LIMIT_BYTES)


def _resident(shape):
    return pl.BlockSpec(shape, lambda *_: (0,) * len(shape), pipeline_mode=pl.Buffered(1))


def _silu(x):
    return x * jax.nn.sigmoid(x)


def _mod_kernel(c_ref, w_ref, b_ref, o_ref):
    s = _silu(c_ref[...]).astype(BF16)
    o_ref[...] = jnp.dot(s, w_ref[...].astype(BF16), preferred_element_type=F32) + b_ref[...]


def _modulation(c_all, ada_w, ada_b):
    rows = c_all.shape[0]
    tn = 512
    return pl.pallas_call(
        _mod_kernel,
        out_shape=jax.ShapeDtypeStruct((rows, 3 * D_MODEL), F32),
        grid=(3 * D_MODEL // tn,),
        in_specs=[pl.BlockSpec((rows, D_MODEL), lambda j: (0, 0)),
                  pl.BlockSpec((D_MODEL, tn), lambda j: (0, j)),
                  pl.BlockSpec((1, tn), lambda j: (0, j))],
        out_specs=pl.BlockSpec((rows, tn), lambda j: (0, j)),
        compiler_params=_params("arbitrary"),
        name="modulation",
    )(c_all, ada_w, ada_b)


NORM_ROWS = 128


def _in_proj_kernel(x_ref, shift_ref, scale_ref, g_ref, w_ref, o_ref, h_ref):
    tm = x_ref.shape[0]
    per_row = shift_ref.shape[0] != 1

    @pl.when(pl.program_id(1) == 0)
    def _():
        def body(r, carry):
            rows = pl.ds(pl.multiple_of(r * NORM_ROWS, NORM_ROWS), NORM_ROWS)
            x = x_ref[rows, :]
            xn = x * lax.rsqrt(jnp.mean(x * x, axis=-1, keepdims=True) + EPS) * g_ref[...]
            scale = scale_ref[rows, :] if per_row else scale_ref[...]
            shift = shift_ref[rows, :] if per_row else shift_ref[...]
            h_ref[rows, :] = (xn * (1.0 + scale) + shift).astype(BF16)
            return carry
        lax.fori_loop(0, tm // NORM_ROWS, body, 0)

    o_ref[...] = jnp.dot(h_ref[...], w_ref[...], preferred_element_type=F32)


def _in_proj(x, mod, g_pre, w1, *, tm, rows_per_mod):
    m = x.shape[0]
    tn = 1024
    if mod.ndim == 3:
        mod_spec = lambda col: pl.BlockSpec((None, 1, D_MODEL), lambda i, j: ((i * tm) // rows_per_mod, 0, col))
    else:
        mod_spec = lambda col: pl.BlockSpec((tm, D_MODEL), lambda i, j: (i, col))
    return pl.pallas_call(
        _in_proj_kernel,
        out_shape=jax.ShapeDtypeStruct((m, D_Z), F32),
        grid=(m // tm, D_Z // tn),
        in_specs=[pl.BlockSpec((tm, D_MODEL), lambda i, j: (i, 0)),
                  mod_spec(0), mod_spec(1),
                  pl.BlockSpec((1, D_MODEL), lambda i, j: (0, 0)),
                  pl.BlockSpec((D_MODEL, tn), lambda i, j: (0, j))],
        out_specs=pl.BlockSpec((tm, tn), lambda i, j: (i, j)),
        scratch_shapes=[pltpu.VMEM((tm, D_MODEL), BF16)],
        compiler_params=_params("parallel", "arbitrary"),
        name="in_proj",
    )(x, mod, mod, g_pre, w1)


def _rotate(x, cos, sin_signed):
    return x * cos + pltpu.roll(x, HEAD_DK // 2, 1) * sin_signed


def _group_norm_gate(o, gn, bg):
    mu = jnp.mean(o, axis=-1, keepdims=True)
    d = o - mu
    var = jnp.mean(d * d, axis=-1, keepdims=True)
    return d * lax.rsqrt(var + EPS) * gn * _silu(bg)


def _pool_project(pooled, pw, ps, ag):
    mixed = jnp.dot(pooled.astype(BF16), pw, preferred_element_type=F32)
    return mixed * ps * _silu(ag)


def _prompt_mix_kernel(cdec_ref, ax_ref, ag_ref, q_ref, k_ref, v_ref, bg_ref,
                       cos_ref, sin_ref, dmask_ref, qdec_ref, kdec_ref, pw_ref, ps_ref, gn_ref,
                       ya_ref, yb_ref, npool_ref, nret_ref, ext_ref):
    c = pl.program_id(1)
    rows = ax_ref.shape[0]

    @pl.when(c == 0)
    def _():
        ext_ref[0:HALO, :] = jnp.zeros((HALO, D_POOL), F32)
        nret_ref[...] = jnp.zeros(nret_ref.shape, F32)

    xa = ax_ref[...]
    ext_ref[HALO:HALO + rows, :] = xa
    pos = c * rows + lax.broadcasted_iota(jnp.int32, (rows, 1), 0)
    for g, w in enumerate(POOL_WINDOWS):
        lo, hi = g * POOL_GROUP_DIM, (g + 1) * POOL_GROUP_DIM
        win = xa[:, lo:hi]
        for s in range(1, w):
            win = win + ext_ref[HALO - s:HALO - s + rows, lo:hi]
        inv_cnt = 1.0 / jnp.minimum(pos + 1, w).astype(F32)
        pooled = win * inv_cnt - xa[:, lo:hi]
        ya = _pool_project(pooled, pw_ref[g], ps_ref[:, lo:hi], ag_ref[:, lo:hi])
        ya_ref[:, lo:hi] = ya.astype(BF16)

    @pl.when(c == pl.num_programs(1) - 1)
    def _():
        npool_ref[...] = ext_ref[HALO + rows - POOL_BUF:HALO + rows, :]

    ext_ref[0:HALO, :] = ext_ref[rows:rows + HALO, :]

    cos = cos_ref[...]
    sin = sin_ref[...]
    for h in range(N_HEADS):
        qs = slice(h * HEAD_DK, (h + 1) * HEAD_DK)
        vs = slice(h * HEAD_DV, (h + 1) * HEAD_DV)
        q = _rotate(q_ref[:, qs], cos, sin)
        k = _rotate(k_ref[:, qs], cos, sin) * K_SCALE
        v = v_ref[:, vs].astype(BF16)
        s_old = nret_ref[h]
        scores = lax.dot_general(q.astype(BF16), k.astype(BF16), (((1,), (1,)), ((), ())),
                                 preferred_element_type=F32) * dmask_ref[h]
        o = (jnp.dot(scores.astype(BF16), v, preferred_element_type=F32)
             + jnp.dot((q * qdec_ref[h]).astype(BF16), s_old.astype(BF16), preferred_element_type=F32))
        kv = lax.dot_general((k * kdec_ref[h]).astype(BF16), v, (((0,), (0,)), ((), ())),
                             preferred_element_type=F32)
        nret_ref[h] = cdec_ref[h] * s_old + kv
        yb_ref[:, vs] = _group_norm_gate(o, gn_ref[:, vs], bg_ref[:, vs]).astype(BF16)


def _prompt_mix(z, batch, seq, tables, pool_w, pool_scale, gn_g):
    cos, sin, dmask, qdec, kdec, cdec = tables
    nc = seq // CHUNK
    row = lambda b, c: b * nc + c
    zcol = lambda width, col: pl.BlockSpec((CHUNK, width), lambda b, c: (row(b, c), col))
    m = batch * seq
    return pl.pallas_call(
        _prompt_mix_kernel,
        out_shape=(jax.ShapeDtypeStruct((m, D_POOL), BF16),
                   jax.ShapeDtypeStruct((m, D_V), BF16),
                   jax.ShapeDtypeStruct((batch, POOL_BUF, D_POOL), F32),
                   jax.ShapeDtypeStruct((batch, N_HEADS, HEAD_DK, HEAD_DV), F32)),
        grid=(batch, nc),
        in_specs=[pl.BlockSpec(memory_space=pltpu.SMEM),
                  zcol(D_POOL, 0), zcol(D_POOL, 1), zcol(D_QK, 2), zcol(D_QK, 3), zcol(D_V, 2), zcol(D_V, 3),
                  pl.BlockSpec((CHUNK, HEAD_DK), lambda b, c: (c, 0)),
                  pl.BlockSpec((CHUNK, HEAD_DK), lambda b, c: (c, 0)),
                  _resident(dmask.shape), _resident(qdec.shape), _resident(kdec.shape),
                  _resident(pool_w.shape), _resident(pool_scale.shape), _resident(gn_g.shape)],
        out_specs=(pl.BlockSpec((CHUNK, D_POOL), lambda b, c: (row(b, c), 0)),
                   pl.BlockSpec((CHUNK, D_V), lambda b, c: (row(b, c), 0)),
                   pl.BlockSpec((None, POOL_BUF, D_POOL), lambda b, c: (b, 0, 0)),
                   pl.BlockSpec((None, N_HEADS, HEAD_DK, HEAD_DV), lambda b, c: (b, 0, 0, 0))),
        scratch_shapes=[pltpu.VMEM((HALO + CHUNK, D_POOL), F32)],
        compiler_params=_params("parallel", "arbitrary"),
        name="prompt_mix",
    )(cdec, z, z, z, z, z, z, cos, sin, dmask, qdec, kdec, pool_w, pool_scale, gn_g)


SAMPLE_TILE = 8


def _sample_mix_kernel(dec_ref, ax_ref, ag_ref, q_ref, k_ref, v_ref, bg_ref, cos_ref, sin_ref,
                       pw_ref, ps_ref, gn_ref, spool_ref, sret_ref,
                       ya_ref, yb_ref, npool_ref, nret_ref, o_ref, *, inv_cnt):
    bt = ax_ref.shape[0]

    xa = ax_ref[...]
    run = xa
    wins = {1: xa}
    for j in range(1, POOL_BUF + 1):
        run = run + spool_ref[:, POOL_BUF - j, :]
        wins[j + 1] = run
    for g, w in enumerate(POOL_WINDOWS):
        lo, hi = g * POOL_GROUP_DIM, (g + 1) * POOL_GROUP_DIM
        pooled = wins[w][:, lo:hi] * inv_cnt[g] - xa[:, lo:hi]
        ya = _pool_project(pooled, pw_ref[g], ps_ref[:, lo:hi], ag_ref[:, lo:hi])
        ya_ref[:, lo:hi] = ya.astype(BF16)
    for j in range(POOL_BUF - 1):
        npool_ref[:, j, :] = spool_ref[:, j + 1, :]
    npool_ref[:, POOL_BUF - 1, :] = xa

    cos = cos_ref[...]
    sin = sin_ref[...]
    for h in range(N_HEADS):
        qs = slice(h * HEAD_DK, (h + 1) * HEAD_DK)
        vs = slice(h * HEAD_DV, (h + 1) * HEAD_DV)
        q = _rotate(q_ref[:, qs], cos, sin)
        k = _rotate(k_ref[:, qs], cos, sin) * K_SCALE
        v = v_ref[:, vs]
        score = jnp.sum(q * k, axis=1, keepdims=True) * dec_ref[0, h]
        q_cols = jnp.transpose(q * dec_ref[1, h])
        k_cols = jnp.transpose(k * dec_ref[2, h])
        for r in range(bt):
            s_old = sret_ref[r, h]
            v_row = v[r:r + 1, :]
            o_row = score[r:r + 1, :] * v_row + jnp.sum(q_cols[:, r:r + 1] * s_old, axis=0, keepdims=True)
            nret_ref[r, h] = dec_ref[3, h] * s_old + k_cols[:, r:r + 1] * v_row
            o_ref[r:r + 1, vs] = o_row
    for h in range(N_HEADS):
        vs = slice(h * HEAD_DV, (h + 1) * HEAD_DV)
        yb_ref[:, vs] = _group_norm_gate(o_ref[:, vs], gn_ref[:, vs], bg_ref[:, vs]).astype(BF16)


def _sample_mix(z, state_pool, state_ret, tables, inv_cnt, pool_w, pool_scale, gn_g):
    cos, sin, dec = tables
    batch = z.shape[0]
    bt = SAMPLE_TILE
    zcol = lambda width, col: pl.BlockSpec((bt, width), lambda i: (i, col))
    return pl.pallas_call(
        functools.partial(_sample_mix_kernel, inv_cnt=inv_cnt),
        out_shape=(jax.ShapeDtypeStruct((batch, D_POOL), BF16),
                   jax.ShapeDtypeStruct((batch, D_V), BF16),
                   jax.ShapeDtypeStruct(state_pool.shape, state_pool.dtype),
                   jax.ShapeDtypeStruct(state_ret.shape, state_ret.dtype)),
        grid=(batch // bt,),
        in_specs=[pl.BlockSpec(memory_space=pltpu.SMEM),
                  zcol(D_POOL, 0), zcol(D_POOL, 1), zcol(D_QK, 2), zcol(D_QK, 3), zcol(D_V, 2), zcol(D_V, 3),
                  _resident(cos.shape), _resident(sin.shape),
                  _resident(pool_w.shape), _resident(pool_scale.shape), _resident(gn_g.shape),
                  pl.BlockSpec((bt, POOL_BUF, D_POOL), lambda i: (i, 0, 0)),
                  pl.BlockSpec((bt, N_HEADS, HEAD_DK, HEAD_DV), lambda i: (i, 0, 0, 0))],
        out_specs=(pl.BlockSpec((bt, D_POOL), lambda i: (i, 0)),
                   pl.BlockSpec((bt, D_V), lambda i: (i, 0)),
                   pl.BlockSpec((bt, POOL_BUF, D_POOL), lambda i: (i, 0, 0)),
                   pl.BlockSpec((bt, N_HEADS, HEAD_DK, HEAD_DV), lambda i: (i, 0, 0, 0))),
        scratch_shapes=[pltpu.VMEM((bt, D_V), F32)],
        compiler_params=_params("parallel"),
        name="sample_mix",
    )(dec, z, z, z, z, z, z, cos, sin, pool_w, pool_scale, gn_g, state_pool, state_ret)


def _out_proj_kernel(ya_ref, yb_ref, gp_ref, x_ref, gate_ref, bm_ref, wa_ref, wb_ref, wo_ref, gpost_ref, y_ref):
    ya = jnp.dot(ya_ref[...], wa_ref[...], preferred_element_type=F32)
    yb = jnp.dot(yb_ref[...], wb_ref[...], preferred_element_type=F32)
    g_a = jax.nn.sigmoid(gp_ref[:, :D_MODEL] + bm_ref[:, :D_MODEL])
    g_b = jax.nn.sigmoid(gp_ref[:, D_MODEL:] + bm_ref[:, D_MODEL:])
    merged = (g_a * ya + g_b * yb).astype(BF16)
    o = jnp.dot(merged, wo_ref[...], preferred_element_type=F32)
    o = o * lax.rsqrt(jnp.mean(o * o, axis=-1, keepdims=True) + EPS) * gpost_ref[...]
    y_ref[...] = x_ref[...] + gate_ref[...] * o


def _out_proj(ya, yb, z, x, mod, b_merge, w_a, w_b, w_o, g_post, *, tm, rows_per_mod):
    m = x.shape[0]
    if mod.ndim == 3:
        gate_spec = pl.BlockSpec((None, 1, D_MODEL), lambda i: ((i * tm) // rows_per_mod, 0, 2))
    else:
        gate_spec = pl.BlockSpec((tm, D_MODEL), lambda i: (i, 2))
    return pl.pallas_call(
        _out_proj_kernel,
        out_shape=jax.ShapeDtypeStruct((m, D_MODEL), F32),
        grid=(m // tm,),
        in_specs=[pl.BlockSpec((tm, D_POOL), lambda i: (i, 0)),
                  pl.BlockSpec((tm, D_V), lambda i: (i, 0)),
                  pl.BlockSpec((tm, 2 * D_MODEL), lambda i: (i, D_IN // (2 * D_MODEL))),
                  pl.BlockSpec((tm, D_MODEL), lambda i: (i, 0)),
                  gate_spec,
                  _resident(b_merge.shape), _resident(w_a.shape), _resident(w_b.shape), _resident(w_o.shape),
                  _resident(g_post.shape)],
        out_specs=pl.BlockSpec((tm, D_MODEL), lambda i: (i, 0)),
        compiler_params=_params("parallel"),
        name="out_proj",
    )(ya, yb, z, x, mod, b_merge, w_a, w_b, w_o, g_post)


def _rotary_tables(start, length):
    half = HEAD_DK // 2
    inv = ROPE_BASE ** (-jnp.arange(half, dtype=F32) / half)
    pos = start + jnp.arange(length, dtype=F32)
    ang = pos[:, None] * inv[None, :]
    cos, sin = jnp.cos(ang), jnp.sin(ang)
    return jnp.concatenate([cos, cos], axis=-1), jnp.concatenate([-sin, sin], axis=-1)


def _decay_tables(c):
    lg = jnp.log1p(-jnp.power(2.0, -5.0 - jnp.arange(N_HEADS, dtype=F32)))
    idx = jnp.arange(c, dtype=F32)
    diff = idx[:, None] - idx[None, :]
    dmask = jnp.where(diff[None] >= 0, jnp.exp(jnp.maximum(diff, 0.0)[None] * lg[:, None, None]), 0.0)
    q_dec = jnp.exp((idx + 1.0)[None, :] * lg[:, None])
    k_dec = jnp.exp((c - 1.0 - idx)[None, :] * lg[:, None])
    chunk_dec = jnp.exp(c * lg)
    return dmask, q_dec, k_dec, chunk_dec


def _layer(xp, xs, c_all, state_pool, state_ret, ada_w, ada_b, g_pre, g_post, w_in, pool_w, pool_scale, gn_g,
           w_a_proj, w_b_proj, w_merge, b_merge, w_out):
    batch, seq, _ = xp.shape
    dec_batch, dec_seq, _ = xs.shape
    assert dec_seq == 1 and seq % CHUNK == 0

    w1 = jnp.concatenate([w_in, w_merge], axis=1).astype(BF16)
    w_a, w_b, w_o, pw = (w.astype(BF16) for w in (w_a_proj, w_b_proj, w_out, pool_w))
    row = lambda v: v.reshape(1, -1)
    g_pre, g_post, pool_scale, gn_g, b_merge = map(row, (g_pre, g_post, pool_scale, gn_g, b_merge))

    mod = _modulation(c_all, ada_w, row(ada_b))
    mod_p = mod[:batch].reshape(batch, 1, 3 * D_MODEL)
    mod_s = mod[batch:batch + dec_batch]

    x2 = xp.reshape(batch * seq, D_MODEL)
    z = _in_proj(x2, mod_p, g_pre, w1, tm=1024, rows_per_mod=seq)
    cos, sin = _rotary_tables(0, seq)
    dmask, q_dec, k_dec, chunk_dec = _decay_tables(CHUNK)
    wide = lambda d: jnp.broadcast_to(d[:, :, None], (N_HEADS, CHUNK, HEAD_DK))
    ya, yb, pool_p, ret_p = _prompt_mix(z, batch, seq, (cos, sin, dmask, wide(q_dec), wide(k_dec), chunk_dec),
                                        pw, pool_scale, gn_g)
    yp = _out_proj(ya, yb, z, x2, mod_p, b_merge, w_a, w_b, w_o, g_post, tm=256, rows_per_mod=seq)

    xs2 = xs.reshape(dec_batch, D_MODEL)
    zs = _in_proj(xs2, mod_s, g_pre, w1, tm=dec_batch, rows_per_mod=1)
    cos_s, sin_s = _rotary_tables(PAST_LEN, 1)
    dmask_s, q_dec_s, k_dec_s, chunk_dec_s = _decay_tables(1)
    dec_s = jnp.stack([dmask_s[:, 0, 0], q_dec_s[:, 0], k_dec_s[:, 0], chunk_dec_s])
    inv_cnt = tuple(1.0 / min(PAST_LEN + 1, w) for w in POOL_WINDOWS)
    ya_s, yb_s, pool_s, ret_s = _sample_mix(zs, state_pool, state_ret, (cos_s, sin_s, dec_s), inv_cnt,
                                            pw, pool_scale, gn_g)
    ys = _out_proj(ya_s, yb_s, zs, xs2, mod_s, b_merge, w_a, w_b, w_o, g_post, tm=dec_batch, rows_per_mod=1)

    return yp.reshape(xp.shape), ys.reshape(xs.shape), pool_p, ret_p, pool_s, ret_s


def kernel(x_prompt, x_sample, state_pool, state_ret, c_prompt, c_sample, ada_w, ada_b, g_pre, g_post,
           w_in, pool_w, pool_scale, gn_g, w_a_proj, w_b_proj, w_merge, b_merge, w_out):
    depth = ada_w.shape[0]
    xp, xs = x_prompt, x_sample
    rows = c_prompt.shape[0] + c_sample.shape[0]
    pad = (-rows) % 8
    c_all = jnp.concatenate([c_prompt, c_sample, jnp.zeros((pad, D_MODEL), c_prompt.dtype)], axis=0)
    pool_p, ret_p, pool_s, ret_s = [], [], [], []
    for l in range(depth):
        xp, xs, bp, sp, bs, ss = _layer(
            xp, xs, c_all, state_pool[l], state_ret[l], ada_w[l], ada_b[l], g_pre[l], g_post[l], w_in[l],
            pool_w[l], pool_scale[l], gn_g[l].reshape(-1), w_a_proj[l], w_b_proj[l], w_merge[l], b_merge[l],
            w_out[l])
        pool_p.append(bp)
        ret_p.append(sp)
        pool_s.append(bs)
        ret_s.append(ss)
    return (xp, xs, jnp.stack(pool_p), jnp.stack(ret_p), jnp.stack(pool_s), jnp.stack(ret_s))
```

```python
import functools

import jax
import jax.numpy as jnp
from jax import lax
from jax.experimental import pallas as pl
from jax.experimental.pallas import tpu as pltpu

F32 = jnp.float32
BF16 = jnp.bfloat16

D_MODEL = 2048
PAST_LEN = 16384
D_POOL = D_MODEL // 2
POOL_WINDOWS = (2, 4, 8, 16)
POOL_GROUP_DIM = D_POOL // len(POOL_WINDOWS)
POOL_BUF = max(POOL_WINDOWS) - 1
N_HEADS = 8
HEAD_DK = D_MODEL // 16
HEAD_DV = D_MODEL // 8
D_QK = N_HEADS * HEAD_DK
D_V = N_HEADS * HEAD_DV
CHUNK = 128
ROPE_BASE = 10000.0
EPS = 1e-6
D_IN = 2 * D_POOL + 2 * D_QK + 2 * D_V
D_Z = D_IN + 2 * D_MODEL
K_SCALE = HEAD_DK ** -0.5

VMEM_LIMIT_BYTES = 56 * 1024 * 1024
HALO = 16


def _params(*semantics):
    return pltpu.CompilerParams(dimension_semantics=semantics, vmem_limit_bytes=VMEM_LIMIT_BYTES)


def _resident(shape):
    return pl.BlockSpec(shape, lambda *_: (0,) * len(shape), pipeline_mode=pl.Buffered(1))


def _silu(x):
    return x * jax.nn.sigmoid(x)


def _mod_kernel(c_ref, w_ref, b_ref, o_ref):
    s = _silu(c_ref[...]).astype(BF16)
    o_ref[...] = jnp.dot(s, w_ref[...].astype(BF16), preferred_element_type=F32) + b_ref[...]


def _modulation(c_all, ada_w, ada_b):
    rows = c_all.shape[0]
    tn = 512
    return pl.pallas_call(
        _mod_kernel,
        out_shape=jax.ShapeDtypeStruct((rows, 3 * D_MODEL), F32),
        grid=(3 * D_MODEL // tn,),
        in_specs=[pl.BlockSpec((rows, D_MODEL), lambda j: (0, 0)),
                  pl.BlockSpec((D_MODEL, tn), lambda j: (0, j)),
                  pl.BlockSpec((1, tn), lambda j: (0, j))],
        out_specs=pl.BlockSpec((rows, tn), lambda j: (0, j)),
        compiler_params=_params("arbitrary"),
        name="modulation",
    )(c_all, ada_w, ada_b)


NORM_ROWS = 128


def _mod_rows(ref, rows, tm, rows_per_mod):
    if rows_per_mod == 1:
        return ref[rows, :]
    return ref[pl.ds((pl.program_id(0) * tm) // rows_per_mod, 1), :]


def _mod_spec(tm, rows_per_mod, mod_row0, col):
    if rows_per_mod == 1:
        assert mod_row0 % tm == 0
        return pl.BlockSpec((tm, D_MODEL), lambda i, *_: (mod_row0 // tm + i, col))
    assert mod_row0 % 8 == 0
    return pl.BlockSpec((8, D_MODEL), lambda i, *_: (mod_row0 // 8, col))


def _in_proj_kernel(x_ref, shift_ref, scale_ref, g_ref, w_in_ref, w_mg_ref, o_ref, h_ref, *, rows_per_mod, n_in_tiles):
    tm = x_ref.shape[0]
    j = pl.program_id(1)

    @pl.when(j == 0)
    def _():
        def body(r, carry):
            rows = pl.ds(pl.multiple_of(r * NORM_ROWS, NORM_ROWS), NORM_ROWS)
            x = x_ref[rows, :]
            xn = x * lax.rsqrt(jnp.mean(x * x, axis=-1, keepdims=True) + EPS) * g_ref[...]
            scale = _mod_rows(scale_ref, rows, tm, rows_per_mod)
            shift = _mod_rows(shift_ref, rows, tm, rows_per_mod)
            h_ref[rows, :] = (xn * (1.0 + scale) + shift).astype(BF16)
            return carry
        lax.fori_loop(0, tm // NORM_ROWS, body, 0)

    @pl.when(j < n_in_tiles)
    def _():
        o_ref[...] = jnp.dot(h_ref[...], w_in_ref[...], preferred_element_type=F32)

    @pl.when(j >= n_in_tiles)
    def _():
        o_ref[...] = jnp.dot(h_ref[...], w_mg_ref[...], preferred_element_type=F32)


def _in_proj(x, mod, g_pre, w_in, w_mg, *, tm, rows_per_mod, mod_row0):
    m = x.shape[0]
    tn = 1024
    n_in, n_mg = w_in.shape[1] // tn, w_mg.shape[1] // tn
    assert rows_per_mod == 1 or m // rows_per_mod <= 8
    mod_spec = lambda col: _mod_spec(tm, rows_per_mod, mod_row0, col)
    return pl.pallas_call(
        functools.partial(_in_proj_kernel, rows_per_mod=rows_per_mod, n_in_tiles=n_in),
        out_shape=jax.ShapeDtypeStruct((m, D_Z), F32),
        grid=(m // tm, n_in + n_mg),
        in_specs=[pl.BlockSpec((tm, D_MODEL), lambda i, j: (i, 0)),
                  mod_spec(0), mod_spec(1),
                  pl.BlockSpec((1, D_MODEL), lambda i, j: (0, 0)),
                  pl.BlockSpec((D_MODEL, tn), lambda i, j: (0, jnp.minimum(j, n_in - 1))),
                  pl.BlockSpec((D_MODEL, tn), lambda i, j: (0, jnp.maximum(j - n_in, 0)))],
        out_specs=pl.BlockSpec((tm, tn), lambda i, j: (i, j)),
        scratch_shapes=[pltpu.VMEM((tm, D_MODEL), BF16)],
        compiler_params=_params("parallel", "arbitrary"),
        name="in_proj",
    )(x, mod, mod, g_pre, w_in, w_mg)


def _rotate(x, cos, sin_signed):
    return x * cos + pltpu.roll(x, HEAD_DK // 2, 1) * sin_signed


def _group_norm_gate(o, gn, bg):
    mu = jnp.mean(o, axis=-1, keepdims=True)
    d = o - mu
    var = jnp.mean(d * d, axis=-1, keepdims=True)
    return d * lax.rsqrt(var + EPS) * gn * _silu(bg)


def _pool_project(pooled, pw, ps, ag):
    mixed = jnp.dot(pooled.astype(BF16), pw, preferred_element_type=F32)
    return mixed * ps * _silu(ag)


def _prompt_mix_kernel(cdec_ref, ax_ref, ag_ref, q_ref, k_ref, v_ref, bg_ref,
                       cos_ref, sin_ref, dmask_ref, qdec_ref, kdec_ref, pw_ref, ps_ref, gn_ref,
                       ya_ref, yb_ref, npool_ref, nret_ref, ext_ref):
    c = pl.program_id(1)
    rows = ax_ref.shape[0]

    @pl.when(c == 0)
    def _():
        ext_ref[0:HALO, :] = jnp.zeros((HALO, D_POOL), F32)
        nret_ref[...] = jnp.zeros(nret_ref.shape, F32)

    xa = ax_ref[...]
    ext_ref[HALO:HALO + rows, :] = xa
    pos = c * rows + lax.broadcasted_iota(jnp.int32, (rows, 1), 0)
    for g, w in enumerate(POOL_WINDOWS):
        lo, hi = g * POOL_GROUP_DIM, (g + 1) * POOL_GROUP_DIM
        win = xa[:, lo:hi]
        for s in range(1, w):
            win = win + ext_ref[HALO - s:HALO - s + rows, lo:hi]
        inv_cnt = 1.0 / jnp.minimum(pos + 1, w).astype(F32)
        pooled = win * inv_cnt - xa[:, lo:hi]
        ya = _pool_project(pooled, pw_ref[g], ps_ref[:, lo:hi], ag_ref[:, lo:hi])
        ya_ref[:, lo:hi] = ya.astype(BF16)

    @pl.when(c == pl.num_programs(1) - 1)
    def _():
        npool_ref[...] = ext_ref[HALO + rows - POOL_BUF:HALO + rows, :]

    ext_ref[0:HALO, :] = ext_ref[rows:rows + HALO, :]

    cos = cos_ref[...]
    sin = sin_ref[...]
    for h in range(N_HEADS):
        qs = slice(h * HEAD_DK, (h + 1) * HEAD_DK)
        vs = slice(h * HEAD_DV, (h + 1) * HEAD_DV)
        q = _rotate(q_ref[:, qs], cos, sin)
        k = _rotate(k_ref[:, qs], cos, sin) * K_SCALE
        v = v_ref[:, vs].astype(BF16)
        s_old = nret_ref[h]
        scores = lax.dot_general(q.astype(BF16), k.astype(BF16), (((1,), (1,)), ((), ())),
                                 preferred_element_type=F32) * dmask_ref[h]
        o = (jnp.dot(scores.astype(BF16), v, preferred_element_type=F32)
             + jnp.dot((q * qdec_ref[h]).astype(BF16), s_old.astype(BF16), preferred_element_type=F32))
        kv = lax.dot_general((k * kdec_ref[h]).astype(BF16), v, (((0,), (0,)), ((), ())),
                             preferred_element_type=F32)
        nret_ref[h] = cdec_ref[h] * s_old + kv
        yb_ref[:, vs] = _group_norm_gate(o, gn_ref[:, vs], bg_ref[:, vs]).astype(BF16)


def _prompt_mix(z, batch, seq, tables, pool_w, pool_scale, gn_g):
    cos, sin, dmask, qdec, kdec, cdec = tables
    nc = seq // CHUNK
    row = lambda b, c: b * nc + c
    zcol = lambda width, col: pl.BlockSpec((CHUNK, width), lambda b, c: (row(b, c), col))
    m = batch * seq
    return pl.pallas_call(
        _prompt_mix_kernel,
        out_shape=(jax.ShapeDtypeStruct((m, D_POOL), BF16),
                   jax.ShapeDtypeStruct((m, D_V), BF16),
                   jax.ShapeDtypeStruct((batch, POOL_BUF, D_POOL), F32),
                   jax.ShapeDtypeStruct((batch, N_HEADS, HEAD_DK, HEAD_DV), F32)),
        grid=(batch, nc),
        in_specs=[pl.BlockSpec(memory_space=pltpu.SMEM),
                  zcol(D_POOL, 0), zcol(D_POOL, 1), zcol(D_QK, 2), zcol(D_QK, 3), zcol(D_V, 2), zcol(D_V, 3),
                  pl.BlockSpec((CHUNK, HEAD_DK), lambda b, c: (c, 0)),
                  pl.BlockSpec((CHUNK, HEAD_DK), lambda b, c: (c, 0)),
                  _resident(dmask.shape), _resident(qdec.shape), _resident(kdec.shape),
                  _resident(pool_w.shape), _resident(pool_scale.shape), _resident(gn_g.shape)],
        out_specs=(pl.BlockSpec((CHUNK, D_POOL), lambda b, c: (row(b, c), 0)),
                   pl.BlockSpec((CHUNK, D_V), lambda b, c: (row(b, c), 0)),
                   pl.BlockSpec((None, POOL_BUF, D_POOL), lambda b, c: (b, 0, 0)),
                   pl.BlockSpec((None, N_HEADS, HEAD_DK, HEAD_DV), lambda b, c: (b, 0, 0, 0))),
        scratch_shapes=[pltpu.VMEM((HALO + CHUNK, D_POOL), F32)],
        compiler_params=_params("parallel", "arbitrary"),
        name="prompt_mix",
    )(cdec, z, z, z, z, z, z, cos, sin, dmask, qdec, kdec, pool_w, pool_scale, gn_g)


SAMPLE_TILE = 8


def _sample_mix_kernel(dec_ref, ax_ref, ag_ref, q_ref, k_ref, v_ref, bg_ref, cos_ref, sin_ref,
                       pw_ref, ps_ref, gn_ref, spool_ref, sret_ref,
                       ya_ref, yb_ref, npool_ref, nret_ref, o_ref, *, inv_cnt):
    bt = ax_ref.shape[0]

    xa = ax_ref[...]
    run = xa
    wins = {1: xa}
    for j in range(1, POOL_BUF + 1):
        run = run + spool_ref[POOL_BUF - j]
        wins[j + 1] = run
    for g, w in enumerate(POOL_WINDOWS):
        lo, hi = g * POOL_GROUP_DIM, (g + 1) * POOL_GROUP_DIM
        pooled = wins[w][:, lo:hi] * inv_cnt[g] - xa[:, lo:hi]
        ya = _pool_project(pooled, pw_ref[g], ps_ref[:, lo:hi], ag_ref[:, lo:hi])
        ya_ref[:, lo:hi] = ya.astype(BF16)
    for j in range(POOL_BUF - 1):
        npool_ref[j] = spool_ref[j + 1]
    npool_ref[POOL_BUF - 1] = xa

    cos = cos_ref[...]
    sin = sin_ref[...]
    for h in range(N_HEADS):
        qs = slice(h * HEAD_DK, (h + 1) * HEAD_DK)
        vs = slice(h * HEAD_DV, (h + 1) * HEAD_DV)
        q = _rotate(q_ref[:, qs], cos, sin)
        k = _rotate(k_ref[:, qs], cos, sin) * K_SCALE
        v = v_ref[:, vs]
        score = jnp.sum(q * k, axis=1, keepdims=True) * dec_ref[0, h]
        q_cols = jnp.transpose(q * dec_ref[1, h])
        k_cols = jnp.transpose(k * dec_ref[2, h])
        for r in range(bt):
            s_old = sret_ref[r, h]
            v_row = v[r:r + 1, :]
            o_row = score[r:r + 1, :] * v_row + jnp.sum(q_cols[:, r:r + 1] * s_old, axis=0, keepdims=True)
            nret_ref[r, h] = dec_ref[3, h] * s_old + k_cols[:, r:r + 1] * v_row
            o_ref[r:r + 1, vs] = o_row
    for h in range(N_HEADS):
        vs = slice(h * HEAD_DV, (h + 1) * HEAD_DV)
        yb_ref[:, vs] = _group_norm_gate(o_ref[:, vs], gn_ref[:, vs], bg_ref[:, vs]).astype(BF16)


def _sample_mix(z, state_pool, state_ret, tables, inv_cnt, pool_w, pool_scale, gn_g):
    cos, sin, dec = tables
    batch = z.shape[0]
    bt = SAMPLE_TILE
    zcol = lambda width, col: pl.BlockSpec((bt, width), lambda i: (i, col))
    return pl.pallas_call(
        functools.partial(_sample_mix_kernel, inv_cnt=inv_cnt),
        out_shape=(jax.ShapeDtypeStruct((batch, D_POOL), BF16),
                   jax.ShapeDtypeStruct((batch, D_V), BF16),
                   jax.ShapeDtypeStruct(state_pool.shape, state_pool.dtype),
                   jax.ShapeDtypeStruct(state_ret.shape, state_ret.dtype)),
        grid=(batch // bt,),
        in_specs=[pl.BlockSpec(memory_space=pltpu.SMEM),
                  zcol(D_POOL, 0), zcol(D_POOL, 1), zcol(D_QK, 2), zcol(D_QK, 3), zcol(D_V, 2), zcol(D_V, 3),
                  _resident(cos.shape), _resident(sin.shape),
                  _resident(pool_w.shape), _resident(pool_scale.shape), _resident(gn_g.shape),
                  pl.BlockSpec((POOL_BUF, bt, D_POOL), lambda i: (0, i, 0)),
                  pl.BlockSpec((bt, N_HEADS, HEAD_DK, HEAD_DV), lambda i: (i, 0, 0, 0))],
        out_specs=(pl.BlockSpec((bt, D_POOL), lambda i: (i, 0)),
                   pl.BlockSpec((bt, D_V), lambda i: (i, 0)),
                   pl.BlockSpec((POOL_BUF, bt, D_POOL), lambda i: (0, i, 0)),
                   pl.BlockSpec((bt, N_HEADS, HEAD_DK, HEAD_DV), lambda i: (i, 0, 0, 0))),
        scratch_shapes=[pltpu.VMEM((bt, D_V), F32)],
        compiler_params=_params("parallel"),
        name="sample_mix",
    )(dec, z, z, z, z, z, z, cos, sin, pool_w, pool_scale, gn_g, state_pool, state_ret)


def _out_proj_kernel(ya_ref, yb_ref, gp_ref, x_ref, gate_ref, bm_ref, wa_ref, wb_ref, wo_ref, gpost_ref, y_ref,
                     *, rows_per_mod):
    ya = jnp.dot(ya_ref[...], wa_ref[...], preferred_element_type=F32)
    yb = jnp.dot(yb_ref[...], wb_ref[...], preferred_element_type=F32)
    g_a = jax.nn.sigmoid(gp_ref[:, :D_MODEL] + bm_ref[:, :D_MODEL])
    g_b = jax.nn.sigmoid(gp_ref[:, D_MODEL:] + bm_ref[:, D_MODEL:])
    merged = (g_a * ya + g_b * yb).astype(BF16)
    o = jnp.dot(merged, wo_ref[...], preferred_element_type=F32)
    o = o * lax.rsqrt(jnp.mean(o * o, axis=-1, keepdims=True) + EPS) * gpost_ref[...]
    gate = _mod_rows(gate_ref, slice(None), x_ref.shape[0], rows_per_mod)
    y_ref[...] = x_ref[...] + gate * o


def _out_proj(ya, yb, z, x, mod, b_merge, w_a, w_b, w_o, g_post, *, tm, rows_per_mod, mod_row0):
    m = x.shape[0]
    return pl.pallas_call(
        functools.partial(_out_proj_kernel, rows_per_mod=rows_per_mod),
        out_shape=jax.ShapeDtypeStruct((m, D_MODEL), F32),
        grid=(m // tm,),
        in_specs=[pl.BlockSpec((tm, D_POOL), lambda i: (i, 0)),
                  pl.BlockSpec((tm, D_V), lambda i: (i, 0)),
                  pl.BlockSpec((tm, 2 * D_MODEL), lambda i: (i, D_IN // (2 * D_MODEL))),
                  pl.BlockSpec((tm, D_MODEL), lambda i: (i, 0)),
                  _mod_spec(tm, rows_per_mod, mod_row0, 2),
                  _resident(b_merge.shape), _resident(w_a.shape), _resident(w_b.shape), _resident(w_o.shape),
                  _resident(g_post.shape)],
        out_specs=pl.BlockSpec((tm, D_MODEL), lambda i: (i, 0)),
        compiler_params=_params("parallel"),
        name="out_proj",
    )(ya, yb, z, x, mod, b_merge, w_a, w_b, w_o, g_post)


def _rotary_tables(start, length):
    half = HEAD_DK // 2
    inv = ROPE_BASE ** (-jnp.arange(half, dtype=F32) / half)
    pos = start + jnp.arange(length, dtype=F32)
    ang = pos[:, None] * inv[None, :]
    cos, sin = jnp.cos(ang), jnp.sin(ang)
    return jnp.concatenate([cos, cos], axis=-1), jnp.concatenate([-sin, sin], axis=-1)


def _decay_tables(c):
    lg = jnp.log1p(-jnp.power(2.0, -5.0 - jnp.arange(N_HEADS, dtype=F32)))
    idx = jnp.arange(c, dtype=F32)
    diff = idx[:, None] - idx[None, :]
    dmask = jnp.where(diff[None] >= 0, jnp.exp(jnp.maximum(diff, 0.0)[None] * lg[:, None, None]), 0.0)
    q_dec = jnp.exp((idx + 1.0)[None, :] * lg[:, None])
    k_dec = jnp.exp((c - 1.0 - idx)[None, :] * lg[:, None])
    chunk_dec = jnp.exp(c * lg)
    return dmask, q_dec, k_dec, chunk_dec


def _layer(xp, xs, c_all, state_pool, state_ret, ada_w, ada_b, g_pre, g_post, w_in, pool_w, pool_scale, gn_g,
           w_a_proj, w_b_proj, w_merge, b_merge, w_out):
    batch, seq, _ = xp.shape
    dec_batch, dec_seq, _ = xs.shape
    assert dec_seq == 1 and seq % CHUNK == 0

    w_i, w_m, w_a, w_b, w_o, pw = (w.astype(BF16) for w in (w_in, w_merge, w_a_proj, w_b_proj, w_out, pool_w))
    row = lambda v: v.reshape(1, -1)
    g_pre, g_post, pool_scale, gn_g, b_merge = map(row, (g_pre, g_post, pool_scale, gn_g, b_merge))

    mod = _modulation(c_all, ada_w, row(ada_b))

    x2 = xp.reshape(batch * seq, D_MODEL)
    z = _in_proj(x2, mod, g_pre, w_i, w_m, tm=1024, rows_per_mod=seq, mod_row0=dec_batch)
    cos, sin = _rotary_tables(0, seq)
    dmask, q_dec, k_dec, chunk_dec = _decay_tables(CHUNK)
    wide = lambda d: jnp.broadcast_to(d[:, :, None], (N_HEADS, CHUNK, HEAD_DK))
    ya, yb, pool_p, ret_p = _prompt_mix(z, batch, seq, (cos, sin, dmask, wide(q_dec), wide(k_dec), chunk_dec),
                                        pw, pool_scale, gn_g)
    yp = _out_proj(ya, yb, z, x2, mod, b_merge, w_a, w_b, w_o, g_post, tm=256, rows_per_mod=seq, mod_row0=dec_batch)

    xs2 = xs.reshape(dec_batch, D_MODEL)
    zs = _in_proj(xs2, mod, g_pre, w_i, w_m, tm=dec_batch, rows_per_mod=1, mod_row0=0)
    cos_s, sin_s = _rotary_tables(PAST_LEN, 1)
    dmask_s, q_dec_s, k_dec_s, chunk_dec_s = _decay_tables(1)
    dec_s = jnp.stack([dmask_s[:, 0, 0], q_dec_s[:, 0], k_dec_s[:, 0], chunk_dec_s])
    inv_cnt = tuple(1.0 / min(PAST_LEN + 1, w) for w in POOL_WINDOWS)
    ya_s, yb_s, pool_s, ret_s = _sample_mix(zs, jnp.transpose(state_pool, (1, 0, 2)), state_ret,
                                            (cos_s, sin_s, dec_s), inv_cnt, pw, pool_scale, gn_g)
    pool_s = jnp.transpose(pool_s, (1, 0, 2))
    ys = _out_proj(ya_s, yb_s, zs, xs2, mod, b_merge, w_a, w_b, w_o, g_post, tm=dec_batch, rows_per_mod=1, mod_row0=0)

    return yp.reshape(xp.shape), ys.reshape(xs.shape), pool_p, ret_p, pool_s, ret_s


def kernel(x_prompt, x_sample, state_pool, state_ret, c_prompt, c_sample, ada_w, ada_b, g_pre, g_post,
           w_in, pool_w, pool_scale, gn_g, w_a_proj, w_b_proj, w_merge, b_merge, w_out):
    depth = ada_w.shape[0]
    xp, xs = x_prompt, x_sample
    assert c_sample.shape[0] % 8 == 0
    pad = (-c_prompt.shape[0]) % 8
    c_all = jnp.concatenate([c_sample, c_prompt, jnp.zeros((pad, D_MODEL), c_prompt.dtype)], axis=0)
    pool_p, ret_p, pool_s, ret_s = [], [], [], []
    for l in range(depth):
        xp, xs, bp, sp, bs, ss = _layer(
            xp, xs, c_all, state_pool[l], state_ret[l], ada_w[l], ada_b[l], g_pre[l], g_post[l], w_in[l],
            pool_w[l], pool_scale[l], gn_g[l].reshape(-1), w_a_proj[l], w_b_proj[l], w_merge[l], b_merge[l],
            w_out[l])
        pool_p.append(bp)
        ret_p.append(sp)
        pool_s.append(bs)
        ret_s.append(ss)
    return (xp, xs, jnp.stack(pool_p), jnp.stack(ret_p), jnp.stack(pool_s), jnp.stack(ret_s))
```

```python
import functools

import jax
import jax.numpy as jnp
import numpy as np
from jax import lax
from jax.experimental import pallas as pl
from jax.experimental.pallas import tpu as pltpu

F32 = jnp.float32
BF16 = jnp.bfloat16

D_MODEL = 2048
PAST_LEN = 16384
D_POOL = D_MODEL // 2
POOL_WINDOWS = (2, 4, 8, 16)
POOL_GROUP_DIM = D_POOL // len(POOL_WINDOWS)
POOL_BUF = max(POOL_WINDOWS) - 1
N_HEADS = 8
HEAD_DK = D_MODEL // 16
HEAD_DV = D_MODEL // 8
D_QK = N_HEADS * HEAD_DK
D_V = N_HEADS * HEAD_DV
CHUNK = 128
ROPE_BASE = 10000.0
EPS = 1e-6
D_IN = 2 * D_POOL + 2 * D_QK + 2 * D_V
D_Z = D_IN + 2 * D_MODEL
K_SCALE = HEAD_DK ** -0.5

VMEM_LIMIT_BYTES = 56 * 1024 * 1024
SUBLANES = 8
HALO = 16
CAST_STEPS = 32


def _params(*semantics):
    return pltpu.CompilerParams(dimension_semantics=semantics, vmem_limit_bytes=VMEM_LIMIT_BYTES)


def _cast_specs(w, step_of):
    rows = w.shape[0] // CAST_STEPS
    assert rows * CAST_STEPS == w.shape[0] and rows % (2 * SUBLANES) == 0
    spec = pl.BlockSpec((rows, w.shape[1]), lambda *idx: (jnp.minimum(step_of(*idx), CAST_STEPS - 1), 0))
    return spec, spec, jax.ShapeDtypeStruct(w.shape, BF16)


def _resident(shape):
    return pl.BlockSpec(shape, lambda *_: (0,) * len(shape), pipeline_mode=pl.Buffered(1))


def _silu(x):
    return x * jax.nn.sigmoid(x)


MOD_TN = 1536


def _mod_kernel(cs_ref, cp_ref, w_ref, b_ref, o_ref):
    ns, n_pad = cs_ref.shape[0], o_ref.shape[0] - cs_ref.shape[0] - cp_ref.shape[0]
    w = w_ref[...].astype(BF16)
    cp = jnp.concatenate([cp_ref[...], jnp.zeros((n_pad, D_MODEL), F32)], axis=0)
    o_ref[0:ns, :] = jnp.dot(_silu(cs_ref[...]).astype(BF16), w, preferred_element_type=F32) + b_ref[...]
    o_ref[ns:, :] = jnp.dot(_silu(cp).astype(BF16), w, preferred_element_type=F32) + b_ref[...]


def _modulation(c_sample, c_prompt, ada_w, ada_b):
    ns, n_p = c_sample.shape[0], c_prompt.shape[0]
    assert ns % SUBLANES == 0
    rows = ns + n_p + (-n_p) % SUBLANES
    return pl.pallas_call(
        _mod_kernel,
        out_shape=jax.ShapeDtypeStruct((rows, 3 * D_MODEL), F32),
        grid=(3 * D_MODEL // MOD_TN,),
        in_specs=[pl.BlockSpec((ns, D_MODEL), lambda j: (0, 0)),
                  pl.BlockSpec((n_p, D_MODEL), lambda j: (0, 0)),
                  pl.BlockSpec((D_MODEL, MOD_TN), lambda j: (0, j)),
                  pl.BlockSpec((1, MOD_TN), lambda j: (0, j))],
        out_specs=pl.BlockSpec((rows, MOD_TN), lambda j: (0, j)),
        compiler_params=_params("arbitrary"),
        name="modulation",
    )(c_sample, c_prompt, ada_w, ada_b)


NORM_ROWS = 128
NORM_TM = 2048


def _mod_rows(ref, rows, tm, rows_per_mod):
    if rows_per_mod == 1:
        return ref[rows, :]
    return ref[pl.ds((pl.program_id(0) * tm) // rows_per_mod, 1), :]


def _mod_spec(tm, rows_per_mod, mod_row0, col):
    if rows_per_mod == 1:
        assert mod_row0 % tm == 0
        return pl.BlockSpec((tm, D_MODEL), lambda i, *_: (mod_row0 // tm + i, col))
    assert mod_row0 % SUBLANES == 0
    return pl.BlockSpec((SUBLANES, D_MODEL), lambda i, *_: (mod_row0 // SUBLANES, col))


def _norm_mod(x, g, scale, shift):
    xn = x * lax.rsqrt(jnp.mean(x * x, axis=-1, keepdims=True) + EPS) * g
    return (xn * (1.0 + scale) + shift).astype(BF16)


def _norm_kernel(x_ref, shift_ref, scale_ref, g_ref, h_ref, *, rows_per_mod):
    tm = x_ref.shape[0]

    def body(r, carry):
        rows = pl.ds(pl.multiple_of(r * NORM_ROWS, NORM_ROWS), NORM_ROWS)
        h_ref[rows, :] = _norm_mod(x_ref[rows, :], g_ref[...], _mod_rows(scale_ref, rows, tm, rows_per_mod),
                                   _mod_rows(shift_ref, rows, tm, rows_per_mod))
        return carry
    lax.fori_loop(0, tm // NORM_ROWS, body, 0)


def _norm(x, mod, g_pre, *, tm, rows_per_mod, mod_row0):
    m = x.shape[0]
    assert rows_per_mod == 1 or m // rows_per_mod <= SUBLANES
    mod_spec = lambda col: _mod_spec(tm, rows_per_mod, mod_row0, col)
    return pl.pallas_call(
        functools.partial(_norm_kernel, rows_per_mod=rows_per_mod),
        out_shape=jax.ShapeDtypeStruct((m, D_MODEL), BF16),
        grid=(m // tm,),
        in_specs=[pl.BlockSpec((tm, D_MODEL), lambda i: (i, 0)), mod_spec(0), mod_spec(1),
                  pl.BlockSpec((1, D_MODEL), lambda i: (0, 0))],
        out_specs=pl.BlockSpec((tm, D_MODEL), lambda i: (i, 0)),
        compiler_params=_params("parallel"),
        name="norm",
    )(x, mod, mod, g_pre)


IN_TM = 1024
IN_TN = 2048
IN_DOT_COLS = 512


def _in_proj_kernel(h_ref, xs_ref, shift_s_ref, scale_s_ref, g_ref, wa_ref, wb_ref, w_in_hbm, w_mg_hbm,
                    z_ref, zs_ref, wa_bf_ref, wb_bf_ref, wbf0_ref, wbf1_ref, stage_ref, hs_ref, sem,
                    *, n_in_tiles, n_j, n_i):
    j, i = pl.program_id(0), pl.program_id(1)
    t = j * n_i + i
    total = n_j * n_i
    tn = wbf0_ref.shape[1]
    chunk = stage_ref.shape[1]

    def chunk_copy(w_hbm, col0, r, b):
        rows = pl.ds(pl.multiple_of(r * chunk, chunk), chunk)
        cols = pl.ds(col0 if isinstance(col0, int) else pl.multiple_of(col0, tn), tn)
        return pltpu.make_async_copy(w_hbm.at[rows, cols], stage_ref.at[b], sem.at[b])

    def start_chunk(g):
        g = lax.rem(jnp.asarray(g, jnp.int32), total)
        jt, r, b = lax.div(g, n_i), lax.rem(g, n_i), lax.rem(g, 2)

        @pl.when(jt < n_in_tiles)
        def _():
            chunk_copy(w_in_hbm, jt * tn, r, b).start()

        @pl.when(jt >= n_in_tiles)
        def _():
            chunk_copy(w_mg_hbm, (jt - n_in_tiles) * tn, r, b).start()

    def land_chunk(g, dst_ref):
        g = jnp.asarray(g, jnp.int32)
        r, b = lax.rem(g, n_i), lax.rem(g, 2)
        chunk_copy(w_in_hbm, 0, r, b).wait()
        dst_ref[pl.ds(pl.multiple_of(r * chunk, chunk), chunk), :] = stage_ref[b].astype(BF16)

    @pl.when(t == 0)
    def _():
        hs_ref[...] = _norm_mod(xs_ref[...], g_ref[...], scale_s_ref[...], shift_s_ref[...])
        start_chunk(0)

        def body(g, carry):
            start_chunk(g + 1)
            land_chunk(g, wbf0_ref)
            return carry
        lax.fori_loop(0, n_i, body, 0)

    @pl.when(t + 1 < total)
    def _():
        start_chunk(t + n_i + 1)

    def multiply(w_cur, w_nxt):
        land_chunk(t + n_i, w_nxt)
        wa_bf_ref[...] = wa_ref[...].astype(BF16)
        wb_bf_ref[...] = wb_ref[...].astype(BF16)
        for c0 in range(0, tn, IN_DOT_COLS):
            cols = slice(c0, c0 + IN_DOT_COLS)
            z_ref[:, cols] = jnp.dot(h_ref[...], w_cur[:, cols], preferred_element_type=F32)

        @pl.when(i == 0)
        def _():
            zs_ref[...] = jnp.dot(hs_ref[...], w_cur[...], preferred_element_type=F32)

    @pl.when(lax.rem(j, 2) == 0)
    def _():
        multiply(wbf0_ref, wbf1_ref)

    @pl.when(lax.rem(j, 2) == 1)
    def _():
        multiply(wbf1_ref, wbf0_ref)


def _in_proj(h, xs, mod, g_pre, w_in, w_mg, w_a, w_b):
    m, ms = h.shape[0], xs.shape[0]
    n_in, n_mg = w_in.shape[1] // IN_TN, w_mg.shape[1] // IN_TN
    n_i = m // IN_TM
    chunk = D_MODEL // n_i
    assert chunk * n_i == D_MODEL and chunk % 16 == 0 and (n_in + n_mg) * n_i >= CAST_STEPS
    const = lambda shape, col: pl.BlockSpec(shape, lambda j, i: (0, col), pipeline_mode=pl.Buffered(1))
    wa_in, wa_out, wa_shape = _cast_specs(w_a, lambda j, i: j * n_i + i)
    wb_in, wb_out, wb_shape = _cast_specs(w_b, lambda j, i: j * n_i + i)
    return pl.pallas_call(
        functools.partial(_in_proj_kernel, n_in_tiles=n_in, n_j=n_in + n_mg, n_i=n_i),
        out_shape=(jax.ShapeDtypeStruct((m, D_Z), F32), jax.ShapeDtypeStruct((ms, D_Z), F32), wa_shape, wb_shape),
        grid=(n_in + n_mg, n_i),
        in_specs=[pl.BlockSpec((IN_TM, D_MODEL), lambda j, i: (i, 0)),
                  const((ms, D_MODEL), 0), const((ms, D_MODEL), 0), const((ms, D_MODEL), 1),
                  const((1, D_MODEL), 0), wa_in, wb_in,
                  pl.BlockSpec(memory_space=pl.ANY), pl.BlockSpec(memory_space=pl.ANY)],
        out_specs=(pl.BlockSpec((IN_TM, IN_TN), lambda j, i: (i, j)),
                   pl.BlockSpec((ms, IN_TN), lambda j, i: (0, j)), wa_out, wb_out),
        scratch_shapes=[pltpu.VMEM((D_MODEL, IN_TN), BF16), pltpu.VMEM((D_MODEL, IN_TN), BF16),
                        pltpu.VMEM((2, chunk, IN_TN), F32),
                        pltpu.VMEM((ms, D_MODEL), BF16),
                        pltpu.SemaphoreType.DMA((2,))],
        compiler_params=_params("arbitrary", "arbitrary"),
        name="in_proj",
    )(h, xs, mod, mod, g_pre, w_a, w_b, w_in, w_mg)


def _rotate(x, cos, sin_signed):
    return x * cos + pltpu.roll(x, HEAD_DK // 2, 1) * sin_signed


def _group_norm_gate(o, gn, bg):
    mu = jnp.mean(o, axis=-1, keepdims=True)
    d = o - mu
    var = jnp.mean(d * d, axis=-1, keepdims=True)
    return d * lax.rsqrt(var + EPS) * gn * _silu(bg)


def _pool_project(pooled, pw, ps, ag):
    mixed = jnp.dot(pooled.astype(BF16), pw.astype(BF16), preferred_element_type=F32)
    return mixed * ps * _silu(ag)


PROJ_SPLIT = 8
PROJ_COLS = D_MODEL // PROJ_SPLIT


def _merge_chunk(y_a, y_b, j, gp_ref, bm_ref, wa_ref, wb_ref):
    lo, hi = j * PROJ_COLS, (j + 1) * PROJ_COLS
    ya = jnp.dot(y_a, wa_ref[:, lo:hi], preferred_element_type=F32)
    yb = jnp.dot(y_b, wb_ref[:, lo:hi], preferred_element_type=F32)
    g_a = jax.nn.sigmoid(gp_ref[:, lo:hi] + bm_ref[:, lo:hi])
    g_b = jax.nn.sigmoid(gp_ref[:, D_MODEL + lo:D_MODEL + hi] + bm_ref[:, D_MODEL + lo:D_MODEL + hi])
    return (g_a * ya + g_b * yb).astype(BF16)


def _out_chunk(merged, j, wo_ref):
    return jnp.dot(merged, wo_ref[:, j * PROJ_COLS:(j + 1) * PROJ_COLS], preferred_element_type=F32)


def _post_norm(o, gpost_ref):
    return o * lax.rsqrt(jnp.mean(o * o, axis=-1, keepdims=True) + EPS) * gpost_ref[...]


def _merge_project(y_a, y_b, gp_ref, bm_ref, wa_ref, wb_ref, wo_ref, gpost_ref):
    merged = jnp.concatenate([_merge_chunk(y_a, y_b, j, gp_ref, bm_ref, wa_ref, wb_ref)
                              for j in range(PROJ_SPLIT)], axis=1)
    o = jnp.concatenate([_out_chunk(merged, j, wo_ref) for j in range(PROJ_SPLIT)], axis=1)
    return _post_norm(o, gpost_ref)


MIX_ROWS = 2 * CHUNK
MIX_HEADS = 4
Z_AX, Z_AG, Z_Q, Z_K, Z_V, Z_BG = 0, D_POOL, 2 * D_POOL, 2 * D_POOL + D_QK, 2 * D_POOL + 2 * D_QK, D_IN - D_V


def _prompt_kernel(cdec_ref, z_ref, rot_ref, dmask_ref, qdec_ref, kdec_ref, pw_ref, ps_ref, gn_ref,
                   gp_ref, bm_ref, wa_ref, wb_ref, wo_ref,
                   m_ref, npool_ref, nret_ref, wo_bf_ref, ya0_ref, yb0_ref, ya1_ref, yb1_ref, ext_ref, *, tiles_per_seq):
    s = pl.program_id(0)
    n_tiles = pl.num_programs(0) - 1
    live = s < n_tiles
    c = lax.rem(jnp.minimum(s, n_tiles - 1), tiles_per_seq)
    rows = z_ref.shape[0]

    @pl.when(s == 0)
    def _():
        for ref in (ya0_ref, yb0_ref, ya1_ref, yb1_ref):
            ref[...] = jnp.zeros(ref.shape, BF16)

    @pl.when(c == 0)
    def _():
        ext_ref[0:HALO, :] = jnp.zeros((HALO, D_POOL), F32)
        nret_ref[...] = jnp.zeros(nret_ref.shape, F32)

    def step(ya_rd, yb_rd, ya_wr, yb_wr):
        y_a, y_b = ya_rd[...], yb_rd[...]
        wo_bf_ref[...] = wo_ref[...].astype(BF16)
        xa = z_ref[:, Z_AX:Z_AX + D_POOL]
        ext_ref[HALO:HALO + rows, :] = xa
        pos = c * rows + lax.broadcasted_iota(jnp.int32, (rows, 1), 0)
        nt = (((1,), (1,)), ((), ()))
        tn = (((0,), (0,)), ((), ()))

        def merge_piece(j):
            m_ref[:, j * PROJ_COLS:(j + 1) * PROJ_COLS] = _merge_chunk(y_a, y_b, j, gp_ref, bm_ref, wa_ref, wb_ref)

        def pool_group(g):
            w = POOL_WINDOWS[g]
            lo, hi = g * POOL_GROUP_DIM, (g + 1) * POOL_GROUP_DIM
            acc = ext_ref[:, lo:hi]
            span = 1
            while span < w:
                acc = acc + pltpu.roll(acc, span, 0)
                span *= 2
            inv_cnt = 1.0 / jnp.minimum(pos + 1, w).astype(F32)
            pooled = acc[HALO:, :] * inv_cnt - xa[:, lo:hi]
            ya = _pool_project(pooled, pw_ref[g], ps_ref[:, lo:hi], z_ref[:, Z_AG + lo:Z_AG + hi])
            ya_wr[:, lo:hi] = ya.astype(BF16)

        per_pool = PROJ_SPLIT // len(POOL_WINDOWS)
        assert per_pool * len(POOL_WINDOWS) == PROJ_SPLIT
        fillers = [f for g in range(len(POOL_WINDOWS))
                   for f in [functools.partial(merge_piece, g * per_pool + j) for j in range(per_pool)]
                   + [functools.partial(pool_group, g)]]
        n_sub = rows // CHUNK
        n_slots = 2 * (N_HEADS // MIX_HEADS) * n_sub
        slot = [0]

        def fill():
            lo, hi = (slot[0] * len(fillers)) // n_slots, ((slot[0] + 1) * len(fillers)) // n_slots
            slot[0] += 1
            for f in fillers[lo:hi]:
                f()

        for h0 in range(0, N_HEADS, MIX_HEADS):
            heads = range(h0, h0 + MIX_HEADS)
            s_cur = {h: nret_ref[h] for h in heads}
            for ci in range(n_sub):
                rs = slice(ci * CHUNK, (ci + 1) * CHUNK)
                cos = rot_ref[rs, :HEAD_DK]
                sin = rot_ref[rs, HEAD_DK:]
                q = {h: _rotate(z_ref[rs, Z_Q + h * HEAD_DK:Z_Q + (h + 1) * HEAD_DK], cos, sin) for h in heads}
                k = {h: _rotate(z_ref[rs, Z_K + h * HEAD_DK:Z_K + (h + 1) * HEAD_DK], cos, sin) * K_SCALE
                     for h in heads}
                v = {h: z_ref[rs, Z_V + h * HEAD_DV:Z_V + (h + 1) * HEAD_DV].astype(BF16) for h in heads}
                fill()
                scores = {h: lax.dot_general(q[h].astype(BF16), k[h].astype(BF16), nt, preferred_element_type=F32)
                          for h in heads}
                carry = {h: jnp.dot((q[h] * qdec_ref[h]).astype(BF16), s_cur[h].astype(BF16),
                                    preferred_element_type=F32) for h in heads}
                kv = {h: lax.dot_general((k[h] * kdec_ref[h]).astype(BF16), v[h], tn, preferred_element_type=F32)
                      for h in heads}
                masked = {h: (scores[h] * dmask_ref[h]).astype(BF16) for h in heads}
                fill()
                for h in heads:
                    o = jnp.dot(masked[h], v[h], preferred_element_type=F32) + carry[h]
                    s_cur[h] = cdec_ref[h] * s_cur[h] + kv[h]
                    vs = slice(h * HEAD_DV, (h + 1) * HEAD_DV)
                    bg = z_ref[rs, Z_BG + h * HEAD_DV:Z_BG + (h + 1) * HEAD_DV]
                    yb_wr[rs, vs] = _group_norm_gate(o, gn_ref[:, vs], bg).astype(BF16)
            for h in heads:
                nret_ref[h] = jnp.where(live, s_cur[h], nret_ref[h])
        assert slot[0] == n_slots

    @pl.when(lax.rem(s, 2) == 0)
    def _():
        step(ya1_ref, yb1_ref, ya0_ref, yb0_ref)

    @pl.when(lax.rem(s, 2) == 1)
    def _():
        step(ya0_ref, yb0_ref, ya1_ref, yb1_ref)

    @pl.when(c == tiles_per_seq - 1)
    def _():
        npool_ref[...] = ext_ref[HALO + rows - POOL_BUF:HALO + rows, :]

    ext_ref[0:HALO, :] = ext_ref[rows:rows + HALO, :]


def _prompt_mix_merge(z, batch, seq, tables, pool_w, pool_scale, gn_g, b_merge, w_a, w_b, w_o):
    rot, dmask, qdec, kdec, cdec = tables
    tps = seq // MIX_ROWS
    n_tiles = batch * tps
    assert n_tiles >= CAST_STEPS
    cur = lambda s: jnp.minimum(s, n_tiles - 1)
    prev = lambda s: jnp.maximum(s - 1, 0)
    m = batch * seq
    wo_in, wo_out, wo_shape = _cast_specs(w_o, lambda s: s)
    return pl.pallas_call(
        functools.partial(_prompt_kernel, tiles_per_seq=tps),
        out_shape=(jax.ShapeDtypeStruct((m, D_MODEL), BF16),
                   jax.ShapeDtypeStruct((batch, POOL_BUF, D_POOL), F32),
                   jax.ShapeDtypeStruct((batch, N_HEADS, HEAD_DK, HEAD_DV), F32), wo_shape),
        grid=(n_tiles + 1,),
        in_specs=[pl.BlockSpec(memory_space=pltpu.SMEM),
                  pl.BlockSpec((MIX_ROWS, D_IN), lambda s: (cur(s), 0)),
                  pl.BlockSpec((MIX_ROWS, 2 * HEAD_DK), lambda s: (cur(s) % tps, 0)),
                  _resident(dmask.shape), _resident(qdec.shape), _resident(kdec.shape),
                  _resident(pool_w.shape), _resident(pool_scale.shape), _resident(gn_g.shape),
                  pl.BlockSpec((MIX_ROWS, 2 * D_MODEL), lambda s: (prev(s), D_IN // (2 * D_MODEL))),
                  _resident(b_merge.shape), _resident(w_a.shape), _resident(w_b.shape), wo_in],
        out_specs=(pl.BlockSpec((MIX_ROWS, D_MODEL), lambda s: (prev(s), 0)),
                   pl.BlockSpec((None, POOL_BUF, D_POOL), lambda s: (cur(s) // tps, 0, 0)),
                   pl.BlockSpec((None, N_HEADS, HEAD_DK, HEAD_DV), lambda s: (cur(s) // tps, 0, 0, 0)), wo_out),
        scratch_shapes=[pltpu.VMEM((MIX_ROWS, D_POOL), BF16), pltpu.VMEM((MIX_ROWS, D_V), BF16),
                        pltpu.VMEM((MIX_ROWS, D_POOL), BF16), pltpu.VMEM((MIX_ROWS, D_V), BF16),
                        pltpu.VMEM((HALO + MIX_ROWS, D_POOL), F32)],
        compiler_params=_params("arbitrary"),
        name="prompt_mix_merge",
    )(cdec, z, rot, dmask, qdec, kdec, pool_w, pool_scale, gn_g, z, b_merge, w_a, w_b, w_o)


SAMPLE_TILE = 8
SAMPLE_HEADS = 4
TAIL_ROWS = 256


def _tail_kernel(dec_ref, m_ref, x_ref, gate_ref, wo_ref, gpost_ref,
                 ax_ref, ag_ref, q_ref, k_ref, v_ref, bg_ref, cos_ref, sin_ref,
                 pw_ref, ps_ref, gn_ref, spool_ref, sret_ref,
                 y_ref, ya_ref, yb_ref, npool_ref, nret_ref, o_ref, *, inv_cnt, rows_per_mod):
    bt = ax_ref.shape[0]
    hg = pl.program_id(1)

    o = _post_norm(jnp.dot(m_ref[...], wo_ref[...], preferred_element_type=F32), gpost_ref)
    tile = pl.program_id(0) * pl.num_programs(1) + hg
    gate = gate_ref[pl.ds((tile * x_ref.shape[0]) // rows_per_mod, 1), :]
    y_ref[...] = x_ref[...] + gate * o

    xa = ax_ref[...]
    run = xa
    wins = {1: xa}
    for j in range(1, POOL_BUF + 1):
        run = run + spool_ref[POOL_BUF - j]
        wins[j + 1] = run
    for g, w in enumerate(POOL_WINDOWS):
        lo, hi = g * POOL_GROUP_DIM, (g + 1) * POOL_GROUP_DIM
        pooled = wins[w][:, lo:hi] * inv_cnt[g] - xa[:, lo:hi]
        ya = _pool_project(pooled, pw_ref[g], ps_ref[:, lo:hi], ag_ref[:, lo:hi])
        ya_ref[:, lo:hi] = ya.astype(BF16)
    for j in range(POOL_BUF - 1):
        npool_ref[j] = spool_ref[j + 1]
    npool_ref[POOL_BUF - 1] = xa

    cos = cos_ref[...]
    sin = sin_ref[...]
    for hl in range(SAMPLE_HEADS):
        h = hg * SAMPLE_HEADS + hl
        qs = slice(hl * HEAD_DK, (hl + 1) * HEAD_DK)
        vs = slice(hl * HEAD_DV, (hl + 1) * HEAD_DV)
        q = _rotate(q_ref[:, qs], cos, sin)
        k = _rotate(k_ref[:, qs], cos, sin) * K_SCALE
        v = v_ref[:, vs]
        score = jnp.sum(q * k, axis=1, keepdims=True) * dec_ref[0, h]
        q_cols = jnp.transpose(q * dec_ref[1, h])
        k_cols = jnp.transpose(k * dec_ref[2, h])
        for r in range(bt):
            s_old = sret_ref[r, hl]
            v_row = v[r:r + 1, :]
            o_row = score[r:r + 1, :] * v_row + jnp.sum(q_cols[:, r:r + 1] * s_old, axis=0, keepdims=True)
            nret_ref[r, hl] = dec_ref[3, h] * s_old + k_cols[:, r:r + 1] * v_row
            o_ref[r:r + 1, vs] = o_row
    for hl in range(SAMPLE_HEADS):
        vs = slice(hl * HEAD_DV, (hl + 1) * HEAD_DV)
        yb_ref[:, vs] = _group_norm_gate(o_ref[:, vs], gn_ref[:, vs], bg_ref[:, vs]).astype(BF16)


def _tail(merged, x, mod, w_o, g_post, zs, state_pool, state_ret, tables, inv_cnt, pool_w, pool_scale, gn_g,
          *, rows_per_mod, mod_row0):
    cos, sin, dec = tables
    m, batch = x.shape[0], zs.shape[0]
    bt, hs = SAMPLE_TILE, SAMPLE_HEADS
    n_hg = N_HEADS // hs
    assert m == (batch // bt) * n_hg * TAIL_ROWS and mod_row0 % SUBLANES == 0 and m // rows_per_mod <= SUBLANES
    qw, vw = hs * HEAD_DK, hs * HEAD_DV
    rows = lambda i, g: (i * n_hg + g, 0)
    zcol = lambda width, col0, per_group: pl.BlockSpec(
        (bt, width), lambda i, g: (i, col0 // width + (g if per_group else 0)))
    return pl.pallas_call(
        functools.partial(_tail_kernel, inv_cnt=inv_cnt, rows_per_mod=rows_per_mod),
        out_shape=(jax.ShapeDtypeStruct((m, D_MODEL), F32),
                   jax.ShapeDtypeStruct((batch, D_POOL), BF16),
                   jax.ShapeDtypeStruct((batch, D_V), BF16),
                   jax.ShapeDtypeStruct(state_pool.shape, state_pool.dtype),
                   jax.ShapeDtypeStruct(state_ret.shape, state_ret.dtype)),
        grid=(batch // bt, n_hg),
        in_specs=[pl.BlockSpec(memory_space=pltpu.SMEM),
                  pl.BlockSpec((TAIL_ROWS, D_MODEL), rows), pl.BlockSpec((TAIL_ROWS, D_MODEL), rows),
                  pl.BlockSpec((SUBLANES, D_MODEL), lambda i, g: (mod_row0 // SUBLANES, 2)),
                  _resident(w_o.shape), _resident(g_post.shape),
                  zcol(D_POOL, Z_AX, False), zcol(D_POOL, Z_AG, False),
                  zcol(qw, Z_Q, True), zcol(qw, Z_K, True), zcol(vw, Z_V, True), zcol(vw, Z_BG, True),
                  _resident(cos.shape), _resident(sin.shape),
                  _resident(pool_w.shape), _resident(pool_scale.shape),
                  pl.BlockSpec((1, vw), lambda i, g: (0, g)),
                  pl.BlockSpec((POOL_BUF, bt, D_POOL), lambda i, g: (0, i, 0)),
                  pl.BlockSpec((bt, hs, HEAD_DK, HEAD_DV), lambda i, g: (i, g, 0, 0))],
        out_specs=(pl.BlockSpec((TAIL_ROWS, D_MODEL), rows),
                   pl.BlockSpec((bt, D_POOL), lambda i, g: (i, 0)),
                   pl.BlockSpec((bt, vw), lambda i, g: (i, g)),
                   pl.BlockSpec((POOL_BUF, bt, D_POOL), lambda i, g: (0, i, 0)),
                   pl.BlockSpec((bt, hs, HEAD_DK, HEAD_DV), lambda i, g: (i, g, 0, 0))),
        scratch_shapes=[pltpu.VMEM((bt, vw), F32)],
        compiler_params=_params("arbitrary", "arbitrary"),
        name="tail",
    )(dec, merged, x, mod, w_o, g_post, zs, zs, zs, zs, zs, zs, cos, sin, pool_w, pool_scale, gn_g,
      state_pool, state_ret)


def _out_proj_kernel(ya_ref, yb_ref, gp_ref, x_ref, gate_ref, bm_ref, wa_ref, wb_ref, wo_ref, gpost_ref, y_ref,
                     *, rows_per_mod):
    o = _merge_project(ya_ref[...], yb_ref[...], gp_ref, bm_ref, wa_ref, wb_ref, wo_ref, gpost_ref)
    gate = _mod_rows(gate_ref, slice(None), x_ref.shape[0], rows_per_mod)
    y_ref[...] = x_ref[...] + gate * o


def _out_proj(ya, yb, z, x, mod, b_merge, w_a, w_b, w_o, g_post, *, tm, rows_per_mod, mod_row0):
    m = x.shape[0]
    return pl.pallas_call(
        functools.partial(_out_proj_kernel, rows_per_mod=rows_per_mod),
        out_shape=jax.ShapeDtypeStruct((m, D_MODEL), F32),
        grid=(m // tm,),
        in_specs=[pl.BlockSpec((tm, D_POOL), lambda i: (i, 0)),
                  pl.BlockSpec((tm, D_V), lambda i: (i, 0)),
                  pl.BlockSpec((tm, 2 * D_MODEL), lambda i: (i, D_IN // (2 * D_MODEL))),
                  pl.BlockSpec((tm, D_MODEL), lambda i: (i, 0)),
                  _mod_spec(tm, rows_per_mod, mod_row0, 2),
                  _resident(b_merge.shape), _resident(w_a.shape), _resident(w_b.shape), _resident(w_o.shape),
                  _resident(g_post.shape)],
        out_specs=pl.BlockSpec((tm, D_MODEL), lambda i: (i, 0)),
        compiler_params=_params("parallel"),
        name="out_proj",
    )(ya, yb, z, x, mod, b_merge, w_a, w_b, w_o, g_post)


def _rotary_tables(start, length):
    half = HEAD_DK // 2
    inv = ROPE_BASE ** (-np.arange(half, dtype=np.float64) / half)
    ang = (start + np.arange(length, dtype=np.float64))[:, None] * inv[None, :]
    cos, sin = np.cos(ang), np.sin(ang)
    return (np.concatenate([cos, cos], axis=-1).astype(np.float32),
            np.concatenate([-sin, sin], axis=-1).astype(np.float32))


def _decay_tables(c):
    lg = np.log1p(-np.power(2.0, -5.0 - np.arange(N_HEADS, dtype=np.float64)))
    idx = np.arange(c, dtype=np.float64)
    diff = idx[:, None] - idx[None, :]
    dmask = np.where(diff[None] >= 0, np.exp(np.maximum(diff, 0.0)[None] * lg[:, None, None]), 0.0)
    q_dec = np.exp((idx + 1.0)[None, :] * lg[:, None])
    k_dec = np.exp((c - 1.0 - idx)[None, :] * lg[:, None])
    chunk_dec = np.exp(c * lg)
    return tuple(a.astype(np.float32) for a in (dmask, q_dec, k_dec, chunk_dec))


def _layer(xp, xs, c_prompt, c_sample, state_pool, state_ret, ada_w, ada_b, g_pre, g_post, w_in, pool_w, pool_scale, gn_g,
           w_a_proj, w_b_proj, w_merge, b_merge, w_out):
    batch, seq, _ = xp.shape
    dec_batch, dec_seq, _ = xs.shape
    assert dec_seq == 1 and seq % CHUNK == 0

    row = lambda v: v.reshape(1, -1)
    g_pre, g_post, pool_scale, gn_g, b_merge = map(row, (g_pre, g_post, pool_scale, gn_g, b_merge))

    mod = _modulation(c_sample, c_prompt, ada_w, row(ada_b))

    x2 = xp.reshape(batch * seq, D_MODEL)
    xs2 = xs.reshape(dec_batch, D_MODEL)
    h = _norm(x2, mod, g_pre, tm=NORM_TM, rows_per_mod=seq, mod_row0=dec_batch)
    z, zs, w_a, w_b = _in_proj(h, xs2, mod, g_pre, w_in, w_merge, w_a_proj, w_b_proj)

    cos, sin = _rotary_tables(0, seq)
    dmask, q_dec, k_dec, chunk_dec = _decay_tables(CHUNK)
    wide = lambda d: np.ascontiguousarray(np.broadcast_to(d[:, :, None], (N_HEADS, CHUNK, HEAD_DK)))
    rot = np.concatenate([cos, sin], axis=1)
    merged, pool_p, ret_p, w_o = _prompt_mix_merge(z, batch, seq, (rot, dmask, wide(q_dec), wide(k_dec), chunk_dec),
                                                   pool_w, pool_scale, gn_g, b_merge, w_a, w_b, w_out)

    cos_s, sin_s = _rotary_tables(PAST_LEN, 1)
    dmask_s, q_dec_s, k_dec_s, chunk_dec_s = _decay_tables(1)
    dec_s = np.stack([dmask_s[:, 0, 0], q_dec_s[:, 0], k_dec_s[:, 0], chunk_dec_s])
    inv_cnt = tuple(1.0 / min(PAST_LEN + 1, w) for w in POOL_WINDOWS)
    yp, ya_s, yb_s, pool_s, ret_s = _tail(merged, x2, mod, w_o, g_post, zs, jnp.transpose(state_pool, (1, 0, 2)),
                                          state_ret, (cos_s, sin_s, dec_s), inv_cnt, pool_w, pool_scale, gn_g,
                                          rows_per_mod=seq, mod_row0=dec_batch)
    pool_s = jnp.transpose(pool_s, (1, 0, 2))
    ys = _out_proj(ya_s, yb_s, zs, xs2, mod, b_merge, w_a, w_b, w_o, g_post, tm=dec_batch, rows_per_mod=1, mod_row0=0)

    return yp.reshape(xp.shape), ys.reshape(xs.shape), pool_p, ret_p, pool_s, ret_s


def kernel(x_prompt, x_sample, state_pool, state_ret, c_prompt, c_sample, ada_w, ada_b, g_pre, g_post,
           w_in, pool_w, pool_scale, gn_g, w_a_proj, w_b_proj, w_merge, b_merge, w_out):
    depth = ada_w.shape[0]
    xp, xs = x_prompt, x_sample
    pool_p, ret_p, pool_s, ret_s = [], [], [], []
    for l in range(depth):
        xp, xs, bp, sp, bs, ss = _layer(
            xp, xs, c_prompt, c_sample, state_pool[l], state_ret[l], ada_w[l], ada_b[l], g_pre[l], g_post[l], w_in[l],
            pool_w[l], pool_scale[l], gn_g[l].reshape(-1), w_a_proj[l], w_b_proj[l], w_merge[l], b_merge[l],
            w_out[l])
        pool_p.append(bp)
        ret_p.append(sp)
        pool_s.append(bs)
        ret_s.append(ss)
    return (xp, xs, jnp.stack(pool_p), jnp.stack(ret_p), jnp.stack(pool_s), jnp.stack(ret_s))
```

```python
import functools

import jax
import jax.numpy as jnp
import numpy as np
from jax import lax
from jax.experimental import pallas as pl
from jax.experimental.pallas import tpu as pltpu

F32 = jnp.float32
BF16 = jnp.bfloat16

D_MODEL = 2048
PAST_LEN = 16384
D_POOL = D_MODEL // 2
POOL_WINDOWS = (2, 4, 8, 16)
POOL_GROUP_DIM = D_POOL // len(POOL_WINDOWS)
POOL_BUF = max(POOL_WINDOWS) - 1
N_HEADS = 8
HEAD_DK = D_MODEL // 16
HEAD_DV = D_MODEL // 8
D_QK = N_HEADS * HEAD_DK
D_V = N_HEADS * HEAD_DV
CHUNK = 128
ROPE_BASE = 10000.0
EPS = 1e-6
D_IN = 2 * D_POOL + 2 * D_QK + 2 * D_V
D_Z = D_IN + 2 * D_MODEL
K_SCALE = HEAD_DK ** -0.5

VMEM_LIMIT_BYTES = 56 * 1024 * 1024
SUBLANES = 8
HALO = 16
CAST_STEPS = 32


def _params(*semantics):
    return pltpu.CompilerParams(dimension_semantics=semantics, vmem_limit_bytes=VMEM_LIMIT_BYTES)


def _cast_specs(w, step_of):
    rows = w.shape[0] // CAST_STEPS
    assert rows * CAST_STEPS == w.shape[0] and rows % (2 * SUBLANES) == 0
    spec = pl.BlockSpec((rows, w.shape[1]), lambda *idx: (jnp.minimum(step_of(*idx), CAST_STEPS - 1), 0))
    return spec, spec, jax.ShapeDtypeStruct(w.shape, BF16)


def _resident(shape):
    return pl.BlockSpec(shape, lambda *_: (0,) * len(shape), pipeline_mode=pl.Buffered(1))


def _silu(x):
    return x * jax.nn.sigmoid(x)


MOD_TN = 1536


def _mod_kernel(cs_ref, cp_ref, w_ref, b_ref, o_ref):
    ns, n_pad = cs_ref.shape[0], o_ref.shape[0] - cs_ref.shape[0] - cp_ref.shape[0]
    w = w_ref[...].astype(BF16)
    cp = jnp.concatenate([cp_ref[...], jnp.zeros((n_pad, D_MODEL), F32)], axis=0)
    o_ref[0:ns, :] = jnp.dot(_silu(cs_ref[...]).astype(BF16), w, preferred_element_type=F32) + b_ref[...]
    o_ref[ns:, :] = jnp.dot(_silu(cp).astype(BF16), w, preferred_element_type=F32) + b_ref[...]


def _modulation(c_sample, c_prompt, ada_w, ada_b):
    ns, n_p = c_sample.shape[0], c_prompt.shape[0]
    assert ns % SUBLANES == 0
    rows = ns + n_p + (-n_p) % SUBLANES
    return pl.pallas_call(
        _mod_kernel,
        out_shape=jax.ShapeDtypeStruct((rows, 3 * D_MODEL), F32),
        grid=(3 * D_MODEL // MOD_TN,),
        in_specs=[pl.BlockSpec((ns, D_MODEL), lambda j: (0, 0)),
                  pl.BlockSpec((n_p, D_MODEL), lambda j: (0, 0)),
                  pl.BlockSpec((D_MODEL, MOD_TN), lambda j: (0, j)),
                  pl.BlockSpec((1, MOD_TN), lambda j: (0, j))],
        out_specs=pl.BlockSpec((rows, MOD_TN), lambda j: (0, j)),
        compiler_params=_params("arbitrary"),
        name="modulation",
    )(c_sample, c_prompt, ada_w, ada_b)


NORM_ROWS = 128
NORM_TM = 2048


def _mod_rows(ref, rows, tm, rows_per_mod):
    if rows_per_mod == 1:
        return ref[rows, :]
    return ref[pl.ds((pl.program_id(0) * tm) // rows_per_mod, 1), :]


def _mod_spec(tm, rows_per_mod, mod_row0, col):
    if rows_per_mod == 1:
        assert mod_row0 % tm == 0
        return pl.BlockSpec((tm, D_MODEL), lambda i, *_: (mod_row0 // tm + i, col))
    assert mod_row0 % SUBLANES == 0
    return pl.BlockSpec((SUBLANES, D_MODEL), lambda i, *_: (mod_row0 // SUBLANES, col))


def _norm_mod(x, g, scale, shift):
    xn = x * lax.rsqrt(jnp.mean(x * x, axis=-1, keepdims=True) + EPS) * g
    return (xn * (1.0 + scale) + shift).astype(BF16)


def _norm_kernel(x_ref, shift_ref, scale_ref, g_ref, h_ref, *, rows_per_mod):
    tm = x_ref.shape[0]

    def body(r, carry):
        rows = pl.ds(pl.multiple_of(r * NORM_ROWS, NORM_ROWS), NORM_ROWS)
        h_ref[rows, :] = _norm_mod(x_ref[rows, :], g_ref[...], _mod_rows(scale_ref, rows, tm, rows_per_mod),
                                   _mod_rows(shift_ref, rows, tm, rows_per_mod))
        return carry
    lax.fori_loop(0, tm // NORM_ROWS, body, 0)


def _norm(x, mod, g_pre, *, tm, rows_per_mod, mod_row0):
    m = x.shape[0]
    assert rows_per_mod == 1 or m // rows_per_mod <= SUBLANES
    mod_spec = lambda col: _mod_spec(tm, rows_per_mod, mod_row0, col)
    return pl.pallas_call(
        functools.partial(_norm_kernel, rows_per_mod=rows_per_mod),
        out_shape=jax.ShapeDtypeStruct((m, D_MODEL), BF16),
        grid=(m // tm,),
        in_specs=[pl.BlockSpec((tm, D_MODEL), lambda i: (i, 0)), mod_spec(0), mod_spec(1),
                  pl.BlockSpec((1, D_MODEL), lambda i: (0, 0))],
        out_specs=pl.BlockSpec((tm, D_MODEL), lambda i: (i, 0)),
        compiler_params=_params("parallel"),
        name="norm",
    )(x, mod, mod, g_pre)


IN_TM = 1024
IN_TN = 2048
IN_DOT_COLS = 512


def _in_proj_kernel(h_ref, xs_ref, shift_s_ref, scale_s_ref, g_ref, wa_ref, wb_ref, w_in_hbm, w_mg_hbm,
                    z_ref, zs_ref, wa_bf_ref, wb_bf_ref, wbf0_ref, wbf1_ref, stage_ref, hs_ref, sem,
                    *, n_in_tiles, n_j, n_i):
    j, i = pl.program_id(0), pl.program_id(1)
    t = j * n_i + i
    total = n_j * n_i
    tn = wbf0_ref.shape[1]
    chunk = stage_ref.shape[1]

    def chunk_copy(w_hbm, col0, r, b):
        rows = pl.ds(pl.multiple_of(r * chunk, chunk), chunk)
        cols = pl.ds(col0 if isinstance(col0, int) else pl.multiple_of(col0, tn), tn)
        return pltpu.make_async_copy(w_hbm.at[rows, cols], stage_ref.at[b], sem.at[b])

    def start_chunk(g):
        g = lax.rem(jnp.asarray(g, jnp.int32), total)
        jt, r, b = lax.div(g, n_i), lax.rem(g, n_i), lax.rem(g, 2)

        @pl.when(jt < n_in_tiles)
        def _():
            chunk_copy(w_in_hbm, jt * tn, r, b).start()

        @pl.when(jt >= n_in_tiles)
        def _():
            chunk_copy(w_mg_hbm, (jt - n_in_tiles) * tn, r, b).start()

    def land_chunk(g, dst_ref):
        g = jnp.asarray(g, jnp.int32)
        r, b = lax.rem(g, n_i), lax.rem(g, 2)
        chunk_copy(w_in_hbm, 0, r, b).wait()
        dst_ref[pl.ds(pl.multiple_of(r * chunk, chunk), chunk), :] = stage_ref[b].astype(BF16)

    @pl.when(t == 0)
    def _():
        hs_ref[...] = _norm_mod(xs_ref[...], g_ref[...], scale_s_ref[...], shift_s_ref[...])
        start_chunk(0)

        def body(g, carry):
            start_chunk(g + 1)
            land_chunk(g, wbf0_ref)
            return carry
        lax.fori_loop(0, n_i, body, 0)

    @pl.when(t + 1 < total)
    def _():
        start_chunk(t + n_i + 1)

    def multiply(w_cur, w_nxt):
        land_chunk(t + n_i, w_nxt)
        wa_bf_ref[...] = wa_ref[...].astype(BF16)
        wb_bf_ref[...] = wb_ref[...].astype(BF16)
        for c0 in range(0, tn, IN_DOT_COLS):
            cols = slice(c0, c0 + IN_DOT_COLS)
            z_ref[:, cols] = jnp.dot(h_ref[...], w_cur[:, cols], preferred_element_type=F32)

        @pl.when(i == 0)
        def _():
            zs_ref[...] = jnp.dot(hs_ref[...], w_cur[...], preferred_element_type=F32)

    @pl.when(lax.rem(j, 2) == 0)
    def _():
        multiply(wbf0_ref, wbf1_ref)

    @pl.when(lax.rem(j, 2) == 1)
    def _():
        multiply(wbf1_ref, wbf0_ref)


def _in_proj(h, xs, mod, g_pre, w_in, w_mg, w_a, w_b):
    m, ms = h.shape[0], xs.shape[0]
    n_in, n_mg = w_in.shape[1] // IN_TN, w_mg.shape[1] // IN_TN
    n_i = m // IN_TM
    chunk = D_MODEL // n_i
    assert chunk * n_i == D_MODEL and chunk % 16 == 0 and (n_in + n_mg) * n_i >= CAST_STEPS
    const = lambda shape, col: pl.BlockSpec(shape, lambda j, i: (0, col), pipeline_mode=pl.Buffered(1))
    wa_in, wa_out, wa_shape = _cast_specs(w_a, lambda j, i: j * n_i + i)
    wb_in, wb_out, wb_shape = _cast_specs(w_b, lambda j, i: j * n_i + i)
    return pl.pallas_call(
        functools.partial(_in_proj_kernel, n_in_tiles=n_in, n_j=n_in + n_mg, n_i=n_i),
        out_shape=(jax.ShapeDtypeStruct((m, D_Z), F32), jax.ShapeDtypeStruct((ms, D_Z), F32), wa_shape, wb_shape),
        grid=(n_in + n_mg, n_i),
        in_specs=[pl.BlockSpec((IN_TM, D_MODEL), lambda j, i: (i, 0)),
                  const((ms, D_MODEL), 0), const((ms, D_MODEL), 0), const((ms, D_MODEL), 1),
                  const((1, D_MODEL), 0), wa_in, wb_in,
                  pl.BlockSpec(memory_space=pl.ANY), pl.BlockSpec(memory_space=pl.ANY)],
        out_specs=(pl.BlockSpec((IN_TM, IN_TN), lambda j, i: (i, j)),
                   pl.BlockSpec((ms, IN_TN), lambda j, i: (0, j)), wa_out, wb_out),
        scratch_shapes=[pltpu.VMEM((D_MODEL, IN_TN), BF16), pltpu.VMEM((D_MODEL, IN_TN), BF16),
                        pltpu.VMEM((2, chunk, IN_TN), F32),
                        pltpu.VMEM((ms, D_MODEL), BF16),
                        pltpu.SemaphoreType.DMA((2,))],
        compiler_params=_params("arbitrary", "arbitrary"),
        name="in_proj",
    )(h, xs, mod, mod, g_pre, w_a, w_b, w_in, w_mg)


def _rotate(x, cos, sin_signed):
    return x * cos + pltpu.roll(x, HEAD_DK // 2, 1) * sin_signed


def _group_norm_gate(o, gn, bg):
    mu = jnp.mean(o, axis=-1, keepdims=True)
    d = o - mu
    var = jnp.mean(d * d, axis=-1, keepdims=True)
    return d * lax.rsqrt(var + EPS) * gn * _silu(bg)


def _pool_project(pooled, pw, ps, ag):
    mixed = jnp.dot(pooled.astype(BF16), pw.astype(BF16), preferred_element_type=F32)
    return mixed * ps * _silu(ag)


PROJ_SPLIT = 8
PROJ_COLS = D_MODEL // PROJ_SPLIT


def _merge_chunk(y_a, y_b, j, gp_ref, bm_ref, wa_ref, wb_ref):
    lo, hi = j * PROJ_COLS, (j + 1) * PROJ_COLS
    ya = jnp.dot(y_a, wa_ref[:, lo:hi], preferred_element_type=F32)
    yb = jnp.dot(y_b, wb_ref[:, lo:hi], preferred_element_type=F32)
    g_a = jax.nn.sigmoid(gp_ref[:, lo:hi] + bm_ref[:, lo:hi])
    g_b = jax.nn.sigmoid(gp_ref[:, D_MODEL + lo:D_MODEL + hi] + bm_ref[:, D_MODEL + lo:D_MODEL + hi])
    return (g_a * ya + g_b * yb).astype(BF16)


def _out_chunk(merged, j, wo_ref):
    return jnp.dot(merged, wo_ref[:, j * PROJ_COLS:(j + 1) * PROJ_COLS], preferred_element_type=F32)


def _post_norm(o, gpost_ref):
    return o * lax.rsqrt(jnp.mean(o * o, axis=-1, keepdims=True) + EPS) * gpost_ref[...]


def _merge_project(y_a, y_b, gp_ref, bm_ref, wa_ref, wb_ref, wo_ref, gpost_ref):
    merged = jnp.concatenate([_merge_chunk(y_a, y_b, j, gp_ref, bm_ref, wa_ref, wb_ref)
                              for j in range(PROJ_SPLIT)], axis=1)
    o = jnp.concatenate([_out_chunk(merged, j, wo_ref) for j in range(PROJ_SPLIT)], axis=1)
    return _post_norm(o, gpost_ref)


MIX_ROWS = 2 * CHUNK
MIX_HEADS = 4
Z_AX, Z_AG, Z_Q, Z_K, Z_V, Z_BG = 0, D_POOL, 2 * D_POOL, 2 * D_POOL + D_QK, 2 * D_POOL + 2 * D_QK, D_IN - D_V


def _prompt_kernel(cdec_ref, z_ref, rot_ref, dmask_ref, qdec_ref, kdec_ref, pw_ref, ps_ref, gn_ref,
                   gp_ref, bm_ref, wa_ref, wb_ref, wo_ref,
                   m_ref, npool_ref, nret_ref, wo_bf_ref, ya0_ref, yb0_ref, ya1_ref, yb1_ref, ext_ref, *, tiles_per_seq):
    s = pl.program_id(0)
    n_tiles = pl.num_programs(0) - 1
    live = s < n_tiles
    c = lax.rem(jnp.minimum(s, n_tiles - 1), tiles_per_seq)
    rows = z_ref.shape[0]

    @pl.when(s == 0)
    def _():
        for ref in (ya0_ref, yb0_ref, ya1_ref, yb1_ref):
            ref[...] = jnp.zeros(ref.shape, BF16)

    @pl.when(c == 0)
    def _():
        ext_ref[0:HALO, :] = jnp.zeros((HALO, D_POOL), F32)
        nret_ref[...] = jnp.zeros(nret_ref.shape, F32)

    def step(ya_rd, yb_rd, ya_wr, yb_wr):
        y_a, y_b = ya_rd[...], yb_rd[...]
        wo_bf_ref[...] = wo_ref[...].astype(BF16)
        xa = z_ref[:, Z_AX:Z_AX + D_POOL]
        ext_ref[HALO:HALO + rows, :] = xa
        pos = c * rows + lax.broadcasted_iota(jnp.int32, (rows, 1), 0)
        nt = (((1,), (1,)), ((), ()))
        tn = (((0,), (0,)), ((), ()))

        def merge_piece(j):
            m_ref[:, j * PROJ_COLS:(j + 1) * PROJ_COLS] = _merge_chunk(y_a, y_b, j, gp_ref, bm_ref, wa_ref, wb_ref)

        def pool_group(g):
            w = POOL_WINDOWS[g]
            lo, hi = g * POOL_GROUP_DIM, (g + 1) * POOL_GROUP_DIM
            acc = ext_ref[:, lo:hi]
            span = 1
            while span < w:
                acc = acc + pltpu.roll(acc, span, 0)
                span *= 2
            inv_cnt = 1.0 / jnp.minimum(pos + 1, w).astype(F32)
            pooled = acc[HALO:, :] * inv_cnt - xa[:, lo:hi]
            ya = _pool_project(pooled, pw_ref[g], ps_ref[:, lo:hi], z_ref[:, Z_AG + lo:Z_AG + hi])
            ya_wr[:, lo:hi] = ya.astype(BF16)

        per_pool = PROJ_SPLIT // len(POOL_WINDOWS)
        assert per_pool * len(POOL_WINDOWS) == PROJ_SPLIT
        fillers = [f for g in range(len(POOL_WINDOWS))
                   for f in [functools.partial(merge_piece, g * per_pool + j) for j in range(per_pool)]
                   + [functools.partial(pool_group, g)]]
        n_sub = rows // CHUNK
        n_slots = 2 * (N_HEADS // MIX_HEADS) * n_sub
        slot = [0]

        def fill():
            lo, hi = (slot[0] * len(fillers)) // n_slots, ((slot[0] + 1) * len(fillers)) // n_slots
            slot[0] += 1
            for f in fillers[lo:hi]:
                f()

        for h0 in range(0, N_HEADS, MIX_HEADS):
            heads = range(h0, h0 + MIX_HEADS)
            s_cur = {h: nret_ref[h] for h in heads}
            for ci in range(n_sub):
                rs = slice(ci * CHUNK, (ci + 1) * CHUNK)
                cos = rot_ref[rs, :HEAD_DK]
                sin = rot_ref[rs, HEAD_DK:]
                q = {h: _rotate(z_ref[rs, Z_Q + h * HEAD_DK:Z_Q + (h + 1) * HEAD_DK], cos, sin) for h in heads}
                k = {h: _rotate(z_ref[rs, Z_K + h * HEAD_DK:Z_K + (h + 1) * HEAD_DK], cos, sin) * K_SCALE
                     for h in heads}
                v = {h: z_ref[rs, Z_V + h * HEAD_DV:Z_V + (h + 1) * HEAD_DV].astype(BF16) for h in heads}
                fill()
                scores = {h: lax.dot_general(q[h].astype(BF16), k[h].astype(BF16), nt, preferred_element_type=F32)
                          for h in heads}
                kv = {h: lax.dot_general((k[h] * kdec_ref[h]).astype(BF16), v[h], tn, preferred_element_type=F32)
                      for h in heads}
                lhs = {h: jnp.concatenate([(scores[h] * dmask_ref[h]).astype(BF16),
                                           (q[h] * qdec_ref[h]).astype(BF16)], axis=1) for h in heads}
                fill()
                for h in heads:
                    rhs = jnp.concatenate([v[h], s_cur[h].astype(BF16)], axis=0)
                    o = jnp.dot(lhs[h], rhs, preferred_element_type=F32)
                    s_cur[h] = cdec_ref[h] * s_cur[h] + kv[h]
                    vs = slice(h * HEAD_DV, (h + 1) * HEAD_DV)
                    bg = z_ref[rs, Z_BG + h * HEAD_DV:Z_BG + (h + 1) * HEAD_DV]
                    yb_wr[rs, vs] = _group_norm_gate(o, gn_ref[:, vs], bg).astype(BF16)
            for h in heads:
                nret_ref[h] = jnp.where(live, s_cur[h], nret_ref[h])
        assert slot[0] == n_slots

    @pl.when(lax.rem(s, 2) == 0)
    def _():
        step(ya1_ref, yb1_ref, ya0_ref, yb0_ref)

    @pl.when(lax.rem(s, 2) == 1)
    def _():
        step(ya0_ref, yb0_ref, ya1_ref, yb1_ref)

    @pl.when(c == tiles_per_seq - 1)
    def _():
        npool_ref[...] = ext_ref[HALO + rows - POOL_BUF:HALO + rows, :]

    ext_ref[0:HALO, :] = ext_ref[rows:rows + HALO, :]


def _prompt_mix_merge(z, batch, seq, tables, pool_w, pool_scale, gn_g, b_merge, w_a, w_b, w_o):
    rot, dmask, qdec, kdec, cdec = tables
    tps = seq // MIX_ROWS
    n_tiles = batch * tps
    assert n_tiles >= CAST_STEPS
    cur = lambda s: jnp.minimum(s, n_tiles - 1)
    prev = lambda s: jnp.maximum(s - 1, 0)
    m = batch * seq
    wo_in, wo_out, wo_shape = _cast_specs(w_o, lambda s: s)
    return pl.pallas_call(
        functools.partial(_prompt_kernel, tiles_per_seq=tps),
        out_shape=(jax.ShapeDtypeStruct((m, D_MODEL), BF16),
                   jax.ShapeDtypeStruct((batch, POOL_BUF, D_POOL), F32),
                   jax.ShapeDtypeStruct((batch, N_HEADS, HEAD_DK, HEAD_DV), F32), wo_shape),
        grid=(n_tiles + 1,),
        in_specs=[pl.BlockSpec(memory_space=pltpu.SMEM),
                  pl.BlockSpec((MIX_ROWS, D_IN), lambda s: (cur(s), 0)),
                  pl.BlockSpec((MIX_ROWS, 2 * HEAD_DK), lambda s: (cur(s) % tps, 0)),
                  _resident(dmask.shape), _resident(qdec.shape), _resident(kdec.shape),
                  _resident(pool_w.shape), _resident(pool_scale.shape), _resident(gn_g.shape),
                  pl.BlockSpec((MIX_ROWS, 2 * D_MODEL), lambda s: (prev(s), D_IN // (2 * D_MODEL))),
                  _resident(b_merge.shape), _resident(w_a.shape), _resident(w_b.shape), wo_in],
        out_specs=(pl.BlockSpec((MIX_ROWS, D_MODEL), lambda s: (prev(s), 0)),
                   pl.BlockSpec((None, POOL_BUF, D_POOL), lambda s: (cur(s) // tps, 0, 0)),
                   pl.BlockSpec((None, N_HEADS, HEAD_DK, HEAD_DV), lambda s: (cur(s) // tps, 0, 0, 0)), wo_out),
        scratch_shapes=[pltpu.VMEM((MIX_ROWS, D_POOL), BF16), pltpu.VMEM((MIX_ROWS, D_V), BF16),
                        pltpu.VMEM((MIX_ROWS, D_POOL), BF16), pltpu.VMEM((MIX_ROWS, D_V), BF16),
                        pltpu.VMEM((HALO + MIX_ROWS, D_POOL), F32)],
        compiler_params=_params("arbitrary"),
        name="prompt_mix_merge",
    )(cdec, z, rot, dmask, qdec, kdec, pool_w, pool_scale, gn_g, z, b_merge, w_a, w_b, w_o)


SAMPLE_TILE = 8
SAMPLE_HEADS = 4
TAIL_ROWS = 256


def _tail_kernel(dec_ref, m_ref, x_ref, gate_ref, wo_ref, gpost_ref,
                 ax_ref, ag_ref, q_ref, k_ref, v_ref, bg_ref, cos_ref, sin_ref,
                 pw_ref, ps_ref, gn_ref, spool_ref, sret_ref,
                 y_ref, ya_ref, yb_ref, npool_ref, nret_ref, o_ref, *, inv_cnt, rows_per_mod):
    bt = ax_ref.shape[0]
    hg = pl.program_id(1)

    o = _post_norm(jnp.dot(m_ref[...], wo_ref[...], preferred_element_type=F32), gpost_ref)
    tile = pl.program_id(0) * pl.num_programs(1) + hg
    gate = gate_ref[pl.ds((tile * x_ref.shape[0]) // rows_per_mod, 1), :]
    y_ref[...] = x_ref[...] + gate * o

    xa = ax_ref[...]
    run = xa
    wins = {1: xa}
    for j in range(1, POOL_BUF + 1):
        run = run + spool_ref[POOL_BUF - j]
        wins[j + 1] = run
    for g, w in enumerate(POOL_WINDOWS):
        lo, hi = g * POOL_GROUP_DIM, (g + 1) * POOL_GROUP_DIM
        pooled = wins[w][:, lo:hi] * inv_cnt[g] - xa[:, lo:hi]
        ya = _pool_project(pooled, pw_ref[g], ps_ref[:, lo:hi], ag_ref[:, lo:hi])
        ya_ref[:, lo:hi] = ya.astype(BF16)
    for j in range(POOL_BUF - 1):
        npool_ref[j] = spool_ref[j + 1]
    npool_ref[POOL_BUF - 1] = xa

    cos = cos_ref[...]
    sin = sin_ref[...]
    for hl in range(SAMPLE_HEADS):
        h = hg * SAMPLE_HEADS + hl
        qs = slice(hl * HEAD_DK, (hl + 1) * HEAD_DK)
        vs = slice(hl * HEAD_DV, (hl + 1) * HEAD_DV)
        q = _rotate(q_ref[:, qs], cos, sin)
        k = _rotate(k_ref[:, qs], cos, sin) * K_SCALE
        v = v_ref[:, vs]
        score = jnp.sum(q * k, axis=1, keepdims=True) * dec_ref[0, h]
        q_cols = jnp.transpose(q * dec_ref[1, h])
        k_cols = jnp.transpose(k * dec_ref[2, h])
        for r in range(bt):
            s_old = sret_ref[r, hl]
            v_row = v[r:r + 1, :]
            o_row = score[r:r + 1, :] * v_row + jnp.sum(q_cols[:, r:r + 1] * s_old, axis=0, keepdims=True)
            nret_ref[r, hl] = dec_ref[3, h] * s_old + k_cols[:, r:r + 1] * v_row
            o_ref[r:r + 1, vs] = o_row
    for hl in range(SAMPLE_HEADS):
        vs = slice(hl * HEAD_DV, (hl + 1) * HEAD_DV)
        yb_ref[:, vs] = _group_norm_gate(o_ref[:, vs], gn_ref[:, vs], bg_ref[:, vs]).astype(BF16)


def _tail(merged, x, mod, w_o, g_post, zs, state_pool, state_ret, tables, inv_cnt, pool_w, pool_scale, gn_g,
          *, rows_per_mod, mod_row0):
    cos, sin, dec = tables
    m, batch = x.shape[0], zs.shape[0]
    bt, hs = SAMPLE_TILE, SAMPLE_HEADS
    n_hg = N_HEADS // hs
    assert m == (batch // bt) * n_hg * TAIL_ROWS and mod_row0 % SUBLANES == 0 and m // rows_per_mod <= SUBLANES
    qw, vw = hs * HEAD_DK, hs * HEAD_DV
    rows = lambda i, g: (i * n_hg + g, 0)
    zcol = lambda width, col0, per_group: pl.BlockSpec(
        (bt, width), lambda i, g: (i, col0 // width + (g if per_group else 0)))
    return pl.pallas_call(
        functools.partial(_tail_kernel, inv_cnt=inv_cnt, rows_per_mod=rows_per_mod),
        out_shape=(jax.ShapeDtypeStruct((m, D_MODEL), F32),
                   jax.ShapeDtypeStruct((batch, D_POOL), BF16),
                   jax.ShapeDtypeStruct((batch, D_V), BF16),
                   jax.ShapeDtypeStruct(state_pool.shape, state_pool.dtype),
                   jax.ShapeDtypeStruct(state_ret.shape, state_ret.dtype)),
        grid=(batch // bt, n_hg),
        in_specs=[pl.BlockSpec(memory_space=pltpu.SMEM),
                  pl.BlockSpec((TAIL_ROWS, D_MODEL), rows), pl.BlockSpec((TAIL_ROWS, D_MODEL), rows),
                  pl.BlockSpec((SUBLANES, D_MODEL), lambda i, g: (mod_row0 // SUBLANES, 2)),
                  _resident(w_o.shape), _resident(g_post.shape),
                  zcol(D_POOL, Z_AX, False), zcol(D_POOL, Z_AG, False),
                  zcol(qw, Z_Q, True), zcol(qw, Z_K, True), zcol(vw, Z_V, True), zcol(vw, Z_BG, True),
                  _resident(cos.shape), _resident(sin.shape),
                  _resident(pool_w.shape), _resident(pool_scale.shape),
                  pl.BlockSpec((1, vw), lambda i, g: (0, g)),
                  pl.BlockSpec((POOL_BUF, bt, D_POOL), lambda i, g: (0, i, 0)),
                  pl.BlockSpec((bt, hs, HEAD_DK, HEAD_DV), lambda i, g: (i, g, 0, 0))],
        out_specs=(pl.BlockSpec((TAIL_ROWS, D_MODEL), rows),
                   pl.BlockSpec((bt, D_POOL), lambda i, g: (i, 0)),
                   pl.BlockSpec((bt, vw), lambda i, g: (i, g)),
                   pl.BlockSpec((POOL_BUF, bt, D_POOL), lambda i, g: (0, i, 0)),
                   pl.BlockSpec((bt, hs, HEAD_DK, HEAD_DV), lambda i, g: (i, g, 0, 0))),
        scratch_shapes=[pltpu.VMEM((bt, vw), F32)],
        compiler_params=_params("arbitrary", "arbitrary"),
        name="tail",
    )(dec, merged, x, mod, w_o, g_post, zs, zs, zs, zs, zs, zs, cos, sin, pool_w, pool_scale, gn_g,
      state_pool, state_ret)


def _out_proj_kernel(ya_ref, yb_ref, gp_ref, x_ref, gate_ref, bm_ref, wa_ref, wb_ref, wo_ref, gpost_ref, y_ref,
                     *, rows_per_mod):
    o = _merge_project(ya_ref[...], yb_ref[...], gp_ref, bm_ref, wa_ref, wb_ref, wo_ref, gpost_ref)
    gate = _mod_rows(gate_ref, slice(None), x_ref.shape[0], rows_per_mod)
    y_ref[...] = x_ref[...] + gate * o


def _out_proj(ya, yb, z, x, mod, b_merge, w_a, w_b, w_o, g_post, *, tm, rows_per_mod, mod_row0):
    m = x.shape[0]
    return pl.pallas_call(
        functools.partial(_out_proj_kernel, rows_per_mod=rows_per_mod),
        out_shape=jax.ShapeDtypeStruct((m, D_MODEL), F32),
        grid=(m // tm,),
        in_specs=[pl.BlockSpec((tm, D_POOL), lambda i: (i, 0)),
                  pl.BlockSpec((tm, D_V), lambda i: (i, 0)),
                  pl.BlockSpec((tm, 2 * D_MODEL), lambda i: (i, D_IN // (2 * D_MODEL))),
                  pl.BlockSpec((tm, D_MODEL), lambda i: (i, 0)),
                  _mod_spec(tm, rows_per_mod, mod_row0, 2),
                  _resident(b_merge.shape), _resident(w_a.shape), _resident(w_b.shape), _resident(w_o.shape),
                  _resident(g_post.shape)],
        out_specs=pl.BlockSpec((tm, D_MODEL), lambda i: (i, 0)),
        compiler_params=_params("parallel"),
        name="out_proj",
    )(ya, yb, z, x, mod, b_merge, w_a, w_b, w_o, g_post)


def _rotary_tables(start, length):
    half = HEAD_DK // 2
    inv = ROPE_BASE ** (-np.arange(half, dtype=np.float64) / half)
    ang = (start + np.arange(length, dtype=np.float64))[:, None] * inv[None, :]
    cos, sin = np.cos(ang), np.sin(ang)
    return (np.concatenate([cos, cos], axis=-1).astype(np.float32),
            np.concatenate([-sin, sin], axis=-1).astype(np.float32))


def _decay_tables(c):
    lg = np.log1p(-np.power(2.0, -5.0 - np.arange(N_HEADS, dtype=np.float64)))
    idx = np.arange(c, dtype=np.float64)
    diff = idx[:, None] - idx[None, :]
    dmask = np.where(diff[None] >= 0, np.exp(np.maximum(diff, 0.0)[None] * lg[:, None, None]), 0.0)
    q_dec = np.exp((idx + 1.0)[None, :] * lg[:, None])
    k_dec = np.exp((c - 1.0 - idx)[None, :] * lg[:, None])
    chunk_dec = np.exp(c * lg)
    return tuple(a.astype(np.float32) for a in (dmask, q_dec, k_dec, chunk_dec))


def _layer(xp, xs, c_prompt, c_sample, state_pool, state_ret, ada_w, ada_b, g_pre, g_post, w_in, pool_w, pool_scale, gn_g,
           w_a_proj, w_b_proj, w_merge, b_merge, w_out):
    batch, seq, _ = xp.shape
    dec_batch, dec_seq, _ = xs.shape
    assert dec_seq == 1 and seq % CHUNK == 0

    row = lambda v: v.reshape(1, -1)
    g_pre, g_post, pool_scale, gn_g, b_merge = map(row, (g_pre, g_post, pool_scale, gn_g, b_merge))

    mod = _modulation(c_sample, c_prompt, ada_w, row(ada_b))

    x2 = xp.reshape(batch * seq, D_MODEL)
    xs2 = xs.reshape(dec_batch, D_MODEL)
    h = _norm(x2, mod, g_pre, tm=NORM_TM, rows_per_mod=seq, mod_row0=dec_batch)
    z, zs, w_a, w_b = _in_proj(h, xs2, mod, g_pre, w_in, w_merge, w_a_proj, w_b_proj)

    cos, sin = _rotary_tables(0, seq)
    dmask, q_dec, k_dec, chunk_dec = _decay_tables(CHUNK)
    wide = lambda d: np.ascontiguousarray(np.broadcast_to(d[:, :, None], (N_HEADS, CHUNK, HEAD_DK)))
    rot = np.concatenate([cos, sin], axis=1)
    merged, pool_p, ret_p, w_o = _prompt_mix_merge(z, batch, seq, (rot, dmask, wide(q_dec), wide(k_dec), chunk_dec),
                                                   pool_w, pool_scale, gn_g, b_merge, w_a, w_b, w_out)

    cos_s, sin_s = _rotary_tables(PAST_LEN, 1)
    dmask_s, q_dec_s, k_dec_s, chunk_dec_s = _decay_tables(1)
    dec_s = np.stack([dmask_s[:, 0, 0], q_dec_s[:, 0], k_dec_s[:, 0], chunk_dec_s])
    inv_cnt = tuple(1.0 / min(PAST_LEN + 1, w) for w in POOL_WINDOWS)
    yp, ya_s, yb_s, pool_s, ret_s = _tail(merged, x2, mod, w_o, g_post, zs, jnp.transpose(state_pool, (1, 0, 2)),
                                          state_ret, (cos_s, sin_s, dec_s), inv_cnt, pool_w, pool_scale, gn_g,
                                          rows_per_mod=seq, mod_row0=dec_batch)
    pool_s = jnp.transpose(pool_s, (1, 0, 2))
    ys = _out_proj(ya_s, yb_s, zs, xs2, mod, b_merge, w_a, w_b, w_o, g_post, tm=dec_batch, rows_per_mod=1, mod_row0=0)

    return yp.reshape(xp.shape), ys.reshape(xs.shape), pool_p, ret_p, pool_s, ret_s


def kernel(x_prompt, x_sample, state_pool, state_ret, c_prompt, c_sample, ada_w, ada_b, g_pre, g_post,
           w_in, pool_w, pool_scale, gn_g, w_a_proj, w_b_proj, w_merge, b_merge, w_out):
    depth = ada_w.shape[0]
    xp, xs = x_prompt, x_sample
    pool_p, ret_p, pool_s, ret_s = [], [], [], []
    for l in range(depth):
        xp, xs, bp, sp, bs, ss = _layer(
            xp, xs, c_prompt, c_sample, state_pool[l], state_ret[l], ada_w[l], ada_b[l], g_pre[l], g_post[l], w_in[l],
            pool_w[l], pool_scale[l], gn_g[l].reshape(-1), w_a_proj[l], w_b_proj[l], w_merge[l], b_merge[l],
            w_out[l])
        pool_p.append(bp)
        ret_p.append(sp)
        pool_s.append(bs)
        ret_s.append(ss)
    return (xp, xs, jnp.stack(pool_p), jnp.stack(ret_p), jnp.stack(pool_s), jnp.stack(ret_s))
```

```python
import functools

import jax
import jax.numpy as jnp
import numpy as np
from jax import lax
from jax.experimental import pallas as pl
from jax.experimental.pallas import tpu as pltpu

F32 = jnp.float32
BF16 = jnp.bfloat16

D_MODEL = 2048
PAST_LEN = 16384
D_POOL = D_MODEL // 2
POOL_WINDOWS = (2, 4, 8, 16)
POOL_GROUP_DIM = D_POOL // len(POOL_WINDOWS)
POOL_BUF = max(POOL_WINDOWS) - 1
N_HEADS = 8
HEAD_DK = D_MODEL // 16
HEAD_DV = D_MODEL // 8
D_QK = N_HEADS * HEAD_DK
D_V = N_HEADS * HEAD_DV
CHUNK = 128
ROPE_BASE = 10000.0
EPS = 1e-6
D_IN = 2 * D_POOL + 2 * D_QK + 2 * D_V
D_Z = D_IN + 2 * D_MODEL
K_SCALE = HEAD_DK ** -0.5

VMEM_LIMIT_BYTES = 56 * 1024 * 1024
SUBLANES = 8
HALO = 16
CAST_STEPS = 32


def _params(*semantics):
    return pltpu.CompilerParams(dimension_semantics=semantics, vmem_limit_bytes=VMEM_LIMIT_BYTES)


def _cast_specs(w, step_of):
    rows = w.shape[0] // CAST_STEPS
    assert rows * CAST_STEPS == w.shape[0] and rows % (2 * SUBLANES) == 0
    spec = pl.BlockSpec((rows, w.shape[1]), lambda *idx: (jnp.minimum(step_of(*idx), CAST_STEPS - 1), 0))
    return spec, spec, jax.ShapeDtypeStruct(w.shape, BF16)


def _resident(shape):
    return pl.BlockSpec(shape, lambda *_: (0,) * len(shape), pipeline_mode=pl.Buffered(1))


def _silu(x):
    return x * jax.nn.sigmoid(x)


MOD_TN = 1536


def _mod_kernel(cs_ref, cp_ref, w_ref, b_ref, o_ref):
    ns, n_pad = cs_ref.shape[0], o_ref.shape[0] - cs_ref.shape[0] - cp_ref.shape[0]
    w = w_ref[...].astype(BF16)
    cp = jnp.concatenate([cp_ref[...], jnp.zeros((n_pad, D_MODEL), F32)], axis=0)
    o_ref[0:ns, :] = jnp.dot(_silu(cs_ref[...]).astype(BF16), w, preferred_element_type=F32) + b_ref[...]
    o_ref[ns:, :] = jnp.dot(_silu(cp).astype(BF16), w, preferred_element_type=F32) + b_ref[...]


def _modulation(c_sample, c_prompt, ada_w, ada_b):
    ns, n_p = c_sample.shape[0], c_prompt.shape[0]
    assert ns % SUBLANES == 0
    rows = ns + n_p + (-n_p) % SUBLANES
    return pl.pallas_call(
        _mod_kernel,
        out_shape=jax.ShapeDtypeStruct((rows, 3 * D_MODEL), F32),
        grid=(3 * D_MODEL // MOD_TN,),
        in_specs=[pl.BlockSpec((ns, D_MODEL), lambda j: (0, 0)),
                  pl.BlockSpec((n_p, D_MODEL), lambda j: (0, 0)),
                  pl.BlockSpec((D_MODEL, MOD_TN), lambda j: (0, j)),
                  pl.BlockSpec((1, MOD_TN), lambda j: (0, j))],
        out_specs=pl.BlockSpec((rows, MOD_TN), lambda j: (0, j)),
        compiler_params=_params("arbitrary"),
        name="modulation",
    )(c_sample, c_prompt, ada_w, ada_b)


NORM_ROWS = 128
NORM_TM = 2048


def _mod_rows(ref, rows, tm, rows_per_mod):
    if rows_per_mod == 1:
        return ref[rows, :]
    return ref[pl.ds((pl.program_id(0) * tm) // rows_per_mod, 1), :]


def _mod_spec(tm, rows_per_mod, mod_row0, col):
    if rows_per_mod == 1:
        assert mod_row0 % tm == 0
        return pl.BlockSpec((tm, D_MODEL), lambda i, *_: (mod_row0 // tm + i, col))
    assert mod_row0 % SUBLANES == 0
    return pl.BlockSpec((SUBLANES, D_MODEL), lambda i, *_: (mod_row0 // SUBLANES, col))


def _norm_mod(x, g, scale, shift):
    xn = x * lax.rsqrt(jnp.mean(x * x, axis=-1, keepdims=True) + EPS) * g
    return (xn * (1.0 + scale) + shift).astype(BF16)


def _norm_kernel(x_ref, shift_ref, scale_ref, g_ref, h_ref, *, rows_per_mod):
    tm = x_ref.shape[0]

    def body(r, carry):
        rows = pl.ds(pl.multiple_of(r * NORM_ROWS, NORM_ROWS), NORM_ROWS)
        h_ref[rows, :] = _norm_mod(x_ref[rows, :], g_ref[...], _mod_rows(scale_ref, rows, tm, rows_per_mod),
                                   _mod_rows(shift_ref, rows, tm, rows_per_mod))
        return carry
    lax.fori_loop(0, tm // NORM_ROWS, body, 0)


def _norm(x, mod, g_pre, *, tm, rows_per_mod, mod_row0):
    m = x.shape[0]
    assert rows_per_mod == 1 or m // rows_per_mod <= SUBLANES
    mod_spec = lambda col: _mod_spec(tm, rows_per_mod, mod_row0, col)
    return pl.pallas_call(
        functools.partial(_norm_kernel, rows_per_mod=rows_per_mod),
        out_shape=jax.ShapeDtypeStruct((m, D_MODEL), BF16),
        grid=(m // tm,),
        in_specs=[pl.BlockSpec((tm, D_MODEL), lambda i: (i, 0)), mod_spec(0), mod_spec(1),
                  pl.BlockSpec((1, D_MODEL), lambda i: (0, 0))],
        out_specs=pl.BlockSpec((tm, D_MODEL), lambda i: (i, 0)),
        compiler_params=_params("parallel"),
        name="norm",
    )(x, mod, mod, g_pre)


IN_TM = 1024
IN_TN = 2048
IN_DOT_COLS = 512


def _in_proj_kernel(h_ref, xs_ref, shift_s_ref, scale_s_ref, g_ref, wa_ref, wb_ref, w_in_hbm, w_mg_hbm,
                    z_ref, zs_ref, wa_bf_ref, wb_bf_ref, wbf0_ref, wbf1_ref, stage_ref, hs_ref, sem,
                    *, n_in_tiles, n_j, n_i):
    j, i = pl.program_id(0), pl.program_id(1)
    t = j * n_i + i
    total = n_j * n_i
    tn = wbf0_ref.shape[1]
    chunk = stage_ref.shape[1]

    def chunk_copy(w_hbm, col0, r, b):
        rows = pl.ds(pl.multiple_of(r * chunk, chunk), chunk)
        cols = pl.ds(col0 if isinstance(col0, int) else pl.multiple_of(col0, tn), tn)
        return pltpu.make_async_copy(w_hbm.at[rows, cols], stage_ref.at[b], sem.at[b])

    def start_chunk(g):
        g = lax.rem(jnp.asarray(g, jnp.int32), total)
        jt, r, b = lax.div(g, n_i), lax.rem(g, n_i), lax.rem(g, 2)

        @pl.when(jt < n_in_tiles)
        def _():
            chunk_copy(w_in_hbm, jt * tn, r, b).start()

        @pl.when(jt >= n_in_tiles)
        def _():
            chunk_copy(w_mg_hbm, (jt - n_in_tiles) * tn, r, b).start()

    def land_chunk(g, dst_ref):
        g = jnp.asarray(g, jnp.int32)
        r, b = lax.rem(g, n_i), lax.rem(g, 2)
        chunk_copy(w_in_hbm, 0, r, b).wait()
        dst_ref[pl.ds(pl.multiple_of(r * chunk, chunk), chunk), :] = stage_ref[b].astype(BF16)

    @pl.when(t == 0)
    def _():
        hs_ref[...] = _norm_mod(xs_ref[...], g_ref[...], scale_s_ref[...], shift_s_ref[...])
        start_chunk(0)

        def body(g, carry):
            start_chunk(g + 1)
            land_chunk(g, wbf0_ref)
            return carry
        lax.fori_loop(0, n_i, body, 0)

    @pl.when(t + 1 < total)
    def _():
        start_chunk(t + n_i + 1)

    def multiply(w_cur, w_nxt):
        land_chunk(t + n_i, w_nxt)
        wa_bf_ref[...] = wa_ref[...].astype(BF16)
        wb_bf_ref[...] = wb_ref[...].astype(BF16)
        for c0 in range(0, tn, IN_DOT_COLS):
            cols = slice(c0, c0 + IN_DOT_COLS)
            z_ref[:, cols] = jnp.dot(h_ref[...], w_cur[:, cols], preferred_element_type=F32)

        @pl.when(i == 0)
        def _():
            zs_ref[...] = jnp.dot(hs_ref[...], w_cur[...], preferred_element_type=F32)

    @pl.when(lax.rem(j, 2) == 0)
    def _():
        multiply(wbf0_ref, wbf1_ref)

    @pl.when(lax.rem(j, 2) == 1)
    def _():
        multiply(wbf1_ref, wbf0_ref)


def _in_proj(h, xs, mod, g_pre, w_in, w_mg, w_a, w_b):
    m, ms = h.shape[0], xs.shape[0]
    n_in, n_mg = w_in.shape[1] // IN_TN, w_mg.shape[1] // IN_TN
    n_i = m // IN_TM
    chunk = D_MODEL // n_i
    assert chunk * n_i == D_MODEL and chunk % 16 == 0 and (n_in + n_mg) * n_i >= CAST_STEPS
    const = lambda shape, col: pl.BlockSpec(shape, lambda j, i: (0, col), pipeline_mode=pl.Buffered(1))
    wa_in, wa_out, wa_shape = _cast_specs(w_a, lambda j, i: j * n_i + i)
    wb_in, wb_out, wb_shape = _cast_specs(w_b, lambda j, i: j * n_i + i)
    return pl.pallas_call(
        functools.partial(_in_proj_kernel, n_in_tiles=n_in, n_j=n_in + n_mg, n_i=n_i),
        out_shape=(jax.ShapeDtypeStruct((m, D_Z), F32), jax.ShapeDtypeStruct((ms, D_Z), F32), wa_shape, wb_shape),
        grid=(n_in + n_mg, n_i),
        in_specs=[pl.BlockSpec((IN_TM, D_MODEL), lambda j, i: (i, 0)),
                  const((ms, D_MODEL), 0), const((ms, D_MODEL), 0), const((ms, D_MODEL), 1),
                  const((1, D_MODEL), 0), wa_in, wb_in,
                  pl.BlockSpec(memory_space=pl.ANY), pl.BlockSpec(memory_space=pl.ANY)],
        out_specs=(pl.BlockSpec((IN_TM, IN_TN), lambda j, i: (i, j)),
                   pl.BlockSpec((ms, IN_TN), lambda j, i: (0, j)), wa_out, wb_out),
        scratch_shapes=[pltpu.VMEM((D_MODEL, IN_TN), BF16), pltpu.VMEM((D_MODEL, IN_TN), BF16),
                        pltpu.VMEM((2, chunk, IN_TN), F32),
                        pltpu.VMEM((ms, D_MODEL), BF16),
                        pltpu.SemaphoreType.DMA((2,))],
        compiler_params=_params("arbitrary", "arbitrary"),
        name="in_proj",
    )(h, xs, mod, mod, g_pre, w_a, w_b, w_in, w_mg)


def _rotate(x, cos, sin_signed):
    return x * cos + pltpu.roll(x, HEAD_DK // 2, 1) * sin_signed


def _group_norm_gate(o, gn, bg):
    mu = jnp.mean(o, axis=-1, keepdims=True)
    d = o - mu
    var = jnp.mean(d * d, axis=-1, keepdims=True)
    return d * lax.rsqrt(var + EPS) * gn * _silu(bg)


def _pool_project(pooled, pw, ps, ag):
    mixed = jnp.dot(pooled.astype(BF16), pw.astype(BF16), preferred_element_type=F32)
    return mixed * ps * _silu(ag)


PROJ_SPLIT = 8
PROJ_COLS = D_MODEL // PROJ_SPLIT


def _merge_chunk(y_a, y_b, j, gp_ref, bm_ref, wa_ref, wb_ref):
    lo, hi = j * PROJ_COLS, (j + 1) * PROJ_COLS
    ya = jnp.dot(y_a, wa_ref[:, lo:hi], preferred_element_type=F32)
    yb = jnp.dot(y_b, wb_ref[:, lo:hi], preferred_element_type=F32)
    g_a = jax.nn.sigmoid(gp_ref[:, lo:hi] + bm_ref[:, lo:hi])
    g_b = jax.nn.sigmoid(gp_ref[:, D_MODEL + lo:D_MODEL + hi] + bm_ref[:, D_MODEL + lo:D_MODEL + hi])
    return (g_a * ya + g_b * yb).astype(BF16)


def _out_chunk(merged, j, wo_ref):
    return jnp.dot(merged, wo_ref[:, j * PROJ_COLS:(j + 1) * PROJ_COLS], preferred_element_type=F32)


def _post_norm(o, gpost_ref):
    return o * lax.rsqrt(jnp.mean(o * o, axis=-1, keepdims=True) + EPS) * gpost_ref[...]


def _merge_project(y_a, y_b, gp_ref, bm_ref, wa_ref, wb_ref, wo_ref, gpost_ref):
    merged = jnp.concatenate([_merge_chunk(y_a, y_b, j, gp_ref, bm_ref, wa_ref, wb_ref)
                              for j in range(PROJ_SPLIT)], axis=1)
    o = jnp.concatenate([_out_chunk(merged, j, wo_ref) for j in range(PROJ_SPLIT)], axis=1)
    return _post_norm(o, gpost_ref)


MIX_ROWS = 2 * CHUNK
MIX_HEADS = 4
Z_AX, Z_AG, Z_Q, Z_K, Z_V, Z_BG = 0, D_POOL, 2 * D_POOL, 2 * D_POOL + D_QK, 2 * D_POOL + 2 * D_QK, D_IN - D_V


def _prompt_kernel(cdec_ref, z_ref, rot_ref, dmask_ref, qdec_ref, kdec_ref, pw_ref, ps_ref, gn_ref,
                   gp_ref, bm_ref, wa_ref, wb_ref, wo_ref,
                   m_ref, npool_ref, nret_ref, wo_bf_ref, ya0_ref, yb0_ref, ya1_ref, yb1_ref, ext_ref, *, tiles_per_seq):
    s = pl.program_id(0)
    n_tiles = pl.num_programs(0) - 1
    live = s < n_tiles
    c = lax.rem(jnp.minimum(s, n_tiles - 1), tiles_per_seq)
    rows = z_ref.shape[0]

    @pl.when(s == 0)
    def _():
        for ref in (ya0_ref, yb0_ref, ya1_ref, yb1_ref):
            ref[...] = jnp.zeros(ref.shape, BF16)

    @pl.when(c == 0)
    def _():
        ext_ref[0:HALO, :] = jnp.zeros((HALO, D_POOL), F32)
        nret_ref[...] = jnp.zeros(nret_ref.shape, F32)

    def step(ya_rd, yb_rd, ya_wr, yb_wr):
        y_a, y_b = ya_rd[...], yb_rd[...]
        wo_bf_ref[...] = wo_ref[...].astype(BF16)
        xa = z_ref[:, Z_AX:Z_AX + D_POOL]
        ext_ref[HALO:HALO + rows, :] = xa
        pos = c * rows + lax.broadcasted_iota(jnp.int32, (rows, 1), 0)
        nt = (((1,), (1,)), ((), ()))
        tn = (((0,), (0,)), ((), ()))

        def merge_piece(j):
            m_ref[:, j * PROJ_COLS:(j + 1) * PROJ_COLS] = _merge_chunk(y_a, y_b, j, gp_ref, bm_ref, wa_ref, wb_ref)

        def pool_group(g):
            w = POOL_WINDOWS[g]
            lo, hi = g * POOL_GROUP_DIM, (g + 1) * POOL_GROUP_DIM
            acc = ext_ref[:, lo:hi]
            span = 1
            while span < w:
                acc = acc + pltpu.roll(acc, span, 0)
                span *= 2
            inv_cnt = 1.0 / jnp.minimum(pos + 1, w).astype(F32)
            pooled = acc[HALO:, :] * inv_cnt - xa[:, lo:hi]
            ya = _pool_project(pooled, pw_ref[g], ps_ref[:, lo:hi], z_ref[:, Z_AG + lo:Z_AG + hi])
            ya_wr[:, lo:hi] = ya.astype(BF16)

        per_pool = PROJ_SPLIT // len(POOL_WINDOWS)
        assert per_pool * len(POOL_WINDOWS) == PROJ_SPLIT
        fillers = [f for g in range(len(POOL_WINDOWS))
                   for f in [functools.partial(merge_piece, g * per_pool + j) for j in range(per_pool)]
                   + [functools.partial(pool_group, g)]]
        n_sub = rows // CHUNK
        n_slots = 2 * (N_HEADS // MIX_HEADS) * n_sub
        slot = [0]

        def fill():
            lo, hi = (slot[0] * len(fillers)) // n_slots, ((slot[0] + 1) * len(fillers)) // n_slots
            slot[0] += 1
            for f in fillers[lo:hi]:
                f()

        for h0 in range(0, N_HEADS, MIX_HEADS):
            heads = range(h0, h0 + MIX_HEADS)
            s_cur = {h: nret_ref[h] for h in heads}
            for ci in range(n_sub):
                rs = slice(ci * CHUNK, (ci + 1) * CHUNK)
                cos = rot_ref[rs, :HEAD_DK]
                sin = rot_ref[rs, HEAD_DK:]
                q, qd, k, kd, v = {}, {}, {}, {}, {}
                for h in heads:
                    qf = _rotate(z_ref[rs, Z_Q + h * HEAD_DK:Z_Q + (h + 1) * HEAD_DK], cos, sin)
                    kf = _rotate(z_ref[rs, Z_K + h * HEAD_DK:Z_K + (h + 1) * HEAD_DK], cos, sin) * K_SCALE
                    q[h], qd[h] = qf.astype(BF16), (qf * qdec_ref[h]).astype(BF16)
                    k[h], kd[h] = kf.astype(BF16), (kf * kdec_ref[h]).astype(BF16)
                    v[h] = z_ref[rs, Z_V + h * HEAD_DV:Z_V + (h + 1) * HEAD_DV].astype(BF16)
                fill()
                scores = {h: lax.dot_general(q[h], k[h], nt, preferred_element_type=F32) for h in heads}
                kv = {h: lax.dot_general(kd[h], v[h], tn, preferred_element_type=F32) for h in heads}
                lhs = {h: jnp.concatenate([(scores[h] * dmask_ref[h]).astype(BF16), qd[h]], axis=1) for h in heads}
                fill()
                for h in heads:
                    rhs = jnp.concatenate([v[h], s_cur[h].astype(BF16)], axis=0)
                    o = jnp.dot(lhs[h], rhs, preferred_element_type=F32)
                    s_cur[h] = cdec_ref[h] * s_cur[h] + kv[h]
                    vs = slice(h * HEAD_DV, (h + 1) * HEAD_DV)
                    bg = z_ref[rs, Z_BG + h * HEAD_DV:Z_BG + (h + 1) * HEAD_DV]
                    yb_wr[rs, vs] = _group_norm_gate(o, gn_ref[:, vs], bg).astype(BF16)
            for h in heads:
                nret_ref[h] = jnp.where(live, s_cur[h], nret_ref[h])
        assert slot[0] == n_slots

    @pl.when(lax.rem(s, 2) == 0)
    def _():
        step(ya1_ref, yb1_ref, ya0_ref, yb0_ref)

    @pl.when(lax.rem(s, 2) == 1)
    def _():
        step(ya0_ref, yb0_ref, ya1_ref, yb1_ref)

    @pl.when(c == tiles_per_seq - 1)
    def _():
        npool_ref[...] = ext_ref[HALO + rows - POOL_BUF:HALO + rows, :]

    ext_ref[0:HALO, :] = ext_ref[rows:rows + HALO, :]


def _prompt_mix_merge(z, batch, seq, tables, pool_w, pool_scale, gn_g, b_merge, w_a, w_b, w_o):
    rot, dmask, qdec, kdec, cdec = tables
    tps = seq // MIX_ROWS
    n_tiles = batch * tps
    assert n_tiles >= CAST_STEPS
    cur = lambda s: jnp.minimum(s, n_tiles - 1)
    prev = lambda s: jnp.maximum(s - 1, 0)
    m = batch * seq
    wo_in, wo_out, wo_shape = _cast_specs(w_o, lambda s: s)
    return pl.pallas_call(
        functools.partial(_prompt_kernel, tiles_per_seq=tps),
        out_shape=(jax.ShapeDtypeStruct((m, D_MODEL), BF16),
                   jax.ShapeDtypeStruct((batch, POOL_BUF, D_POOL), F32),
                   jax.ShapeDtypeStruct((batch, N_HEADS, HEAD_DK, HEAD_DV), F32), wo_shape),
        grid=(n_tiles + 1,),
        in_specs=[pl.BlockSpec(memory_space=pltpu.SMEM),
                  pl.BlockSpec((MIX_ROWS, D_IN), lambda s: (cur(s), 0)),
                  pl.BlockSpec((MIX_ROWS, 2 * HEAD_DK), lambda s: (cur(s) % tps, 0)),
                  _resident(dmask.shape), _resident(qdec.shape), _resident(kdec.shape),
                  _resident(pool_w.shape), _resident(pool_scale.shape), _resident(gn_g.shape),
                  pl.BlockSpec((MIX_ROWS, 2 * D_MODEL), lambda s: (prev(s), D_IN // (2 * D_MODEL))),
                  _resident(b_merge.shape), _resident(w_a.shape), _resident(w_b.shape), wo_in],
        out_specs=(pl.BlockSpec((MIX_ROWS, D_MODEL), lambda s: (prev(s), 0)),
                   pl.BlockSpec((None, POOL_BUF, D_POOL), lambda s: (cur(s) // tps, 0, 0)),
                   pl.BlockSpec((None, N_HEADS, HEAD_DK, HEAD_DV), lambda s: (cur(s) // tps, 0, 0, 0)), wo_out),
        scratch_shapes=[pltpu.VMEM((MIX_ROWS, D_POOL), BF16), pltpu.VMEM((MIX_ROWS, D_V), BF16),
                        pltpu.VMEM((MIX_ROWS, D_POOL), BF16), pltpu.VMEM((MIX_ROWS, D_V), BF16),
                        pltpu.VMEM((HALO + MIX_ROWS, D_POOL), F32)],
        compiler_params=_params("arbitrary"),
        name="prompt_mix_merge",
    )(cdec, z, rot, dmask, qdec, kdec, pool_w, pool_scale, gn_g, z, b_merge, w_a, w_b, w_o)


SAMPLE_TILE = 8
SAMPLE_HEADS = 4
TAIL_ROWS = 256


def _tail_kernel(dec_ref, m_ref, x_ref, gate_ref, wo_ref, gpost_ref,
                 zs_ref, rot_ref, pw_ref, ps_ref, gn_ref, spool_ref, sret_ref,
                 y_ref, ya_ref, yb_ref, npool_ref, nret_ref, o_ref, *, inv_cnt, rows_per_mod):
    bt = zs_ref.shape[0]
    n_hg = N_HEADS // SAMPLE_HEADS

    def step(hg):
        o = _post_norm(jnp.dot(m_ref[...], wo_ref[...], preferred_element_type=F32), gpost_ref)
        tile = pl.program_id(0) * n_hg + hg
        gate = gate_ref[pl.ds((tile * x_ref.shape[0]) // rows_per_mod, 1), :]
        y_ref[...] = x_ref[...] + gate * o

        if hg == 0:
            xa = zs_ref[:, Z_AX:Z_AX + D_POOL]
            run = xa
            wins = {1: xa}
            for j in range(1, POOL_BUF + 1):
                run = run + spool_ref[POOL_BUF - j]
                wins[j + 1] = run
            for g, w in enumerate(POOL_WINDOWS):
                lo, hi = g * POOL_GROUP_DIM, (g + 1) * POOL_GROUP_DIM
                pooled = wins[w][:, lo:hi] * inv_cnt[g] - xa[:, lo:hi]
                ya = _pool_project(pooled, pw_ref[g], ps_ref[:, lo:hi], zs_ref[:, Z_AG + lo:Z_AG + hi])
                ya_ref[:, lo:hi] = ya.astype(BF16)
            for j in range(POOL_BUF - 1):
                npool_ref[j] = spool_ref[j + 1]
            npool_ref[POOL_BUF - 1] = xa

        cos = rot_ref[:, :HEAD_DK]
        sin = rot_ref[:, HEAD_DK:]
        for hl in range(SAMPLE_HEADS):
            h = hg * SAMPLE_HEADS + hl
            vs = slice(h * HEAD_DV, (h + 1) * HEAD_DV)
            q = _rotate(zs_ref[:, Z_Q + h * HEAD_DK:Z_Q + (h + 1) * HEAD_DK], cos, sin)
            k = _rotate(zs_ref[:, Z_K + h * HEAD_DK:Z_K + (h + 1) * HEAD_DK], cos, sin) * K_SCALE
            v = zs_ref[:, Z_V + h * HEAD_DV:Z_V + (h + 1) * HEAD_DV]
            score = jnp.sum(q * k, axis=1, keepdims=True) * dec_ref[0, h]
            q_cols = jnp.transpose(q * dec_ref[1, h])
            k_cols = jnp.transpose(k * dec_ref[2, h])
            for r in range(bt):
                s_old = sret_ref[r, hl]
                v_row = v[r:r + 1, :]
                o_row = score[r:r + 1, :] * v_row + jnp.sum(q_cols[:, r:r + 1] * s_old, axis=0, keepdims=True)
                nret_ref[r, hl] = dec_ref[3, h] * s_old + k_cols[:, r:r + 1] * v_row
                o_ref[r:r + 1, vs] = o_row
        for hl in range(SAMPLE_HEADS):
            h = hg * SAMPLE_HEADS + hl
            vs = slice(h * HEAD_DV, (h + 1) * HEAD_DV)
            bg = zs_ref[:, Z_BG + h * HEAD_DV:Z_BG + (h + 1) * HEAD_DV]
            yb_ref[:, vs] = _group_norm_gate(o_ref[:, vs], gn_ref[:, vs], bg).astype(BF16)

    for hg in range(n_hg):
        pl.when(pl.program_id(1) == hg)(functools.partial(step, hg))


def _tail(merged, x, mod, w_o, g_post, zs, state_pool, state_ret, tables, inv_cnt, pool_w, pool_scale, gn_g,
          *, rows_per_mod, mod_row0):
    rot, dec = tables
    m, batch = x.shape[0], zs.shape[0]
    bt, hs = SAMPLE_TILE, SAMPLE_HEADS
    n_hg = N_HEADS // hs
    assert m == (batch // bt) * n_hg * TAIL_ROWS and mod_row0 % SUBLANES == 0 and m // rows_per_mod <= SUBLANES
    rows = lambda i, g: (i * n_hg + g, 0)
    per_tile = lambda width: pl.BlockSpec((bt, width), lambda i, g: (i, 0))
    return pl.pallas_call(
        functools.partial(_tail_kernel, inv_cnt=inv_cnt, rows_per_mod=rows_per_mod),
        out_shape=(jax.ShapeDtypeStruct((m, D_MODEL), F32),
                   jax.ShapeDtypeStruct((batch, D_POOL), BF16),
                   jax.ShapeDtypeStruct((batch, D_V), BF16),
                   jax.ShapeDtypeStruct(state_pool.shape, state_pool.dtype),
                   jax.ShapeDtypeStruct(state_ret.shape, state_ret.dtype)),
        grid=(batch // bt, n_hg),
        in_specs=[pl.BlockSpec(memory_space=pltpu.SMEM),
                  pl.BlockSpec((TAIL_ROWS, D_MODEL), rows), pl.BlockSpec((TAIL_ROWS, D_MODEL), rows),
                  pl.BlockSpec((SUBLANES, D_MODEL), lambda i, g: (mod_row0 // SUBLANES, 2)),
                  _resident(w_o.shape), _resident(g_post.shape),
                  per_tile(D_IN), _resident(rot.shape),
                  _resident(pool_w.shape), _resident(pool_scale.shape), _resident(gn_g.shape),
                  pl.BlockSpec((POOL_BUF, bt, D_POOL), lambda i, g: (0, i, 0)),
                  pl.BlockSpec((bt, hs, HEAD_DK, HEAD_DV), lambda i, g: (i, g, 0, 0))],
        out_specs=(pl.BlockSpec((TAIL_ROWS, D_MODEL), rows),
                   per_tile(D_POOL), per_tile(D_V),
                   pl.BlockSpec((POOL_BUF, bt, D_POOL), lambda i, g: (0, i, 0)),
                   pl.BlockSpec((bt, hs, HEAD_DK, HEAD_DV), lambda i, g: (i, g, 0, 0))),
        scratch_shapes=[pltpu.VMEM((bt, D_V), F32)],
        compiler_params=_params("arbitrary", "arbitrary"),
        name="tail",
    )(dec, merged, x, mod, w_o, g_post, zs, rot, pool_w, pool_scale, gn_g, state_pool, state_ret)


def _out_proj_kernel(ya_ref, yb_ref, gp_ref, x_ref, gate_ref, bm_ref, wa_ref, wb_ref, wo_ref, gpost_ref, y_ref,
                     *, rows_per_mod):
    o = _merge_project(ya_ref[...], yb_ref[...], gp_ref, bm_ref, wa_ref, wb_ref, wo_ref, gpost_ref)
    gate = _mod_rows(gate_ref, slice(None), x_ref.shape[0], rows_per_mod)
    y_ref[...] = x_ref[...] + gate * o


def _out_proj(ya, yb, z, x, mod, b_merge, w_a, w_b, w_o, g_post, *, tm, rows_per_mod, mod_row0):
    m = x.shape[0]
    return pl.pallas_call(
        functools.partial(_out_proj_kernel, rows_per_mod=rows_per_mod),
        out_shape=jax.ShapeDtypeStruct((m, D_MODEL), F32),
        grid=(m // tm,),
        in_specs=[pl.BlockSpec((tm, D_POOL), lambda i: (i, 0)),
                  pl.BlockSpec((tm, D_V), lambda i: (i, 0)),
                  pl.BlockSpec((tm, 2 * D_MODEL), lambda i: (i, D_IN // (2 * D_MODEL))),
                  pl.BlockSpec((tm, D_MODEL), lambda i: (i, 0)),
                  _mod_spec(tm, rows_per_mod, mod_row0, 2),
                  _resident(b_merge.shape), _resident(w_a.shape), _resident(w_b.shape), _resident(w_o.shape),
                  _resident(g_post.shape)],
        out_specs=pl.BlockSpec((tm, D_MODEL), lambda i: (i, 0)),
        compiler_params=_params("parallel"),
        name="out_proj",
    )(ya, yb, z, x, mod, b_merge, w_a, w_b, w_o, g_post)


def _rotary_tables(start, length):
    half = HEAD_DK // 2
    inv = ROPE_BASE ** (-np.arange(half, dtype=np.float64) / half)
    ang = (start + np.arange(length, dtype=np.float64))[:, None] * inv[None, :]
    cos, sin = np.cos(ang), np.sin(ang)
    return (np.concatenate([cos, cos], axis=-1).astype(np.float32),
            np.concatenate([-sin, sin], axis=-1).astype(np.float32))


def _decay_tables(c):
    lg = np.log1p(-np.power(2.0, -5.0 - np.arange(N_HEADS, dtype=np.float64)))
    idx = np.arange(c, dtype=np.float64)
    diff = idx[:, None] - idx[None, :]
    dmask = np.where(diff[None] >= 0, np.exp(np.maximum(diff, 0.0)[None] * lg[:, None, None]), 0.0)
    q_dec = np.exp((idx + 1.0)[None, :] * lg[:, None])
    k_dec = np.exp((c - 1.0 - idx)[None, :] * lg[:, None])
    chunk_dec = np.exp(c * lg)
    return tuple(a.astype(np.float32) for a in (dmask, q_dec, k_dec, chunk_dec))


def _layer(xp, xs, c_prompt, c_sample, state_pool, state_ret, ada_w, ada_b, g_pre, g_post, w_in, pool_w, pool_scale, gn_g,
           w_a_proj, w_b_proj, w_merge, b_merge, w_out):
    batch, seq, _ = xp.shape
    dec_batch, dec_seq, _ = xs.shape
    assert dec_seq == 1 and seq % CHUNK == 0

    row = lambda v: v.reshape(1, -1)
    g_pre, g_post, pool_scale, gn_g, b_merge = map(row, (g_pre, g_post, pool_scale, gn_g, b_merge))

    mod = _modulation(c_sample, c_prompt, ada_w, row(ada_b))

    x2 = xp.reshape(batch * seq, D_MODEL)
    xs2 = xs.reshape(dec_batch, D_MODEL)
    h = _norm(x2, mod, g_pre, tm=NORM_TM, rows_per_mod=seq, mod_row0=dec_batch)
    z, zs, w_a, w_b = _in_proj(h, xs2, mod, g_pre, w_in, w_merge, w_a_proj, w_b_proj)

    cos, sin = _rotary_tables(0, seq)
    dmask, q_dec, k_dec, chunk_dec = _decay_tables(CHUNK)
    wide = lambda d: np.ascontiguousarray(np.broadcast_to(d[:, :, None], (N_HEADS, CHUNK, HEAD_DK)))
    rot = np.concatenate([cos, sin], axis=1)
    merged, pool_p, ret_p, w_o = _prompt_mix_merge(z, batch, seq, (rot, dmask, wide(q_dec), wide(k_dec), chunk_dec),
                                                   pool_w, pool_scale, gn_g, b_merge, w_a, w_b, w_out)

    cos_s, sin_s = _rotary_tables(PAST_LEN, 1)
    dmask_s, q_dec_s, k_dec_s, chunk_dec_s = _decay_tables(1)
    dec_s = np.stack([dmask_s[:, 0, 0], q_dec_s[:, 0], k_dec_s[:, 0], chunk_dec_s])
    inv_cnt = tuple(1.0 / min(PAST_LEN + 1, w) for w in POOL_WINDOWS)
    yp, ya_s, yb_s, pool_s, ret_s = _tail(merged, x2, mod, w_o, g_post, zs, jnp.transpose(state_pool, (1, 0, 2)),
                                          state_ret, (np.concatenate([cos_s, sin_s], axis=1), dec_s), inv_cnt,
                                          pool_w, pool_scale, gn_g,
                                          rows_per_mod=seq, mod_row0=dec_batch)
    pool_s = jnp.transpose(pool_s, (1, 0, 2))
    ys = _out_proj(ya_s, yb_s, zs, xs2, mod, b_merge, w_a, w_b, w_o, g_post, tm=dec_batch, rows_per_mod=1, mod_row0=0)

    return yp.reshape(xp.shape), ys.reshape(xs.shape), pool_p, ret_p, pool_s, ret_s


def kernel(x_prompt, x_sample, state_pool, state_ret, c_prompt, c_sample, ada_w, ada_b, g_pre, g_post,
           w_in, pool_w, pool_scale, gn_g, w_a_proj, w_b_proj, w_merge, b_merge, w_out):
    depth = ada_w.shape[0]
    xp, xs = x_prompt, x_sample
    pool_p, ret_p, pool_s, ret_s = [], [], [], []
    for l in range(depth):
        xp, xs, bp, sp, bs, ss = _layer(
            xp, xs, c_prompt, c_sample, state_pool[l], state_ret[l], ada_w[l], ada_b[l], g_pre[l], g_post[l], w_in[l],
            pool_w[l], pool_scale[l], gn_g[l].reshape(-1), w_a_proj[l], w_b_proj[l], w_merge[l], b_merge[l],
            w_out[l])
        pool_p.append(bp)
        ret_p.append(sp)
        pool_s.append(bs)
        ret_s.append(ss)
    return (xp, xs, jnp.stack(pool_p), jnp.stack(ret_p), jnp.stack(pool_s), jnp.stack(ret_s))
```

```python
import functools

import jax
import jax.numpy as jnp
import numpy as np
from jax import lax
from jax.experimental import pallas as pl
from jax.experimental.pallas import tpu as pltpu

F32 = jnp.float32
BF16 = jnp.bfloat16

D_MODEL = 2048
PAST_LEN = 16384
D_POOL = D_MODEL // 2
POOL_WINDOWS = (2, 4, 8, 16)
POOL_GROUP_DIM = D_POOL // len(POOL_WINDOWS)
POOL_BUF = max(POOL_WINDOWS) - 1
N_HEADS = 8
HEAD_DK = D_MODEL // 16
HEAD_DV = D_MODEL // 8
D_QK = N_HEADS * HEAD_DK
D_V = N_HEADS * HEAD_DV
CHUNK = 128
ROPE_BASE = 10000.0
EPS = 1e-6
D_IN = 2 * D_POOL + 2 * D_QK + 2 * D_V
D_Z = D_IN + 2 * D_MODEL
K_SCALE = HEAD_DK ** -0.5

VMEM_LIMIT_BYTES = 56 * 1024 * 1024
SUBLANES = 8
HALO = 16
CAST_STEPS = 32


def _params(*semantics):
    return pltpu.CompilerParams(dimension_semantics=semantics, vmem_limit_bytes=VMEM_LIMIT_BYTES)


def _cast_specs(w, step_of):
    rows = w.shape[0] // CAST_STEPS
    assert rows * CAST_STEPS == w.shape[0] and rows % (2 * SUBLANES) == 0
    spec = pl.BlockSpec((rows, w.shape[1]), lambda *idx: (jnp.minimum(step_of(*idx), CAST_STEPS - 1), 0))
    return spec, spec, jax.ShapeDtypeStruct(w.shape, BF16)


def _resident(shape):
    return pl.BlockSpec(shape, lambda *_: (0,) * len(shape), pipeline_mode=pl.Buffered(1))


def _silu(x):
    return x * jax.nn.sigmoid(x)


MOD_TN = 1536


def _mod_kernel(cs_ref, cp_ref, w_ref, b_ref, o_ref):
    ns, n_pad = cs_ref.shape[0], o_ref.shape[0] - cs_ref.shape[0] - cp_ref.shape[0]
    w = w_ref[...].astype(BF16)
    cp = jnp.concatenate([cp_ref[...], jnp.zeros((n_pad, D_MODEL), F32)], axis=0)
    o_ref[0:ns, :] = jnp.dot(_silu(cs_ref[...]).astype(BF16), w, preferred_element_type=F32) + b_ref[...]
    o_ref[ns:, :] = jnp.dot(_silu(cp).astype(BF16), w, preferred_element_type=F32) + b_ref[...]


def _modulation(c_sample, c_prompt, ada_w, ada_b):
    ns, n_p = c_sample.shape[0], c_prompt.shape[0]
    assert ns % SUBLANES == 0
    rows = ns + n_p + (-n_p) % SUBLANES
    return pl.pallas_call(
        _mod_kernel,
        out_shape=jax.ShapeDtypeStruct((rows, 3 * D_MODEL), F32),
        grid=(3 * D_MODEL // MOD_TN,),
        in_specs=[pl.BlockSpec((ns, D_MODEL), lambda j: (0, 0)),
                  pl.BlockSpec((n_p, D_MODEL), lambda j: (0, 0)),
                  pl.BlockSpec((D_MODEL, MOD_TN), lambda j: (0, j)),
                  pl.BlockSpec((1, MOD_TN), lambda j: (0, j))],
        out_specs=pl.BlockSpec((rows, MOD_TN), lambda j: (0, j)),
        compiler_params=_params("arbitrary"),
        name="modulation",
    )(c_sample, c_prompt, ada_w, ada_b)


NORM_ROWS = 128
NORM_TM = 2048


def _mod_rows(ref, rows, tm, rows_per_mod):
    if rows_per_mod == 1:
        return ref[rows, :]
    return ref[pl.ds((pl.program_id(0) * tm) // rows_per_mod, 1), :]


def _mod_spec(tm, rows_per_mod, mod_row0, col):
    if rows_per_mod == 1:
        assert mod_row0 % tm == 0
        return pl.BlockSpec((tm, D_MODEL), lambda i, *_: (mod_row0 // tm + i, col))
    assert mod_row0 % SUBLANES == 0
    return pl.BlockSpec((SUBLANES, D_MODEL), lambda i, *_: (mod_row0 // SUBLANES, col))


def _norm_mod(x, g, scale, shift):
    xn = x * lax.rsqrt(jnp.mean(x * x, axis=-1, keepdims=True) + EPS) * g
    return (xn * (1.0 + scale) + shift).astype(BF16)


def _norm_kernel(x_ref, shift_ref, scale_ref, g_ref, h_ref, *, rows_per_mod):
    tm = x_ref.shape[0]

    def body(r, carry):
        rows = pl.ds(pl.multiple_of(r * NORM_ROWS, NORM_ROWS), NORM_ROWS)
        h_ref[rows, :] = _norm_mod(x_ref[rows, :], g_ref[...], _mod_rows(scale_ref, rows, tm, rows_per_mod),
                                   _mod_rows(shift_ref, rows, tm, rows_per_mod))
        return carry
    lax.fori_loop(0, tm // NORM_ROWS, body, 0)


def _norm(x, mod, g_pre, *, tm, rows_per_mod, mod_row0):
    m = x.shape[0]
    assert rows_per_mod == 1 or m // rows_per_mod <= SUBLANES
    mod_spec = lambda col: _mod_spec(tm, rows_per_mod, mod_row0, col)
    return pl.pallas_call(
        functools.partial(_norm_kernel, rows_per_mod=rows_per_mod),
        out_shape=jax.ShapeDtypeStruct((m, D_MODEL), BF16),
        grid=(m // tm,),
        in_specs=[pl.BlockSpec((tm, D_MODEL), lambda i: (i, 0)), mod_spec(0), mod_spec(1),
                  pl.BlockSpec((1, D_MODEL), lambda i: (0, 0))],
        out_specs=pl.BlockSpec((tm, D_MODEL), lambda i: (i, 0)),
        compiler_params=_params("parallel"),
        name="norm",
    )(x, mod, mod, g_pre)


IN_TM = 1024
IN_TN = 2048
IN_DOT_COLS = 512


def _in_proj_kernel(h_ref, xs_ref, shift_s_ref, scale_s_ref, g_ref, wa_ref, wb_ref, w_in_hbm, w_mg_hbm,
                    z_ref, zs_ref, wa_bf_ref, wb_bf_ref, wbf0_ref, wbf1_ref, stage_ref, hs_ref, sem,
                    *, n_in_tiles, n_j, n_i):
    j, i = pl.program_id(0), pl.program_id(1)
    t = j * n_i + i
    total = n_j * n_i
    tn = wbf0_ref.shape[1]
    chunk = stage_ref.shape[1]

    def chunk_copy(w_hbm, col0, r, b):
        rows = pl.ds(pl.multiple_of(r * chunk, chunk), chunk)
        cols = pl.ds(col0 if isinstance(col0, int) else pl.multiple_of(col0, tn), tn)
        return pltpu.make_async_copy(w_hbm.at[rows, cols], stage_ref.at[b], sem.at[b])

    def start_chunk(g):
        g = lax.rem(jnp.asarray(g, jnp.int32), total)
        jt, r, b = lax.div(g, n_i), lax.rem(g, n_i), lax.rem(g, 2)

        @pl.when(jt < n_in_tiles)
        def _():
            chunk_copy(w_in_hbm, jt * tn, r, b).start()

        @pl.when(jt >= n_in_tiles)
        def _():
            chunk_copy(w_mg_hbm, (jt - n_in_tiles) * tn, r, b).start()

    def land_chunk(g, dst_ref):
        g = jnp.asarray(g, jnp.int32)
        r, b = lax.rem(g, n_i), lax.rem(g, 2)
        chunk_copy(w_in_hbm, 0, r, b).wait()
        dst_ref[pl.ds(pl.multiple_of(r * chunk, chunk), chunk), :] = stage_ref[b].astype(BF16)

    @pl.when(t == 0)
    def _():
        hs_ref[...] = _norm_mod(xs_ref[...], g_ref[...], scale_s_ref[...], shift_s_ref[...])
        start_chunk(0)

        def body(g, carry):
            start_chunk(g + 1)
            land_chunk(g, wbf0_ref)
            return carry
        lax.fori_loop(0, n_i, body, 0)

    @pl.when(t + 1 < total)
    def _():
        start_chunk(t + n_i + 1)

    def multiply(w_cur, w_nxt):
        land_chunk(t + n_i, w_nxt)
        wa_bf_ref[...] = wa_ref[...].astype(BF16)
        wb_bf_ref[...] = wb_ref[...].astype(BF16)
        for c0 in range(0, tn, IN_DOT_COLS):
            cols = slice(c0, c0 + IN_DOT_COLS)
            z_ref[:, cols] = jnp.dot(h_ref[...], w_cur[:, cols], preferred_element_type=F32)

        @pl.when(i == 0)
        def _():
            zs_ref[...] = jnp.dot(hs_ref[...], w_cur[...], preferred_element_type=F32)

    @pl.when(lax.rem(j, 2) == 0)
    def _():
        multiply(wbf0_ref, wbf1_ref)

    @pl.when(lax.rem(j, 2) == 1)
    def _():
        multiply(wbf1_ref, wbf0_ref)


def _in_proj(h, xs, mod, g_pre, w_in, w_mg, w_a, w_b):
    m, ms = h.shape[0], xs.shape[0]
    n_in, n_mg = w_in.shape[1] // IN_TN, w_mg.shape[1] // IN_TN
    n_i = m // IN_TM
    chunk = D_MODEL // n_i
    assert chunk * n_i == D_MODEL and chunk % 16 == 0 and (n_in + n_mg) * n_i >= CAST_STEPS
    const = lambda shape, col: pl.BlockSpec(shape, lambda j, i: (0, col), pipeline_mode=pl.Buffered(1))
    wa_in, wa_out, wa_shape = _cast_specs(w_a, lambda j, i: j * n_i + i)
    wb_in, wb_out, wb_shape = _cast_specs(w_b, lambda j, i: j * n_i + i)
    return pl.pallas_call(
        functools.partial(_in_proj_kernel, n_in_tiles=n_in, n_j=n_in + n_mg, n_i=n_i),
        out_shape=(jax.ShapeDtypeStruct((m, D_Z), F32), jax.ShapeDtypeStruct((ms, D_Z), F32), wa_shape, wb_shape),
        grid=(n_in + n_mg, n_i),
        in_specs=[pl.BlockSpec((IN_TM, D_MODEL), lambda j, i: (i, 0)),
                  const((ms, D_MODEL), 0), const((ms, D_MODEL), 0), const((ms, D_MODEL), 1),
                  const((1, D_MODEL), 0), wa_in, wb_in,
                  pl.BlockSpec(memory_space=pl.ANY), pl.BlockSpec(memory_space=pl.ANY)],
        out_specs=(pl.BlockSpec((IN_TM, IN_TN), lambda j, i: (i, j)),
                   pl.BlockSpec((ms, IN_TN), lambda j, i: (0, j)), wa_out, wb_out),
        scratch_shapes=[pltpu.VMEM((D_MODEL, IN_TN), BF16), pltpu.VMEM((D_MODEL, IN_TN), BF16),
                        pltpu.VMEM((2, chunk, IN_TN), F32),
                        pltpu.VMEM((ms, D_MODEL), BF16),
                        pltpu.SemaphoreType.DMA((2,))],
        compiler_params=_params("arbitrary", "arbitrary"),
        name="in_proj",
    )(h, xs, mod, mod, g_pre, w_a, w_b, w_in, w_mg)


def _rotate(x, cos, sin_signed):
    return x * cos + pltpu.roll(x, HEAD_DK // 2, 1) * sin_signed


def _group_norm_gate(o, gn, bg):
    mu = jnp.mean(o, axis=-1, keepdims=True)
    d = o - mu
    var = jnp.mean(d * d, axis=-1, keepdims=True)
    return d * lax.rsqrt(var + EPS) * gn * _silu(bg)


def _pool_project(pooled, pw, ps, ag):
    mixed = jnp.dot(pooled.astype(BF16), pw.astype(BF16), preferred_element_type=F32)
    return mixed * ps * _silu(ag)


PROJ_SPLIT = 8
PROJ_COLS = D_MODEL // PROJ_SPLIT


def _merge_chunk(y_a, y_b, j, gp_ref, bm_ref, wa_ref, wb_ref):
    lo, hi = j * PROJ_COLS, (j + 1) * PROJ_COLS
    ya = jnp.dot(y_a, wa_ref[:, lo:hi], preferred_element_type=F32)
    yb = jnp.dot(y_b, wb_ref[:, lo:hi], preferred_element_type=F32)
    g_a = jax.nn.sigmoid(gp_ref[:, lo:hi] + bm_ref[:, lo:hi])
    g_b = jax.nn.sigmoid(gp_ref[:, D_MODEL + lo:D_MODEL + hi] + bm_ref[:, D_MODEL + lo:D_MODEL + hi])
    return (g_a * ya + g_b * yb).astype(BF16)


def _out_chunk(merged, j, wo_ref):
    return jnp.dot(merged, wo_ref[:, j * PROJ_COLS:(j + 1) * PROJ_COLS], preferred_element_type=F32)


def _post_norm(o, gpost_ref):
    return o * lax.rsqrt(jnp.mean(o * o, axis=-1, keepdims=True) + EPS) * gpost_ref[...]


def _merge_project(y_a, y_b, gp_ref, bm_ref, wa_ref, wb_ref, wo_ref, gpost_ref):
    merged = jnp.concatenate([_merge_chunk(y_a, y_b, j, gp_ref, bm_ref, wa_ref, wb_ref)
                              for j in range(PROJ_SPLIT)], axis=1)
    o = jnp.concatenate([_out_chunk(merged, j, wo_ref) for j in range(PROJ_SPLIT)], axis=1)
    return _post_norm(o, gpost_ref)


MIX_ROWS = 2 * CHUNK
MIX_HEADS = 4
Z_AX, Z_AG, Z_Q, Z_K, Z_V, Z_BG = 0, D_POOL, 2 * D_POOL, 2 * D_POOL + D_QK, 2 * D_POOL + 2 * D_QK, D_IN - D_V


def _prompt_kernel(cdec_ref, z_ref, rot_ref, dmask_ref, qdec_ref, kdec_ref, pw_ref, ps_ref, gn_ref,
                   gp_ref, bm_ref, wa_ref, wb_ref, wo_ref,
                   m_ref, npool_ref, nret_ref, wo_bf_ref, ya0_ref, yb0_ref, ya1_ref, yb1_ref, ext_ref, *, tiles_per_seq):
    s = pl.program_id(0)
    n_tiles = pl.num_programs(0) - 1
    live = s < n_tiles
    c = lax.rem(jnp.minimum(s, n_tiles - 1), tiles_per_seq)
    rows = z_ref.shape[0]

    @pl.when(s == 0)
    def _():
        for ref in (ya0_ref, yb0_ref, ya1_ref, yb1_ref):
            ref[...] = jnp.zeros(ref.shape, BF16)

    @pl.when(c == 0)
    def _():
        ext_ref[0:HALO, :] = jnp.zeros((HALO, D_POOL), F32)
        nret_ref[...] = jnp.zeros(nret_ref.shape, F32)

    def step(ya_rd, yb_rd, ya_wr, yb_wr):
        y_a, y_b = ya_rd[...], yb_rd[...]
        wo_bf_ref[...] = wo_ref[...].astype(BF16)
        xa = z_ref[:, Z_AX:Z_AX + D_POOL]
        ext_ref[HALO:HALO + rows, :] = xa
        pos = c * rows + lax.broadcasted_iota(jnp.int32, (rows, 1), 0)
        nt = (((1,), (1,)), ((), ()))
        tn = (((0,), (0,)), ((), ()))

        def merge_piece(j):
            m_ref[:, j * PROJ_COLS:(j + 1) * PROJ_COLS] = _merge_chunk(y_a, y_b, j, gp_ref, bm_ref, wa_ref, wb_ref)

        def pool_group(g):
            w = POOL_WINDOWS[g]
            lo, hi = g * POOL_GROUP_DIM, (g + 1) * POOL_GROUP_DIM
            acc = ext_ref[:, lo:hi]
            span = 1
            while span < w:
                acc = acc + pltpu.roll(acc, span, 0)
                span *= 2
            inv_cnt = 1.0 / jnp.minimum(pos + 1, w).astype(F32)
            pooled = acc[HALO:, :] * inv_cnt - xa[:, lo:hi]
            ya = _pool_project(pooled, pw_ref[g], ps_ref[:, lo:hi], z_ref[:, Z_AG + lo:Z_AG + hi])
            ya_wr[:, lo:hi] = ya.astype(BF16)

        per_pool = PROJ_SPLIT // len(POOL_WINDOWS)
        assert per_pool * len(POOL_WINDOWS) == PROJ_SPLIT
        fillers = [f for g in range(len(POOL_WINDOWS))
                   for f in [functools.partial(merge_piece, g * per_pool + j) for j in range(per_pool)]
                   + [functools.partial(pool_group, g)]]
        n_sub = rows // CHUNK
        n_slots = 2 * (N_HEADS // MIX_HEADS) * n_sub
        slot = [0]

        def fill():
            lo, hi = (slot[0] * len(fillers)) // n_slots, ((slot[0] + 1) * len(fillers)) // n_slots
            slot[0] += 1
            for f in fillers[lo:hi]:
                f()

        for h0 in range(0, N_HEADS, MIX_HEADS):
            heads = range(h0, h0 + MIX_HEADS)
            s_cur = {h: nret_ref[h] for h in heads}
            for ci in range(n_sub):
                rs = slice(ci * CHUNK, (ci + 1) * CHUNK)
                cos = rot_ref[rs, :HEAD_DK]
                sin = rot_ref[rs, HEAD_DK:]
                q = {h: _rotate(z_ref[rs, Z_Q + h * HEAD_DK:Z_Q + (h + 1) * HEAD_DK], cos, sin) for h in heads}
                k = {h: _rotate(z_ref[rs, Z_K + h * HEAD_DK:Z_K + (h + 1) * HEAD_DK], cos, sin) * K_SCALE
                     for h in heads}
                v = {h: z_ref[rs, Z_V + h * HEAD_DV:Z_V + (h + 1) * HEAD_DV].astype(BF16) for h in heads}
                fill()
                scores = {h: lax.dot_general(q[h].astype(BF16), k[h].astype(BF16), nt, preferred_element_type=F32)
                          for h in heads}
                kv = {h: lax.dot_general((k[h] * kdec_ref[h]).astype(BF16), v[h], tn, preferred_element_type=F32)
                      for h in heads}
                lhs = {h: jnp.concatenate([(scores[h] * dmask_ref[h]).astype(BF16),
                                           (q[h] * qdec_ref[h]).astype(BF16)], axis=1) for h in heads}
                fill()
                for h in heads:
                    rhs = jnp.concatenate([v[h], s_cur[h].astype(BF16)], axis=0)
                    o = jnp.dot(lhs[h], rhs, preferred_element_type=F32)
                    s_cur[h] = cdec_ref[h] * s_cur[h] + kv[h]
                    vs = slice(h * HEAD_DV, (h + 1) * HEAD_DV)
                    bg = z_ref[rs, Z_BG + h * HEAD_DV:Z_BG + (h + 1) * HEAD_DV]
                    yb_wr[rs, vs] = _group_norm_gate(o, gn_ref[:, vs], bg).astype(BF16)
            for h in heads:
                nret_ref[h] = jnp.where(live, s_cur[h], nret_ref[h])
        assert slot[0] == n_slots

    @pl.when(lax.rem(s, 2) == 0)
    def _():
        step(ya1_ref, yb1_ref, ya0_ref, yb0_ref)

    @pl.when(lax.rem(s, 2) == 1)
    def _():
        step(ya0_ref, yb0_ref, ya1_ref, yb1_ref)

    @pl.when(c == tiles_per_seq - 1)
    def _():
        npool_ref[...] = ext_ref[HALO + rows - POOL_BUF:HALO + rows, :]

    ext_ref[0:HALO, :] = ext_ref[rows:rows + HALO, :]


def _prompt_mix_merge(z, batch, seq, tables, pool_w, pool_scale, gn_g, b_merge, w_a, w_b, w_o):
    rot, dmask, qdec, kdec, cdec = tables
    tps = seq // MIX_ROWS
    n_tiles = batch * tps
    assert n_tiles >= CAST_STEPS
    cur = lambda s: jnp.minimum(s, n_tiles - 1)
    prev = lambda s: jnp.maximum(s - 1, 0)
    m = batch * seq
    wo_in, wo_out, wo_shape = _cast_specs(w_o, lambda s: s)
    return pl.pallas_call(
        functools.partial(_prompt_kernel, tiles_per_seq=tps),
        out_shape=(jax.ShapeDtypeStruct((m, D_MODEL), BF16),
                   jax.ShapeDtypeStruct((batch, POOL_BUF, D_POOL), F32),
                   jax.ShapeDtypeStruct((batch, N_HEADS, HEAD_DK, HEAD_DV), F32), wo_shape),
        grid=(n_tiles + 1,),
        in_specs=[pl.BlockSpec(memory_space=pltpu.SMEM),
                  pl.BlockSpec((MIX_ROWS, D_IN), lambda s: (cur(s), 0)),
                  pl.BlockSpec((MIX_ROWS, 2 * HEAD_DK), lambda s: (cur(s) % tps, 0)),
                  _resident(dmask.shape), _resident(qdec.shape), _resident(kdec.shape),
                  _resident(pool_w.shape), _resident(pool_scale.shape), _resident(gn_g.shape),
                  pl.BlockSpec((MIX_ROWS, 2 * D_MODEL), lambda s: (prev(s), D_IN // (2 * D_MODEL))),
                  _resident(b_merge.shape), _resident(w_a.shape), _resident(w_b.shape), wo_in],
        out_specs=(pl.BlockSpec((MIX_ROWS, D_MODEL), lambda s: (prev(s), 0)),
                   pl.BlockSpec((None, POOL_BUF, D_POOL), lambda s: (cur(s) // tps, 0, 0)),
                   pl.BlockSpec((None, N_HEADS, HEAD_DK, HEAD_DV), lambda s: (cur(s) // tps, 0, 0, 0)), wo_out),
        scratch_shapes=[pltpu.VMEM((MIX_ROWS, D_POOL), BF16), pltpu.VMEM((MIX_ROWS, D_V), BF16),
                        pltpu.VMEM((MIX_ROWS, D_POOL), BF16), pltpu.VMEM((MIX_ROWS, D_V), BF16),
                        pltpu.VMEM((HALO + MIX_ROWS, D_POOL), F32)],
        compiler_params=_params("arbitrary"),
        name="prompt_mix_merge",
    )(cdec, z, rot, dmask, qdec, kdec, pool_w, pool_scale, gn_g, z, b_merge, w_a, w_b, w_o)


SAMPLE_TILE = 8
SAMPLE_HEADS = 4
TAIL_ROWS = 256


def _tail_kernel(dec_ref, m_ref, x_ref, gate_ref, wo_ref, gpost_ref,
                 zs_ref, rot_ref, pw_ref, ps_ref, gn_ref, spool_ref, sret_ref,
                 y_ref, ya_ref, yb_ref, npool_ref, nret_ref, o_ref, *, inv_cnt, rows_per_mod):
    bt = zs_ref.shape[0]
    n_hg = N_HEADS // SAMPLE_HEADS

    def step(hg):
        o = _post_norm(jnp.dot(m_ref[...], wo_ref[...], preferred_element_type=F32), gpost_ref)
        tile = pl.program_id(0) * n_hg + hg
        gate = gate_ref[pl.ds((tile * x_ref.shape[0]) // rows_per_mod, 1), :]
        y_ref[...] = x_ref[...] + gate * o

        if hg == 0:
            xa = zs_ref[:, Z_AX:Z_AX + D_POOL]
            run = xa
            wins = {1: xa}
            for j in range(1, POOL_BUF + 1):
                run = run + spool_ref[POOL_BUF - j]
                wins[j + 1] = run
            for g, w in enumerate(POOL_WINDOWS):
                lo, hi = g * POOL_GROUP_DIM, (g + 1) * POOL_GROUP_DIM
                pooled = wins[w][:, lo:hi] * inv_cnt[g] - xa[:, lo:hi]
                ya = _pool_project(pooled, pw_ref[g], ps_ref[:, lo:hi], zs_ref[:, Z_AG + lo:Z_AG + hi])
                ya_ref[:, lo:hi] = ya.astype(BF16)
            for j in range(POOL_BUF - 1):
                npool_ref[j] = spool_ref[j + 1]
            npool_ref[POOL_BUF - 1] = xa

        cos = rot_ref[:, :HEAD_DK]
        sin = rot_ref[:, HEAD_DK:]
        for hl in range(SAMPLE_HEADS):
            h = hg * SAMPLE_HEADS + hl
            vs = slice(h * HEAD_DV, (h + 1) * HEAD_DV)
            q = _rotate(zs_ref[:, Z_Q + h * HEAD_DK:Z_Q + (h + 1) * HEAD_DK], cos, sin)
            k = _rotate(zs_ref[:, Z_K + h * HEAD_DK:Z_K + (h + 1) * HEAD_DK], cos, sin) * K_SCALE
            v = zs_ref[:, Z_V + h * HEAD_DV:Z_V + (h + 1) * HEAD_DV]
            score = jnp.sum(q * k, axis=1, keepdims=True) * dec_ref[0, h]
            q_cols = jnp.transpose(q * dec_ref[1, h])
            k_cols = jnp.transpose(k * dec_ref[2, h])
            for r in range(bt):
                s_old = sret_ref[r, hl]
                v_row = v[r:r + 1, :]
                o_row = score[r:r + 1, :] * v_row + jnp.sum(q_cols[:, r:r + 1] * s_old, axis=0, keepdims=True)
                nret_ref[r, hl] = dec_ref[3, h] * s_old + k_cols[:, r:r + 1] * v_row
                o_ref[r:r + 1, vs] = o_row
        for hl in range(SAMPLE_HEADS):
            h = hg * SAMPLE_HEADS + hl
            vs = slice(h * HEAD_DV, (h + 1) * HEAD_DV)
            bg = zs_ref[:, Z_BG + h * HEAD_DV:Z_BG + (h + 1) * HEAD_DV]
            yb_ref[:, vs] = _group_norm_gate(o_ref[:, vs], gn_ref[:, vs], bg).astype(BF16)

    for hg in range(n_hg):
        pl.when(pl.program_id(1) == hg)(functools.partial(step, hg))


def _tail(merged, x, mod, w_o, g_post, zs, state_pool, state_ret, tables, inv_cnt, pool_w, pool_scale, gn_g,
          *, rows_per_mod, mod_row0):
    rot, dec = tables
    m, batch = x.shape[0], zs.shape[0]
    bt, hs = SAMPLE_TILE, SAMPLE_HEADS
    n_hg = N_HEADS // hs
    assert m == (batch // bt) * n_hg * TAIL_ROWS and mod_row0 % SUBLANES == 0 and m // rows_per_mod <= SUBLANES
    rows = lambda i, g: (i * n_hg + g, 0)
    per_tile = lambda width: pl.BlockSpec((bt, width), lambda i, g: (i, 0))
    return pl.pallas_call(
        functools.partial(_tail_kernel, inv_cnt=inv_cnt, rows_per_mod=rows_per_mod),
        out_shape=(jax.ShapeDtypeStruct((m, D_MODEL), F32),
                   jax.ShapeDtypeStruct((batch, D_POOL), BF16),
                   jax.ShapeDtypeStruct((batch, D_V), BF16),
                   jax.ShapeDtypeStruct(state_pool.shape, state_pool.dtype),
                   jax.ShapeDtypeStruct(state_ret.shape, state_ret.dtype)),
        grid=(batch // bt, n_hg),
        in_specs=[pl.BlockSpec(memory_space=pltpu.SMEM),
                  pl.BlockSpec((TAIL_ROWS, D_MODEL), rows), pl.BlockSpec((TAIL_ROWS, D_MODEL), rows),
                  pl.BlockSpec((SUBLANES, D_MODEL), lambda i, g: (mod_row0 // SUBLANES, 2)),
                  _resident(w_o.shape), _resident(g_post.shape),
                  per_tile(D_IN), _resident(rot.shape),
                  _resident(pool_w.shape), _resident(pool_scale.shape), _resident(gn_g.shape),
                  pl.BlockSpec((POOL_BUF, bt, D_POOL), lambda i, g: (0, i, 0)),
                  pl.BlockSpec((bt, hs, HEAD_DK, HEAD_DV), lambda i, g: (i, g, 0, 0))],
        out_specs=(pl.BlockSpec((TAIL_ROWS, D_MODEL), rows),
                   per_tile(D_POOL), per_tile(D_V),
                   pl.BlockSpec((POOL_BUF, bt, D_POOL), lambda i, g: (0, i, 0)),
                   pl.BlockSpec((bt, hs, HEAD_DK, HEAD_DV), lambda i, g: (i, g, 0, 0))),
        scratch_shapes=[pltpu.VMEM((bt, D_V), F32)],
        compiler_params=_params("arbitrary", "arbitrary"),
        name="tail",
    )(dec, merged, x, mod, w_o, g_post, zs, rot, pool_w, pool_scale, gn_g, state_pool, state_ret)


def _out_proj_kernel(ya_ref, yb_ref, gp_ref, x_ref, gate_ref, bm_ref, wa_ref, wb_ref, wo_ref, gpost_ref, y_ref,
                     *, rows_per_mod):
    o = _merge_project(ya_ref[...], yb_ref[...], gp_ref, bm_ref, wa_ref, wb_ref, wo_ref, gpost_ref)
    gate = _mod_rows(gate_ref, slice(None), x_ref.shape[0], rows_per_mod)
    y_ref[...] = x_ref[...] + gate * o


def _out_proj(ya, yb, z, x, mod, b_merge, w_a, w_b, w_o, g_post, *, tm, rows_per_mod, mod_row0):
    m = x.shape[0]
    return pl.pallas_call(
        functools.partial(_out_proj_kernel, rows_per_mod=rows_per_mod),
        out_shape=jax.ShapeDtypeStruct((m, D_MODEL), F32),
        grid=(m // tm,),
        in_specs=[pl.BlockSpec((tm, D_POOL), lambda i: (i, 0)),
                  pl.BlockSpec((tm, D_V), lambda i: (i, 0)),
                  pl.BlockSpec((tm, 2 * D_MODEL), lambda i: (i, D_IN // (2 * D_MODEL))),
                  pl.BlockSpec((tm, D_MODEL), lambda i: (i, 0)),
                  _mod_spec(tm, rows_per_mod, mod_row0, 2),
                  _resident(b_merge.shape), _resident(w_a.shape), _resident(w_b.shape), _resident(w_o.shape),
                  _resident(g_post.shape)],
        out_specs=pl.BlockSpec((tm, D_MODEL), lambda i: (i, 0)),
        compiler_params=_params("parallel"),
        name="out_proj",
    )(ya, yb, z, x, mod, b_merge, w_a, w_b, w_o, g_post)


def _rotary_tables(start, length):
    half = HEAD_DK // 2
    inv = ROPE_BASE ** (-np.arange(half, dtype=np.float64) / half)
    ang = (start + np.arange(length, dtype=np.float64))[:, None] * inv[None, :]
    cos, sin = np.cos(ang), np.sin(ang)
    return (np.concatenate([cos, cos], axis=-1).astype(np.float32),
            np.concatenate([-sin, sin], axis=-1).astype(np.float32))


def _decay_tables(c):
    lg = np.log1p(-np.power(2.0, -5.0 - np.arange(N_HEADS, dtype=np.float64)))
    idx = np.arange(c, dtype=np.float64)
    diff = idx[:, None] - idx[None, :]
    dmask = np.where(diff[None] >= 0, np.exp(np.maximum(diff, 0.0)[None] * lg[:, None, None]), 0.0)
    q_dec = np.exp((idx + 1.0)[None, :] * lg[:, None])
    k_dec = np.exp((c - 1.0 - idx)[None, :] * lg[:, None])
    chunk_dec = np.exp(c * lg)
    return tuple(a.astype(np.float32) for a in (dmask, q_dec, k_dec, chunk_dec))


def _layer(xp, xs, c_prompt, c_sample, state_pool, state_ret, ada_w, ada_b, g_pre, g_post, w_in, pool_w, pool_scale, gn_g,
           w_a_proj, w_b_proj, w_merge, b_merge, w_out):
    batch, seq, _ = xp.shape
    dec_batch, dec_seq, _ = xs.shape
    assert dec_seq == 1 and seq % CHUNK == 0

    row = lambda v: v.reshape(1, -1)
    g_pre, g_post, pool_scale, gn_g, b_merge = map(row, (g_pre, g_post, pool_scale, gn_g, b_merge))

    mod = _modulation(c_sample, c_prompt, ada_w, row(ada_b))

    x2 = xp.reshape(batch * seq, D_MODEL)
    xs2 = xs.reshape(dec_batch, D_MODEL)
    h = _norm(x2, mod, g_pre, tm=NORM_TM, rows_per_mod=seq, mod_row0=dec_batch)
    z, zs, w_a, w_b = _in_proj(h, xs2, mod, g_pre, w_in, w_merge, w_a_proj, w_b_proj)

    cos, sin = _rotary_tables(0, seq)
    dmask, q_dec, k_dec, chunk_dec = _decay_tables(CHUNK)
    wide = lambda d: np.ascontiguousarray(np.broadcast_to(d[:, :, None], (N_HEADS, CHUNK, HEAD_DK)))
    rot = np.concatenate([cos, sin], axis=1)
    merged, pool_p, ret_p, w_o = _prompt_mix_merge(z, batch, seq, (rot, dmask, wide(q_dec), wide(k_dec), chunk_dec),
                                                   pool_w, pool_scale, gn_g, b_merge, w_a, w_b, w_out)

    cos_s, sin_s = _rotary_tables(PAST_LEN, 1)
    dmask_s, q_dec_s, k_dec_s, chunk_dec_s = _decay_tables(1)
    dec_s = np.stack([dmask_s[:, 0, 0], q_dec_s[:, 0], k_dec_s[:, 0], chunk_dec_s])
    inv_cnt = tuple(1.0 / min(PAST_LEN + 1, w) for w in POOL_WINDOWS)
    yp, ya_s, yb_s, pool_s, ret_s = _tail(merged, x2, mod, w_o, g_post, zs, jnp.transpose(state_pool, (1, 0, 2)),
                                          state_ret, (np.concatenate([cos_s, sin_s], axis=1), dec_s), inv_cnt,
                                          pool_w, pool_scale, gn_g,
                                          rows_per_mod=seq, mod_row0=dec_batch)
    pool_s = jnp.transpose(pool_s, (1, 0, 2))
    ys = _out_proj(ya_s, yb_s, zs, xs2, mod, b_merge, w_a, w_b, w_o, g_post, tm=dec_batch, rows_per_mod=1, mod_row0=0)

    return yp.reshape(xp.shape), ys.reshape(xs.shape), pool_p, ret_p, pool_s, ret_s


def kernel(x_prompt, x_sample, state_pool, state_ret, c_prompt, c_sample, ada_w, ada_b, g_pre, g_post,
           w_in, pool_w, pool_scale, gn_g, w_a_proj, w_b_proj, w_merge, b_merge, w_out):
    depth = ada_w.shape[0]
    xp, xs = x_prompt, x_sample
    pool_p, ret_p, pool_s, ret_s = [], [], [], []
    for l in range(depth):
        xp, xs, bp, sp, bs, ss = _layer(
            xp, xs, c_prompt, c_sample, state_pool[l], state_ret[l], ada_w[l], ada_b[l], g_pre[l], g_post[l], w_in[l],
            pool_w[l], pool_scale[l], gn_g[l].reshape(-1), w_a_proj[l], w_b_proj[l], w_merge[l], b_merge[l],
            w_out[l])
        pool_p.append(bp)
        ret_p.append(sp)
        pool_s.append(bs)
        ret_s.append(ss)
    return (xp, xs, jnp.stack(pool_p), jnp.stack(ret_p), jnp.stack(pool_s), jnp.stack(ret_s))
```

```python
import functools

import jax
import jax.numpy as jnp
import numpy as np
from jax import lax
from jax.experimental import pallas as pl
from jax.experimental.pallas import tpu as pltpu

F32 = jnp.float32
BF16 = jnp.bfloat16

D_MODEL = 2048
PAST_LEN = 16384
D_POOL = D_MODEL // 2
POOL_WINDOWS = (2, 4, 8, 16)
POOL_GROUP_DIM = D_POOL // len(POOL_WINDOWS)
POOL_BUF = max(POOL_WINDOWS) - 1
N_HEADS = 8
HEAD_DK = D_MODEL // 16
HEAD_DV = D_MODEL // 8
D_QK = N_HEADS * HEAD_DK
D_V = N_HEADS * HEAD_DV
CHUNK = 128
ROPE_BASE = 10000.0
EPS = 1e-6
D_IN = 2 * D_POOL + 2 * D_QK + 2 * D_V
D_Z = D_IN + 2 * D_MODEL
K_SCALE = HEAD_DK ** -0.5

VMEM_LIMIT_BYTES = 56 * 1024 * 1024
SUBLANES = 8
HALO = 16
CAST_STEPS = 32


def _params(*semantics):
    return pltpu.CompilerParams(dimension_semantics=semantics, vmem_limit_bytes=VMEM_LIMIT_BYTES)


def _cast_specs(w, step_of):
    rows = w.shape[0] // CAST_STEPS
    assert rows * CAST_STEPS == w.shape[0] and rows % (2 * SUBLANES) == 0
    spec = pl.BlockSpec((rows, w.shape[1]), lambda *idx: (jnp.minimum(step_of(*idx), CAST_STEPS - 1), 0))
    return spec, spec, jax.ShapeDtypeStruct(w.shape, BF16)


def _resident(shape):
    return pl.BlockSpec(shape, lambda *_: (0,) * len(shape), pipeline_mode=pl.Buffered(1))


def _silu(x):
    return x * jax.nn.sigmoid(x)


MOD_TN = 1536


def _mod_kernel(cs_ref, cp_ref, w_ref, b_ref, o_ref):
    ns, n_pad = cs_ref.shape[0], o_ref.shape[0] - cs_ref.shape[0] - cp_ref.shape[0]
    w = w_ref[...].astype(BF16)
    cp = jnp.concatenate([cp_ref[...], jnp.zeros((n_pad, D_MODEL), F32)], axis=0)
    o_ref[0:ns, :] = jnp.dot(_silu(cs_ref[...]).astype(BF16), w, preferred_element_type=F32) + b_ref[...]
    o_ref[ns:, :] = jnp.dot(_silu(cp).astype(BF16), w, preferred_element_type=F32) + b_ref[...]


def _modulation(c_sample, c_prompt, ada_w, ada_b):
    ns, n_p = c_sample.shape[0], c_prompt.shape[0]
    assert ns % SUBLANES == 0
    rows = ns + n_p + (-n_p) % SUBLANES
    return pl.pallas_call(
        _mod_kernel,
        out_shape=jax.ShapeDtypeStruct((rows, 3 * D_MODEL), F32),
        grid=(3 * D_MODEL // MOD_TN,),
        in_specs=[pl.BlockSpec((ns, D_MODEL), lambda j: (0, 0)),
                  pl.BlockSpec((n_p, D_MODEL), lambda j: (0, 0)),
                  pl.BlockSpec((D_MODEL, MOD_TN), lambda j: (0, j)),
                  pl.BlockSpec((1, MOD_TN), lambda j: (0, j))],
        out_specs=pl.BlockSpec((rows, MOD_TN), lambda j: (0, j)),
        compiler_params=_params("arbitrary"),
        name="modulation",
    )(c_sample, c_prompt, ada_w, ada_b)


NORM_ROWS = 128
NORM_TM = 2048


def _mod_rows(ref, rows, tm, rows_per_mod):
    if rows_per_mod == 1:
        return ref[rows, :]
    return ref[pl.ds((pl.program_id(0) * tm) // rows_per_mod, 1), :]


def _mod_spec(tm, rows_per_mod, mod_row0, col):
    if rows_per_mod == 1:
        assert mod_row0 % tm == 0
        return pl.BlockSpec((tm, D_MODEL), lambda i, *_: (mod_row0 // tm + i, col))
    assert mod_row0 % SUBLANES == 0
    return pl.BlockSpec((SUBLANES, D_MODEL), lambda i, *_: (mod_row0 // SUBLANES, col))


def _norm_mod(x, g, scale, shift):
    xn = x * lax.rsqrt(jnp.mean(x * x, axis=-1, keepdims=True) + EPS) * g
    return (xn * (1.0 + scale) + shift).astype(BF16)


def _norm_kernel(x_ref, shift_ref, scale_ref, g_ref, h_ref, *, rows_per_mod):
    tm = x_ref.shape[0]

    def body(r, carry):
        rows = pl.ds(pl.multiple_of(r * NORM_ROWS, NORM_ROWS), NORM_ROWS)
        h_ref[rows, :] = _norm_mod(x_ref[rows, :], g_ref[...], _mod_rows(scale_ref, rows, tm, rows_per_mod),
                                   _mod_rows(shift_ref, rows, tm, rows_per_mod))
        return carry
    lax.fori_loop(0, tm // NORM_ROWS, body, 0)


def _norm(x, mod, g_pre, *, tm, rows_per_mod, mod_row0):
    m = x.shape[0]
    assert rows_per_mod == 1 or m // rows_per_mod <= SUBLANES
    mod_spec = lambda col: _mod_spec(tm, rows_per_mod, mod_row0, col)
    return pl.pallas_call(
        functools.partial(_norm_kernel, rows_per_mod=rows_per_mod),
        out_shape=jax.ShapeDtypeStruct((m, D_MODEL), BF16),
        grid=(m // tm,),
        in_specs=[pl.BlockSpec((tm, D_MODEL), lambda i: (i, 0)), mod_spec(0), mod_spec(1),
                  pl.BlockSpec((1, D_MODEL), lambda i: (0, 0))],
        out_specs=pl.BlockSpec((tm, D_MODEL), lambda i: (i, 0)),
        compiler_params=_params("parallel"),
        name="norm",
    )(x, mod, mod, g_pre)


IN_TM = 1024
IN_TN = 2048
IN_DOT_COLS = 512


def _in_proj_kernel(h_ref, xs_ref, shift_s_ref, scale_s_ref, g_ref, wa_ref, wb_ref, w_in_hbm, w_mg_hbm,
                    z_ref, zs_ref, wa_bf_ref, wb_bf_ref, wbf0_ref, wbf1_ref, stage_ref, hs_ref, sem,
                    *, n_in_tiles, n_j, n_i):
    j, i = pl.program_id(0), pl.program_id(1)
    t = j * n_i + i
    total = n_j * n_i
    tn = wbf0_ref.shape[1]
    chunk = stage_ref.shape[1]

    def chunk_copy(w_hbm, col0, r, b):
        rows = pl.ds(pl.multiple_of(r * chunk, chunk), chunk)
        cols = pl.ds(col0 if isinstance(col0, int) else pl.multiple_of(col0, tn), tn)
        return pltpu.make_async_copy(w_hbm.at[rows, cols], stage_ref.at[b], sem.at[b])

    def start_chunk(g):
        g = lax.rem(jnp.asarray(g, jnp.int32), total)
        jt, r, b = lax.div(g, n_i), lax.rem(g, n_i), lax.rem(g, 2)

        @pl.when(jt < n_in_tiles)
        def _():
            chunk_copy(w_in_hbm, jt * tn, r, b).start()

        @pl.when(jt >= n_in_tiles)
        def _():
            chunk_copy(w_mg_hbm, (jt - n_in_tiles) * tn, r, b).start()

    def land_chunk(g, dst_ref):
        g = jnp.asarray(g, jnp.int32)
        r, b = lax.rem(g, n_i), lax.rem(g, 2)
        chunk_copy(w_in_hbm, 0, r, b).wait()
        dst_ref[pl.ds(pl.multiple_of(r * chunk, chunk), chunk), :] = stage_ref[b].astype(BF16)

    @pl.when(t == 0)
    def _():
        hs_ref[...] = _norm_mod(xs_ref[...], g_ref[...], scale_s_ref[...], shift_s_ref[...])
        start_chunk(0)

        def body(g, carry):
            start_chunk(g + 1)
            land_chunk(g, wbf0_ref)
            return carry
        lax.fori_loop(0, n_i, body, 0)

    @pl.when(t + 1 < total)
    def _():
        start_chunk(t + n_i + 1)

    def multiply(w_cur, w_nxt):
        land_chunk(t + n_i, w_nxt)
        wa_bf_ref[...] = wa_ref[...].astype(BF16)
        wb_bf_ref[...] = wb_ref[...].astype(BF16)
        for c0 in range(0, tn, IN_DOT_COLS):
            cols = slice(c0, c0 + IN_DOT_COLS)
            z_ref[:, cols] = jnp.dot(h_ref[...], w_cur[:, cols], preferred_element_type=F32)

        @pl.when(i == 0)
        def _():
            zs_ref[...] = jnp.dot(hs_ref[...], w_cur[...], preferred_element_type=F32)

    @pl.when(lax.rem(j, 2) == 0)
    def _():
        multiply(wbf0_ref, wbf1_ref)

    @pl.when(lax.rem(j, 2) == 1)
    def _():
        multiply(wbf1_ref, wbf0_ref)


def _in_proj(h, xs, mod, g_pre, w_in, w_mg, w_a, w_b):
    m, ms = h.shape[0], xs.shape[0]
    n_in, n_mg = w_in.shape[1] // IN_TN, w_mg.shape[1] // IN_TN
    n_i = m // IN_TM
    chunk = D_MODEL // n_i
    assert chunk * n_i == D_MODEL and chunk % 16 == 0 and (n_in + n_mg) * n_i >= CAST_STEPS
    const = lambda shape, col: pl.BlockSpec(shape, lambda j, i: (0, col), pipeline_mode=pl.Buffered(1))
    wa_in, wa_out, wa_shape = _cast_specs(w_a, lambda j, i: j * n_i + i)
    wb_in, wb_out, wb_shape = _cast_specs(w_b, lambda j, i: j * n_i + i)
    return pl.pallas_call(
        functools.partial(_in_proj_kernel, n_in_tiles=n_in, n_j=n_in + n_mg, n_i=n_i),
        out_shape=(jax.ShapeDtypeStruct((m, D_Z), F32), jax.ShapeDtypeStruct((ms, D_Z), F32), wa_shape, wb_shape),
        grid=(n_in + n_mg, n_i),
        in_specs=[pl.BlockSpec((IN_TM, D_MODEL), lambda j, i: (i, 0)),
                  const((ms, D_MODEL), 0), const((ms, D_MODEL), 0), const((ms, D_MODEL), 1),
                  const((1, D_MODEL), 0), wa_in, wb_in,
                  pl.BlockSpec(memory_space=pl.ANY), pl.BlockSpec(memory_space=pl.ANY)],
        out_specs=(pl.BlockSpec((IN_TM, IN_TN), lambda j, i: (i, j)),
                   pl.BlockSpec((ms, IN_TN), lambda j, i: (0, j)), wa_out, wb_out),
        scratch_shapes=[pltpu.VMEM((D_MODEL, IN_TN), BF16), pltpu.VMEM((D_MODEL, IN_TN), BF16),
                        pltpu.VMEM((2, chunk, IN_TN), F32),
                        pltpu.VMEM((ms, D_MODEL), BF16),
                        pltpu.SemaphoreType.DMA((2,))],
        compiler_params=_params("arbitrary", "arbitrary"),
        name="in_proj",
    )(h, xs, mod, mod, g_pre, w_a, w_b, w_in, w_mg)


def _rotate(x, cos, sin_signed):
    return x * cos + pltpu.roll(x, HEAD_DK // 2, 1) * sin_signed


def _group_norm_gate(o, gn, bg):
    mu = jnp.mean(o, axis=-1, keepdims=True)
    d = o - mu
    var = jnp.mean(d * d, axis=-1, keepdims=True)
    return d * lax.rsqrt(var + EPS) * gn * _silu(bg)


def _pool_project(pooled, pw, ps, ag):
    mixed = jnp.dot(pooled.astype(BF16), pw.astype(BF16), preferred_element_type=F32)
    return mixed * ps * _silu(ag)


PROJ_SPLIT = 8
PROJ_COLS = D_MODEL // PROJ_SPLIT


def _merge_chunk(y_a, y_b, j, gp_ref, bm_ref, wa_ref, wb_ref):
    lo, hi = j * PROJ_COLS, (j + 1) * PROJ_COLS
    ya = jnp.dot(y_a, wa_ref[:, lo:hi], preferred_element_type=F32)
    yb = jnp.dot(y_b, wb_ref[:, lo:hi], preferred_element_type=F32)
    g_a = jax.nn.sigmoid(gp_ref[:, lo:hi] + bm_ref[:, lo:hi])
    g_b = jax.nn.sigmoid(gp_ref[:, D_MODEL + lo:D_MODEL + hi] + bm_ref[:, D_MODEL + lo:D_MODEL + hi])
    return (g_a * ya + g_b * yb).astype(BF16)


def _out_chunk(merged, j, wo_ref):
    return jnp.dot(merged, wo_ref[:, j * PROJ_COLS:(j + 1) * PROJ_COLS], preferred_element_type=F32)


def _post_norm(o, gpost_ref):
    return o * lax.rsqrt(jnp.mean(o * o, axis=-1, keepdims=True) + EPS) * gpost_ref[...]


def _merge_project(y_a, y_b, gp_ref, bm_ref, wa_ref, wb_ref, wo_ref, gpost_ref):
    merged = jnp.concatenate([_merge_chunk(y_a, y_b, j, gp_ref, bm_ref, wa_ref, wb_ref)
                              for j in range(PROJ_SPLIT)], axis=1)
    o = jnp.concatenate([_out_chunk(merged, j, wo_ref) for j in range(PROJ_SPLIT)], axis=1)
    return _post_norm(o, gpost_ref)


MIX_ROWS = 2 * CHUNK
MIX_HEADS = 4
Z_AX, Z_AG, Z_Q, Z_K, Z_V, Z_BG = 0, D_POOL, 2 * D_POOL, 2 * D_POOL + D_QK, 2 * D_POOL + 2 * D_QK, D_IN - D_V


def _prompt_kernel(cdec_ref, z_ref, rot_ref, dmask_ref, qdec_ref, kdec_ref, pw_ref, ps_ref, gn_ref,
                   gp_ref, bm_ref, wa_ref, wb_ref, wo_ref,
                   m_ref, npool_ref, nret_ref, wo_bf_ref, ya0_ref, yb0_ref, ya1_ref, yb1_ref, ext_ref, *, tiles_per_seq):
    s = pl.program_id(0)
    n_tiles = pl.num_programs(0) - 1
    live = s < n_tiles
    c = lax.rem(jnp.minimum(s, n_tiles - 1), tiles_per_seq)
    rows = z_ref.shape[0]

    @pl.when(s == 0)
    def _():
        for ref in (ya0_ref, yb0_ref, ya1_ref, yb1_ref):
            ref[...] = jnp.zeros(ref.shape, BF16)

    @pl.when(c == 0)
    def _():
        ext_ref[0:HALO, :] = jnp.zeros((HALO, D_POOL), F32)
        nret_ref[...] = jnp.zeros(nret_ref.shape, F32)

    def step(ya_rd, yb_rd, ya_wr, yb_wr):
        y_a, y_b = ya_rd[...], yb_rd[...]
        wo_bf_ref[...] = wo_ref[...].astype(BF16)
        xa = z_ref[:, Z_AX:Z_AX + D_POOL]
        ext_ref[HALO:HALO + rows, :] = xa
        pos = c * rows + lax.broadcasted_iota(jnp.int32, (rows, 1), 0)
        nt = (((1,), (1,)), ((), ()))
        tn = (((0,), (0,)), ((), ()))

        def merge_piece(j):
            m_ref[:, j * PROJ_COLS:(j + 1) * PROJ_COLS] = _merge_chunk(y_a, y_b, j, gp_ref, bm_ref, wa_ref, wb_ref)

        def pool_group(g):
            w = POOL_WINDOWS[g]
            lo, hi = g * POOL_GROUP_DIM, (g + 1) * POOL_GROUP_DIM
            acc = ext_ref[:, lo:hi]
            span = 1
            while span < w:
                acc = acc + pltpu.roll(acc, span, 0)
                span *= 2
            inv_cnt = 1.0 / jnp.minimum(pos + 1, w).astype(F32)
            pooled = acc[HALO:, :] * inv_cnt - xa[:, lo:hi]
            ya = _pool_project(pooled, pw_ref[g], ps_ref[:, lo:hi], z_ref[:, Z_AG + lo:Z_AG + hi])
            ya_wr[:, lo:hi] = ya.astype(BF16)

        per_pool = PROJ_SPLIT // len(POOL_WINDOWS)
        assert per_pool * len(POOL_WINDOWS) == PROJ_SPLIT
        fillers = [f for g in range(len(POOL_WINDOWS))
                   for f in [functools.partial(merge_piece, g * per_pool + j) for j in range(per_pool)]
                   + [functools.partial(pool_group, g)]]
        n_sub = rows // CHUNK
        n_slots = 2 * (N_HEADS // MIX_HEADS) * n_sub
        slot = [0]

        def fill():
            lo, hi = (slot[0] * len(fillers)) // n_slots, ((slot[0] + 1) * len(fillers)) // n_slots
            slot[0] += 1
            for f in fillers[lo:hi]:
                f()

        for h0 in range(0, N_HEADS, MIX_HEADS):
            heads = range(h0, h0 + MIX_HEADS)
            s_cur = {h: nret_ref[h] for h in heads}
            for ci in range(n_sub):
                rs = slice(ci * CHUNK, (ci + 1) * CHUNK)
                cos = rot_ref[rs, :HEAD_DK]
                sin = rot_ref[rs, HEAD_DK:]
                q = {h: _rotate(z_ref[rs, Z_Q + h * HEAD_DK:Z_Q + (h + 1) * HEAD_DK], cos, sin) for h in heads}
                k = {h: _rotate(z_ref[rs, Z_K + h * HEAD_DK:Z_K + (h + 1) * HEAD_DK], cos, sin) * K_SCALE
                     for h in heads}
                v = {h: z_ref[rs, Z_V + h * HEAD_DV:Z_V + (h + 1) * HEAD_DV].astype(BF16) for h in heads}
                fill()
                scores = {h: lax.dot_general(q[h].astype(BF16), k[h].astype(BF16), nt, preferred_element_type=F32)
                          for h in heads}
                kv = {h: lax.dot_general((k[h] * kdec_ref[h]).astype(BF16), v[h], tn, preferred_element_type=F32)
                      for h in heads}
                lhs = {h: jnp.concatenate([(scores[h] * dmask_ref[h]).astype(BF16),
                                           (q[h] * qdec_ref[h]).astype(BF16)], axis=1) for h in heads}
                fill()
                for h in heads:
                    rhs = jnp.concatenate([v[h], s_cur[h].astype(BF16)], axis=0)
                    o = jnp.dot(lhs[h], rhs, preferred_element_type=F32)
                    s_cur[h] = cdec_ref[h] * s_cur[h] + kv[h]
                    vs = slice(h * HEAD_DV, (h + 1) * HEAD_DV)
                    bg = z_ref[rs, Z_BG + h * HEAD_DV:Z_BG + (h + 1) * HEAD_DV]
                    yb_wr[rs, vs] = _group_norm_gate(o, gn_ref[:, vs], bg).astype(BF16)
            for h in heads:
                nret_ref[h] = jnp.where(live, s_cur[h], nret_ref[h])
        assert slot[0] == n_slots

    @pl.when(lax.rem(s, 2) == 0)
    def _():
        step(ya1_ref, yb1_ref, ya0_ref, yb0_ref)

    @pl.when(lax.rem(s, 2) == 1)
    def _():
        step(ya0_ref, yb0_ref, ya1_ref, yb1_ref)

    @pl.when(c == tiles_per_seq - 1)
    def _():
        npool_ref[...] = ext_ref[HALO + rows - POOL_BUF:HALO + rows, :]

    ext_ref[0:HALO, :] = ext_ref[rows:rows + HALO, :]


def _prompt_mix_merge(z, batch, seq, tables, pool_w, pool_scale, gn_g, b_merge, w_a, w_b, w_o):
    rot, dmask, qdec, kdec, cdec = tables
    tps = seq // MIX_ROWS
    n_tiles = batch * tps
    assert n_tiles >= CAST_STEPS
    cur = lambda s: jnp.minimum(s, n_tiles - 1)
    prev = lambda s: jnp.maximum(s - 1, 0)
    m = batch * seq
    wo_in, wo_out, wo_shape = _cast_specs(w_o, lambda s: s)
    return pl.pallas_call(
        functools.partial(_prompt_kernel, tiles_per_seq=tps),
        out_shape=(jax.ShapeDtypeStruct((m, D_MODEL), BF16),
                   jax.ShapeDtypeStruct((batch, POOL_BUF, D_POOL), F32),
                   jax.ShapeDtypeStruct((batch, N_HEADS, HEAD_DK, HEAD_DV), F32), wo_shape),
        grid=(n_tiles + 1,),
        in_specs=[pl.BlockSpec(memory_space=pltpu.SMEM),
                  pl.BlockSpec((MIX_ROWS, D_IN), lambda s: (cur(s), 0)),
                  pl.BlockSpec((MIX_ROWS, 2 * HEAD_DK), lambda s: (cur(s) % tps, 0)),
                  _resident(dmask.shape), _resident(qdec.shape), _resident(kdec.shape),
                  _resident(pool_w.shape), _resident(pool_scale.shape), _resident(gn_g.shape),
                  pl.BlockSpec((MIX_ROWS, 2 * D_MODEL), lambda s: (prev(s), D_IN // (2 * D_MODEL))),
                  _resident(b_merge.shape), _resident(w_a.shape), _resident(w_b.shape), wo_in],
        out_specs=(pl.BlockSpec((MIX_ROWS, D_MODEL), lambda s: (prev(s), 0)),
                   pl.BlockSpec((None, POOL_BUF, D_POOL), lambda s: (cur(s) // tps, 0, 0)),
                   pl.BlockSpec((None, N_HEADS, HEAD_DK, HEAD_DV), lambda s: (cur(s) // tps, 0, 0, 0)), wo_out),
        scratch_shapes=[pltpu.VMEM((MIX_ROWS, D_POOL), BF16), pltpu.VMEM((MIX_ROWS, D_V), BF16),
                        pltpu.VMEM((MIX_ROWS, D_POOL), BF16), pltpu.VMEM((MIX_ROWS, D_V), BF16),
                        pltpu.VMEM((HALO + MIX_ROWS, D_POOL), F32)],
        compiler_params=_params("arbitrary"),
        name="prompt_mix_merge",
    )(cdec, z, rot, dmask, qdec, kdec, pool_w, pool_scale, gn_g, z, b_merge, w_a, w_b, w_o)


SAMPLE_TILE = 8
SAMPLE_HEADS = 4
TAIL_ROWS = 256


def _tail_kernel(dec_ref, m_ref, x_ref, gate_ref, wo_ref, gpost_ref,
                 zs_ref, rot_ref, pw_ref, ps_ref, gn_ref, spool_ref, sret_hbm,
                 y_ref, ya_ref, yb_ref, npool_ref, nret_hbm, o_ref, in_ring, out_ring, in_sem, out_sem,
                 *, inv_cnt, rows_per_mod, n_steps):
    bt = zs_ref.shape[0]
    n_hg = N_HEADS // SAMPLE_HEADS
    assert n_hg % 2 == 0
    t = pl.program_id(0) * n_hg + pl.program_id(1)

    def state_block(hbm, step):
        seq0 = pl.multiple_of(lax.div(step, n_hg) * bt, bt)
        head0 = pl.multiple_of(lax.rem(step, n_hg) * SAMPLE_HEADS, SAMPLE_HEADS)
        return hbm.at[pl.ds(seq0, bt), pl.ds(head0, SAMPLE_HEADS)]

    def in_copy(step):
        step = jnp.asarray(step, jnp.int32)
        slot = lax.rem(step, 3)
        return pltpu.make_async_copy(state_block(sret_hbm, step), in_ring.at[slot], in_sem.at[slot])

    def out_copy(step):
        step = jnp.asarray(step, jnp.int32)
        slot = lax.rem(step, 2)
        return pltpu.make_async_copy(out_ring.at[slot], state_block(nret_hbm, step), out_sem.at[slot])

    @pl.when(t == 0)
    def _():
        in_copy(0).start()
        in_copy(1).start()

    in_copy(t).wait()

    @pl.when(t + 2 < n_steps)
    def _():
        in_copy(t + 2).start()

    @pl.when(t >= 2)
    def _():
        out_copy(t - 2).wait()

    s_in = in_ring.at[lax.rem(t, 3)]

    def step(hg):
        s_out = out_ring.at[hg % 2]
        o = _post_norm(jnp.dot(m_ref[...], wo_ref[...], preferred_element_type=F32), gpost_ref)
        tile = pl.program_id(0) * n_hg + hg
        gate = gate_ref[pl.ds((tile * x_ref.shape[0]) // rows_per_mod, 1), :]
        y_ref[...] = x_ref[...] + gate * o

        if hg == 0:
            xa = zs_ref[:, Z_AX:Z_AX + D_POOL]
            run = xa
            wins = {1: xa}
            for j in range(1, POOL_BUF + 1):
                run = run + spool_ref[POOL_BUF - j]
                wins[j + 1] = run
            for g, w in enumerate(POOL_WINDOWS):
                lo, hi = g * POOL_GROUP_DIM, (g + 1) * POOL_GROUP_DIM
                pooled = wins[w][:, lo:hi] * inv_cnt[g] - xa[:, lo:hi]
                ya = _pool_project(pooled, pw_ref[g], ps_ref[:, lo:hi], zs_ref[:, Z_AG + lo:Z_AG + hi])
                ya_ref[:, lo:hi] = ya.astype(BF16)
            for j in range(POOL_BUF - 1):
                npool_ref[j] = spool_ref[j + 1]
            npool_ref[POOL_BUF - 1] = xa

        cos = rot_ref[:, :HEAD_DK]
        sin = rot_ref[:, HEAD_DK:]
        for hl in range(SAMPLE_HEADS):
            h = hg * SAMPLE_HEADS + hl
            vs = slice(h * HEAD_DV, (h + 1) * HEAD_DV)
            q = _rotate(zs_ref[:, Z_Q + h * HEAD_DK:Z_Q + (h + 1) * HEAD_DK], cos, sin)
            k = _rotate(zs_ref[:, Z_K + h * HEAD_DK:Z_K + (h + 1) * HEAD_DK], cos, sin) * K_SCALE
            v = zs_ref[:, Z_V + h * HEAD_DV:Z_V + (h + 1) * HEAD_DV]
            score = jnp.sum(q * k, axis=1, keepdims=True) * dec_ref[0, h]
            q_cols = jnp.transpose(q * dec_ref[1, h])
            k_cols = jnp.transpose(k * dec_ref[2, h])
            for r in range(bt):
                s_old = s_in[r, hl]
                v_row = v[r:r + 1, :]
                o_row = score[r:r + 1, :] * v_row + jnp.sum(q_cols[:, r:r + 1] * s_old, axis=0, keepdims=True)
                s_out[r, hl] = dec_ref[3, h] * s_old + k_cols[:, r:r + 1] * v_row
                o_ref[r:r + 1, vs] = o_row
        for hl in range(SAMPLE_HEADS):
            h = hg * SAMPLE_HEADS + hl
            vs = slice(h * HEAD_DV, (h + 1) * HEAD_DV)
            bg = zs_ref[:, Z_BG + h * HEAD_DV:Z_BG + (h + 1) * HEAD_DV]
            yb_ref[:, vs] = _group_norm_gate(o_ref[:, vs], gn_ref[:, vs], bg).astype(BF16)

    for hg in range(n_hg):
        pl.when(pl.program_id(1) == hg)(functools.partial(step, hg))

    out_copy(t).start()

    @pl.when(t == n_steps - 1)
    def _():
        out_copy(t - 1).wait()
        out_copy(t).wait()


def _tail(merged, x, mod, w_o, g_post, zs, state_pool, state_ret, tables, inv_cnt, pool_w, pool_scale, gn_g,
          *, rows_per_mod, mod_row0):
    rot, dec = tables
    m, batch = x.shape[0], zs.shape[0]
    bt, hs = SAMPLE_TILE, SAMPLE_HEADS
    n_hg = N_HEADS // hs
    assert m == (batch // bt) * n_hg * TAIL_ROWS and mod_row0 % SUBLANES == 0 and m // rows_per_mod <= SUBLANES
    rows = lambda i, g: (i * n_hg + g, 0)
    per_tile = lambda width: pl.BlockSpec((bt, width), lambda i, g: (i, 0))
    return pl.pallas_call(
        functools.partial(_tail_kernel, inv_cnt=inv_cnt, rows_per_mod=rows_per_mod, n_steps=(batch // bt) * n_hg),
        out_shape=(jax.ShapeDtypeStruct((m, D_MODEL), F32),
                   jax.ShapeDtypeStruct((batch, D_POOL), BF16),
                   jax.ShapeDtypeStruct((batch, D_V), BF16),
                   jax.ShapeDtypeStruct(state_pool.shape, state_pool.dtype),
                   jax.ShapeDtypeStruct(state_ret.shape, state_ret.dtype)),
        grid=(batch // bt, n_hg),
        in_specs=[pl.BlockSpec(memory_space=pltpu.SMEM),
                  pl.BlockSpec((TAIL_ROWS, D_MODEL), rows), pl.BlockSpec((TAIL_ROWS, D_MODEL), rows),
                  pl.BlockSpec((SUBLANES, D_MODEL), lambda i, g: (mod_row0 // SUBLANES, 2)),
                  _resident(w_o.shape), _resident(g_post.shape),
                  per_tile(D_IN), _resident(rot.shape),
                  _resident(pool_w.shape), _resident(pool_scale.shape), _resident(gn_g.shape),
                  pl.BlockSpec((POOL_BUF, bt, D_POOL), lambda i, g: (0, i, 0)),
                  pl.BlockSpec(memory_space=pl.ANY)],
        out_specs=(pl.BlockSpec((TAIL_ROWS, D_MODEL), rows),
                   per_tile(D_POOL), per_tile(D_V),
                   pl.BlockSpec((POOL_BUF, bt, D_POOL), lambda i, g: (0, i, 0)),
                   pl.BlockSpec(memory_space=pl.ANY)),
        scratch_shapes=[pltpu.VMEM((bt, D_V), F32),
                        pltpu.VMEM((3, bt, hs, HEAD_DK, HEAD_DV), F32), pltpu.VMEM((2, bt, hs, HEAD_DK, HEAD_DV), F32),
                        pltpu.SemaphoreType.DMA((3,)), pltpu.SemaphoreType.DMA((2,))],
        compiler_params=_params("arbitrary", "arbitrary"),
        name="tail",
    )(dec, merged, x, mod, w_o, g_post, zs, rot, pool_w, pool_scale, gn_g, state_pool, state_ret)


def _out_proj_kernel(ya_ref, yb_ref, gp_ref, x_ref, gate_ref, bm_ref, wa_ref, wb_ref, wo_ref, gpost_ref, y_ref,
                     *, rows_per_mod):
    o = _merge_project(ya_ref[...], yb_ref[...], gp_ref, bm_ref, wa_ref, wb_ref, wo_ref, gpost_ref)
    gate = _mod_rows(gate_ref, slice(None), x_ref.shape[0], rows_per_mod)
    y_ref[...] = x_ref[...] + gate * o


def _out_proj(ya, yb, z, x, mod, b_merge, w_a, w_b, w_o, g_post, *, tm, rows_per_mod, mod_row0):
    m = x.shape[0]
    return pl.pallas_call(
        functools.partial(_out_proj_kernel, rows_per_mod=rows_per_mod),
        out_shape=jax.ShapeDtypeStruct((m, D_MODEL), F32),
        grid=(m // tm,),
        in_specs=[pl.BlockSpec((tm, D_POOL), lambda i: (i, 0)),
                  pl.BlockSpec((tm, D_V), lambda i: (i, 0)),
                  pl.BlockSpec((tm, 2 * D_MODEL), lambda i: (i, D_IN // (2 * D_MODEL))),
                  pl.BlockSpec((tm, D_MODEL), lambda i: (i, 0)),
                  _mod_spec(tm, rows_per_mod, mod_row0, 2),
                  _resident(b_merge.shape), _resident(w_a.shape), _resident(w_b.shape), _resident(w_o.shape),
                  _resident(g_post.shape)],
        out_specs=pl.BlockSpec((tm, D_MODEL), lambda i: (i, 0)),
        compiler_params=_params("parallel"),
        name="out_proj",
    )(ya, yb, z, x, mod, b_merge, w_a, w_b, w_o, g_post)


def _rotary_tables(start, length):
    half = HEAD_DK // 2
    inv = ROPE_BASE ** (-np.arange(half, dtype=np.float64) / half)
    ang = (start + np.arange(length, dtype=np.float64))[:, None] * inv[None, :]
    cos, sin = np.cos(ang), np.sin(ang)
    return (np.concatenate([cos, cos], axis=-1).astype(np.float32),
            np.concatenate([-sin, sin], axis=-1).astype(np.float32))


def _decay_tables(c):
    lg = np.log1p(-np.power(2.0, -5.0 - np.arange(N_HEADS, dtype=np.float64)))
    idx = np.arange(c, dtype=np.float64)
    diff = idx[:, None] - idx[None, :]
    dmask = np.where(diff[None] >= 0, np.exp(np.maximum(diff, 0.0)[None] * lg[:, None, None]), 0.0)
    q_dec = np.exp((idx + 1.0)[None, :] * lg[:, None])
    k_dec = np.exp((c - 1.0 - idx)[None, :] * lg[:, None])
    chunk_dec = np.exp(c * lg)
    return tuple(a.astype(np.float32) for a in (dmask, q_dec, k_dec, chunk_dec))


def _layer(xp, xs, c_prompt, c_sample, state_pool, state_ret, ada_w, ada_b, g_pre, g_post, w_in, pool_w, pool_scale, gn_g,
           w_a_proj, w_b_proj, w_merge, b_merge, w_out):
    batch, seq, _ = xp.shape
    dec_batch, dec_seq, _ = xs.shape
    assert dec_seq == 1 and seq % CHUNK == 0

    row = lambda v: v.reshape(1, -1)
    g_pre, g_post, pool_scale, gn_g, b_merge = map(row, (g_pre, g_post, pool_scale, gn_g, b_merge))

    mod = _modulation(c_sample, c_prompt, ada_w, row(ada_b))

    x2 = xp.reshape(batch * seq, D_MODEL)
    xs2 = xs.reshape(dec_batch, D_MODEL)
    h = _norm(x2, mod, g_pre, tm=NORM_TM, rows_per_mod=seq, mod_row0=dec_batch)
    z, zs, w_a, w_b = _in_proj(h, xs2, mod, g_pre, w_in, w_merge, w_a_proj, w_b_proj)

    cos, sin = _rotary_tables(0, seq)
    dmask, q_dec, k_dec, chunk_dec = _decay_tables(CHUNK)
    wide = lambda d: np.ascontiguousarray(np.broadcast_to(d[:, :, None], (N_HEADS, CHUNK, HEAD_DK)))
    rot = np.concatenate([cos, sin], axis=1)
    merged, pool_p, ret_p, w_o = _prompt_mix_merge(z, batch, seq, (rot, dmask, wide(q_dec), wide(k_dec), chunk_dec),
                                                   pool_w, pool_scale, gn_g, b_merge, w_a, w_b, w_out)

    cos_s, sin_s = _rotary_tables(PAST_LEN, 1)
    dmask_s, q_dec_s, k_dec_s, chunk_dec_s = _decay_tables(1)
    dec_s = np.stack([dmask_s[:, 0, 0], q_dec_s[:, 0], k_dec_s[:, 0], chunk_dec_s])
    inv_cnt = tuple(1.0 / min(PAST_LEN + 1, w) for w in POOL_WINDOWS)
    yp, ya_s, yb_s, pool_s, ret_s = _tail(merged, x2, mod, w_o, g_post, zs, jnp.transpose(state_pool, (1, 0, 2)),
                                          state_ret, (np.concatenate([cos_s, sin_s], axis=1), dec_s), inv_cnt,
                                          pool_w, pool_scale, gn_g,
                                          rows_per_mod=seq, mod_row0=dec_batch)
    pool_s = jnp.transpose(pool_s, (1, 0, 2))
    ys = _out_proj(ya_s, yb_s, zs, xs2, mod, b_merge, w_a, w_b, w_o, g_post, tm=dec_batch, rows_per_mod=1, mod_row0=0)

    return yp.reshape(xp.shape), ys.reshape(xs.shape), pool_p, ret_p, pool_s, ret_s


def kernel(x_prompt, x_sample, state_pool, state_ret, c_prompt, c_sample, ada_w, ada_b, g_pre, g_post,
           w_in, pool_w, pool_scale, gn_g, w_a_proj, w_b_proj, w_merge, b_merge, w_out):
    depth = ada_w.shape[0]
    xp, xs = x_prompt, x_sample
    pool_p, ret_p, pool_s, ret_s = [], [], [], []
    for l in range(depth):
        xp, xs, bp, sp, bs, ss = _layer(
            xp, xs, c_prompt, c_sample, state_pool[l], state_ret[l], ada_w[l], ada_b[l], g_pre[l], g_post[l], w_in[l],
            pool_w[l], pool_scale[l], gn_g[l].reshape(-1), w_a_proj[l], w_b_proj[l], w_merge[l], b_merge[l],
            w_out[l])
        pool_p.append(bp)
        ret_p.append(sp)
        pool_s.append(bs)
        ret_s.append(ss)
    return (xp, xs, jnp.stack(pool_p), jnp.stack(ret_p), jnp.stack(pool_s), jnp.stack(ret_s))
```

```python
import functools

import jax
import jax.numpy as jnp
import numpy as np
from jax import lax
from jax.experimental import pallas as pl
from jax.experimental.pallas import tpu as pltpu

F32 = jnp.float32
BF16 = jnp.bfloat16

D_MODEL = 2048
PAST_LEN = 16384
D_POOL = D_MODEL // 2
POOL_WINDOWS = (2, 4, 8, 16)
POOL_GROUP_DIM = D_POOL // len(POOL_WINDOWS)
POOL_BUF = max(POOL_WINDOWS) - 1
N_HEADS = 8
HEAD_DK = D_MODEL // 16
HEAD_DV = D_MODEL // 8
D_QK = N_HEADS * HEAD_DK
D_V = N_HEADS * HEAD_DV
CHUNK = 128
ROPE_BASE = 10000.0
EPS = 1e-6
D_IN = 2 * D_POOL + 2 * D_QK + 2 * D_V
D_Z = D_IN + 2 * D_MODEL
K_SCALE = HEAD_DK ** -0.5

VMEM_LIMIT_BYTES = 56 * 1024 * 1024
SUBLANES = 8
HALO = 16
CAST_STEPS = 32


def _params(*semantics):
    return pltpu.CompilerParams(dimension_semantics=semantics, vmem_limit_bytes=VMEM_LIMIT_BYTES)


def _cast_specs(w, step_of):
    rows = w.shape[0] // CAST_STEPS
    assert rows * CAST_STEPS == w.shape[0] and rows % (2 * SUBLANES) == 0
    spec = pl.BlockSpec((rows, w.shape[1]), lambda *idx: (jnp.minimum(step_of(*idx), CAST_STEPS - 1), 0))
    return spec, spec, jax.ShapeDtypeStruct(w.shape, BF16)


def _resident(shape):
    return pl.BlockSpec(shape, lambda *_: (0,) * len(shape), pipeline_mode=pl.Buffered(1))


def _silu(x):
    return x * jax.nn.sigmoid(x)


MOD_TN = 1536


def _mod_kernel(cs_ref, cp_ref, w_ref, b_ref, o_ref):
    ns, n_pad = cs_ref.shape[0], o_ref.shape[0] - cs_ref.shape[0] - cp_ref.shape[0]
    w = w_ref[...].astype(BF16)
    cp = jnp.concatenate([cp_ref[...], jnp.zeros((n_pad, D_MODEL), F32)], axis=0)
    o_ref[0:ns, :] = jnp.dot(_silu(cs_ref[...]).astype(BF16), w, preferred_element_type=F32) + b_ref[...]
    o_ref[ns:, :] = jnp.dot(_silu(cp).astype(BF16), w, preferred_element_type=F32) + b_ref[...]


def _modulation(c_sample, c_prompt, ada_w, ada_b):
    ns, n_p = c_sample.shape[0], c_prompt.shape[0]
    assert ns % SUBLANES == 0
    rows = ns + n_p + (-n_p) % SUBLANES
    return pl.pallas_call(
        _mod_kernel,
        out_shape=jax.ShapeDtypeStruct((rows, 3 * D_MODEL), F32),
        grid=(3 * D_MODEL // MOD_TN,),
        in_specs=[pl.BlockSpec((ns, D_MODEL), lambda j: (0, 0)),
                  pl.BlockSpec((n_p, D_MODEL), lambda j: (0, 0)),
                  pl.BlockSpec((D_MODEL, MOD_TN), lambda j: (0, j)),
                  pl.BlockSpec((1, MOD_TN), lambda j: (0, j))],
        out_specs=pl.BlockSpec((rows, MOD_TN), lambda j: (0, j)),
        compiler_params=_params("arbitrary"),
        name="modulation",
    )(c_sample, c_prompt, ada_w, ada_b)


NORM_ROWS = 128
NORM_TM = 512


def _mod_rows(ref, rows, tm, rows_per_mod):
    if rows_per_mod == 1:
        return ref[rows, :]
    return ref[pl.ds((pl.program_id(0) * tm) // rows_per_mod, 1), :]


def _mod_spec(tm, rows_per_mod, mod_row0, col):
    if rows_per_mod == 1:
        assert mod_row0 % tm == 0
        return pl.BlockSpec((tm, D_MODEL), lambda i, *_: (mod_row0 // tm + i, col))
    assert mod_row0 % SUBLANES == 0
    return pl.BlockSpec((SUBLANES, D_MODEL), lambda i, *_: (mod_row0 // SUBLANES, col))


def _norm_mod(x, g, scale, shift):
    xn = x * lax.rsqrt(jnp.mean(x * x, axis=-1, keepdims=True) + EPS) * g
    return (xn * (1.0 + scale) + shift).astype(BF16)


def _norm_kernel(x_hbm, shift_ref, scale_ref, g_ref, h_hbm, x_ring, h_ring, in_sem, out_sem,
                 *, rows_per_mod, n_steps):
    tm = x_ring.shape[1]
    t = pl.program_id(0)

    def rows_of(step):
        return pl.ds(pl.multiple_of(step * tm, tm), tm)

    def in_copy(step):
        step = jnp.asarray(step, jnp.int32)
        slot = lax.rem(step, 3)
        return pltpu.make_async_copy(x_hbm.at[rows_of(step)], x_ring.at[slot], in_sem.at[slot])

    def out_copy(step):
        step = jnp.asarray(step, jnp.int32)
        slot = lax.rem(step, 2)
        return pltpu.make_async_copy(h_ring.at[slot], h_hbm.at[rows_of(step)], out_sem.at[slot])

    @pl.when(t == 0)
    def _():
        in_copy(0).start()
        in_copy(1).start()

    in_copy(t).wait()

    @pl.when(t + 2 < n_steps)
    def _():
        in_copy(t + 2).start()

    @pl.when(t >= 2)
    def _():
        out_copy(t - 2).wait()

    x_ref, h_ref = x_ring.at[lax.rem(t, 3)], h_ring.at[lax.rem(t, 2)]

    def body(r, carry):
        rows = pl.ds(pl.multiple_of(r * NORM_ROWS, NORM_ROWS), NORM_ROWS)
        h_ref[rows, :] = _norm_mod(x_ref[rows, :], g_ref[...], _mod_rows(scale_ref, rows, tm, rows_per_mod),
                                   _mod_rows(shift_ref, rows, tm, rows_per_mod))
        return carry
    lax.fori_loop(0, tm // NORM_ROWS, body, 0)

    out_copy(t).start()

    @pl.when(t == n_steps - 1)
    def _():
        out_copy(t - 1).wait()
        out_copy(t).wait()


def _norm(x, mod, g_pre, *, tm, rows_per_mod, mod_row0):
    m = x.shape[0]
    n_steps = m // tm
    assert (rows_per_mod == 1 or m // rows_per_mod <= SUBLANES) and n_steps >= 2
    mod_spec = lambda col: _mod_spec(tm, rows_per_mod, mod_row0, col)
    return pl.pallas_call(
        functools.partial(_norm_kernel, rows_per_mod=rows_per_mod, n_steps=n_steps),
        out_shape=jax.ShapeDtypeStruct((m, D_MODEL), BF16),
        grid=(n_steps,),
        in_specs=[pl.BlockSpec(memory_space=pl.ANY), mod_spec(0), mod_spec(1),
                  pl.BlockSpec((1, D_MODEL), lambda i: (0, 0))],
        out_specs=pl.BlockSpec(memory_space=pl.ANY),
        scratch_shapes=[pltpu.VMEM((3, tm, D_MODEL), F32), pltpu.VMEM((2, tm, D_MODEL), BF16),
                        pltpu.SemaphoreType.DMA((3,)), pltpu.SemaphoreType.DMA((2,))],
        compiler_params=_params("arbitrary"),
        name="norm",
    )(x, mod, mod, g_pre)


IN_TM = 1024
IN_TN = 2048
IN_DOT_COLS = 512


def _in_proj_kernel(h_ref, xs_ref, shift_s_ref, scale_s_ref, g_ref, wa_ref, wb_ref, w_in_hbm, w_mg_hbm,
                    z_ref, zs_ref, wa_bf_ref, wb_bf_ref, wbf0_ref, wbf1_ref, stage_ref, hs_ref, sem,
                    *, n_in_tiles, n_j, n_i):
    j, i = pl.program_id(0), pl.program_id(1)
    t = j * n_i + i
    total = n_j * n_i
    tn = wbf0_ref.shape[1]
    chunk = stage_ref.shape[1]

    def chunk_copy(w_hbm, col0, r, b):
        rows = pl.ds(pl.multiple_of(r * chunk, chunk), chunk)
        cols = pl.ds(col0 if isinstance(col0, int) else pl.multiple_of(col0, tn), tn)
        return pltpu.make_async_copy(w_hbm.at[rows, cols], stage_ref.at[b], sem.at[b])

    def start_chunk(g):
        g = lax.rem(jnp.asarray(g, jnp.int32), total)
        jt, r, b = lax.div(g, n_i), lax.rem(g, n_i), lax.rem(g, 2)

        @pl.when(jt < n_in_tiles)
        def _():
            chunk_copy(w_in_hbm, jt * tn, r, b).start()

        @pl.when(jt >= n_in_tiles)
        def _():
            chunk_copy(w_mg_hbm, (jt - n_in_tiles) * tn, r, b).start()

    def land_chunk(g, dst_ref):
        g = jnp.asarray(g, jnp.int32)
        r, b = lax.rem(g, n_i), lax.rem(g, 2)
        chunk_copy(w_in_hbm, 0, r, b).wait()
        dst_ref[pl.ds(pl.multiple_of(r * chunk, chunk), chunk), :] = stage_ref[b].astype(BF16)

    @pl.when(t == 0)
    def _():
        hs_ref[...] = _norm_mod(xs_ref[...], g_ref[...], scale_s_ref[...], shift_s_ref[...])
        start_chunk(0)

        def body(g, carry):
            start_chunk(g + 1)
            land_chunk(g, wbf0_ref)
            return carry
        lax.fori_loop(0, n_i, body, 0)

    @pl.when(t + 1 < total)
    def _():
        start_chunk(t + n_i + 1)

    def multiply(w_cur, w_nxt):
        land_chunk(t + n_i, w_nxt)
        wa_bf_ref[...] = wa_ref[...].astype(BF16)
        wb_bf_ref[...] = wb_ref[...].astype(BF16)
        for c0 in range(0, tn, IN_DOT_COLS):
            cols = slice(c0, c0 + IN_DOT_COLS)
            z_ref[:, cols] = jnp.dot(h_ref[...], w_cur[:, cols], preferred_element_type=F32)

        @pl.when(i == 0)
        def _():
            zs_ref[...] = jnp.dot(hs_ref[...], w_cur[...], preferred_element_type=F32)

    @pl.when(lax.rem(j, 2) == 0)
    def _():
        multiply(wbf0_ref, wbf1_ref)

    @pl.when(lax.rem(j, 2) == 1)
    def _():
        multiply(wbf1_ref, wbf0_ref)


def _in_proj(h, xs, mod, g_pre, w_in, w_mg, w_a, w_b):
    m, ms = h.shape[0], xs.shape[0]
    n_in, n_mg = w_in.shape[1] // IN_TN, w_mg.shape[1] // IN_TN
    n_i = m // IN_TM
    chunk = D_MODEL // n_i
    assert chunk * n_i == D_MODEL and chunk % 16 == 0 and (n_in + n_mg) * n_i >= CAST_STEPS
    const = lambda shape, col: pl.BlockSpec(shape, lambda j, i: (0, col), pipeline_mode=pl.Buffered(1))
    wa_in, wa_out, wa_shape = _cast_specs(w_a, lambda j, i: j * n_i + i)
    wb_in, wb_out, wb_shape = _cast_specs(w_b, lambda j, i: j * n_i + i)
    return pl.pallas_call(
        functools.partial(_in_proj_kernel, n_in_tiles=n_in, n_j=n_in + n_mg, n_i=n_i),
        out_shape=(jax.ShapeDtypeStruct((m, D_Z), F32), jax.ShapeDtypeStruct((ms, D_Z), F32), wa_shape, wb_shape),
        grid=(n_in + n_mg, n_i),
        in_specs=[pl.BlockSpec((IN_TM, D_MODEL), lambda j, i: (i, 0)),
                  const((ms, D_MODEL), 0), const((ms, D_MODEL), 0), const((ms, D_MODEL), 1),
                  const((1, D_MODEL), 0), wa_in, wb_in,
                  pl.BlockSpec(memory_space=pl.ANY), pl.BlockSpec(memory_space=pl.ANY)],
        out_specs=(pl.BlockSpec((IN_TM, IN_TN), lambda j, i: (i, j)),
                   pl.BlockSpec((ms, IN_TN), lambda j, i: (0, j)), wa_out, wb_out),
        scratch_shapes=[pltpu.VMEM((D_MODEL, IN_TN), BF16), pltpu.VMEM((D_MODEL, IN_TN), BF16),
                        pltpu.VMEM((2, chunk, IN_TN), F32),
                        pltpu.VMEM((ms, D_MODEL), BF16),
                        pltpu.SemaphoreType.DMA((2,))],
        compiler_params=_params("arbitrary", "arbitrary"),
        name="in_proj",
    )(h, xs, mod, mod, g_pre, w_a, w_b, w_in, w_mg)


def _rotate(x, cos, sin_signed):
    return x * cos + pltpu.roll(x, HEAD_DK // 2, 1) * sin_signed


def _group_norm_gate(o, gn, bg):
    mu = jnp.mean(o, axis=-1, keepdims=True)
    d = o - mu
    var = jnp.mean(d * d, axis=-1, keepdims=True)
    return d * lax.rsqrt(var + EPS) * gn * _silu(bg)


def _pool_project(pooled, pw, ps, ag):
    mixed = jnp.dot(pooled.astype(BF16), pw.astype(BF16), preferred_element_type=F32)
    return mixed * ps * _silu(ag)


PROJ_SPLIT = 8
PROJ_COLS = D_MODEL // PROJ_SPLIT


def _merge_chunk(y_a, y_b, j, gp_ref, bm_ref, wa_ref, wb_ref):
    lo, hi = j * PROJ_COLS, (j + 1) * PROJ_COLS
    ya = jnp.dot(y_a, wa_ref[:, lo:hi], preferred_element_type=F32)
    yb = jnp.dot(y_b, wb_ref[:, lo:hi], preferred_element_type=F32)
    g_a = jax.nn.sigmoid(gp_ref[:, lo:hi] + bm_ref[:, lo:hi])
    g_b = jax.nn.sigmoid(gp_ref[:, D_MODEL + lo:D_MODEL + hi] + bm_ref[:, D_MODEL + lo:D_MODEL + hi])
    return (g_a * ya + g_b * yb).astype(BF16)


def _out_chunk(merged, j, wo_ref):
    return jnp.dot(merged, wo_ref[:, j * PROJ_COLS:(j + 1) * PROJ_COLS], preferred_element_type=F32)


def _post_norm(o, gpost_ref):
    return o * lax.rsqrt(jnp.mean(o * o, axis=-1, keepdims=True) + EPS) * gpost_ref[...]


def _merge_project(y_a, y_b, gp_ref, bm_ref, wa_ref, wb_ref, wo_ref, gpost_ref):
    merged = jnp.concatenate([_merge_chunk(y_a, y_b, j, gp_ref, bm_ref, wa_ref, wb_ref)
                              for j in range(PROJ_SPLIT)], axis=1)
    o = jnp.concatenate([_out_chunk(merged, j, wo_ref) for j in range(PROJ_SPLIT)], axis=1)
    return _post_norm(o, gpost_ref)


MIX_ROWS = 2 * CHUNK
MIX_HEADS = 4
Z_AX, Z_AG, Z_Q, Z_K, Z_V, Z_BG = 0, D_POOL, 2 * D_POOL, 2 * D_POOL + D_QK, 2 * D_POOL + 2 * D_QK, D_IN - D_V


def _prompt_kernel(cdec_ref, z_ref, rot_ref, dmask_ref, qdec_ref, kdec_ref, pw_ref, ps_ref, gn_ref,
                   gp_ref, bm_ref, wa_ref, wb_ref, wo_ref,
                   m_ref, npool_ref, nret_ref, wo_bf_ref, ya0_ref, yb0_ref, ya1_ref, yb1_ref, ext_ref, *, tiles_per_seq):
    s = pl.program_id(0)
    n_tiles = pl.num_programs(0) - 1
    live = s < n_tiles
    c = lax.rem(jnp.minimum(s, n_tiles - 1), tiles_per_seq)
    rows = z_ref.shape[0]

    @pl.when(s == 0)
    def _():
        for ref in (ya0_ref, yb0_ref, ya1_ref, yb1_ref):
            ref[...] = jnp.zeros(ref.shape, BF16)

    @pl.when(c == 0)
    def _():
        ext_ref[0:HALO, :] = jnp.zeros((HALO, D_POOL), F32)
        nret_ref[...] = jnp.zeros(nret_ref.shape, F32)

    def step(ya_rd, yb_rd, ya_wr, yb_wr):
        y_a, y_b = ya_rd[...], yb_rd[...]
        wo_bf_ref[...] = wo_ref[...].astype(BF16)
        xa = z_ref[:, Z_AX:Z_AX + D_POOL]
        ext_ref[HALO:HALO + rows, :] = xa
        pos = c * rows + lax.broadcasted_iota(jnp.int32, (rows, 1), 0)
        nt = (((1,), (1,)), ((), ()))
        tn = (((0,), (0,)), ((), ()))

        def merge_piece(j):
            m_ref[:, j * PROJ_COLS:(j + 1) * PROJ_COLS] = _merge_chunk(y_a, y_b, j, gp_ref, bm_ref, wa_ref, wb_ref)

        def pool_group(g):
            w = POOL_WINDOWS[g]
            lo, hi = g * POOL_GROUP_DIM, (g + 1) * POOL_GROUP_DIM
            acc = ext_ref[:, lo:hi]
            span = 1
            while span < w:
                acc = acc + pltpu.roll(acc, span, 0)
                span *= 2
            inv_cnt = 1.0 / jnp.minimum(pos + 1, w).astype(F32)
            pooled = acc[HALO:, :] * inv_cnt - xa[:, lo:hi]
            ya = _pool_project(pooled, pw_ref[g], ps_ref[:, lo:hi], z_ref[:, Z_AG + lo:Z_AG + hi])
            ya_wr[:, lo:hi] = ya.astype(BF16)

        per_pool = PROJ_SPLIT // len(POOL_WINDOWS)
        assert per_pool * len(POOL_WINDOWS) == PROJ_SPLIT
        fillers = [f for g in range(len(POOL_WINDOWS))
                   for f in [functools.partial(merge_piece, g * per_pool + j) for j in range(per_pool)]
                   + [functools.partial(pool_group, g)]]
        n_sub = rows // CHUNK
        n_slots = 2 * (N_HEADS // MIX_HEADS) * n_sub
        slot = [0]

        def fill():
            lo, hi = (slot[0] * len(fillers)) // n_slots, ((slot[0] + 1) * len(fillers)) // n_slots
            slot[0] += 1
            for f in fillers[lo:hi]:
                f()

        for h0 in range(0, N_HEADS, MIX_HEADS):
            heads = range(h0, h0 + MIX_HEADS)
            s_cur = {h: nret_ref[h] for h in heads}
            for ci in range(n_sub):
                rs = slice(ci * CHUNK, (ci + 1) * CHUNK)
                cos = rot_ref[rs, :HEAD_DK]
                sin = rot_ref[rs, HEAD_DK:]
                q = {h: _rotate(z_ref[rs, Z_Q + h * HEAD_DK:Z_Q + (h + 1) * HEAD_DK], cos, sin) for h in heads}
                k = {h: _rotate(z_ref[rs, Z_K + h * HEAD_DK:Z_K + (h + 1) * HEAD_DK], cos, sin) * K_SCALE
                     for h in heads}
                v = {h: z_ref[rs, Z_V + h * HEAD_DV:Z_V + (h + 1) * HEAD_DV].astype(BF16) for h in heads}
                fill()
                scores = {h: lax.dot_general(q[h].astype(BF16), k[h].astype(BF16), nt, preferred_element_type=F32)
                          for h in heads}
                kv = {h: lax.dot_general((k[h] * kdec_ref[h]).astype(BF16), v[h], tn, preferred_element_type=F32)
                      for h in heads}
                lhs = {h: jnp.concatenate([(scores[h] * dmask_ref[h]).astype(BF16),
                                           (q[h] * qdec_ref[h]).astype(BF16)], axis=1) for h in heads}
                fill()
                for h in heads:
                    rhs = jnp.concatenate([v[h], s_cur[h].astype(BF16)], axis=0)
                    o = jnp.dot(lhs[h], rhs, preferred_element_type=F32)
                    s_cur[h] = cdec_ref[h] * s_cur[h] + kv[h]
                    vs = slice(h * HEAD_DV, (h + 1) * HEAD_DV)
                    bg = z_ref[rs, Z_BG + h * HEAD_DV:Z_BG + (h + 1) * HEAD_DV]
                    yb_wr[rs, vs] = _group_norm_gate(o, gn_ref[:, vs], bg).astype(BF16)
            for h in heads:
                nret_ref[h] = jnp.where(live, s_cur[h], nret_ref[h])
        assert slot[0] == n_slots

    @pl.when(lax.rem(s, 2) == 0)
    def _():
        step(ya1_ref, yb1_ref, ya0_ref, yb0_ref)

    @pl.when(lax.rem(s, 2) == 1)
    def _():
        step(ya0_ref, yb0_ref, ya1_ref, yb1_ref)

    @pl.when(c == tiles_per_seq - 1)
    def _():
        npool_ref[...] = ext_ref[HALO + rows - POOL_BUF:HALO + rows, :]

    ext_ref[0:HALO, :] = ext_ref[rows:rows + HALO, :]


def _prompt_mix_merge(z, batch, seq, tables, pool_w, pool_scale, gn_g, b_merge, w_a, w_b, w_o):
    rot, dmask, qdec, kdec, cdec = tables
    tps = seq // MIX_ROWS
    n_tiles = batch * tps
    assert n_tiles >= CAST_STEPS
    cur = lambda s: jnp.minimum(s, n_tiles - 1)
    prev = lambda s: jnp.maximum(s - 1, 0)
    m = batch * seq
    wo_in, wo_out, wo_shape = _cast_specs(w_o, lambda s: s)
    return pl.pallas_call(
        functools.partial(_prompt_kernel, tiles_per_seq=tps),
        out_shape=(jax.ShapeDtypeStruct((m, D_MODEL), BF16),
                   jax.ShapeDtypeStruct((batch, POOL_BUF, D_POOL), F32),
                   jax.ShapeDtypeStruct((batch, N_HEADS, HEAD_DK, HEAD_DV), F32), wo_shape),
        grid=(n_tiles + 1,),
        in_specs=[pl.BlockSpec(memory_space=pltpu.SMEM),
                  pl.BlockSpec((MIX_ROWS, D_IN), lambda s: (cur(s), 0)),
                  pl.BlockSpec((MIX_ROWS, 2 * HEAD_DK), lambda s: (cur(s) % tps, 0)),
                  _resident(dmask.shape), _resident(qdec.shape), _resident(kdec.shape),
                  _resident(pool_w.shape), _resident(pool_scale.shape), _resident(gn_g.shape),
                  pl.BlockSpec((MIX_ROWS, 2 * D_MODEL), lambda s: (prev(s), D_IN // (2 * D_MODEL))),
                  _resident(b_merge.shape), _resident(w_a.shape), _resident(w_b.shape), wo_in],
        out_specs=(pl.BlockSpec((MIX_ROWS, D_MODEL), lambda s: (prev(s), 0)),
                   pl.BlockSpec((None, POOL_BUF, D_POOL), lambda s: (cur(s) // tps, 0, 0)),
                   pl.BlockSpec((None, N_HEADS, HEAD_DK, HEAD_DV), lambda s: (cur(s) // tps, 0, 0, 0)), wo_out),
        scratch_shapes=[pltpu.VMEM((MIX_ROWS, D_POOL), BF16), pltpu.VMEM((MIX_ROWS, D_V), BF16),
                        pltpu.VMEM((MIX_ROWS, D_POOL), BF16), pltpu.VMEM((MIX_ROWS, D_V), BF16),
                        pltpu.VMEM((HALO + MIX_ROWS, D_POOL), F32)],
        compiler_params=_params("arbitrary"),
        name="prompt_mix_merge",
    )(cdec, z, rot, dmask, qdec, kdec, pool_w, pool_scale, gn_g, z, b_merge, w_a, w_b, w_o)


SAMPLE_TILE = 8
SAMPLE_HEADS = 4
TAIL_ROWS = 256


def _tail_kernel(dec_ref, m_ref, x_ref, gate_ref, wo_ref, gpost_ref,
                 zs_ref, rot_ref, pw_ref, ps_ref, gn_ref, spool_ref, sret_hbm,
                 y_ref, ya_ref, yb_ref, npool_ref, nret_hbm, o_ref, in_ring, out_ring, in_sem, out_sem,
                 *, inv_cnt, rows_per_mod, n_steps):
    bt = zs_ref.shape[0]
    n_hg = N_HEADS // SAMPLE_HEADS
    assert n_hg % 2 == 0
    t = pl.program_id(0) * n_hg + pl.program_id(1)

    def state_block(hbm, step):
        seq0 = pl.multiple_of(lax.div(step, n_hg) * bt, bt)
        head0 = pl.multiple_of(lax.rem(step, n_hg) * SAMPLE_HEADS, SAMPLE_HEADS)
        return hbm.at[pl.ds(seq0, bt), pl.ds(head0, SAMPLE_HEADS)]

    def in_copy(step):
        step = jnp.asarray(step, jnp.int32)
        slot = lax.rem(step, 3)
        return pltpu.make_async_copy(state_block(sret_hbm, step), in_ring.at[slot], in_sem.at[slot])

    def out_copy(step):
        step = jnp.asarray(step, jnp.int32)
        slot = lax.rem(step, 2)
        return pltpu.make_async_copy(out_ring.at[slot], state_block(nret_hbm, step), out_sem.at[slot])

    @pl.when(t == 0)
    def _():
        in_copy(0).start()
        in_copy(1).start()

    in_copy(t).wait()

    @pl.when(t + 2 < n_steps)
    def _():
        in_copy(t + 2).start()

    @pl.when(t >= 2)
    def _():
        out_copy(t - 2).wait()

    s_in = in_ring.at[lax.rem(t, 3)]

    def step(hg):
        s_out = out_ring.at[hg % 2]
        o = _post_norm(jnp.dot(m_ref[...], wo_ref[...], preferred_element_type=F32), gpost_ref)
        tile = pl.program_id(0) * n_hg + hg
        gate = gate_ref[pl.ds((tile * x_ref.shape[0]) // rows_per_mod, 1), :]
        y_ref[...] = x_ref[...] + gate * o

        if hg == 0:
            xa = zs_ref[:, Z_AX:Z_AX + D_POOL]
            run = xa
            wins = {1: xa}
            for j in range(1, POOL_BUF + 1):
                run = run + spool_ref[POOL_BUF - j]
                wins[j + 1] = run
            for g, w in enumerate(POOL_WINDOWS):
                lo, hi = g * POOL_GROUP_DIM, (g + 1) * POOL_GROUP_DIM
                pooled = wins[w][:, lo:hi] * inv_cnt[g] - xa[:, lo:hi]
                ya = _pool_project(pooled, pw_ref[g], ps_ref[:, lo:hi], zs_ref[:, Z_AG + lo:Z_AG + hi])
                ya_ref[:, lo:hi] = ya.astype(BF16)
            for j in range(POOL_BUF - 1):
                npool_ref[j] = spool_ref[j + 1]
            npool_ref[POOL_BUF - 1] = xa

        cos = rot_ref[:, :HEAD_DK]
        sin = rot_ref[:, HEAD_DK:]
        for hl in range(SAMPLE_HEADS):
            h = hg * SAMPLE_HEADS + hl
            vs = slice(h * HEAD_DV, (h + 1) * HEAD_DV)
            q = _rotate(zs_ref[:, Z_Q + h * HEAD_DK:Z_Q + (h + 1) * HEAD_DK], cos, sin)
            k = _rotate(zs_ref[:, Z_K + h * HEAD_DK:Z_K + (h + 1) * HEAD_DK], cos, sin) * K_SCALE
            v = zs_ref[:, Z_V + h * HEAD_DV:Z_V + (h + 1) * HEAD_DV]
            score = jnp.sum(q * k, axis=1, keepdims=True) * dec_ref[0, h]
            q_cols = jnp.transpose(q * dec_ref[1, h])
            k_cols = jnp.transpose(k * dec_ref[2, h])
            for r in range(bt):
                s_old = s_in[r, hl]
                v_row = v[r:r + 1, :]
                o_row = score[r:r + 1, :] * v_row + jnp.sum(q_cols[:, r:r + 1] * s_old, axis=0, keepdims=True)
                s_out[r, hl] = dec_ref[3, h] * s_old + k_cols[:, r:r + 1] * v_row
                o_ref[r:r + 1, vs] = o_row
        for hl in range(SAMPLE_HEADS):
            h = hg * SAMPLE_HEADS + hl
            vs = slice(h * HEAD_DV, (h + 1) * HEAD_DV)
            bg = zs_ref[:, Z_BG + h * HEAD_DV:Z_BG + (h + 1) * HEAD_DV]
            yb_ref[:, vs] = _group_norm_gate(o_ref[:, vs], gn_ref[:, vs], bg).astype(BF16)

    for hg in range(n_hg):
        pl.when(pl.program_id(1) == hg)(functools.partial(step, hg))

    out_copy(t).start()

    @pl.when(t == n_steps - 1)
    def _():
        out_copy(t - 1).wait()
        out_copy(t).wait()


def _tail(merged, x, mod, w_o, g_post, zs, state_pool, state_ret, tables, inv_cnt, pool_w, pool_scale, gn_g,
          *, rows_per_mod, mod_row0):
    rot, dec = tables
    m, batch = x.shape[0], zs.shape[0]
    bt, hs = SAMPLE_TILE, SAMPLE_HEADS
    n_hg = N_HEADS // hs
    assert m == (batch // bt) * n_hg * TAIL_ROWS and mod_row0 % SUBLANES == 0 and m // rows_per_mod <= SUBLANES
    rows = lambda i, g: (i * n_hg + g, 0)
    per_tile = lambda width: pl.BlockSpec((bt, width), lambda i, g: (i, 0))
    return pl.pallas_call(
        functools.partial(_tail_kernel, inv_cnt=inv_cnt, rows_per_mod=rows_per_mod, n_steps=(batch // bt) * n_hg),
        out_shape=(jax.ShapeDtypeStruct((m, D_MODEL), F32),
                   jax.ShapeDtypeStruct((batch, D_POOL), BF16),
                   jax.ShapeDtypeStruct((batch, D_V), BF16),
                   jax.ShapeDtypeStruct(state_pool.shape, state_pool.dtype),
                   jax.ShapeDtypeStruct(state_ret.shape, state_ret.dtype)),
        grid=(batch // bt, n_hg),
        in_specs=[pl.BlockSpec(memory_space=pltpu.SMEM),
                  pl.BlockSpec((TAIL_ROWS, D_MODEL), rows), pl.BlockSpec((TAIL_ROWS, D_MODEL), rows),
                  pl.BlockSpec((SUBLANES, D_MODEL), lambda i, g: (mod_row0 // SUBLANES, 2)),
                  _resident(w_o.shape), _resident(g_post.shape),
                  per_tile(D_IN), _resident(rot.shape),
                  _resident(pool_w.shape), _resident(pool_scale.shape), _resident(gn_g.shape),
                  pl.BlockSpec((POOL_BUF, bt, D_POOL), lambda i, g: (0, i, 0)),
                  pl.BlockSpec(memory_space=pl.ANY)],
        out_specs=(pl.BlockSpec((TAIL_ROWS, D_MODEL), rows),
                   per_tile(D_POOL), per_tile(D_V),
                   pl.BlockSpec((POOL_BUF, bt, D_POOL), lambda i, g: (0, i, 0)),
                   pl.BlockSpec(memory_space=pl.ANY)),
        scratch_shapes=[pltpu.VMEM((bt, D_V), F32),
                        pltpu.VMEM((3, bt, hs, HEAD_DK, HEAD_DV), F32), pltpu.VMEM((2, bt, hs, HEAD_DK, HEAD_DV), F32),
                        pltpu.SemaphoreType.DMA((3,)), pltpu.SemaphoreType.DMA((2,))],
        compiler_params=_params("arbitrary", "arbitrary"),
        name="tail",
    )(dec, merged, x, mod, w_o, g_post, zs, rot, pool_w, pool_scale, gn_g, state_pool, state_ret)


def _out_proj_kernel(ya_ref, yb_ref, gp_ref, x_ref, gate_ref, bm_ref, wa_ref, wb_ref, wo_ref, gpost_ref, y_ref,
                     *, rows_per_mod):
    o = _merge_project(ya_ref[...], yb_ref[...], gp_ref, bm_ref, wa_ref, wb_ref, wo_ref, gpost_ref)
    gate = _mod_rows(gate_ref, slice(None), x_ref.shape[0], rows_per_mod)
    y_ref[...] = x_ref[...] + gate * o


def _out_proj(ya, yb, z, x, mod, b_merge, w_a, w_b, w_o, g_post, *, tm, rows_per_mod, mod_row0):
    m = x.shape[0]
    return pl.pallas_call(
        functools.partial(_out_proj_kernel, rows_per_mod=rows_per_mod),
        out_shape=jax.ShapeDtypeStruct((m, D_MODEL), F32),
        grid=(m // tm,),
        in_specs=[pl.BlockSpec((tm, D_POOL), lambda i: (i, 0)),
                  pl.BlockSpec((tm, D_V), lambda i: (i, 0)),
                  pl.BlockSpec((tm, 2 * D_MODEL), lambda i: (i, D_IN // (2 * D_MODEL))),
                  pl.BlockSpec((tm, D_MODEL), lambda i: (i, 0)),
                  _mod_spec(tm, rows_per_mod, mod_row0, 2),
                  _resident(b_merge.shape), _resident(w_a.shape), _resident(w_b.shape), _resident(w_o.shape),
                  _resident(g_post.shape)],
        out_specs=pl.BlockSpec((tm, D_MODEL), lambda i: (i, 0)),
        compiler_params=_params("parallel"),
        name="out_proj",
    )(ya, yb, z, x, mod, b_merge, w_a, w_b, w_o, g_post)


def _rotary_tables(start, length):
    half = HEAD_DK // 2
    inv = ROPE_BASE ** (-np.arange(half, dtype=np.float64) / half)
    ang = (start + np.arange(length, dtype=np.float64))[:, None] * inv[None, :]
    cos, sin = np.cos(ang), np.sin(ang)
    return (np.concatenate([cos, cos], axis=-1).astype(np.float32),
            np.concatenate([-sin, sin], axis=-1).astype(np.float32))


def _decay_tables(c):
    lg = np.log1p(-np.power(2.0, -5.0 - np.arange(N_HEADS, dtype=np.float64)))
    idx = np.arange(c, dtype=np.float64)
    diff = idx[:, None] - idx[None, :]
    dmask = np.where(diff[None] >= 0, np.exp(np.maximum(diff, 0.0)[None] * lg[:, None, None]), 0.0)
    q_dec = np.exp((idx + 1.0)[None, :] * lg[:, None])
    k_dec = np.exp((c - 1.0 - idx)[None, :] * lg[:, None])
    chunk_dec = np.exp(c * lg)
    return tuple(a.astype(np.float32) for a in (dmask, q_dec, k_dec, chunk_dec))


def _layer(xp, xs, c_prompt, c_sample, state_pool, state_ret, ada_w, ada_b, g_pre, g_post, w_in, pool_w, pool_scale, gn_g,
           w_a_proj, w_b_proj, w_merge, b_merge, w_out):
    batch, seq, _ = xp.shape
    dec_batch, dec_seq, _ = xs.shape
    assert dec_seq == 1 and seq % CHUNK == 0

    row = lambda v: v.reshape(1, -1)
    g_pre, g_post, pool_scale, gn_g, b_merge = map(row, (g_pre, g_post, pool_scale, gn_g, b_merge))

    mod = _modulation(c_sample, c_prompt, ada_w, row(ada_b))

    x2 = xp.reshape(batch * seq, D_MODEL)
    xs2 = xs.reshape(dec_batch, D_MODEL)
    h = _norm(x2, mod, g_pre, tm=NORM_TM, rows_per_mod=seq, mod_row0=dec_batch)
    z, zs, w_a, w_b = _in_proj(h, xs2, mod, g_pre, w_in, w_merge, w_a_proj, w_b_proj)

    cos, sin = _rotary_tables(0, seq)
    dmask, q_dec, k_dec, chunk_dec = _decay_tables(CHUNK)
    wide = lambda d: np.ascontiguousarray(np.broadcast_to(d[:, :, None], (N_HEADS, CHUNK, HEAD_DK)))
    rot = np.concatenate([cos, sin], axis=1)
    merged, pool_p, ret_p, w_o = _prompt_mix_merge(z, batch, seq, (rot, dmask, wide(q_dec), wide(k_dec), chunk_dec),
                                                   pool_w, pool_scale, gn_g, b_merge, w_a, w_b, w_out)

    cos_s, sin_s = _rotary_tables(PAST_LEN, 1)
    dmask_s, q_dec_s, k_dec_s, chunk_dec_s = _decay_tables(1)
    dec_s = np.stack([dmask_s[:, 0, 0], q_dec_s[:, 0], k_dec_s[:, 0], chunk_dec_s])
    inv_cnt = tuple(1.0 / min(PAST_LEN + 1, w) for w in POOL_WINDOWS)
    yp, ya_s, yb_s, pool_s, ret_s = _tail(merged, x2, mod, w_o, g_post, zs, jnp.transpose(state_pool, (1, 0, 2)),
                                          state_ret, (np.concatenate([cos_s, sin_s], axis=1), dec_s), inv_cnt,
                                          pool_w, pool_scale, gn_g,
                                          rows_per_mod=seq, mod_row0=dec_batch)
    pool_s = jnp.transpose(pool_s, (1, 0, 2))
    ys = _out_proj(ya_s, yb_s, zs, xs2, mod, b_merge, w_a, w_b, w_o, g_post, tm=dec_batch, rows_per_mod=1, mod_row0=0)

    return yp.reshape(xp.shape), ys.reshape(xs.shape), pool_p, ret_p, pool_s, ret_s


def kernel(x_prompt, x_sample, state_pool, state_ret, c_prompt, c_sample, ada_w, ada_b, g_pre, g_post,
           w_in, pool_w, pool_scale, gn_g, w_a_proj, w_b_proj, w_merge, b_merge, w_out):
    depth = ada_w.shape[0]
    xp, xs = x_prompt, x_sample
    pool_p, ret_p, pool_s, ret_s = [], [], [], []
    for l in range(depth):
        xp, xs, bp, sp, bs, ss = _layer(
            xp, xs, c_prompt, c_sample, state_pool[l], state_ret[l], ada_w[l], ada_b[l], g_pre[l], g_post[l], w_in[l],
            pool_w[l], pool_scale[l], gn_g[l].reshape(-1), w_a_proj[l], w_b_proj[l], w_merge[l], b_merge[l],
            w_out[l])
        pool_p.append(bp)
        ret_p.append(sp)
        pool_s.append(bs)
        ret_s.append(ss)
    return (xp, xs, jnp.stack(pool_p), jnp.stack(ret_p), jnp.stack(pool_s), jnp.stack(ret_s))
```

```python
import functools

import jax
import jax.numpy as jnp
import numpy as np
from jax import lax
from jax.experimental import pallas as pl
from jax.experimental.pallas import tpu as pltpu

F32 = jnp.float32
BF16 = jnp.bfloat16

D_MODEL = 2048
PAST_LEN = 16384
D_POOL = D_MODEL // 2
POOL_WINDOWS = (2, 4, 8, 16)
POOL_GROUP_DIM = D_POOL // len(POOL_WINDOWS)
POOL_BUF = max(POOL_WINDOWS) - 1
N_HEADS = 8
HEAD_DK = D_MODEL // 16
HEAD_DV = D_MODEL // 8
D_QK = N_HEADS * HEAD_DK
D_V = N_HEADS * HEAD_DV
CHUNK = 128
ROPE_BASE = 10000.0
EPS = 1e-6
D_IN = 2 * D_POOL + 2 * D_QK + 2 * D_V
D_Z = D_IN + 2 * D_MODEL
K_SCALE = HEAD_DK ** -0.5

VMEM_LIMIT_BYTES = 56 * 1024 * 1024
SUBLANES = 8
HALO = 16
CAST_STEPS = 32


def _params(*semantics):
    return pltpu.CompilerParams(dimension_semantics=semantics, vmem_limit_bytes=VMEM_LIMIT_BYTES)


def _cast_specs(w, step_of):
    rows = w.shape[0] // CAST_STEPS
    assert rows * CAST_STEPS == w.shape[0] and rows % (2 * SUBLANES) == 0
    spec = pl.BlockSpec((rows, w.shape[1]), lambda *idx: (jnp.minimum(step_of(*idx), CAST_STEPS - 1), 0))
    return spec, spec, jax.ShapeDtypeStruct(w.shape, BF16)


def _resident(shape):
    return pl.BlockSpec(shape, lambda *_: (0,) * len(shape), pipeline_mode=pl.Buffered(1))


def _silu(x):
    return x * jax.nn.sigmoid(x)


MOD_TN = 512


def _mod_kernel(cs_ref, cp_ref, w_hbm, b_ref, o_ref, w_ring, sem, *, n_steps):
    t = pl.program_id(0)
    tn = w_ring.shape[2]

    def in_copy(step):
        step = jnp.asarray(step, jnp.int32)
        slot = lax.rem(step, 3)
        cols = pl.ds(pl.multiple_of(step * tn, tn), tn)
        return pltpu.make_async_copy(w_hbm.at[:, cols], w_ring.at[slot], sem.at[slot])

    @pl.when(t == 0)
    def _():
        in_copy(0).start()
        in_copy(1).start()

    in_copy(t).wait()

    @pl.when(t + 2 < n_steps)
    def _():
        in_copy(t + 2).start()

    n_pad = o_ref.shape[0] - cs_ref.shape[0] - cp_ref.shape[0]
    c = jnp.concatenate([cs_ref[...], cp_ref[...], jnp.zeros((n_pad, D_MODEL), F32)], axis=0)
    w = w_ring[lax.rem(t, 3)].astype(BF16)
    o_ref[...] = jnp.dot(_silu(c).astype(BF16), w, preferred_element_type=F32) + b_ref[...]


def _modulation(c_sample, c_prompt, ada_w, ada_b):
    ns, n_p = c_sample.shape[0], c_prompt.shape[0]
    assert ns % SUBLANES == 0
    rows = ns + n_p + (-n_p) % SUBLANES
    n_steps = 3 * D_MODEL // MOD_TN
    return pl.pallas_call(
        functools.partial(_mod_kernel, n_steps=n_steps),
        out_shape=jax.ShapeDtypeStruct((rows, 3 * D_MODEL), F32),
        grid=(n_steps,),
        in_specs=[pl.BlockSpec((ns, D_MODEL), lambda j: (0, 0)),
                  pl.BlockSpec((n_p, D_MODEL), lambda j: (0, 0)),
                  pl.BlockSpec(memory_space=pl.ANY),
                  pl.BlockSpec((1, MOD_TN), lambda j: (0, j))],
        out_specs=pl.BlockSpec((rows, MOD_TN), lambda j: (0, j)),
        scratch_shapes=[pltpu.VMEM((3, D_MODEL, MOD_TN), F32), pltpu.SemaphoreType.DMA((3,))],
        compiler_params=_params("arbitrary"),
        name="modulation",
    )(c_sample, c_prompt, ada_w, ada_b)


NORM_ROWS = 128
NORM_TM = 512


def _mod_rows(ref, rows, tm, rows_per_mod):
    if rows_per_mod == 1:
        return ref[rows, :]
    return ref[pl.ds((pl.program_id(0) * tm) // rows_per_mod, 1), :]


def _mod_spec(tm, rows_per_mod, mod_row0, col):
    if rows_per_mod == 1:
        assert mod_row0 % tm == 0
        return pl.BlockSpec((tm, D_MODEL), lambda i, *_: (mod_row0 // tm + i, col))
    assert mod_row0 % SUBLANES == 0
    return pl.BlockSpec((SUBLANES, D_MODEL), lambda i, *_: (mod_row0 // SUBLANES, col))


def _norm_mod(x, g, scale, shift):
    xn = x * lax.rsqrt(jnp.mean(x * x, axis=-1, keepdims=True) + EPS) * g
    return (xn * (1.0 + scale) + shift).astype(BF16)


def _norm_kernel(x_hbm, shift_ref, scale_ref, g_ref, h_hbm, x_ring, h_ring, in_sem, out_sem,
                 *, rows_per_mod, n_steps):
    tm = x_ring.shape[1]
    t = pl.program_id(0)

    def rows_of(step):
        return pl.ds(pl.multiple_of(step * tm, tm), tm)

    def in_copy(step):
        step = jnp.asarray(step, jnp.int32)
        slot = lax.rem(step, 3)
        return pltpu.make_async_copy(x_hbm.at[rows_of(step)], x_ring.at[slot], in_sem.at[slot])

    def out_copy(step):
        step = jnp.asarray(step, jnp.int32)
        slot = lax.rem(step, 2)
        return pltpu.make_async_copy(h_ring.at[slot], h_hbm.at[rows_of(step)], out_sem.at[slot])

    @pl.when(t == 0)
    def _():
        in_copy(0).start()
        in_copy(1).start()

    in_copy(t).wait()

    @pl.when(t + 2 < n_steps)
    def _():
        in_copy(t + 2).start()

    @pl.when(t >= 2)
    def _():
        out_copy(t - 2).wait()

    x_ref, h_ref = x_ring.at[lax.rem(t, 3)], h_ring.at[lax.rem(t, 2)]

    def body(r, carry):
        rows = pl.ds(pl.multiple_of(r * NORM_ROWS, NORM_ROWS), NORM_ROWS)
        h_ref[rows, :] = _norm_mod(x_ref[rows, :], g_ref[...], _mod_rows(scale_ref, rows, tm, rows_per_mod),
                                   _mod_rows(shift_ref, rows, tm, rows_per_mod))
        return carry
    lax.fori_loop(0, tm // NORM_ROWS, body, 0)

    out_copy(t).start()

    @pl.when(t == n_steps - 1)
    def _():
        out_copy(t - 1).wait()
        out_copy(t).wait()


def _norm(x, mod, g_pre, *, tm, rows_per_mod, mod_row0):
    m = x.shape[0]
    n_steps = m // tm
    assert (rows_per_mod == 1 or m // rows_per_mod <= SUBLANES) and n_steps >= 2
    mod_spec = lambda col: _mod_spec(tm, rows_per_mod, mod_row0, col)
    return pl.pallas_call(
        functools.partial(_norm_kernel, rows_per_mod=rows_per_mod, n_steps=n_steps),
        out_shape=jax.ShapeDtypeStruct((m, D_MODEL), BF16),
        grid=(n_steps,),
        in_specs=[pl.BlockSpec(memory_space=pl.ANY), mod_spec(0), mod_spec(1),
                  pl.BlockSpec((1, D_MODEL), lambda i: (0, 0))],
        out_specs=pl.BlockSpec(memory_space=pl.ANY),
        scratch_shapes=[pltpu.VMEM((3, tm, D_MODEL), F32), pltpu.VMEM((2, tm, D_MODEL), BF16),
                        pltpu.SemaphoreType.DMA((3,)), pltpu.SemaphoreType.DMA((2,))],
        compiler_params=_params("arbitrary"),
        name="norm",
    )(x, mod, mod, g_pre)


IN_TM = 1024
IN_TN = 2048
IN_DOT_COLS = 512


def _in_proj_kernel(h_ref, xs_ref, shift_s_ref, scale_s_ref, g_ref, wa_ref, wb_ref, w_in_hbm, w_mg_hbm,
                    z_ref, zs_ref, wa_bf_ref, wb_bf_ref, wbf0_ref, wbf1_ref, stage_ref, hs_ref, sem,
                    *, n_in_tiles, n_j, n_i):
    j, i = pl.program_id(0), pl.program_id(1)
    t = j * n_i + i
    total = n_j * n_i
    tn = wbf0_ref.shape[1]
    chunk = stage_ref.shape[1]

    def chunk_copy(w_hbm, col0, r, b):
        rows = pl.ds(pl.multiple_of(r * chunk, chunk), chunk)
        cols = pl.ds(col0 if isinstance(col0, int) else pl.multiple_of(col0, tn), tn)
        return pltpu.make_async_copy(w_hbm.at[rows, cols], stage_ref.at[b], sem.at[b])

    def start_chunk(g):
        g = lax.rem(jnp.asarray(g, jnp.int32), total)
        jt, r, b = lax.div(g, n_i), lax.rem(g, n_i), lax.rem(g, 2)

        @pl.when(jt < n_in_tiles)
        def _():
            chunk_copy(w_in_hbm, jt * tn, r, b).start()

        @pl.when(jt >= n_in_tiles)
        def _():
            chunk_copy(w_mg_hbm, (jt - n_in_tiles) * tn, r, b).start()

    def land_chunk(g, dst_ref):
        g = jnp.asarray(g, jnp.int32)
        r, b = lax.rem(g, n_i), lax.rem(g, 2)
        chunk_copy(w_in_hbm, 0, r, b).wait()
        dst_ref[pl.ds(pl.multiple_of(r * chunk, chunk), chunk), :] = stage_ref[b].astype(BF16)

    @pl.when(t == 0)
    def _():
        hs_ref[...] = _norm_mod(xs_ref[...], g_ref[...], scale_s_ref[...], shift_s_ref[...])
        start_chunk(0)

        def body(g, carry):
            start_chunk(g + 1)
            land_chunk(g, wbf0_ref)
            return carry
        lax.fori_loop(0, n_i, body, 0)

    @pl.when(t + 1 < total)
    def _():
        start_chunk(t + n_i + 1)

    def multiply(w_cur, w_nxt):
        land_chunk(t + n_i, w_nxt)
        wa_bf_ref[...] = wa_ref[...].astype(BF16)
        wb_bf_ref[...] = wb_ref[...].astype(BF16)
        for c0 in range(0, tn, IN_DOT_COLS):
            cols = slice(c0, c0 + IN_DOT_COLS)
            z_ref[:, cols] = jnp.dot(h_ref[...], w_cur[:, cols], preferred_element_type=F32)

        @pl.when(i == 0)
        def _():
            zs_ref[...] = jnp.dot(hs_ref[...], w_cur[...], preferred_element_type=F32)

    @pl.when(lax.rem(j, 2) == 0)
    def _():
        multiply(wbf0_ref, wbf1_ref)

    @pl.when(lax.rem(j, 2) == 1)
    def _():
        multiply(wbf1_ref, wbf0_ref)


def _in_proj(h, xs, mod, g_pre, w_in, w_mg, w_a, w_b):
    m, ms = h.shape[0], xs.shape[0]
    n_in, n_mg = w_in.shape[1] // IN_TN, w_mg.shape[1] // IN_TN
    n_i = m // IN_TM
    chunk = D_MODEL // n_i
    assert chunk * n_i == D_MODEL and chunk % 16 == 0 and (n_in + n_mg) * n_i >= CAST_STEPS
    const = lambda shape, col: pl.BlockSpec(shape, lambda j, i: (0, col), pipeline_mode=pl.Buffered(1))
    wa_in, wa_out, wa_shape = _cast_specs(w_a, lambda j, i: j * n_i + i)
    wb_in, wb_out, wb_shape = _cast_specs(w_b, lambda j, i: j * n_i + i)
    return pl.pallas_call(
        functools.partial(_in_proj_kernel, n_in_tiles=n_in, n_j=n_in + n_mg, n_i=n_i),
        out_shape=(jax.ShapeDtypeStruct((m, D_Z), F32), jax.ShapeDtypeStruct((ms, D_Z), F32), wa_shape, wb_shape),
        grid=(n_in + n_mg, n_i),
        in_specs=[pl.BlockSpec((IN_TM, D_MODEL), lambda j, i: (i, 0)),
                  const((ms, D_MODEL), 0), const((ms, D_MODEL), 0), const((ms, D_MODEL), 1),
                  const((1, D_MODEL), 0), wa_in, wb_in,
                  pl.BlockSpec(memory_space=pl.ANY), pl.BlockSpec(memory_space=pl.ANY)],
        out_specs=(pl.BlockSpec((IN_TM, IN_TN), lambda j, i: (i, j)),
                   pl.BlockSpec((ms, IN_TN), lambda j, i: (0, j)), wa_out, wb_out),
        scratch_shapes=[pltpu.VMEM((D_MODEL, IN_TN), BF16), pltpu.VMEM((D_MODEL, IN_TN), BF16),
                        pltpu.VMEM((2, chunk, IN_TN), F32),
                        pltpu.VMEM((ms, D_MODEL), BF16),
                        pltpu.SemaphoreType.DMA((2,))],
        compiler_params=_params("arbitrary", "arbitrary"),
        name="in_proj",
    )(h, xs, mod, mod, g_pre, w_a, w_b, w_in, w_mg)


def _rotate(x, cos, sin_signed):
    return x * cos + pltpu.roll(x, HEAD_DK // 2, 1) * sin_signed


def _group_norm_gate(o, gn, bg):
    mu = jnp.mean(o, axis=-1, keepdims=True)
    d = o - mu
    var = jnp.mean(d * d, axis=-1, keepdims=True)
    return d * lax.rsqrt(var + EPS) * gn * _silu(bg)


def _pool_project(pooled, pw, ps, ag):
    mixed = jnp.dot(pooled.astype(BF16), pw.astype(BF16), preferred_element_type=F32)
    return mixed * ps * _silu(ag)


PROJ_SPLIT = 8
PROJ_COLS = D_MODEL // PROJ_SPLIT


def _merge_chunk(y_a, y_b, j, gp_ref, bm_ref, wa_ref, wb_ref):
    lo, hi = j * PROJ_COLS, (j + 1) * PROJ_COLS
    ya = jnp.dot(y_a, wa_ref[:, lo:hi], preferred_element_type=F32)
    yb = jnp.dot(y_b, wb_ref[:, lo:hi], preferred_element_type=F32)
    g_a = jax.nn.sigmoid(gp_ref[:, lo:hi] + bm_ref[:, lo:hi])
    g_b = jax.nn.sigmoid(gp_ref[:, D_MODEL + lo:D_MODEL + hi] + bm_ref[:, D_MODEL + lo:D_MODEL + hi])
    return (g_a * ya + g_b * yb).astype(BF16)


def _out_chunk(merged, j, wo_ref):
    return jnp.dot(merged, wo_ref[:, j * PROJ_COLS:(j + 1) * PROJ_COLS], preferred_element_type=F32)


def _post_norm(o, gpost_ref):
    return o * lax.rsqrt(jnp.mean(o * o, axis=-1, keepdims=True) + EPS) * gpost_ref[...]


def _merge_project(y_a, y_b, gp_ref, bm_ref, wa_ref, wb_ref, wo_ref, gpost_ref):
    merged = jnp.concatenate([_merge_chunk(y_a, y_b, j, gp_ref, bm_ref, wa_ref, wb_ref)
                              for j in range(PROJ_SPLIT)], axis=1)
    o = jnp.concatenate([_out_chunk(merged, j, wo_ref) for j in range(PROJ_SPLIT)], axis=1)
    return _post_norm(o, gpost_ref)


MIX_ROWS = 2 * CHUNK
MIX_HEADS = 4
Z_AX, Z_AG, Z_Q, Z_K, Z_V, Z_BG = 0, D_POOL, 2 * D_POOL, 2 * D_POOL + D_QK, 2 * D_POOL + 2 * D_QK, D_IN - D_V


def _prompt_kernel(cdec_ref, z_ref, rot_ref, dmask_ref, qdec_ref, kdec_ref, pw_ref, ps_ref, gn_ref,
                   gp_ref, bm_ref, wa_ref, wb_ref, wo_ref,
                   m_ref, npool_ref, nret_ref, wo_bf_ref, ya0_ref, yb0_ref, ya1_ref, yb1_ref, ext_ref, *, tiles_per_seq):
    s = pl.program_id(0)
    n_tiles = pl.num_programs(0) - 1
    live = s < n_tiles
    c = lax.rem(jnp.minimum(s, n_tiles - 1), tiles_per_seq)
    rows = z_ref.shape[0]

    @pl.when(s == 0)
    def _():
        for ref in (ya0_ref, yb0_ref, ya1_ref, yb1_ref):
            ref[...] = jnp.zeros(ref.shape, BF16)

    @pl.when(c == 0)
    def _():
        ext_ref[0:HALO, :] = jnp.zeros((HALO, D_POOL), F32)
        nret_ref[...] = jnp.zeros(nret_ref.shape, F32)

    def step(ya_rd, yb_rd, ya_wr, yb_wr):
        y_a, y_b = ya_rd[...], yb_rd[...]
        wo_bf_ref[...] = wo_ref[...].astype(BF16)
        xa = z_ref[:, Z_AX:Z_AX + D_POOL]
        ext_ref[HALO:HALO + rows, :] = xa
        pos = c * rows + lax.broadcasted_iota(jnp.int32, (rows, 1), 0)
        nt = (((1,), (1,)), ((), ()))
        tn = (((0,), (0,)), ((), ()))

        def merge_piece(j):
            m_ref[:, j * PROJ_COLS:(j + 1) * PROJ_COLS] = _merge_chunk(y_a, y_b, j, gp_ref, bm_ref, wa_ref, wb_ref)

        def pool_group(g):
            w = POOL_WINDOWS[g]
            lo, hi = g * POOL_GROUP_DIM, (g + 1) * POOL_GROUP_DIM
            acc = ext_ref[:, lo:hi]
            span = 1
            while span < w:
                acc = acc + pltpu.roll(acc, span, 0)
                span *= 2
            inv_cnt = 1.0 / jnp.minimum(pos + 1, w).astype(F32)
            pooled = acc[HALO:, :] * inv_cnt - xa[:, lo:hi]
            ya = _pool_project(pooled, pw_ref[g], ps_ref[:, lo:hi], z_ref[:, Z_AG + lo:Z_AG + hi])
            ya_wr[:, lo:hi] = ya.astype(BF16)

        per_pool = PROJ_SPLIT // len(POOL_WINDOWS)
        assert per_pool * len(POOL_WINDOWS) == PROJ_SPLIT
        fillers = [f for g in range(len(POOL_WINDOWS))
                   for f in [functools.partial(merge_piece, g * per_pool + j) for j in range(per_pool)]
                   + [functools.partial(pool_group, g)]]
        n_sub = rows // CHUNK
        n_slots = 2 * (N_HEADS // MIX_HEADS) * n_sub
        slot = [0]

        def fill():
            lo, hi = (slot[0] * len(fillers)) // n_slots, ((slot[0] + 1) * len(fillers)) // n_slots
            slot[0] += 1
            for f in fillers[lo:hi]:
                f()

        for h0 in range(0, N_HEADS, MIX_HEADS):
            heads = range(h0, h0 + MIX_HEADS)
            s_cur = {h: nret_ref[h] for h in heads}
            for ci in range(n_sub):
                rs = slice(ci * CHUNK, (ci + 1) * CHUNK)
                cos = rot_ref[rs, :HEAD_DK]
                sin = rot_ref[rs, HEAD_DK:]
                q = {h: _rotate(z_ref[rs, Z_Q + h * HEAD_DK:Z_Q + (h + 1) * HEAD_DK], cos, sin) for h in heads}
                k = {h: _rotate(z_ref[rs, Z_K + h * HEAD_DK:Z_K + (h + 1) * HEAD_DK], cos, sin) * K_SCALE
                     for h in heads}
                v = {h: z_ref[rs, Z_V + h * HEAD_DV:Z_V + (h + 1) * HEAD_DV].astype(BF16) for h in heads}
                fill()
                scores = {h: lax.dot_general(q[h].astype(BF16), k[h].astype(BF16), nt, preferred_element_type=F32)
                          for h in heads}
                kv = {h: lax.dot_general((k[h] * kdec_ref[h]).astype(BF16), v[h], tn, preferred_element_type=F32)
                      for h in heads}
                lhs = {h: jnp.concatenate([(scores[h] * dmask_ref[h]).astype(BF16),
                                           (q[h] * qdec_ref[h]).astype(BF16)], axis=1) for h in heads}
                fill()
                for h in heads:
                    rhs = jnp.concatenate([v[h], s_cur[h].astype(BF16)], axis=0)
                    o = jnp.dot(lhs[h], rhs, preferred_element_type=F32)
                    s_cur[h] = cdec_ref[h] * s_cur[h] + kv[h]
                    vs = slice(h * HEAD_DV, (h + 1) * HEAD_DV)
                    bg = z_ref[rs, Z_BG + h * HEAD_DV:Z_BG + (h + 1) * HEAD_DV]
                    yb_wr[rs, vs] = _group_norm_gate(o, gn_ref[:, vs], bg).astype(BF16)
            for h in heads:
                nret_ref[h] = jnp.where(live, s_cur[h], nret_ref[h])
        assert slot[0] == n_slots

    @pl.when(lax.rem(s, 2) == 0)
    def _():
        step(ya1_ref, yb1_ref, ya0_ref, yb0_ref)

    @pl.when(lax.rem(s, 2) == 1)
    def _():
        step(ya0_ref, yb0_ref, ya1_ref, yb1_ref)

    @pl.when(c == tiles_per_seq - 1)
    def _():
        npool_ref[...] = ext_ref[HALO + rows - POOL_BUF:HALO + rows, :]

    ext_ref[0:HALO, :] = ext_ref[rows:rows + HALO, :]


def _prompt_mix_merge(z, batch, seq, tables, pool_w, pool_scale, gn_g, b_merge, w_a, w_b, w_o):
    rot, dmask, qdec, kdec, cdec = tables
    tps = seq // MIX_ROWS
    n_tiles = batch * tps
    assert n_tiles >= CAST_STEPS
    cur = lambda s: jnp.minimum(s, n_tiles - 1)
    prev = lambda s: jnp.maximum(s - 1, 0)
    m = batch * seq
    wo_in, wo_out, wo_shape = _cast_specs(w_o, lambda s: s)
    return pl.pallas_call(
        functools.partial(_prompt_kernel, tiles_per_seq=tps),
        out_shape=(jax.ShapeDtypeStruct((m, D_MODEL), BF16),
                   jax.ShapeDtypeStruct((batch, POOL_BUF, D_POOL), F32),
                   jax.ShapeDtypeStruct((batch, N_HEADS, HEAD_DK, HEAD_DV), F32), wo_shape),
        grid=(n_tiles + 1,),
        in_specs=[pl.BlockSpec(memory_space=pltpu.SMEM),
                  pl.BlockSpec((MIX_ROWS, D_IN), lambda s: (cur(s), 0)),
                  pl.BlockSpec((MIX_ROWS, 2 * HEAD_DK), lambda s: (cur(s) % tps, 0)),
                  _resident(dmask.shape), _resident(qdec.shape), _resident(kdec.shape),
                  _resident(pool_w.shape), _resident(pool_scale.shape), _resident(gn_g.shape),
                  pl.BlockSpec((MIX_ROWS, 2 * D_MODEL), lambda s: (prev(s), D_IN // (2 * D_MODEL))),
                  _resident(b_merge.shape), _resident(w_a.shape), _resident(w_b.shape), wo_in],
        out_specs=(pl.BlockSpec((MIX_ROWS, D_MODEL), lambda s: (prev(s), 0)),
                   pl.BlockSpec((None, POOL_BUF, D_POOL), lambda s: (cur(s) // tps, 0, 0)),
                   pl.BlockSpec((None, N_HEADS, HEAD_DK, HEAD_DV), lambda s: (cur(s) // tps, 0, 0, 0)), wo_out),
        scratch_shapes=[pltpu.VMEM((MIX_ROWS, D_POOL), BF16), pltpu.VMEM((MIX_ROWS, D_V), BF16),
                        pltpu.VMEM((MIX_ROWS, D_POOL), BF16), pltpu.VMEM((MIX_ROWS, D_V), BF16),
                        pltpu.VMEM((HALO + MIX_ROWS, D_POOL), F32)],
        compiler_params=_params("arbitrary"),
        name="prompt_mix_merge",
    )(cdec, z, rot, dmask, qdec, kdec, pool_w, pool_scale, gn_g, z, b_merge, w_a, w_b, w_o)


SAMPLE_TILE = 8
SAMPLE_HEADS = 4
TAIL_ROWS = 256


def _tail_kernel(dec_ref, m_ref, x_ref, gate_ref, wo_ref, gpost_ref,
                 zs_ref, rot_ref, pw_ref, ps_ref, gn_ref, spool_ref, sret_hbm,
                 y_ref, ya_ref, yb_ref, npool_ref, nret_hbm, o_ref, in_ring, out_ring, in_sem, out_sem,
                 *, inv_cnt, rows_per_mod, n_steps):
    bt = zs_ref.shape[0]
    n_hg = N_HEADS // SAMPLE_HEADS
    assert n_hg % 2 == 0
    t = pl.program_id(0) * n_hg + pl.program_id(1)

    def state_block(hbm, step):
        seq0 = pl.multiple_of(lax.div(step, n_hg) * bt, bt)
        head0 = pl.multiple_of(lax.rem(step, n_hg) * SAMPLE_HEADS, SAMPLE_HEADS)
        return hbm.at[pl.ds(seq0, bt), pl.ds(head0, SAMPLE_HEADS)]

    def in_copy(step):
        step = jnp.asarray(step, jnp.int32)
        slot = lax.rem(step, 3)
        return pltpu.make_async_copy(state_block(sret_hbm, step), in_ring.at[slot], in_sem.at[slot])

    def out_copy(step):
        step = jnp.asarray(step, jnp.int32)
        slot = lax.rem(step, 2)
        return pltpu.make_async_copy(out_ring.at[slot], state_block(nret_hbm, step), out_sem.at[slot])

    @pl.when(t == 0)
    def _():
        in_copy(0).start()
        in_copy(1).start()

    in_copy(t).wait()

    @pl.when(t + 2 < n_steps)
    def _():
        in_copy(t + 2).start()

    @pl.when(t >= 2)
    def _():
        out_copy(t - 2).wait()

    s_in = in_ring.at[lax.rem(t, 3)]

    def step(hg):
        s_out = out_ring.at[hg % 2]
        o = _post_norm(jnp.dot(m_ref[...], wo_ref[...], preferred_element_type=F32), gpost_ref)
        tile = pl.program_id(0) * n_hg + hg
        gate = gate_ref[pl.ds((tile * x_ref.shape[0]) // rows_per_mod, 1), :]
        y_ref[...] = x_ref[...] + gate * o

        if hg == 0:
            xa = zs_ref[:, Z_AX:Z_AX + D_POOL]
            run = xa
            wins = {1: xa}
            for j in range(1, POOL_BUF + 1):
                run = run + spool_ref[POOL_BUF - j]
                wins[j + 1] = run
            for g, w in enumerate(POOL_WINDOWS):
                lo, hi = g * POOL_GROUP_DIM, (g + 1) * POOL_GROUP_DIM
                pooled = wins[w][:, lo:hi] * inv_cnt[g] - xa[:, lo:hi]
                ya = _pool_project(pooled, pw_ref[g], ps_ref[:, lo:hi], zs_ref[:, Z_AG + lo:Z_AG + hi])
                ya_ref[:, lo:hi] = ya.astype(BF16)
            for j in range(POOL_BUF - 1):
                npool_ref[j] = spool_ref[j + 1]
            npool_ref[POOL_BUF - 1] = xa

        cos = rot_ref[:, :HEAD_DK]
        sin = rot_ref[:, HEAD_DK:]
        for hl in range(SAMPLE_HEADS):
            h = hg * SAMPLE_HEADS + hl
            vs = slice(h * HEAD_DV, (h + 1) * HEAD_DV)
            q = _rotate(zs_ref[:, Z_Q + h * HEAD_DK:Z_Q + (h + 1) * HEAD_DK], cos, sin)
            k = _rotate(zs_ref[:, Z_K + h * HEAD_DK:Z_K + (h + 1) * HEAD_DK], cos, sin) * K_SCALE
            v = zs_ref[:, Z_V + h * HEAD_DV:Z_V + (h + 1) * HEAD_DV]
            score = jnp.sum(q * k, axis=1, keepdims=True) * dec_ref[0, h]
            q_cols = jnp.transpose(q * dec_ref[1, h])
            k_cols = jnp.transpose(k * dec_ref[2, h])
            for r in range(bt):
                s_old = s_in[r, hl]
                v_row = v[r:r + 1, :]
                o_row = score[r:r + 1, :] * v_row + jnp.sum(q_cols[:, r:r + 1] * s_old, axis=0, keepdims=True)
                s_out[r, hl] = dec_ref[3, h] * s_old + k_cols[:, r:r + 1] * v_row
                o_ref[r:r + 1, vs] = o_row
        for hl in range(SAMPLE_HEADS):
            h = hg * SAMPLE_HEADS + hl
            vs = slice(h * HEAD_DV, (h + 1) * HEAD_DV)
            bg = zs_ref[:, Z_BG + h * HEAD_DV:Z_BG + (h + 1) * HEAD_DV]
            yb_ref[:, vs] = _group_norm_gate(o_ref[:, vs], gn_ref[:, vs], bg).astype(BF16)

    for hg in range(n_hg):
        pl.when(pl.program_id(1) == hg)(functools.partial(step, hg))

    out_copy(t).start()

    @pl.when(t == n_steps - 1)
    def _():
        out_copy(t - 1).wait()
        out_copy(t).wait()


def _tail(merged, x, mod, w_o, g_post, zs, state_pool, state_ret, tables, inv_cnt, pool_w, pool_scale, gn_g,
          *, rows_per_mod, mod_row0):
    rot, dec = tables
    m, batch = x.shape[0], zs.shape[0]
    bt, hs = SAMPLE_TILE, SAMPLE_HEADS
    n_hg = N_HEADS // hs
    assert m == (batch // bt) * n_hg * TAIL_ROWS and mod_row0 % SUBLANES == 0 and m // rows_per_mod <= SUBLANES
    rows = lambda i, g: (i * n_hg + g, 0)
    per_tile = lambda width: pl.BlockSpec((bt, width), lambda i, g: (i, 0))
    return pl.pallas_call(
        functools.partial(_tail_kernel, inv_cnt=inv_cnt, rows_per_mod=rows_per_mod, n_steps=(batch // bt) * n_hg),
        out_shape=(jax.ShapeDtypeStruct((m, D_MODEL), F32),
                   jax.ShapeDtypeStruct((batch, D_POOL), BF16),
                   jax.ShapeDtypeStruct((batch, D_V), BF16),
                   jax.ShapeDtypeStruct(state_pool.shape, state_pool.dtype),
                   jax.ShapeDtypeStruct(state_ret.shape, state_ret.dtype)),
        grid=(batch // bt, n_hg),
        in_specs=[pl.BlockSpec(memory_space=pltpu.SMEM),
                  pl.BlockSpec((TAIL_ROWS, D_MODEL), rows), pl.BlockSpec((TAIL_ROWS, D_MODEL), rows),
                  pl.BlockSpec((SUBLANES, D_MODEL), lambda i, g: (mod_row0 // SUBLANES, 2)),
                  _resident(w_o.shape), _resident(g_post.shape),
                  per_tile(D_IN), _resident(rot.shape),
                  _resident(pool_w.shape), _resident(pool_scale.shape), _resident(gn_g.shape),
                  pl.BlockSpec((POOL_BUF, bt, D_POOL), lambda i, g: (0, i, 0)),
                  pl.BlockSpec(memory_space=pl.ANY)],
        out_specs=(pl.BlockSpec((TAIL_ROWS, D_MODEL), rows),
                   per_tile(D_POOL), per_tile(D_V),
                   pl.BlockSpec((POOL_BUF, bt, D_POOL), lambda i, g: (0, i, 0)),
                   pl.BlockSpec(memory_space=pl.ANY)),
        scratch_shapes=[pltpu.VMEM((bt, D_V), F32),
                        pltpu.VMEM((3, bt, hs, HEAD_DK, HEAD_DV), F32), pltpu.VMEM((2, bt, hs, HEAD_DK, HEAD_DV), F32),
                        pltpu.SemaphoreType.DMA((3,)), pltpu.SemaphoreType.DMA((2,))],
        compiler_params=_params("arbitrary", "arbitrary"),
        name="tail",
    )(dec, merged, x, mod, w_o, g_post, zs, rot, pool_w, pool_scale, gn_g, state_pool, state_ret)


def _out_proj_kernel(ya_ref, yb_ref, gp_ref, x_ref, gate_ref, bm_ref, wa_ref, wb_ref, wo_ref, gpost_ref, y_ref,
                     *, rows_per_mod):
    o = _merge_project(ya_ref[...], yb_ref[...], gp_ref, bm_ref, wa_ref, wb_ref, wo_ref, gpost_ref)
    gate = _mod_rows(gate_ref, slice(None), x_ref.shape[0], rows_per_mod)
    y_ref[...] = x_ref[...] + gate * o


def _out_proj(ya, yb, z, x, mod, b_merge, w_a, w_b, w_o, g_post, *, tm, rows_per_mod, mod_row0):
    m = x.shape[0]
    return pl.pallas_call(
        functools.partial(_out_proj_kernel, rows_per_mod=rows_per_mod),
        out_shape=jax.ShapeDtypeStruct((m, D_MODEL), F32),
        grid=(m // tm,),
        in_specs=[pl.BlockSpec((tm, D_POOL), lambda i: (i, 0)),
                  pl.BlockSpec((tm, D_V), lambda i: (i, 0)),
                  pl.BlockSpec((tm, 2 * D_MODEL), lambda i: (i, D_IN // (2 * D_MODEL))),
                  pl.BlockSpec((tm, D_MODEL), lambda i: (i, 0)),
                  _mod_spec(tm, rows_per_mod, mod_row0, 2),
                  _resident(b_merge.shape), _resident(w_a.shape), _resident(w_b.shape), _resident(w_o.shape),
                  _resident(g_post.shape)],
        out_specs=pl.BlockSpec((tm, D_MODEL), lambda i: (i, 0)),
        compiler_params=_params("parallel"),
        name="out_proj",
    )(ya, yb, z, x, mod, b_merge, w_a, w_b, w_o, g_post)


def _rotary_tables(start, length):
    half = HEAD_DK // 2
    inv = ROPE_BASE ** (-np.arange(half, dtype=np.float64) / half)
    ang = (start + np.arange(length, dtype=np.float64))[:, None] * inv[None, :]
    cos, sin = np.cos(ang), np.sin(ang)
    return (np.concatenate([cos, cos], axis=-1).astype(np.float32),
            np.concatenate([-sin, sin], axis=-1).astype(np.float32))


def _decay_tables(c):
    lg = np.log1p(-np.power(2.0, -5.0 - np.arange(N_HEADS, dtype=np.float64)))
    idx = np.arange(c, dtype=np.float64)
    diff = idx[:, None] - idx[None, :]
    dmask = np.where(diff[None] >= 0, np.exp(np.maximum(diff, 0.0)[None] * lg[:, None, None]), 0.0)
    q_dec = np.exp((idx + 1.0)[None, :] * lg[:, None])
    k_dec = np.exp((c - 1.0 - idx)[None, :] * lg[:, None])
    chunk_dec = np.exp(c * lg)
    return tuple(a.astype(np.float32) for a in (dmask, q_dec, k_dec, chunk_dec))


def _layer(xp, xs, c_prompt, c_sample, state_pool, state_ret, ada_w, ada_b, g_pre, g_post, w_in, pool_w, pool_scale, gn_g,
           w_a_proj, w_b_proj, w_merge, b_merge, w_out):
    batch, seq, _ = xp.shape
    dec_batch, dec_seq, _ = xs.shape
    assert dec_seq == 1 and seq % CHUNK == 0

    row = lambda v: v.reshape(1, -1)
    g_pre, g_post, pool_scale, gn_g, b_merge = map(row, (g_pre, g_post, pool_scale, gn_g, b_merge))

    mod = _modulation(c_sample, c_prompt, ada_w, row(ada_b))

    x2 = xp.reshape(batch * seq, D_MODEL)
    xs2 = xs.reshape(dec_batch, D_MODEL)
    h = _norm(x2, mod, g_pre, tm=NORM_TM, rows_per_mod=seq, mod_row0=dec_batch)
    z, zs, w_a, w_b = _in_proj(h, xs2, mod, g_pre, w_in, w_merge, w_a_proj, w_b_proj)

    cos, sin = _rotary_tables(0, seq)
    dmask, q_dec, k_dec, chunk_dec = _decay_tables(CHUNK)
    wide = lambda d: np.ascontiguousarray(np.broadcast_to(d[:, :, None], (N_HEADS, CHUNK, HEAD_DK)))
    rot = np.concatenate([cos, sin], axis=1)
    merged, pool_p, ret_p, w_o = _prompt_mix_merge(z, batch, seq, (rot, dmask, wide(q_dec), wide(k_dec), chunk_dec),
                                                   pool_w, pool_scale, gn_g, b_merge, w_a, w_b, w_out)

    cos_s, sin_s = _rotary_tables(PAST_LEN, 1)
    dmask_s, q_dec_s, k_dec_s, chunk_dec_s = _decay_tables(1)
    dec_s = np.stack([dmask_s[:, 0, 0], q_dec_s[:, 0], k_dec_s[:, 0], chunk_dec_s])
    inv_cnt = tuple(1.0 / min(PAST_LEN + 1, w) for w in POOL_WINDOWS)
    yp, ya_s, yb_s, pool_s, ret_s = _tail(merged, x2, mod, w_o, g_post, zs, jnp.transpose(state_pool, (1, 0, 2)),
                                          state_ret, (np.concatenate([cos_s, sin_s], axis=1), dec_s), inv_cnt,
                                          pool_w, pool_scale, gn_g,
                                          rows_per_mod=seq, mod_row0=dec_batch)
    pool_s = jnp.transpose(pool_s, (1, 0, 2))
    ys = _out_proj(ya_s, yb_s, zs, xs2, mod, b_merge, w_a, w_b, w_o, g_post, tm=dec_batch, rows_per_mod=1, mod_row0=0)

    return yp.reshape(xp.shape), ys.reshape(xs.shape), pool_p, ret_p, pool_s, ret_s


def kernel(x_prompt, x_sample, state_pool, state_ret, c_prompt, c_sample, ada_w, ada_b, g_pre, g_post,
           w_in, pool_w, pool_scale, gn_g, w_a_proj, w_b_proj, w_merge, b_merge, w_out):
    depth = ada_w.shape[0]
    xp, xs = x_prompt, x_sample
    pool_p, ret_p, pool_s, ret_s = [], [], [], []
    for l in range(depth):
        xp, xs, bp, sp, bs, ss = _layer(
            xp, xs, c_prompt, c_sample, state_pool[l], state_ret[l], ada_w[l], ada_b[l], g_pre[l], g_post[l], w_in[l],
            pool_w[l], pool_scale[l], gn_g[l].reshape(-1), w_a_proj[l], w_b_proj[l], w_merge[l], b_merge[l],
            w_out[l])
        pool_p.append(bp)
        ret_p.append(sp)
        pool_s.append(bs)
        ret_s.append(ss)
    return (xp, xs, jnp.stack(pool_p), jnp.stack(ret_p), jnp.stack(pool_s), jnp.stack(ret_s))
```

```python
import functools

import jax
import jax.numpy as jnp
import numpy as np
from jax import lax
from jax.experimental import pallas as pl
from jax.experimental.pallas import tpu as pltpu

F32 = jnp.float32
BF16 = jnp.bfloat16

D_MODEL = 2048
PAST_LEN = 16384
D_POOL = D_MODEL // 2
POOL_WINDOWS = (2, 4, 8, 16)
POOL_GROUP_DIM = D_POOL // len(POOL_WINDOWS)
POOL_BUF = max(POOL_WINDOWS) - 1
N_HEADS = 8
HEAD_DK = D_MODEL // 16
HEAD_DV = D_MODEL // 8
D_QK = N_HEADS * HEAD_DK
D_V = N_HEADS * HEAD_DV
CHUNK = 128
ROPE_BASE = 10000.0
EPS = 1e-6
D_IN = 2 * D_POOL + 2 * D_QK + 2 * D_V
D_Z = D_IN + 2 * D_MODEL
K_SCALE = HEAD_DK ** -0.5

VMEM_LIMIT_BYTES = 56 * 1024 * 1024
SUBLANES = 8
HALO = 16
CAST_STEPS = 32


def _params(*semantics):
    return pltpu.CompilerParams(dimension_semantics=semantics, vmem_limit_bytes=VMEM_LIMIT_BYTES)


def _cast_specs(w, step_of):
    rows = w.shape[0] // CAST_STEPS
    assert rows * CAST_STEPS == w.shape[0] and rows % (2 * SUBLANES) == 0
    spec = pl.BlockSpec((rows, w.shape[1]), lambda *idx: (jnp.minimum(step_of(*idx), CAST_STEPS - 1), 0))
    return spec, spec, jax.ShapeDtypeStruct(w.shape, BF16)


def _resident(shape):
    return pl.BlockSpec(shape, lambda *_: (0,) * len(shape), pipeline_mode=pl.Buffered(1))


def _silu(x):
    return x * jax.nn.sigmoid(x)


RING_IN, RING_OUT = 3, 2


def _ring_copy(hbm_block, ring, sem, step, *, to_hbm=False):
    slot = lax.rem(step, ring.shape[0])
    src, dst = (ring.at[slot], hbm_block) if to_hbm else (hbm_block, ring.at[slot])
    return pltpu.make_async_copy(src, dst, sem.at[slot])


def _ring_begin(t, n_steps, in_copy, out_copy=None):
    @pl.when(t == 0)
    def _():
        for k in range(RING_IN - 1):
            in_copy(jnp.int32(k)).start()

    in_copy(t).wait()

    @pl.when(t + RING_IN - 1 < n_steps)
    def _():
        in_copy(t + RING_IN - 1).start()

    if out_copy is not None:
        @pl.when(t >= RING_OUT)
        def _():
            out_copy(t - RING_OUT).wait()


def _ring_end(t, n_steps, out_copy):
    out_copy(t).start()

    @pl.when(t == n_steps - 1)
    def _():
        for k in reversed(range(RING_OUT)):
            out_copy(t - k).wait()


MOD_TN = 512


def _mod_kernel(cs_ref, cp_ref, w_hbm, b_ref, o_ref, w_ring, sem, *, n_steps):
    t = pl.program_id(0)
    tn = w_ring.shape[2]

    def in_copy(step):
        return _ring_copy(w_hbm.at[:, pl.ds(pl.multiple_of(step * tn, tn), tn)], w_ring, sem, step)

    _ring_begin(t, n_steps, in_copy)
    n_pad = o_ref.shape[0] - cs_ref.shape[0] - cp_ref.shape[0]
    c = jnp.concatenate([cs_ref[...], cp_ref[...], jnp.zeros((n_pad, D_MODEL), F32)], axis=0)
    w = w_ring[lax.rem(t, RING_IN)].astype(BF16)
    o_ref[...] = jnp.dot(_silu(c).astype(BF16), w, preferred_element_type=F32) + b_ref[...]


def _modulation(c_sample, c_prompt, ada_w, ada_b):
    ns, n_p = c_sample.shape[0], c_prompt.shape[0]
    assert ns % SUBLANES == 0
    rows = ns + n_p + (-n_p) % SUBLANES
    n_steps = 3 * D_MODEL // MOD_TN
    return pl.pallas_call(
        functools.partial(_mod_kernel, n_steps=n_steps),
        out_shape=jax.ShapeDtypeStruct((rows, 3 * D_MODEL), F32),
        grid=(n_steps,),
        in_specs=[pl.BlockSpec((ns, D_MODEL), lambda j: (0, 0)),
                  pl.BlockSpec((n_p, D_MODEL), lambda j: (0, 0)),
                  pl.BlockSpec(memory_space=pl.ANY),
                  pl.BlockSpec((1, MOD_TN), lambda j: (0, j))],
        out_specs=pl.BlockSpec((rows, MOD_TN), lambda j: (0, j)),
        scratch_shapes=[pltpu.VMEM((RING_IN, D_MODEL, MOD_TN), F32), pltpu.SemaphoreType.DMA((RING_IN,))],
        compiler_params=_params("arbitrary"),
        name="modulation",
    )(c_sample, c_prompt, ada_w, ada_b)


NORM_ROWS = 128
NORM_TM = 512


def _mod_rows(ref, rows, tm, rows_per_mod):
    if rows_per_mod == 1:
        return ref[rows, :]
    return ref[pl.ds((pl.program_id(0) * tm) // rows_per_mod, 1), :]


def _mod_spec(tm, rows_per_mod, mod_row0, col):
    if rows_per_mod == 1:
        assert mod_row0 % tm == 0
        return pl.BlockSpec((tm, D_MODEL), lambda i, *_: (mod_row0 // tm + i, col))
    assert mod_row0 % SUBLANES == 0
    return pl.BlockSpec((SUBLANES, D_MODEL), lambda i, *_: (mod_row0 // SUBLANES, col))


def _norm_mod(x, g, scale, shift):
    xn = x * lax.rsqrt(jnp.mean(x * x, axis=-1, keepdims=True) + EPS) * g
    return (xn * (1.0 + scale) + shift).astype(BF16)


def _norm_kernel(x_hbm, shift_ref, scale_ref, g_ref, h_hbm, x_ring, h_ring, in_sem, out_sem,
                 *, rows_per_mod, n_steps):
    tm = x_ring.shape[1]
    t = pl.program_id(0)

    def rows_of(step):
        return pl.ds(pl.multiple_of(step * tm, tm), tm)

    def in_copy(step):
        return _ring_copy(x_hbm.at[rows_of(step)], x_ring, in_sem, step)

    def out_copy(step):
        return _ring_copy(h_hbm.at[rows_of(step)], h_ring, out_sem, step, to_hbm=True)

    _ring_begin(t, n_steps, in_copy, out_copy)
    x_ref, h_ref = x_ring.at[lax.rem(t, RING_IN)], h_ring.at[lax.rem(t, RING_OUT)]

    def body(r, carry):
        rows = pl.ds(pl.multiple_of(r * NORM_ROWS, NORM_ROWS), NORM_ROWS)
        h_ref[rows, :] = _norm_mod(x_ref[rows, :], g_ref[...], _mod_rows(scale_ref, rows, tm, rows_per_mod),
                                   _mod_rows(shift_ref, rows, tm, rows_per_mod))
        return carry
    lax.fori_loop(0, tm // NORM_ROWS, body, 0)
    _ring_end(t, n_steps, out_copy)


def _norm(x, mod, g_pre, *, tm, rows_per_mod, mod_row0):
    m = x.shape[0]
    n_steps = m // tm
    assert (rows_per_mod == 1 or m // rows_per_mod <= SUBLANES) and n_steps >= 2
    mod_spec = lambda col: _mod_spec(tm, rows_per_mod, mod_row0, col)
    return pl.pallas_call(
        functools.partial(_norm_kernel, rows_per_mod=rows_per_mod, n_steps=n_steps),
        out_shape=jax.ShapeDtypeStruct((m, D_MODEL), BF16),
        grid=(n_steps,),
        in_specs=[pl.BlockSpec(memory_space=pl.ANY), mod_spec(0), mod_spec(1),
                  pl.BlockSpec((1, D_MODEL), lambda i: (0, 0))],
        out_specs=pl.BlockSpec(memory_space=pl.ANY),
        scratch_shapes=[pltpu.VMEM((RING_IN, tm, D_MODEL), F32), pltpu.VMEM((RING_OUT, tm, D_MODEL), BF16),
                        pltpu.SemaphoreType.DMA((RING_IN,)), pltpu.SemaphoreType.DMA((RING_OUT,))],
        compiler_params=_params("arbitrary"),
        name="norm",
    )(x, mod, mod, g_pre)


IN_TM = 1024
IN_TN = 2048
IN_DOT_COLS = 512


def _in_proj_kernel(h_ref, xs_ref, shift_s_ref, scale_s_ref, g_ref, wa_ref, wb_ref, w_in_hbm, w_mg_hbm,
                    z_ref, zs_ref, wa_bf_ref, wb_bf_ref, wbf0_ref, wbf1_ref, stage_ref, hs_ref, sem,
                    *, n_in_tiles, n_j, n_i):
    j, i = pl.program_id(0), pl.program_id(1)
    t = j * n_i + i
    total = n_j * n_i
    tn = wbf0_ref.shape[1]
    chunk = stage_ref.shape[1]

    def chunk_copy(w_hbm, col0, r, b):
        rows = pl.ds(pl.multiple_of(r * chunk, chunk), chunk)
        cols = pl.ds(col0 if isinstance(col0, int) else pl.multiple_of(col0, tn), tn)
        return pltpu.make_async_copy(w_hbm.at[rows, cols], stage_ref.at[b], sem.at[b])

    def start_chunk(g):
        g = lax.rem(jnp.asarray(g, jnp.int32), total)
        jt, r, b = lax.div(g, n_i), lax.rem(g, n_i), lax.rem(g, 2)

        @pl.when(jt < n_in_tiles)
        def _():
            chunk_copy(w_in_hbm, jt * tn, r, b).start()

        @pl.when(jt >= n_in_tiles)
        def _():
            chunk_copy(w_mg_hbm, (jt - n_in_tiles) * tn, r, b).start()

    def land_chunk(g, dst_ref):
        g = jnp.asarray(g, jnp.int32)
        r, b = lax.rem(g, n_i), lax.rem(g, 2)
        chunk_copy(w_in_hbm, 0, r, b).wait()
        dst_ref[pl.ds(pl.multiple_of(r * chunk, chunk), chunk), :] = stage_ref[b].astype(BF16)

    @pl.when(t == 0)
    def _():
        hs_ref[...] = _norm_mod(xs_ref[...], g_ref[...], scale_s_ref[...], shift_s_ref[...])
        start_chunk(0)

        def body(g, carry):
            start_chunk(g + 1)
            land_chunk(g, wbf0_ref)
            return carry
        lax.fori_loop(0, n_i, body, 0)

    @pl.when(t + 1 < total)
    def _():
        start_chunk(t + n_i + 1)

    def multiply(w_cur, w_nxt):
        land_chunk(t + n_i, w_nxt)
        wa_bf_ref[...] = wa_ref[...].astype(BF16)
        wb_bf_ref[...] = wb_ref[...].astype(BF16)
        for c0 in range(0, tn, IN_DOT_COLS):
            cols = slice(c0, c0 + IN_DOT_COLS)
            z_ref[:, cols] = jnp.dot(h_ref[...], w_cur[:, cols], preferred_element_type=F32)

        @pl.when(i == 0)
        def _():
            zs_ref[...] = jnp.dot(hs_ref[...], w_cur[...], preferred_element_type=F32)

    @pl.when(lax.rem(j, 2) == 0)
    def _():
        multiply(wbf0_ref, wbf1_ref)

    @pl.when(lax.rem(j, 2) == 1)
    def _():
        multiply(wbf1_ref, wbf0_ref)


def _in_proj(h, xs, mod, g_pre, w_in, w_mg, w_a, w_b):
    m, ms = h.shape[0], xs.shape[0]
    n_in, n_mg = w_in.shape[1] // IN_TN, w_mg.shape[1] // IN_TN
    n_i = m // IN_TM
    chunk = D_MODEL // n_i
    assert chunk * n_i == D_MODEL and chunk % 16 == 0 and (n_in + n_mg) * n_i >= CAST_STEPS
    const = lambda shape, col: pl.BlockSpec(shape, lambda j, i: (0, col), pipeline_mode=pl.Buffered(1))
    wa_in, wa_out, wa_shape = _cast_specs(w_a, lambda j, i: j * n_i + i)
    wb_in, wb_out, wb_shape = _cast_specs(w_b, lambda j, i: j * n_i + i)
    return pl.pallas_call(
        functools.partial(_in_proj_kernel, n_in_tiles=n_in, n_j=n_in + n_mg, n_i=n_i),
        out_shape=(jax.ShapeDtypeStruct((m, D_Z), F32), jax.ShapeDtypeStruct((ms, D_Z), F32), wa_shape, wb_shape),
        grid=(n_in + n_mg, n_i),
        in_specs=[pl.BlockSpec((IN_TM, D_MODEL), lambda j, i: (i, 0)),
                  const((ms, D_MODEL), 0), const((ms, D_MODEL), 0), const((ms, D_MODEL), 1),
                  const((1, D_MODEL), 0), wa_in, wb_in,
                  pl.BlockSpec(memory_space=pl.ANY), pl.BlockSpec(memory_space=pl.ANY)],
        out_specs=(pl.BlockSpec((IN_TM, IN_TN), lambda j, i: (i, j)),
                   pl.BlockSpec((ms, IN_TN), lambda j, i: (0, j)), wa_out, wb_out),
        scratch_shapes=[pltpu.VMEM((D_MODEL, IN_TN), BF16), pltpu.VMEM((D_MODEL, IN_TN), BF16),
                        pltpu.VMEM((2, chunk, IN_TN), F32),
                        pltpu.VMEM((ms, D_MODEL), BF16),
                        pltpu.SemaphoreType.DMA((2,))],
        compiler_params=_params("arbitrary", "arbitrary"),
        name="in_proj",
    )(h, xs, mod, mod, g_pre, w_a, w_b, w_in, w_mg)


def _rotate(x, cos, sin_signed):
    return x * cos + pltpu.roll(x, HEAD_DK // 2, 1) * sin_signed


def _group_norm_gate(o, gn, bg):
    mu = jnp.mean(o, axis=-1, keepdims=True)
    d = o - mu
    var = jnp.mean(d * d, axis=-1, keepdims=True)
    return d * lax.rsqrt(var + EPS) * gn * _silu(bg)


def _pool_project(pooled, pw, ps, ag):
    mixed = jnp.dot(pooled.astype(BF16), pw.astype(BF16), preferred_element_type=F32)
    return mixed * ps * _silu(ag)


PROJ_SPLIT = 8
PROJ_COLS = D_MODEL // PROJ_SPLIT


def _merge_chunk(y_a, y_b, j, gp_ref, bm_ref, wa_ref, wb_ref):
    lo, hi = j * PROJ_COLS, (j + 1) * PROJ_COLS
    ya = jnp.dot(y_a, wa_ref[:, lo:hi], preferred_element_type=F32)
    yb = jnp.dot(y_b, wb_ref[:, lo:hi], preferred_element_type=F32)
    g_a = jax.nn.sigmoid(gp_ref[:, lo:hi] + bm_ref[:, lo:hi])
    g_b = jax.nn.sigmoid(gp_ref[:, D_MODEL + lo:D_MODEL + hi] + bm_ref[:, D_MODEL + lo:D_MODEL + hi])
    return (g_a * ya + g_b * yb).astype(BF16)


def _out_chunk(merged, j, wo_ref):
    return jnp.dot(merged, wo_ref[:, j * PROJ_COLS:(j + 1) * PROJ_COLS], preferred_element_type=F32)


def _post_norm(o, gpost_ref):
    return o * lax.rsqrt(jnp.mean(o * o, axis=-1, keepdims=True) + EPS) * gpost_ref[...]


def _merge_project(y_a, y_b, gp_ref, bm_ref, wa_ref, wb_ref, wo_ref, gpost_ref):
    merged = jnp.concatenate([_merge_chunk(y_a, y_b, j, gp_ref, bm_ref, wa_ref, wb_ref)
                              for j in range(PROJ_SPLIT)], axis=1)
    o = jnp.concatenate([_out_chunk(merged, j, wo_ref) for j in range(PROJ_SPLIT)], axis=1)
    return _post_norm(o, gpost_ref)


MIX_ROWS = 2 * CHUNK
MIX_HEADS = 4
Z_AX, Z_AG, Z_Q, Z_K, Z_V, Z_BG = 0, D_POOL, 2 * D_POOL, 2 * D_POOL + D_QK, 2 * D_POOL + 2 * D_QK, D_IN - D_V


def _prompt_kernel(cdec_ref, z_ref, rot_ref, dmask_ref, qdec_ref, kdec_ref, pw_ref, ps_ref, gn_ref,
                   gp_ref, bm_ref, wa_ref, wb_ref, wo_ref,
                   m_ref, npool_ref, nret_ref, wo_bf_ref, ya0_ref, yb0_ref, ya1_ref, yb1_ref, ext_ref, *, tiles_per_seq):
    s = pl.program_id(0)
    n_tiles = pl.num_programs(0) - 1
    live = s < n_tiles
    c = lax.rem(jnp.minimum(s, n_tiles - 1), tiles_per_seq)
    rows = z_ref.shape[0]

    @pl.when(s == 0)
    def _():
        for ref in (ya0_ref, yb0_ref, ya1_ref, yb1_ref):
            ref[...] = jnp.zeros(ref.shape, BF16)

    @pl.when(c == 0)
    def _():
        ext_ref[0:HALO, :] = jnp.zeros((HALO, D_POOL), F32)
        nret_ref[...] = jnp.zeros(nret_ref.shape, F32)

    def step(ya_rd, yb_rd, ya_wr, yb_wr):
        y_a, y_b = ya_rd[...], yb_rd[...]
        wo_bf_ref[...] = wo_ref[...].astype(BF16)
        xa = z_ref[:, Z_AX:Z_AX + D_POOL]
        ext_ref[HALO:HALO + rows, :] = xa
        pos = c * rows + lax.broadcasted_iota(jnp.int32, (rows, 1), 0)
        nt = (((1,), (1,)), ((), ()))
        tn = (((0,), (0,)), ((), ()))

        def merge_piece(j):
            m_ref[:, j * PROJ_COLS:(j + 1) * PROJ_COLS] = _merge_chunk(y_a, y_b, j, gp_ref, bm_ref, wa_ref, wb_ref)

        def pool_group(g):
            w = POOL_WINDOWS[g]
            lo, hi = g * POOL_GROUP_DIM, (g + 1) * POOL_GROUP_DIM
            acc = ext_ref[:, lo:hi]
            span = 1
            while span < w:
                acc = acc + pltpu.roll(acc, span, 0)
                span *= 2
            inv_cnt = 1.0 / jnp.minimum(pos + 1, w).astype(F32)
            pooled = acc[HALO:, :] * inv_cnt - xa[:, lo:hi]
            ya = _pool_project(pooled, pw_ref[g], ps_ref[:, lo:hi], z_ref[:, Z_AG + lo:Z_AG + hi])
            ya_wr[:, lo:hi] = ya.astype(BF16)

        per_pool = PROJ_SPLIT // len(POOL_WINDOWS)
        assert per_pool * len(POOL_WINDOWS) == PROJ_SPLIT
        fillers = [f for g in range(len(POOL_WINDOWS))
                   for f in [functools.partial(merge_piece, g * per_pool + j) for j in range(per_pool)]
                   + [functools.partial(pool_group, g)]]
        n_sub = rows // CHUNK
        n_slots = 2 * (N_HEADS // MIX_HEADS) * n_sub
        slot = [0]

        def fill():
            lo, hi = (slot[0] * len(fillers)) // n_slots, ((slot[0] + 1) * len(fillers)) // n_slots
            slot[0] += 1
            for f in fillers[lo:hi]:
                f()

        for h0 in range(0, N_HEADS, MIX_HEADS):
            heads = range(h0, h0 + MIX_HEADS)
            s_cur = {h: nret_ref[h] for h in heads}
            for ci in range(n_sub):
                rs = slice(ci * CHUNK, (ci + 1) * CHUNK)
                cos = rot_ref[rs, :HEAD_DK]
                sin = rot_ref[rs, HEAD_DK:]
                q = {h: _rotate(z_ref[rs, Z_Q + h * HEAD_DK:Z_Q + (h + 1) * HEAD_DK], cos, sin) for h in heads}
                k = {h: _rotate(z_ref[rs, Z_K + h * HEAD_DK:Z_K + (h + 1) * HEAD_DK], cos, sin) * K_SCALE
                     for h in heads}
                v = {h: z_ref[rs, Z_V + h * HEAD_DV:Z_V + (h + 1) * HEAD_DV].astype(BF16) for h in heads}
                fill()
                scores = {h: lax.dot_general(q[h].astype(BF16), k[h].astype(BF16), nt, preferred_element_type=F32)
                          for h in heads}
                kv = {h: lax.dot_general((k[h] * kdec_ref[h]).astype(BF16), v[h], tn, preferred_element_type=F32)
                      for h in heads}
                lhs = {h: jnp.concatenate([(scores[h] * dmask_ref[h]).astype(BF16),
                                           (q[h] * qdec_ref[h]).astype(BF16)], axis=1) for h in heads}
                fill()
                for h in heads:
                    rhs = jnp.concatenate([v[h], s_cur[h].astype(BF16)], axis=0)
                    o = jnp.dot(lhs[h], rhs, preferred_element_type=F32)
                    s_cur[h] = cdec_ref[h] * s_cur[h] + kv[h]
                    vs = slice(h * HEAD_DV, (h + 1) * HEAD_DV)
                    bg = z_ref[rs, Z_BG + h * HEAD_DV:Z_BG + (h + 1) * HEAD_DV]
                    yb_wr[rs, vs] = _group_norm_gate(o, gn_ref[:, vs], bg).astype(BF16)
            for h in heads:
                nret_ref[h] = jnp.where(live, s_cur[h], nret_ref[h])
        assert slot[0] == n_slots

    @pl.when(lax.rem(s, 2) == 0)
    def _():
        step(ya1_ref, yb1_ref, ya0_ref, yb0_ref)

    @pl.when(lax.rem(s, 2) == 1)
    def _():
        step(ya0_ref, yb0_ref, ya1_ref, yb1_ref)

    @pl.when(c == tiles_per_seq - 1)
    def _():
        npool_ref[...] = ext_ref[HALO + rows - POOL_BUF:HALO + rows, :]

    ext_ref[0:HALO, :] = ext_ref[rows:rows + HALO, :]


def _prompt_mix_merge(z, batch, seq, tables, pool_w, pool_scale, gn_g, b_merge, w_a, w_b, w_o):
    rot, dmask, qdec, kdec, cdec = tables
    tps = seq // MIX_ROWS
    n_tiles = batch * tps
    assert n_tiles >= CAST_STEPS
    cur = lambda s: jnp.minimum(s, n_tiles - 1)
    prev = lambda s: jnp.maximum(s - 1, 0)
    m = batch * seq
    wo_in, wo_out, wo_shape = _cast_specs(w_o, lambda s: s)
    return pl.pallas_call(
        functools.partial(_prompt_kernel, tiles_per_seq=tps),
        out_shape=(jax.ShapeDtypeStruct((m, D_MODEL), BF16),
                   jax.ShapeDtypeStruct((batch, POOL_BUF, D_POOL), F32),
                   jax.ShapeDtypeStruct((batch, N_HEADS, HEAD_DK, HEAD_DV), F32), wo_shape),
        grid=(n_tiles + 1,),
        in_specs=[pl.BlockSpec(memory_space=pltpu.SMEM),
                  pl.BlockSpec((MIX_ROWS, D_IN), lambda s: (cur(s), 0)),
                  pl.BlockSpec((MIX_ROWS, 2 * HEAD_DK), lambda s: (cur(s) % tps, 0)),
                  _resident(dmask.shape), _resident(qdec.shape), _resident(kdec.shape),
                  _resident(pool_w.shape), _resident(pool_scale.shape), _resident(gn_g.shape),
                  pl.BlockSpec((MIX_ROWS, 2 * D_MODEL), lambda s: (prev(s), D_IN // (2 * D_MODEL))),
                  _resident(b_merge.shape), _resident(w_a.shape), _resident(w_b.shape), wo_in],
        out_specs=(pl.BlockSpec((MIX_ROWS, D_MODEL), lambda s: (prev(s), 0)),
                   pl.BlockSpec((None, POOL_BUF, D_POOL), lambda s: (cur(s) // tps, 0, 0)),
                   pl.BlockSpec((None, N_HEADS, HEAD_DK, HEAD_DV), lambda s: (cur(s) // tps, 0, 0, 0)), wo_out),
        scratch_shapes=[pltpu.VMEM((MIX_ROWS, D_POOL), BF16), pltpu.VMEM((MIX_ROWS, D_V), BF16),
                        pltpu.VMEM((MIX_ROWS, D_POOL), BF16), pltpu.VMEM((MIX_ROWS, D_V), BF16),
                        pltpu.VMEM((HALO + MIX_ROWS, D_POOL), F32)],
        compiler_params=_params("arbitrary"),
        name="prompt_mix_merge",
    )(cdec, z, rot, dmask, qdec, kdec, pool_w, pool_scale, gn_g, z, b_merge, w_a, w_b, w_o)


SAMPLE_TILE = 8
SAMPLE_HEADS = 4
TAIL_ROWS = 256


def _tail_kernel(dec_ref, m_ref, x_ref, gate_ref, wo_ref, gpost_ref,
                 zs_ref, rot_ref, pw_ref, ps_ref, gn_ref, spool_ref, sret_hbm,
                 y_ref, ya_ref, yb_ref, npool_ref, nret_hbm, o_ref, in_ring, out_ring, in_sem, out_sem,
                 *, inv_cnt, rows_per_mod, n_steps):
    bt = zs_ref.shape[0]
    n_hg = N_HEADS // SAMPLE_HEADS
    assert n_hg % RING_OUT == 0
    t = pl.program_id(0) * n_hg + pl.program_id(1)

    def state_block(hbm, step):
        seq0 = pl.multiple_of(lax.div(step, n_hg) * bt, bt)
        head0 = pl.multiple_of(lax.rem(step, n_hg) * SAMPLE_HEADS, SAMPLE_HEADS)
        return hbm.at[pl.ds(seq0, bt), pl.ds(head0, SAMPLE_HEADS)]

    def in_copy(step):
        return _ring_copy(state_block(sret_hbm, step), in_ring, in_sem, step)

    def out_copy(step):
        return _ring_copy(state_block(nret_hbm, step), out_ring, out_sem, step, to_hbm=True)

    _ring_begin(t, n_steps, in_copy, out_copy)
    s_in = in_ring.at[lax.rem(t, RING_IN)]

    def step(hg):
        s_out = out_ring.at[hg % RING_OUT]
        o = _post_norm(jnp.dot(m_ref[...], wo_ref[...], preferred_element_type=F32), gpost_ref)
        tile = pl.program_id(0) * n_hg + hg
        gate = gate_ref[pl.ds((tile * x_ref.shape[0]) // rows_per_mod, 1), :]
        y_ref[...] = x_ref[...] + gate * o

        if hg == 0:
            xa = zs_ref[:, Z_AX:Z_AX + D_POOL]
            run = xa
            wins = {1: xa}
            for j in range(1, POOL_BUF + 1):
                run = run + spool_ref[POOL_BUF - j]
                wins[j + 1] = run
            for g, w in enumerate(POOL_WINDOWS):
                lo, hi = g * POOL_GROUP_DIM, (g + 1) * POOL_GROUP_DIM
                pooled = wins[w][:, lo:hi] * inv_cnt[g] - xa[:, lo:hi]
                ya = _pool_project(pooled, pw_ref[g], ps_ref[:, lo:hi], zs_ref[:, Z_AG + lo:Z_AG + hi])
                ya_ref[:, lo:hi] = ya.astype(BF16)
            for j in range(POOL_BUF - 1):
                npool_ref[j] = spool_ref[j + 1]
            npool_ref[POOL_BUF - 1] = xa

        cos = rot_ref[:, :HEAD_DK]
        sin = rot_ref[:, HEAD_DK:]
        for hl in range(SAMPLE_HEADS):
            h = hg * SAMPLE_HEADS + hl
            vs = slice(h * HEAD_DV, (h + 1) * HEAD_DV)
            q = _rotate(zs_ref[:, Z_Q + h * HEAD_DK:Z_Q + (h + 1) * HEAD_DK], cos, sin)
            k = _rotate(zs_ref[:, Z_K + h * HEAD_DK:Z_K + (h + 1) * HEAD_DK], cos, sin) * K_SCALE
            v = zs_ref[:, Z_V + h * HEAD_DV:Z_V + (h + 1) * HEAD_DV]
            score = jnp.sum(q * k, axis=1, keepdims=True) * dec_ref[0, h]
            q_cols = jnp.transpose(q * dec_ref[1, h])
            k_cols = jnp.transpose(k * dec_ref[2, h])
            for r in range(bt):
                s_old = s_in[r, hl]
                v_row = v[r:r + 1, :]
                o_row = score[r:r + 1, :] * v_row + jnp.sum(q_cols[:, r:r + 1] * s_old, axis=0, keepdims=True)
                s_out[r, hl] = dec_ref[3, h] * s_old + k_cols[:, r:r + 1] * v_row
                o_ref[r:r + 1, vs] = o_row
        for hl in range(SAMPLE_HEADS):
            h = hg * SAMPLE_HEADS + hl
            vs = slice(h * HEAD_DV, (h + 1) * HEAD_DV)
            bg = zs_ref[:, Z_BG + h * HEAD_DV:Z_BG + (h + 1) * HEAD_DV]
            yb_ref[:, vs] = _group_norm_gate(o_ref[:, vs], gn_ref[:, vs], bg).astype(BF16)

    for hg in range(n_hg):
        pl.when(pl.program_id(1) == hg)(functools.partial(step, hg))

    _ring_end(t, n_steps, out_copy)


def _tail(merged, x, mod, w_o, g_post, zs, state_pool, state_ret, tables, inv_cnt, pool_w, pool_scale, gn_g,
          *, rows_per_mod, mod_row0):
    rot, dec = tables
    m, batch = x.shape[0], zs.shape[0]
    bt, hs = SAMPLE_TILE, SAMPLE_HEADS
    n_hg = N_HEADS // hs
    assert m == (batch // bt) * n_hg * TAIL_ROWS and mod_row0 % SUBLANES == 0 and m // rows_per_mod <= SUBLANES
    rows = lambda i, g: (i * n_hg + g, 0)
    per_tile = lambda width: pl.BlockSpec((bt, width), lambda i, g: (i, 0))
    return pl.pallas_call(
        functools.partial(_tail_kernel, inv_cnt=inv_cnt, rows_per_mod=rows_per_mod, n_steps=(batch // bt) * n_hg),
        out_shape=(jax.ShapeDtypeStruct((m, D_MODEL), F32),
                   jax.ShapeDtypeStruct((batch, D_POOL), BF16),
                   jax.ShapeDtypeStruct((batch, D_V), BF16),
                   jax.ShapeDtypeStruct(state_pool.shape, state_pool.dtype),
                   jax.ShapeDtypeStruct(state_ret.shape, state_ret.dtype)),
        grid=(batch // bt, n_hg),
        in_specs=[pl.BlockSpec(memory_space=pltpu.SMEM),
                  pl.BlockSpec((TAIL_ROWS, D_MODEL), rows), pl.BlockSpec((TAIL_ROWS, D_MODEL), rows),
                  pl.BlockSpec((SUBLANES, D_MODEL), lambda i, g: (mod_row0 // SUBLANES, 2)),
                  _resident(w_o.shape), _resident(g_post.shape),
                  per_tile(D_IN), _resident(rot.shape),
                  _resident(pool_w.shape), _resident(pool_scale.shape), _resident(gn_g.shape),
                  pl.BlockSpec((POOL_BUF, bt, D_POOL), lambda i, g: (0, i, 0)),
                  pl.BlockSpec(memory_space=pl.ANY)],
        out_specs=(pl.BlockSpec((TAIL_ROWS, D_MODEL), rows),
                   per_tile(D_POOL), per_tile(D_V),
                   pl.BlockSpec((POOL_BUF, bt, D_POOL), lambda i, g: (0, i, 0)),
                   pl.BlockSpec(memory_space=pl.ANY)),
        scratch_shapes=[pltpu.VMEM((bt, D_V), F32),
                        pltpu.VMEM((RING_IN, bt, hs, HEAD_DK, HEAD_DV), F32),
                        pltpu.VMEM((RING_OUT, bt, hs, HEAD_DK, HEAD_DV), F32),
                        pltpu.SemaphoreType.DMA((RING_IN,)), pltpu.SemaphoreType.DMA((RING_OUT,))],
        compiler_params=_params("arbitrary", "arbitrary"),
        name="tail",
    )(dec, merged, x, mod, w_o, g_post, zs, rot, pool_w, pool_scale, gn_g, state_pool, state_ret)


def _out_proj_kernel(ya_ref, yb_ref, gp_ref, x_ref, gate_ref, bm_ref, wa_ref, wb_ref, wo_ref, gpost_ref, y_ref,
                     *, rows_per_mod):
    o = _merge_project(ya_ref[...], yb_ref[...], gp_ref, bm_ref, wa_ref, wb_ref, wo_ref, gpost_ref)
    gate = _mod_rows(gate_ref, slice(None), x_ref.shape[0], rows_per_mod)
    y_ref[...] = x_ref[...] + gate * o


def _out_proj(ya, yb, z, x, mod, b_merge, w_a, w_b, w_o, g_post, *, tm, rows_per_mod, mod_row0):
    m = x.shape[0]
    return pl.pallas_call(
        functools.partial(_out_proj_kernel, rows_per_mod=rows_per_mod),
        out_shape=jax.ShapeDtypeStruct((m, D_MODEL), F32),
        grid=(m // tm,),
        in_specs=[pl.BlockSpec((tm, D_POOL), lambda i: (i, 0)),
                  pl.BlockSpec((tm, D_V), lambda i: (i, 0)),
                  pl.BlockSpec((tm, 2 * D_MODEL), lambda i: (i, D_IN // (2 * D_MODEL))),
                  pl.BlockSpec((tm, D_MODEL), lambda i: (i, 0)),
                  _mod_spec(tm, rows_per_mod, mod_row0, 2),
                  _resident(b_merge.shape), _resident(w_a.shape), _resident(w_b.shape), _resident(w_o.shape),
                  _resident(g_post.shape)],
        out_specs=pl.BlockSpec((tm, D_MODEL), lambda i: (i, 0)),
        compiler_params=_params("parallel"),
        name="out_proj",
    )(ya, yb, z, x, mod, b_merge, w_a, w_b, w_o, g_post)


def _rotary_tables(start, length):
    half = HEAD_DK // 2
    inv = ROPE_BASE ** (-np.arange(half, dtype=np.float64) / half)
    ang = (start + np.arange(length, dtype=np.float64))[:, None] * inv[None, :]
    cos, sin = np.cos(ang), np.sin(ang)
    return (np.concatenate([cos, cos], axis=-1).astype(np.float32),
            np.concatenate([-sin, sin], axis=-1).astype(np.float32))


def _decay_tables(c):
    lg = np.log1p(-np.power(2.0, -5.0 - np.arange(N_HEADS, dtype=np.float64)))
    idx = np.arange(c, dtype=np.float64)
    diff = idx[:, None] - idx[None, :]
    dmask = np.where(diff[None] >= 0, np.exp(np.maximum(diff, 0.0)[None] * lg[:, None, None]), 0.0)
    q_dec = np.exp((idx + 1.0)[None, :] * lg[:, None])
    k_dec = np.exp((c - 1.0 - idx)[None, :] * lg[:, None])
    chunk_dec = np.exp(c * lg)
    return tuple(a.astype(np.float32) for a in (dmask, q_dec, k_dec, chunk_dec))


def _layer(xp, xs, c_prompt, c_sample, state_pool, state_ret, ada_w, ada_b, g_pre, g_post, w_in, pool_w, pool_scale, gn_g,
           w_a_proj, w_b_proj, w_merge, b_merge, w_out):
    batch, seq, _ = xp.shape
    dec_batch, dec_seq, _ = xs.shape
    assert dec_seq == 1 and seq % CHUNK == 0

    row = lambda v: v.reshape(1, -1)
    g_pre, g_post, pool_scale, gn_g, b_merge = map(row, (g_pre, g_post, pool_scale, gn_g, b_merge))

    mod = _modulation(c_sample, c_prompt, ada_w, row(ada_b))

    x2 = xp.reshape(batch * seq, D_MODEL)
    xs2 = xs.reshape(dec_batch, D_MODEL)
    h = _norm(x2, mod, g_pre, tm=NORM_TM, rows_per_mod=seq, mod_row0=dec_batch)
    z, zs, w_a, w_b = _in_proj(h, xs2, mod, g_pre, w_in, w_merge, w_a_proj, w_b_proj)

    cos, sin = _rotary_tables(0, seq)
    dmask, q_dec, k_dec, chunk_dec = _decay_tables(CHUNK)
    wide = lambda d: np.ascontiguousarray(np.broadcast_to(d[:, :, None], (N_HEADS, CHUNK, HEAD_DK)))
    rot = np.concatenate([cos, sin], axis=1)
    merged, pool_p, ret_p, w_o = _prompt_mix_merge(z, batch, seq, (rot, dmask, wide(q_dec), wide(k_dec), chunk_dec),
                                                   pool_w, pool_scale, gn_g, b_merge, w_a, w_b, w_out)

    cos_s, sin_s = _rotary_tables(PAST_LEN, 1)
    dmask_s, q_dec_s, k_dec_s, chunk_dec_s = _decay_tables(1)
    dec_s = np.stack([dmask_s[:, 0, 0], q_dec_s[:, 0], k_dec_s[:, 0], chunk_dec_s])
    inv_cnt = tuple(1.0 / min(PAST_LEN + 1, w) for w in POOL_WINDOWS)
    yp, ya_s, yb_s, pool_s, ret_s = _tail(merged, x2, mod, w_o, g_post, zs, jnp.transpose(state_pool, (1, 0, 2)),
                                          state_ret, (np.concatenate([cos_s, sin_s], axis=1), dec_s), inv_cnt,
                                          pool_w, pool_scale, gn_g,
                                          rows_per_mod=seq, mod_row0=dec_batch)
    pool_s = jnp.transpose(pool_s, (1, 0, 2))
    ys = _out_proj(ya_s, yb_s, zs, xs2, mod, b_merge, w_a, w_b, w_o, g_post, tm=dec_batch, rows_per_mod=1, mod_row0=0)

    return yp.reshape(xp.shape), ys.reshape(xs.shape), pool_p, ret_p, pool_s, ret_s


def kernel(x_prompt, x_sample, state_pool, state_ret, c_prompt, c_sample, ada_w, ada_b, g_pre, g_post,
           w_in, pool_w, pool_scale, gn_g, w_a_proj, w_b_proj, w_merge, b_merge, w_out):
    depth = ada_w.shape[0]
    xp, xs = x_prompt, x_sample
    pool_p, ret_p, pool_s, ret_s = [], [], [], []
    for l in range(depth):
        xp, xs, bp, sp, bs, ss = _layer(
            xp, xs, c_prompt, c_sample, state_pool[l], state_ret[l], ada_w[l], ada_b[l], g_pre[l], g_post[l], w_in[l],
            pool_w[l], pool_scale[l], gn_g[l].reshape(-1), w_a_proj[l], w_b_proj[l], w_merge[l], b_merge[l],
            w_out[l])
        pool_p.append(bp)
        ret_p.append(sp)
        pool_s.append(bs)
        ret_s.append(ss)
    return (xp, xs, jnp.stack(pool_p), jnp.stack(ret_p), jnp.stack(pool_s), jnp.stack(ret_s))
```

```python
import functools

import jax
import jax.numpy as jnp
import numpy as np
from jax import lax
from jax.experimental import pallas as pl
from jax.experimental.pallas import tpu as pltpu

F32 = jnp.float32
BF16 = jnp.bfloat16

D_MODEL = 2048
PAST_LEN = 16384
D_POOL = D_MODEL // 2
POOL_WINDOWS = (2, 4, 8, 16)
POOL_GROUP_DIM = D_POOL // len(POOL_WINDOWS)
POOL_BUF = max(POOL_WINDOWS) - 1
N_HEADS = 8
HEAD_DK = D_MODEL // 16
HEAD_DV = D_MODEL // 8
D_QK = N_HEADS * HEAD_DK
D_V = N_HEADS * HEAD_DV
CHUNK = 128
ROPE_BASE = 10000.0
EPS = 1e-6
D_IN = 2 * D_POOL + 2 * D_QK + 2 * D_V
D_Z = D_IN + 2 * D_MODEL
K_SCALE = HEAD_DK ** -0.5

VMEM_LIMIT_BYTES = 56 * 1024 * 1024
SUBLANES = 8
HALO = 16
CAST_STEPS = 32


def _params(*semantics):
    return pltpu.CompilerParams(dimension_semantics=semantics, vmem_limit_bytes=VMEM_LIMIT_BYTES)


def _cast_specs(w, step_of):
    rows = w.shape[0] // CAST_STEPS
    assert rows * CAST_STEPS == w.shape[0] and rows % (2 * SUBLANES) == 0
    spec = pl.BlockSpec((rows, w.shape[1]), lambda *idx: (jnp.minimum(step_of(*idx), CAST_STEPS - 1), 0))
    return spec, spec, jax.ShapeDtypeStruct(w.shape, BF16)


def _resident(shape):
    return pl.BlockSpec(shape, lambda *_: (0,) * len(shape), pipeline_mode=pl.Buffered(1))


def _silu(x):
    return x * jax.nn.sigmoid(x)


RING_IN, RING_OUT = 4, 2


def _ring_copy(hbm_block, ring, sem, step, *, to_hbm=False):
    slot = lax.rem(step, ring.shape[0])
    src, dst = (ring.at[slot], hbm_block) if to_hbm else (hbm_block, ring.at[slot])
    return pltpu.make_async_copy(src, dst, sem.at[slot])


def _ring_begin(t, n_steps, in_copy, out_copy=None):
    @pl.when(t == 0)
    def _():
        for k in range(RING_IN - 1):
            in_copy(jnp.int32(k)).start()

    in_copy(t).wait()

    @pl.when(t + RING_IN - 1 < n_steps)
    def _():
        in_copy(t + RING_IN - 1).start()

    if out_copy is not None:
        @pl.when(t >= RING_OUT)
        def _():
            out_copy(t - RING_OUT).wait()


def _ring_end(t, n_steps, out_copy):
    out_copy(t).start()

    @pl.when(t == n_steps - 1)
    def _():
        for k in reversed(range(RING_OUT)):
            out_copy(t - k).wait()


MOD_TN = 512


def _mod_kernel(cs_ref, cp_ref, w_hbm, b_ref, o_ref, w_ring, sem, *, n_steps):
    t = pl.program_id(0)
    tn = w_ring.shape[2]

    def in_copy(step):
        return _ring_copy(w_hbm.at[:, pl.ds(pl.multiple_of(step * tn, tn), tn)], w_ring, sem, step)

    _ring_begin(t, n_steps, in_copy)
    n_pad = o_ref.shape[0] - cs_ref.shape[0] - cp_ref.shape[0]
    c = jnp.concatenate([cs_ref[...], cp_ref[...], jnp.zeros((n_pad, D_MODEL), F32)], axis=0)
    w = w_ring[lax.rem(t, RING_IN)].astype(BF16)
    o_ref[...] = jnp.dot(_silu(c).astype(BF16), w, preferred_element_type=F32) + b_ref[...]


def _modulation(c_sample, c_prompt, ada_w, ada_b):
    ns, n_p = c_sample.shape[0], c_prompt.shape[0]
    assert ns % SUBLANES == 0
    rows = ns + n_p + (-n_p) % SUBLANES
    n_steps = 3 * D_MODEL // MOD_TN
    return pl.pallas_call(
        functools.partial(_mod_kernel, n_steps=n_steps),
        out_shape=jax.ShapeDtypeStruct((rows, 3 * D_MODEL), F32),
        grid=(n_steps,),
        in_specs=[pl.BlockSpec((ns, D_MODEL), lambda j: (0, 0)),
                  pl.BlockSpec((n_p, D_MODEL), lambda j: (0, 0)),
                  pl.BlockSpec(memory_space=pl.ANY),
                  pl.BlockSpec((1, MOD_TN), lambda j: (0, j))],
        out_specs=pl.BlockSpec((rows, MOD_TN), lambda j: (0, j)),
        scratch_shapes=[pltpu.VMEM((RING_IN, D_MODEL, MOD_TN), F32), pltpu.SemaphoreType.DMA((RING_IN,))],
        compiler_params=_params("arbitrary"),
        name="modulation",
    )(c_sample, c_prompt, ada_w, ada_b)


NORM_ROWS = 128
NORM_TM = 512


def _mod_rows(ref, rows, tm, rows_per_mod):
    if rows_per_mod == 1:
        return ref[rows, :]
    return ref[pl.ds((pl.program_id(0) * tm) // rows_per_mod, 1), :]


def _mod_spec(tm, rows_per_mod, mod_row0, col):
    if rows_per_mod == 1:
        assert mod_row0 % tm == 0
        return pl.BlockSpec((tm, D_MODEL), lambda i, *_: (mod_row0 // tm + i, col))
    assert mod_row0 % SUBLANES == 0
    return pl.BlockSpec((SUBLANES, D_MODEL), lambda i, *_: (mod_row0 // SUBLANES, col))


def _norm_mod(x, g, scale, shift):
    xn = x * lax.rsqrt(jnp.mean(x * x, axis=-1, keepdims=True) + EPS) * g
    return (xn * (1.0 + scale) + shift).astype(BF16)


def _norm_kernel(x_hbm, shift_ref, scale_ref, g_ref, h_hbm, x_ring, h_ring, in_sem, out_sem,
                 *, rows_per_mod, n_steps):
    tm = x_ring.shape[1]
    t = pl.program_id(0)

    def rows_of(step):
        return pl.ds(pl.multiple_of(step * tm, tm), tm)

    def in_copy(step):
        return _ring_copy(x_hbm.at[rows_of(step)], x_ring, in_sem, step)

    def out_copy(step):
        return _ring_copy(h_hbm.at[rows_of(step)], h_ring, out_sem, step, to_hbm=True)

    _ring_begin(t, n_steps, in_copy, out_copy)
    x_ref, h_ref = x_ring.at[lax.rem(t, RING_IN)], h_ring.at[lax.rem(t, RING_OUT)]

    def body(r, carry):
        rows = pl.ds(pl.multiple_of(r * NORM_ROWS, NORM_ROWS), NORM_ROWS)
        h_ref[rows, :] = _norm_mod(x_ref[rows, :], g_ref[...], _mod_rows(scale_ref, rows, tm, rows_per_mod),
                                   _mod_rows(shift_ref, rows, tm, rows_per_mod))
        return carry
    lax.fori_loop(0, tm // NORM_ROWS, body, 0)
    _ring_end(t, n_steps, out_copy)


def _norm(x, mod, g_pre, *, tm, rows_per_mod, mod_row0):
    m = x.shape[0]
    n_steps = m // tm
    assert (rows_per_mod == 1 or m // rows_per_mod <= SUBLANES) and n_steps >= RING_IN
    mod_spec = lambda col: _mod_spec(tm, rows_per_mod, mod_row0, col)
    return pl.pallas_call(
        functools.partial(_norm_kernel, rows_per_mod=rows_per_mod, n_steps=n_steps),
        out_shape=jax.ShapeDtypeStruct((m, D_MODEL), BF16),
        grid=(n_steps,),
        in_specs=[pl.BlockSpec(memory_space=pl.ANY), mod_spec(0), mod_spec(1),
                  pl.BlockSpec((1, D_MODEL), lambda i: (0, 0))],
        out_specs=pl.BlockSpec(memory_space=pl.ANY),
        scratch_shapes=[pltpu.VMEM((RING_IN, tm, D_MODEL), F32), pltpu.VMEM((RING_OUT, tm, D_MODEL), BF16),
                        pltpu.SemaphoreType.DMA((RING_IN,)), pltpu.SemaphoreType.DMA((RING_OUT,))],
        compiler_params=_params("arbitrary"),
        name="norm",
    )(x, mod, mod, g_pre)


IN_TM = 1024
IN_TN = 2048
IN_DOT_COLS = 512


def _in_proj_kernel(h_ref, xs_ref, shift_s_ref, scale_s_ref, g_ref, wa_ref, wb_ref, w_in_hbm, w_mg_hbm,
                    z_ref, zs_ref, wa_bf_ref, wb_bf_ref, wbf0_ref, wbf1_ref, stage_ref, hs_ref, sem,
                    *, n_in_tiles, n_j, n_i):
    j, i = pl.program_id(0), pl.program_id(1)
    t = j * n_i + i
    total = n_j * n_i
    tn = wbf0_ref.shape[1]
    chunk = stage_ref.shape[1]

    def chunk_copy(w_hbm, col0, r, b):
        rows = pl.ds(pl.multiple_of(r * chunk, chunk), chunk)
        cols = pl.ds(col0 if isinstance(col0, int) else pl.multiple_of(col0, tn), tn)
        return pltpu.make_async_copy(w_hbm.at[rows, cols], stage_ref.at[b], sem.at[b])

    def start_chunk(g):
        g = lax.rem(jnp.asarray(g, jnp.int32), total)
        jt, r, b = lax.div(g, n_i), lax.rem(g, n_i), lax.rem(g, 2)

        @pl.when(jt < n_in_tiles)
        def _():
            chunk_copy(w_in_hbm, jt * tn, r, b).start()

        @pl.when(jt >= n_in_tiles)
        def _():
            chunk_copy(w_mg_hbm, (jt - n_in_tiles) * tn, r, b).start()

    def land_chunk(g, dst_ref):
        g = jnp.asarray(g, jnp.int32)
        r, b = lax.rem(g, n_i), lax.rem(g, 2)
        chunk_copy(w_in_hbm, 0, r, b).wait()
        dst_ref[pl.ds(pl.multiple_of(r * chunk, chunk), chunk), :] = stage_ref[b].astype(BF16)

    @pl.when(t == 0)
    def _():
        hs_ref[...] = _norm_mod(xs_ref[...], g_ref[...], scale_s_ref[...], shift_s_ref[...])
        start_chunk(0)

        def body(g, carry):
            start_chunk(g + 1)
            land_chunk(g, wbf0_ref)
            return carry
        lax.fori_loop(0, n_i, body, 0)

    @pl.when(t + 1 < total)
    def _():
        start_chunk(t + n_i + 1)

    def multiply(w_cur, w_nxt):
        land_chunk(t + n_i, w_nxt)
        wa_bf_ref[...] = wa_ref[...].astype(BF16)
        wb_bf_ref[...] = wb_ref[...].astype(BF16)
        for c0 in range(0, tn, IN_DOT_COLS):
            cols = slice(c0, c0 + IN_DOT_COLS)
            z_ref[:, cols] = jnp.dot(h_ref[...], w_cur[:, cols], preferred_element_type=F32)

        @pl.when(i == 0)
        def _():
            zs_ref[...] = jnp.dot(hs_ref[...], w_cur[...], preferred_element_type=F32)

    @pl.when(lax.rem(j, 2) == 0)
    def _():
        multiply(wbf0_ref, wbf1_ref)

    @pl.when(lax.rem(j, 2) == 1)
    def _():
        multiply(wbf1_ref, wbf0_ref)


def _in_proj(h, xs, mod, g_pre, w_in, w_mg, w_a, w_b):
    m, ms = h.shape[0], xs.shape[0]
    n_in, n_mg = w_in.shape[1] // IN_TN, w_mg.shape[1] // IN_TN
    n_i = m // IN_TM
    chunk = D_MODEL // n_i
    assert chunk * n_i == D_MODEL and chunk % 16 == 0 and (n_in + n_mg) * n_i >= CAST_STEPS
    const = lambda shape, col: pl.BlockSpec(shape, lambda j, i: (0, col), pipeline_mode=pl.Buffered(1))
    wa_in, wa_out, wa_shape = _cast_specs(w_a, lambda j, i: j * n_i + i)
    wb_in, wb_out, wb_shape = _cast_specs(w_b, lambda j, i: j * n_i + i)
    return pl.pallas_call(
        functools.partial(_in_proj_kernel, n_in_tiles=n_in, n_j=n_in + n_mg, n_i=n_i),
        out_shape=(jax.ShapeDtypeStruct((m, D_Z), F32), jax.ShapeDtypeStruct((ms, D_Z), F32), wa_shape, wb_shape),
        grid=(n_in + n_mg, n_i),
        in_specs=[pl.BlockSpec((IN_TM, D_MODEL), lambda j, i: (i, 0)),
                  const((ms, D_MODEL), 0), const((ms, D_MODEL), 0), const((ms, D_MODEL), 1),
                  const((1, D_MODEL), 0), wa_in, wb_in,
                  pl.BlockSpec(memory_space=pl.ANY), pl.BlockSpec(memory_space=pl.ANY)],
        out_specs=(pl.BlockSpec((IN_TM, IN_TN), lambda j, i: (i, j)),
                   pl.BlockSpec((ms, IN_TN), lambda j, i: (0, j)), wa_out, wb_out),
        scratch_shapes=[pltpu.VMEM((D_MODEL, IN_TN), BF16), pltpu.VMEM((D_MODEL, IN_TN), BF16),
                        pltpu.VMEM((2, chunk, IN_TN), F32),
                        pltpu.VMEM((ms, D_MODEL), BF16),
                        pltpu.SemaphoreType.DMA((2,))],
        compiler_params=_params("arbitrary", "arbitrary"),
        name="in_proj",
    )(h, xs, mod, mod, g_pre, w_a, w_b, w_in, w_mg)


def _rotate(x, cos, sin_signed):
    return x * cos + pltpu.roll(x, HEAD_DK // 2, 1) * sin_signed


def _group_norm_gate(o, gn, bg):
    mu = jnp.mean(o, axis=-1, keepdims=True)
    d = o - mu
    var = jnp.mean(d * d, axis=-1, keepdims=True)
    return d * lax.rsqrt(var + EPS) * gn * _silu(bg)


def _pool_project(pooled, pw, ps, ag):
    mixed = jnp.dot(pooled.astype(BF16), pw.astype(BF16), preferred_element_type=F32)
    return mixed * ps * _silu(ag)


PROJ_SPLIT = 8
PROJ_COLS = D_MODEL // PROJ_SPLIT


def _merge_chunk(y_a, y_b, j, gp_ref, bm_ref, wa_ref, wb_ref):
    lo, hi = j * PROJ_COLS, (j + 1) * PROJ_COLS
    ya = jnp.dot(y_a, wa_ref[:, lo:hi], preferred_element_type=F32)
    yb = jnp.dot(y_b, wb_ref[:, lo:hi], preferred_element_type=F32)
    g_a = jax.nn.sigmoid(gp_ref[:, lo:hi] + bm_ref[:, lo:hi])
    g_b = jax.nn.sigmoid(gp_ref[:, D_MODEL + lo:D_MODEL + hi] + bm_ref[:, D_MODEL + lo:D_MODEL + hi])
    return (g_a * ya + g_b * yb).astype(BF16)


def _out_chunk(merged, j, wo_ref):
    return jnp.dot(merged, wo_ref[:, j * PROJ_COLS:(j + 1) * PROJ_COLS], preferred_element_type=F32)


def _post_norm(o, gpost_ref):
    return o * lax.rsqrt(jnp.mean(o * o, axis=-1, keepdims=True) + EPS) * gpost_ref[...]


def _merge_project(y_a, y_b, gp_ref, bm_ref, wa_ref, wb_ref, wo_ref, gpost_ref):
    merged = jnp.concatenate([_merge_chunk(y_a, y_b, j, gp_ref, bm_ref, wa_ref, wb_ref)
                              for j in range(PROJ_SPLIT)], axis=1)
    o = jnp.concatenate([_out_chunk(merged, j, wo_ref) for j in range(PROJ_SPLIT)], axis=1)
    return _post_norm(o, gpost_ref)


MIX_ROWS = 2 * CHUNK
MIX_HEADS = 4
Z_AX, Z_AG, Z_Q, Z_K, Z_V, Z_BG = 0, D_POOL, 2 * D_POOL, 2 * D_POOL + D_QK, 2 * D_POOL + 2 * D_QK, D_IN - D_V


def _prompt_kernel(cdec_ref, z_ref, rot_ref, dmask_ref, qdec_ref, kdec_ref, pw_ref, ps_ref, gn_ref,
                   gp_ref, bm_ref, wa_ref, wb_ref, wo_ref,
                   m_ref, npool_ref, nret_ref, wo_bf_ref, ya0_ref, yb0_ref, ya1_ref, yb1_ref, ext_ref, *, tiles_per_seq):
    s = pl.program_id(0)
    n_tiles = pl.num_programs(0) - 1
    live = s < n_tiles
    c = lax.rem(jnp.minimum(s, n_tiles - 1), tiles_per_seq)
    rows = z_ref.shape[0]

    @pl.when(s == 0)
    def _():
        for ref in (ya0_ref, yb0_ref, ya1_ref, yb1_ref):
            ref[...] = jnp.zeros(ref.shape, BF16)

    @pl.when(c == 0)
    def _():
        ext_ref[0:HALO, :] = jnp.zeros((HALO, D_POOL), F32)
        nret_ref[...] = jnp.zeros(nret_ref.shape, F32)

    def step(ya_rd, yb_rd, ya_wr, yb_wr):
        y_a, y_b = ya_rd[...], yb_rd[...]
        wo_bf_ref[...] = wo_ref[...].astype(BF16)
        xa = z_ref[:, Z_AX:Z_AX + D_POOL]
        ext_ref[HALO:HALO + rows, :] = xa
        pos = c * rows + lax.broadcasted_iota(jnp.int32, (rows, 1), 0)
        nt = (((1,), (1,)), ((), ()))
        tn = (((0,), (0,)), ((), ()))

        def merge_piece(j):
            m_ref[:, j * PROJ_COLS:(j + 1) * PROJ_COLS] = _merge_chunk(y_a, y_b, j, gp_ref, bm_ref, wa_ref, wb_ref)

        def pool_group(g):
            w = POOL_WINDOWS[g]
            lo, hi = g * POOL_GROUP_DIM, (g + 1) * POOL_GROUP_DIM
            acc = ext_ref[:, lo:hi]
            span = 1
            while span < w:
                acc = acc + pltpu.roll(acc, span, 0)
                span *= 2
            inv_cnt = 1.0 / jnp.minimum(pos + 1, w).astype(F32)
            pooled = acc[HALO:, :] * inv_cnt - xa[:, lo:hi]
            ya = _pool_project(pooled, pw_ref[g], ps_ref[:, lo:hi], z_ref[:, Z_AG + lo:Z_AG + hi])
            ya_wr[:, lo:hi] = ya.astype(BF16)

        per_pool = PROJ_SPLIT // len(POOL_WINDOWS)
        assert per_pool * len(POOL_WINDOWS) == PROJ_SPLIT
        fillers = [f for g in range(len(POOL_WINDOWS))
                   for f in [functools.partial(merge_piece, g * per_pool + j) for j in range(per_pool)]
                   + [functools.partial(pool_group, g)]]
        n_sub = rows // CHUNK
        n_slots = 2 * (N_HEADS // MIX_HEADS) * n_sub
        slot = [0]

        def fill():
            lo, hi = (slot[0] * len(fillers)) // n_slots, ((slot[0] + 1) * len(fillers)) // n_slots
            slot[0] += 1
            for f in fillers[lo:hi]:
                f()

        for h0 in range(0, N_HEADS, MIX_HEADS):
            heads = range(h0, h0 + MIX_HEADS)
            s_cur = {h: nret_ref[h] for h in heads}
            for ci in range(n_sub):
                rs = slice(ci * CHUNK, (ci + 1) * CHUNK)
                cos = rot_ref[rs, :HEAD_DK]
                sin = rot_ref[rs, HEAD_DK:]
                q = {h: _rotate(z_ref[rs, Z_Q + h * HEAD_DK:Z_Q + (h + 1) * HEAD_DK], cos, sin) for h in heads}
                k = {h: _rotate(z_ref[rs, Z_K + h * HEAD_DK:Z_K + (h + 1) * HEAD_DK], cos, sin) * K_SCALE
                     for h in heads}
                v = {h: z_ref[rs, Z_V + h * HEAD_DV:Z_V + (h + 1) * HEAD_DV].astype(BF16) for h in heads}
                fill()
                scores = {h: lax.dot_general(q[h].astype(BF16), k[h].astype(BF16), nt, preferred_element_type=F32)
                          for h in heads}
                kv = {h: lax.dot_general((k[h] * kdec_ref[h]).astype(BF16), v[h], tn, preferred_element_type=F32)
                      for h in heads}
                lhs = {h: jnp.concatenate([(scores[h] * dmask_ref[h]).astype(BF16),
                                           (q[h] * qdec_ref[h]).astype(BF16)], axis=1) for h in heads}
                fill()
                for h in heads:
                    rhs = jnp.concatenate([v[h], s_cur[h].astype(BF16)], axis=0)
                    o = jnp.dot(lhs[h], rhs, preferred_element_type=F32)
                    s_cur[h] = cdec_ref[h] * s_cur[h] + kv[h]
                    vs = slice(h * HEAD_DV, (h + 1) * HEAD_DV)
                    bg = z_ref[rs, Z_BG + h * HEAD_DV:Z_BG + (h + 1) * HEAD_DV]
                    yb_wr[rs, vs] = _group_norm_gate(o, gn_ref[:, vs], bg).astype(BF16)
            for h in heads:
                nret_ref[h] = jnp.where(live, s_cur[h], nret_ref[h])
        assert slot[0] == n_slots

    @pl.when(lax.rem(s, 2) == 0)
    def _():
        step(ya1_ref, yb1_ref, ya0_ref, yb0_ref)

    @pl.when(lax.rem(s, 2) == 1)
    def _():
        step(ya0_ref, yb0_ref, ya1_ref, yb1_ref)

    @pl.when(c == tiles_per_seq - 1)
    def _():
        npool_ref[...] = ext_ref[HALO + rows - POOL_BUF:HALO + rows, :]

    ext_ref[0:HALO, :] = ext_ref[rows:rows + HALO, :]


def _prompt_mix_merge(z, batch, seq, tables, pool_w, pool_scale, gn_g, b_merge, w_a, w_b, w_o):
    rot, dmask, qdec, kdec, cdec = tables
    tps = seq // MIX_ROWS
    n_tiles = batch * tps
    assert n_tiles >= CAST_STEPS
    cur = lambda s: jnp.minimum(s, n_tiles - 1)
    prev = lambda s: jnp.maximum(s - 1, 0)
    m = batch * seq
    wo_in, wo_out, wo_shape = _cast_specs(w_o, lambda s: s)
    return pl.pallas_call(
        functools.partial(_prompt_kernel, tiles_per_seq=tps),
        out_shape=(jax.ShapeDtypeStruct((m, D_MODEL), BF16),
                   jax.ShapeDtypeStruct((batch, POOL_BUF, D_POOL), F32),
                   jax.ShapeDtypeStruct((batch, N_HEADS, HEAD_DK, HEAD_DV), F32), wo_shape),
        grid=(n_tiles + 1,),
        in_specs=[pl.BlockSpec(memory_space=pltpu.SMEM),
                  pl.BlockSpec((MIX_ROWS, D_IN), lambda s: (cur(s), 0)),
                  pl.BlockSpec((MIX_ROWS, 2 * HEAD_DK), lambda s: (cur(s) % tps, 0)),
                  _resident(dmask.shape), _resident(qdec.shape), _resident(kdec.shape),
                  _resident(pool_w.shape), _resident(pool_scale.shape), _resident(gn_g.shape),
                  pl.BlockSpec((MIX_ROWS, 2 * D_MODEL), lambda s: (prev(s), D_IN // (2 * D_MODEL))),
                  _resident(b_merge.shape), _resident(w_a.shape), _resident(w_b.shape), wo_in],
        out_specs=(pl.BlockSpec((MIX_ROWS, D_MODEL), lambda s: (prev(s), 0)),
                   pl.BlockSpec((None, POOL_BUF, D_POOL), lambda s: (cur(s) // tps, 0, 0)),
                   pl.BlockSpec((None, N_HEADS, HEAD_DK, HEAD_DV), lambda s: (cur(s) // tps, 0, 0, 0)), wo_out),
        scratch_shapes=[pltpu.VMEM((MIX_ROWS, D_POOL), BF16), pltpu.VMEM((MIX_ROWS, D_V), BF16),
                        pltpu.VMEM((MIX_ROWS, D_POOL), BF16), pltpu.VMEM((MIX_ROWS, D_V), BF16),
                        pltpu.VMEM((HALO + MIX_ROWS, D_POOL), F32)],
        compiler_params=_params("arbitrary"),
        name="prompt_mix_merge",
    )(cdec, z, rot, dmask, qdec, kdec, pool_w, pool_scale, gn_g, z, b_merge, w_a, w_b, w_o)


SAMPLE_TILE = 8
SAMPLE_HEADS = 4
TAIL_ROWS = 256


def _tail_kernel(dec_ref, m_ref, x_ref, gate_ref, wo_ref, gpost_ref,
                 zs_ref, rot_ref, pw_ref, ps_ref, gn_ref, spool_ref, sret_hbm,
                 y_ref, ya_ref, yb_ref, npool_ref, nret_hbm, o_ref, in_ring, out_ring, in_sem, out_sem,
                 *, inv_cnt, rows_per_mod, n_steps):
    bt = zs_ref.shape[0]
    n_hg = N_HEADS // SAMPLE_HEADS
    assert n_hg % RING_OUT == 0
    t = pl.program_id(0) * n_hg + pl.program_id(1)

    def state_block(hbm, step):
        seq0 = pl.multiple_of(lax.div(step, n_hg) * bt, bt)
        head0 = pl.multiple_of(lax.rem(step, n_hg) * SAMPLE_HEADS, SAMPLE_HEADS)
        return hbm.at[pl.ds(seq0, bt), pl.ds(head0, SAMPLE_HEADS)]

    def in_copy(step):
        return _ring_copy(state_block(sret_hbm, step), in_ring, in_sem, step)

    def out_copy(step):
        return _ring_copy(state_block(nret_hbm, step), out_ring, out_sem, step, to_hbm=True)

    _ring_begin(t, n_steps, in_copy, out_copy)
    s_in = in_ring.at[lax.rem(t, RING_IN)]

    def step(hg):
        s_out = out_ring.at[hg % RING_OUT]
        o = _post_norm(jnp.dot(m_ref[...], wo_ref[...], preferred_element_type=F32), gpost_ref)
        tile = pl.program_id(0) * n_hg + hg
        gate = gate_ref[pl.ds((tile * x_ref.shape[0]) // rows_per_mod, 1), :]
        y_ref[...] = x_ref[...] + gate * o

        if hg == 0:
            xa = zs_ref[:, Z_AX:Z_AX + D_POOL]
            run = xa
            wins = {1: xa}
            for j in range(1, POOL_BUF + 1):
                run = run + spool_ref[POOL_BUF - j]
                wins[j + 1] = run
            for g, w in enumerate(POOL_WINDOWS):
                lo, hi = g * POOL_GROUP_DIM, (g + 1) * POOL_GROUP_DIM
                pooled = wins[w][:, lo:hi] * inv_cnt[g] - xa[:, lo:hi]
                ya = _pool_project(pooled, pw_ref[g], ps_ref[:, lo:hi], zs_ref[:, Z_AG + lo:Z_AG + hi])
                ya_ref[:, lo:hi] = ya.astype(BF16)
            for j in range(POOL_BUF - 1):
                npool_ref[j] = spool_ref[j + 1]
            npool_ref[POOL_BUF - 1] = xa

        cos = rot_ref[:, :HEAD_DK]
        sin = rot_ref[:, HEAD_DK:]
        for hl in range(SAMPLE_HEADS):
            h = hg * SAMPLE_HEADS + hl
            vs = slice(h * HEAD_DV, (h + 1) * HEAD_DV)
            q = _rotate(zs_ref[:, Z_Q + h * HEAD_DK:Z_Q + (h + 1) * HEAD_DK], cos, sin)
            k = _rotate(zs_ref[:, Z_K + h * HEAD_DK:Z_K + (h + 1) * HEAD_DK], cos, sin) * K_SCALE
            v = zs_ref[:, Z_V + h * HEAD_DV:Z_V + (h + 1) * HEAD_DV]
            score = jnp.sum(q * k, axis=1, keepdims=True) * dec_ref[0, h]
            q_cols = jnp.transpose(q * dec_ref[1, h])
            k_cols = jnp.transpose(k * dec_ref[2, h])
            for r in range(bt):
                s_old = s_in[r, hl]
                v_row = v[r:r + 1, :]
                o_row = score[r:r + 1, :] * v_row + jnp.sum(q_cols[:, r:r + 1] * s_old, axis=0, keepdims=True)
                s_out[r, hl] = dec_ref[3, h] * s_old + k_cols[:, r:r + 1] * v_row
                o_ref[r:r + 1, vs] = o_row
        for hl in range(SAMPLE_HEADS):
            h = hg * SAMPLE_HEADS + hl
            vs = slice(h * HEAD_DV, (h + 1) * HEAD_DV)
            bg = zs_ref[:, Z_BG + h * HEAD_DV:Z_BG + (h + 1) * HEAD_DV]
            yb_ref[:, vs] = _group_norm_gate(o_ref[:, vs], gn_ref[:, vs], bg).astype(BF16)

    for hg in range(n_hg):
        pl.when(pl.program_id(1) == hg)(functools.partial(step, hg))

    _ring_end(t, n_steps, out_copy)


def _tail(merged, x, mod, w_o, g_post, zs, state_pool, state_ret, tables, inv_cnt, pool_w, pool_scale, gn_g,
          *, rows_per_mod, mod_row0):
    rot, dec = tables
    m, batch = x.shape[0], zs.shape[0]
    bt, hs = SAMPLE_TILE, SAMPLE_HEADS
    n_hg = N_HEADS // hs
    assert m == (batch // bt) * n_hg * TAIL_ROWS and mod_row0 % SUBLANES == 0 and m // rows_per_mod <= SUBLANES
    rows = lambda i, g: (i * n_hg + g, 0)
    per_tile = lambda width: pl.BlockSpec((bt, width), lambda i, g: (i, 0))
    return pl.pallas_call(
        functools.partial(_tail_kernel, inv_cnt=inv_cnt, rows_per_mod=rows_per_mod, n_steps=(batch // bt) * n_hg),
        out_shape=(jax.ShapeDtypeStruct((m, D_MODEL), F32),
                   jax.ShapeDtypeStruct((batch, D_POOL), BF16),
                   jax.ShapeDtypeStruct((batch, D_V), BF16),
                   jax.ShapeDtypeStruct(state_pool.shape, state_pool.dtype),
                   jax.ShapeDtypeStruct(state_ret.shape, state_ret.dtype)),
        grid=(batch // bt, n_hg),
        in_specs=[pl.BlockSpec(memory_space=pltpu.SMEM),
                  pl.BlockSpec((TAIL_ROWS, D_MODEL), rows), pl.BlockSpec((TAIL_ROWS, D_MODEL), rows),
                  pl.BlockSpec((SUBLANES, D_MODEL), lambda i, g: (mod_row0 // SUBLANES, 2)),
                  _resident(w_o.shape), _resident(g_post.shape),
                  per_tile(D_IN), _resident(rot.shape),
                  _resident(pool_w.shape), _resident(pool_scale.shape), _resident(gn_g.shape),
                  pl.BlockSpec((POOL_BUF, bt, D_POOL), lambda i, g: (0, i, 0)),
                  pl.BlockSpec(memory_space=pl.ANY)],
        out_specs=(pl.BlockSpec((TAIL_ROWS, D_MODEL), rows),
                   per_tile(D_POOL), per_tile(D_V),
                   pl.BlockSpec((POOL_BUF, bt, D_POOL), lambda i, g: (0, i, 0)),
                   pl.BlockSpec(memory_space=pl.ANY)),
        scratch_shapes=[pltpu.VMEM((bt, D_V), F32),
                        pltpu.VMEM((RING_IN, bt, hs, HEAD_DK, HEAD_DV), F32),
                        pltpu.VMEM((RING_OUT, bt, hs, HEAD_DK, HEAD_DV), F32),
                        pltpu.SemaphoreType.DMA((RING_IN,)), pltpu.SemaphoreType.DMA((RING_OUT,))],
        compiler_params=_params("arbitrary", "arbitrary"),
        name="tail",
    )(dec, merged, x, mod, w_o, g_post, zs, rot, pool_w, pool_scale, gn_g, state_pool, state_ret)


def _out_proj_kernel(ya_ref, yb_ref, gp_ref, x_ref, gate_ref, bm_ref, wa_ref, wb_ref, wo_ref, gpost_ref, y_ref,
                     *, rows_per_mod):
    o = _merge_project(ya_ref[...], yb_ref[...], gp_ref, bm_ref, wa_ref, wb_ref, wo_ref, gpost_ref)
    gate = _mod_rows(gate_ref, slice(None), x_ref.shape[0], rows_per_mod)
    y_ref[...] = x_ref[...] + gate * o


def _out_proj(ya, yb, z, x, mod, b_merge, w_a, w_b, w_o, g_post, *, tm, rows_per_mod, mod_row0):
    m = x.shape[0]
    return pl.pallas_call(
        functools.partial(_out_proj_kernel, rows_per_mod=rows_per_mod),
        out_shape=jax.ShapeDtypeStruct((m, D_MODEL), F32),
        grid=(m // tm,),
        in_specs=[pl.BlockSpec((tm, D_POOL), lambda i: (i, 0)),
                  pl.BlockSpec((tm, D_V), lambda i: (i, 0)),
                  pl.BlockSpec((tm, 2 * D_MODEL), lambda i: (i, D_IN // (2 * D_MODEL))),
                  pl.BlockSpec((tm, D_MODEL), lambda i: (i, 0)),
                  _mod_spec(tm, rows_per_mod, mod_row0, 2),
                  _resident(b_merge.shape), _resident(w_a.shape), _resident(w_b.shape), _resident(w_o.shape),
                  _resident(g_post.shape)],
        out_specs=pl.BlockSpec((tm, D_MODEL), lambda i: (i, 0)),
        compiler_params=_params("parallel"),
        name="out_proj",
    )(ya, yb, z, x, mod, b_merge, w_a, w_b, w_o, g_post)


def _rotary_tables(start, length):
    half = HEAD_DK // 2
    inv = ROPE_BASE ** (-np.arange(half, dtype=np.float64) / half)
    ang = (start + np.arange(length, dtype=np.float64))[:, None] * inv[None, :]
    cos, sin = np.cos(ang), np.sin(ang)
    return (np.concatenate([cos, cos], axis=-1).astype(np.float32),
            np.concatenate([-sin, sin], axis=-1).astype(np.float32))


def _decay_tables(c):
    lg = np.log1p(-np.power(2.0, -5.0 - np.arange(N_HEADS, dtype=np.float64)))
    idx = np.arange(c, dtype=np.float64)
    diff = idx[:, None] - idx[None, :]
    dmask = np.where(diff[None] >= 0, np.exp(np.maximum(diff, 0.0)[None] * lg[:, None, None]), 0.0)
    q_dec = np.exp((idx + 1.0)[None, :] * lg[:, None])
    k_dec = np.exp((c - 1.0 - idx)[None, :] * lg[:, None])
    chunk_dec = np.exp(c * lg)
    return tuple(a.astype(np.float32) for a in (dmask, q_dec, k_dec, chunk_dec))


def _layer(xp, xs, c_prompt, c_sample, state_pool, state_ret, ada_w, ada_b, g_pre, g_post, w_in, pool_w, pool_scale, gn_g,
           w_a_proj, w_b_proj, w_merge, b_merge, w_out):
    batch, seq, _ = xp.shape
    dec_batch, dec_seq, _ = xs.shape
    assert dec_seq == 1 and seq % CHUNK == 0

    row = lambda v: v.reshape(1, -1)
    g_pre, g_post, pool_scale, gn_g, b_merge = map(row, (g_pre, g_post, pool_scale, gn_g, b_merge))

    mod = _modulation(c_sample, c_prompt, ada_w, row(ada_b))

    x2 = xp.reshape(batch * seq, D_MODEL)
    xs2 = xs.reshape(dec_batch, D_MODEL)
    h = _norm(x2, mod, g_pre, tm=NORM_TM, rows_per_mod=seq, mod_row0=dec_batch)
    z, zs, w_a, w_b = _in_proj(h, xs2, mod, g_pre, w_in, w_merge, w_a_proj, w_b_proj)

    cos, sin = _rotary_tables(0, seq)
    dmask, q_dec, k_dec, chunk_dec = _decay_tables(CHUNK)
    wide = lambda d: np.ascontiguousarray(np.broadcast_to(d[:, :, None], (N_HEADS, CHUNK, HEAD_DK)))
    rot = np.concatenate([cos, sin], axis=1)
    merged, pool_p, ret_p, w_o = _prompt_mix_merge(z, batch, seq, (rot, dmask, wide(q_dec), wide(k_dec), chunk_dec),
                                                   pool_w, pool_scale, gn_g, b_merge, w_a, w_b, w_out)

    cos_s, sin_s = _rotary_tables(PAST_LEN, 1)
    dmask_s, q_dec_s, k_dec_s, chunk_dec_s = _decay_tables(1)
    dec_s = np.stack([dmask_s[:, 0, 0], q_dec_s[:, 0], k_dec_s[:, 0], chunk_dec_s])
    inv_cnt = tuple(1.0 / min(PAST_LEN + 1, w) for w in POOL_WINDOWS)
    yp, ya_s, yb_s, pool_s, ret_s = _tail(merged, x2, mod, w_o, g_post, zs, jnp.transpose(state_pool, (1, 0, 2)),
                                          state_ret, (np.concatenate([cos_s, sin_s], axis=1), dec_s), inv_cnt,
                                          pool_w, pool_scale, gn_g,
                                          rows_per_mod=seq, mod_row0=dec_batch)
    pool_s = jnp.transpose(pool_s, (1, 0, 2))
    ys = _out_proj(ya_s, yb_s, zs, xs2, mod, b_merge, w_a, w_b, w_o, g_post, tm=dec_batch, rows_per_mod=1, mod_row0=0)

    return yp.reshape(xp.shape), ys.reshape(xs.shape), pool_p, ret_p, pool_s, ret_s


def kernel(x_prompt, x_sample, state_pool, state_ret, c_prompt, c_sample, ada_w, ada_b, g_pre, g_post,
           w_in, pool_w, pool_scale, gn_g, w_a_proj, w_b_proj, w_merge, b_merge, w_out):
    depth = ada_w.shape[0]
    xp, xs = x_prompt, x_sample
    pool_p, ret_p, pool_s, ret_s = [], [], [], []
    for l in range(depth):
        xp, xs, bp, sp, bs, ss = _layer(
            xp, xs, c_prompt, c_sample, state_pool[l], state_ret[l], ada_w[l], ada_b[l], g_pre[l], g_post[l], w_in[l],
            pool_w[l], pool_scale[l], gn_g[l].reshape(-1), w_a_proj[l], w_b_proj[l], w_merge[l], b_merge[l],
            w_out[l])
        pool_p.append(bp)
        ret_p.append(sp)
        pool_s.append(bs)
        ret_s.append(ss)
    return (xp, xs, jnp.stack(pool_p), jnp.stack(ret_p), jnp.stack(pool_s), jnp.stack(ret_s))
```

```python
import functools

import jax
import jax.numpy as jnp
import numpy as np
from jax import lax
from jax.experimental import pallas as pl
from jax.experimental.pallas import tpu as pltpu

F32 = jnp.float32
BF16 = jnp.bfloat16

D_MODEL = 2048
PAST_LEN = 16384
D_POOL = D_MODEL // 2
POOL_WINDOWS = (2, 4, 8, 16)
POOL_GROUP_DIM = D_POOL // len(POOL_WINDOWS)
POOL_BUF = max(POOL_WINDOWS) - 1
N_HEADS = 8
HEAD_DK = D_MODEL // 16
HEAD_DV = D_MODEL // 8
D_QK = N_HEADS * HEAD_DK
D_V = N_HEADS * HEAD_DV
CHUNK = 128
ROPE_BASE = 10000.0
EPS = 1e-6
D_IN = 2 * D_POOL + 2 * D_QK + 2 * D_V
D_Z = D_IN + 2 * D_MODEL
K_SCALE = HEAD_DK ** -0.5

VMEM_LIMIT_BYTES = 56 * 1024 * 1024
SUBLANES = 8
HALO = 16
CAST_STEPS = 32


def _params(*semantics, vmem_limit_bytes=VMEM_LIMIT_BYTES):
    return pltpu.CompilerParams(dimension_semantics=semantics, vmem_limit_bytes=vmem_limit_bytes)


def _cast_specs(w, step_of):
    rows = w.shape[0] // CAST_STEPS
    assert rows * CAST_STEPS == w.shape[0] and rows % (2 * SUBLANES) == 0
    spec = pl.BlockSpec((rows, w.shape[1]), lambda *idx: (jnp.minimum(step_of(*idx), CAST_STEPS - 1), 0))
    return spec, spec, jax.ShapeDtypeStruct(w.shape, BF16)


def _resident(shape):
    return pl.BlockSpec(shape, lambda *_: (0,) * len(shape), pipeline_mode=pl.Buffered(1))


def _silu(x):
    return x * jax.nn.sigmoid(x)


RING_IN, RING_OUT = 3, 2


def _ring_copy(hbm_block, ring, sem, step, *, to_hbm=False):
    slot = lax.rem(step, ring.shape[0])
    src, dst = (ring.at[slot], hbm_block) if to_hbm else (hbm_block, ring.at[slot])
    return pltpu.make_async_copy(src, dst, sem.at[slot])


def _ring_begin(t, n_steps, in_copy, out_copy=None):
    @pl.when(t == 0)
    def _():
        for k in range(RING_IN - 1):
            in_copy(jnp.int32(k)).start()

    in_copy(t).wait()

    @pl.when(t + RING_IN - 1 < n_steps)
    def _():
        in_copy(t + RING_IN - 1).start()

    if out_copy is not None:
        @pl.when(t >= RING_OUT)
        def _():
            out_copy(t - RING_OUT).wait()


def _ring_end(t, n_steps, out_copy):
    out_copy(t).start()

    @pl.when(t == n_steps - 1)
    def _():
        for k in reversed(range(RING_OUT)):
            out_copy(t - k).wait()


MOD_TN = 512


def _mod_kernel(cs_ref, cp_ref, w_hbm, b_ref, o_ref, w_ring, sem, *, n_steps):
    t = pl.program_id(0)
    tn = w_ring.shape[2]

    def in_copy(step):
        return _ring_copy(w_hbm.at[:, pl.ds(pl.multiple_of(step * tn, tn), tn)], w_ring, sem, step)

    _ring_begin(t, n_steps, in_copy)
    n_pad = o_ref.shape[0] - cs_ref.shape[0] - cp_ref.shape[0]
    c = jnp.concatenate([cs_ref[...], cp_ref[...], jnp.zeros((n_pad, D_MODEL), F32)], axis=0)
    w = w_ring[lax.rem(t, RING_IN)].astype(BF16)
    o_ref[...] = jnp.dot(_silu(c).astype(BF16), w, preferred_element_type=F32) + b_ref[...]


def _modulation(c_sample, c_prompt, ada_w, ada_b):
    ns, n_p = c_sample.shape[0], c_prompt.shape[0]
    assert ns % SUBLANES == 0
    rows = ns + n_p + (-n_p) % SUBLANES
    n_steps = 3 * D_MODEL // MOD_TN
    return pl.pallas_call(
        functools.partial(_mod_kernel, n_steps=n_steps),
        out_shape=jax.ShapeDtypeStruct((rows, 3 * D_MODEL), F32),
        grid=(n_steps,),
        in_specs=[pl.BlockSpec((ns, D_MODEL), lambda j: (0, 0)),
                  pl.BlockSpec((n_p, D_MODEL), lambda j: (0, 0)),
                  pl.BlockSpec(memory_space=pl.ANY),
                  pl.BlockSpec((1, MOD_TN), lambda j: (0, j))],
        out_specs=pl.BlockSpec((rows, MOD_TN), lambda j: (0, j)),
        scratch_shapes=[pltpu.VMEM((RING_IN, D_MODEL, MOD_TN), F32), pltpu.SemaphoreType.DMA((RING_IN,))],
        compiler_params=_params("arbitrary"),
        name="modulation",
    )(c_sample, c_prompt, ada_w, ada_b)


NORM_ROWS = 128
NORM_TM = 512


def _mod_rows(ref, rows, tm, rows_per_mod):
    if rows_per_mod == 1:
        return ref[rows, :]
    return ref[pl.ds((pl.program_id(0) * tm) // rows_per_mod, 1), :]


def _mod_spec(tm, rows_per_mod, mod_row0, col):
    if rows_per_mod == 1:
        assert mod_row0 % tm == 0
        return pl.BlockSpec((tm, D_MODEL), lambda i, *_: (mod_row0 // tm + i, col))
    assert mod_row0 % SUBLANES == 0
    return pl.BlockSpec((SUBLANES, D_MODEL), lambda i, *_: (mod_row0 // SUBLANES, col))


def _norm_mod(x, g, scale, shift):
    xn = x * lax.rsqrt(jnp.mean(x * x, axis=-1, keepdims=True) + EPS) * g
    return (xn * (1.0 + scale) + shift).astype(BF16)


def _norm_kernel(x_hbm, shift_ref, scale_ref, g_ref, h_hbm, x_ring, h_ring, in_sem, out_sem,
                 *, rows_per_mod, n_steps):
    tm = x_ring.shape[1]
    t = pl.program_id(0)

    def rows_of(step):
        return pl.ds(pl.multiple_of(step * tm, tm), tm)

    def in_copy(step):
        return _ring_copy(x_hbm.at[rows_of(step)], x_ring, in_sem, step)

    def out_copy(step):
        return _ring_copy(h_hbm.at[rows_of(step)], h_ring, out_sem, step, to_hbm=True)

    _ring_begin(t, n_steps, in_copy, out_copy)
    x_ref, h_ref = x_ring.at[lax.rem(t, RING_IN)], h_ring.at[lax.rem(t, RING_OUT)]

    def body(r, carry):
        rows = pl.ds(pl.multiple_of(r * NORM_ROWS, NORM_ROWS), NORM_ROWS)
        h_ref[rows, :] = _norm_mod(x_ref[rows, :], g_ref[...], _mod_rows(scale_ref, rows, tm, rows_per_mod),
                                   _mod_rows(shift_ref, rows, tm, rows_per_mod))
        return carry
    lax.fori_loop(0, tm // NORM_ROWS, body, 0)
    _ring_end(t, n_steps, out_copy)


def _norm(x, mod, g_pre, *, tm, rows_per_mod, mod_row0):
    m = x.shape[0]
    n_steps = m // tm
    assert (rows_per_mod == 1 or m // rows_per_mod <= SUBLANES) and n_steps >= 2
    mod_spec = lambda col: _mod_spec(tm, rows_per_mod, mod_row0, col)
    return pl.pallas_call(
        functools.partial(_norm_kernel, rows_per_mod=rows_per_mod, n_steps=n_steps),
        out_shape=jax.ShapeDtypeStruct((m, D_MODEL), BF16),
        grid=(n_steps,),
        in_specs=[pl.BlockSpec(memory_space=pl.ANY), mod_spec(0), mod_spec(1),
                  pl.BlockSpec((1, D_MODEL), lambda i: (0, 0))],
        out_specs=pl.BlockSpec(memory_space=pl.ANY),
        scratch_shapes=[pltpu.VMEM((RING_IN, tm, D_MODEL), F32), pltpu.VMEM((RING_OUT, tm, D_MODEL), BF16),
                        pltpu.SemaphoreType.DMA((RING_IN,)), pltpu.SemaphoreType.DMA((RING_OUT,))],
        compiler_params=_params("arbitrary"),
        name="norm",
    )(x, mod, mod, g_pre)


IN_TM = 1024
IN_TN = 2048
IN_DOT_COLS = 512


def _in_proj_kernel(h_ref, xs_ref, shift_s_ref, scale_s_ref, g_ref, wa_ref, wb_ref, wo_ref, w_in_hbm, w_mg_hbm,
                    z_ref, zs_ref, wa_bf_ref, wb_bf_ref, wo_bf_ref, wbf0_ref, wbf1_ref, stage_ref, hs_ref, sem,
                    *, n_in_tiles, n_j, n_i):
    j, i = pl.program_id(0), pl.program_id(1)
    t = j * n_i + i
    total = n_j * n_i
    tn = wbf0_ref.shape[1]
    chunk = stage_ref.shape[1]

    def chunk_copy(w_hbm, col0, r, b):
        rows = pl.ds(pl.multiple_of(r * chunk, chunk), chunk)
        cols = pl.ds(col0 if isinstance(col0, int) else pl.multiple_of(col0, tn), tn)
        return pltpu.make_async_copy(w_hbm.at[rows, cols], stage_ref.at[b], sem.at[b])

    def start_chunk(g):
        g = lax.rem(jnp.asarray(g, jnp.int32), total)
        jt, r, b = lax.div(g, n_i), lax.rem(g, n_i), lax.rem(g, 2)

        @pl.when(jt < n_in_tiles)
        def _():
            chunk_copy(w_in_hbm, jt * tn, r, b).start()

        @pl.when(jt >= n_in_tiles)
        def _():
            chunk_copy(w_mg_hbm, (jt - n_in_tiles) * tn, r, b).start()

    def land_chunk(g, dst_ref):
        g = jnp.asarray(g, jnp.int32)
        r, b = lax.rem(g, n_i), lax.rem(g, 2)
        chunk_copy(w_in_hbm, 0, r, b).wait()
        dst_ref[pl.ds(pl.multiple_of(r * chunk, chunk), chunk), :] = stage_ref[b].astype(BF16)

    @pl.when(t == 0)
    def _():
        hs_ref[...] = _norm_mod(xs_ref[...], g_ref[...], scale_s_ref[...], shift_s_ref[...])
        start_chunk(0)

        def body(g, carry):
            start_chunk(g + 1)
            land_chunk(g, wbf0_ref)
            return carry
        lax.fori_loop(0, n_i, body, 0)

    @pl.when(t + 1 < total)
    def _():
        start_chunk(t + n_i + 1)

    def multiply(w_cur, w_nxt):
        land_chunk(t + n_i, w_nxt)
        wa_bf_ref[...] = wa_ref[...].astype(BF16)
        wb_bf_ref[...] = wb_ref[...].astype(BF16)
        wo_bf_ref[...] = wo_ref[...].astype(BF16)
        for c0 in range(0, tn, IN_DOT_COLS):
            cols = slice(c0, c0 + IN_DOT_COLS)
            z_ref[:, cols] = jnp.dot(h_ref[...], w_cur[:, cols], preferred_element_type=F32)

        @pl.when(i == 0)
        def _():
            zs_ref[...] = jnp.dot(hs_ref[...], w_cur[...], preferred_element_type=F32)

    @pl.when(lax.rem(j, 2) == 0)
    def _():
        multiply(wbf0_ref, wbf1_ref)

    @pl.when(lax.rem(j, 2) == 1)
    def _():
        multiply(wbf1_ref, wbf0_ref)


def _in_proj(h, xs, mod, g_pre, w_in, w_mg, w_a, w_b, w_o):
    m, ms = h.shape[0], xs.shape[0]
    n_in, n_mg = w_in.shape[1] // IN_TN, w_mg.shape[1] // IN_TN
    n_i = m // IN_TM
    chunk = D_MODEL // n_i
    assert chunk * n_i == D_MODEL and chunk % 16 == 0 and (n_in + n_mg) * n_i >= CAST_STEPS
    const = lambda shape, col: pl.BlockSpec(shape, lambda j, i: (0, col), pipeline_mode=pl.Buffered(1))
    wa_in, wa_out, wa_shape = _cast_specs(w_a, lambda j, i: j * n_i + i)
    wb_in, wb_out, wb_shape = _cast_specs(w_b, lambda j, i: j * n_i + i)
    wo_in, wo_out, wo_shape = _cast_specs(w_o, lambda j, i: j * n_i + i)
    return pl.pallas_call(
        functools.partial(_in_proj_kernel, n_in_tiles=n_in, n_j=n_in + n_mg, n_i=n_i),
        out_shape=(jax.ShapeDtypeStruct((m, D_Z), F32), jax.ShapeDtypeStruct((ms, D_Z), F32), wa_shape, wb_shape,
                   wo_shape),
        grid=(n_in + n_mg, n_i),
        in_specs=[pl.BlockSpec((IN_TM, D_MODEL), lambda j, i: (i, 0)),
                  const((ms, D_MODEL), 0), const((ms, D_MODEL), 0), const((ms, D_MODEL), 1),
                  const((1, D_MODEL), 0), wa_in, wb_in, wo_in,
                  pl.BlockSpec(memory_space=pl.ANY), pl.BlockSpec(memory_space=pl.ANY)],
        out_specs=(pl.BlockSpec((IN_TM, IN_TN), lambda j, i: (i, j)),
                   pl.BlockSpec((ms, IN_TN), lambda j, i: (0, j)), wa_out, wb_out, wo_out),
        scratch_shapes=[pltpu.VMEM((D_MODEL, IN_TN), BF16), pltpu.VMEM((D_MODEL, IN_TN), BF16),
                        pltpu.VMEM((2, chunk, IN_TN), F32),
                        pltpu.VMEM((ms, D_MODEL), BF16),
                        pltpu.SemaphoreType.DMA((2,))],
        compiler_params=_params("arbitrary", "arbitrary"),
        name="in_proj",
    )(h, xs, mod, mod, g_pre, w_a, w_b, w_o, w_in, w_mg)


def _rotate(x, cos, sin_signed):
    return x * cos + pltpu.roll(x, HEAD_DK // 2, 1) * sin_signed


def _group_norm_gate(o, gn, bg):
    mu = jnp.mean(o, axis=-1, keepdims=True)
    d = o - mu
    var = jnp.mean(d * d, axis=-1, keepdims=True)
    return d * lax.rsqrt(var + EPS) * gn * _silu(bg)


def _pool_project(pooled, pw, ps, ag):
    mixed = jnp.dot(pooled.astype(BF16), pw.astype(BF16), preferred_element_type=F32)
    return mixed * ps * _silu(ag)


PROJ_SPLIT = 8
PROJ_COLS = D_MODEL // PROJ_SPLIT


def _merge_chunk(y_a, y_b, j, gp_ref, bm_ref, wa_ref, wb_ref):
    lo, hi = j * PROJ_COLS, (j + 1) * PROJ_COLS
    ya = jnp.dot(y_a, wa_ref[:, lo:hi], preferred_element_type=F32)
    yb = jnp.dot(y_b, wb_ref[:, lo:hi], preferred_element_type=F32)
    g_a = jax.nn.sigmoid(gp_ref[:, lo:hi] + bm_ref[:, lo:hi])
    g_b = jax.nn.sigmoid(gp_ref[:, D_MODEL + lo:D_MODEL + hi] + bm_ref[:, D_MODEL + lo:D_MODEL + hi])
    return (g_a * ya + g_b * yb).astype(BF16)


def _out_chunk(merged, j, wo_ref):
    return jnp.dot(merged, wo_ref[:, j * PROJ_COLS:(j + 1) * PROJ_COLS], preferred_element_type=F32)


def _post_norm(o, gpost_ref):
    return o * lax.rsqrt(jnp.mean(o * o, axis=-1, keepdims=True) + EPS) * gpost_ref[...]


def _merge_project(y_a, y_b, gp_ref, bm_ref, wa_ref, wb_ref, wo_ref, gpost_ref):
    merged = jnp.concatenate([_merge_chunk(y_a, y_b, j, gp_ref, bm_ref, wa_ref, wb_ref)
                              for j in range(PROJ_SPLIT)], axis=1)
    o = jnp.concatenate([_out_chunk(merged, j, wo_ref) for j in range(PROJ_SPLIT)], axis=1)
    return _post_norm(o, gpost_ref)


MIX_ROWS = 2 * CHUNK
MIX_HEADS = 4
MIX_VMEM_LIMIT_BYTES = 62 * 1024 * 1024
Z_AX, Z_AG, Z_Q, Z_K, Z_V, Z_BG = 0, D_POOL, 2 * D_POOL, 2 * D_POOL + D_QK, 2 * D_POOL + 2 * D_QK, D_IN - D_V


def _prompt_kernel(cdec_ref, z_hbm, rot_ref, dmask_ref, qdec_ref, kdec_ref, pw_ref, ps_ref, gn_ref,
                   gp_ref, bm_ref, wa_ref, wb_ref,
                   m_ref, npool_ref, nret_ref, ya0_ref, yb0_ref, ya1_ref, yb1_ref, ext_ref, z_ring, z_sem,
                   *, tiles_per_seq):
    s = pl.program_id(0)
    n_tiles = pl.num_programs(0) - 1
    live = s < n_tiles
    c = lax.rem(jnp.minimum(s, n_tiles - 1), tiles_per_seq)
    rows = z_ring.shape[1]

    def z_copy(step):
        tile = jnp.minimum(step, n_tiles - 1)
        return _ring_copy(z_hbm.at[pl.ds(pl.multiple_of(tile * rows, rows), rows), pl.ds(0, D_IN)], z_ring, z_sem, step)

    _ring_begin(s, n_tiles + 1, z_copy)
    z_ref = z_ring.at[lax.rem(s, RING_IN)]

    @pl.when(s == 0)
    def _():
        for ref in (ya0_ref, yb0_ref, ya1_ref, yb1_ref):
            ref[...] = jnp.zeros(ref.shape, BF16)

    @pl.when(c == 0)
    def _():
        ext_ref[0:HALO, :] = jnp.zeros((HALO, D_POOL), F32)
        nret_ref[...] = jnp.zeros(nret_ref.shape, F32)

    def step(ya_rd, yb_rd, ya_wr, yb_wr):
        y_a, y_b = ya_rd[...], yb_rd[...]
        xa = z_ref[:, Z_AX:Z_AX + D_POOL]
        ext_ref[HALO:HALO + rows, :] = xa
        pos = c * rows + lax.broadcasted_iota(jnp.int32, (rows, 1), 0)
        nt = (((1,), (1,)), ((), ()))
        tn = (((0,), (0,)), ((), ()))

        def merge_piece(j):
            m_ref[:, j * PROJ_COLS:(j + 1) * PROJ_COLS] = _merge_chunk(y_a, y_b, j, gp_ref, bm_ref, wa_ref, wb_ref)

        def pool_group(g):
            w = POOL_WINDOWS[g]
            lo, hi = g * POOL_GROUP_DIM, (g + 1) * POOL_GROUP_DIM
            acc = ext_ref[:, lo:hi]
            span = 1
            while span < w:
                acc = acc + pltpu.roll(acc, span, 0)
                span *= 2
            inv_cnt = 1.0 / jnp.minimum(pos + 1, w).astype(F32)
            pooled = acc[HALO:, :] * inv_cnt - xa[:, lo:hi]
            ya = _pool_project(pooled, pw_ref[g], ps_ref[:, lo:hi], z_ref[:, Z_AG + lo:Z_AG + hi])
            ya_wr[:, lo:hi] = ya.astype(BF16)

        per_pool = PROJ_SPLIT // len(POOL_WINDOWS)
        assert per_pool * len(POOL_WINDOWS) == PROJ_SPLIT
        fillers = [f for g in range(len(POOL_WINDOWS))
                   for f in [functools.partial(merge_piece, g * per_pool + j) for j in range(per_pool)]
                   + [functools.partial(pool_group, g)]]
        n_sub = rows // CHUNK
        n_slots = 2 * (N_HEADS // MIX_HEADS) * n_sub
        slot = [0]

        def fill():
            lo, hi = (slot[0] * len(fillers)) // n_slots, ((slot[0] + 1) * len(fillers)) // n_slots
            slot[0] += 1
            for f in fillers[lo:hi]:
                f()

        for h0 in range(0, N_HEADS, MIX_HEADS):
            heads = range(h0, h0 + MIX_HEADS)
            s_cur = {h: nret_ref[h] for h in heads}
            for ci in range(n_sub):
                rs = slice(ci * CHUNK, (ci + 1) * CHUNK)
                cos = rot_ref[rs, :HEAD_DK]
                sin = rot_ref[rs, HEAD_DK:]
                q = {h: _rotate(z_ref[rs, Z_Q + h * HEAD_DK:Z_Q + (h + 1) * HEAD_DK], cos, sin) for h in heads}
                k = {h: _rotate(z_ref[rs, Z_K + h * HEAD_DK:Z_K + (h + 1) * HEAD_DK], cos, sin) * K_SCALE
                     for h in heads}
                v = {h: z_ref[rs, Z_V + h * HEAD_DV:Z_V + (h + 1) * HEAD_DV].astype(BF16) for h in heads}
                fill()
                scores = {h: lax.dot_general(q[h].astype(BF16), k[h].astype(BF16), nt, preferred_element_type=F32)
                          for h in heads}
                kv = {h: lax.dot_general((k[h] * kdec_ref[h]).astype(BF16), v[h], tn, preferred_element_type=F32)
                      for h in heads}
                lhs = {h: jnp.concatenate([(scores[h] * dmask_ref[h]).astype(BF16),
                                           (q[h] * qdec_ref[h]).astype(BF16)], axis=1) for h in heads}
                fill()
                for h in heads:
                    rhs = jnp.concatenate([v[h], s_cur[h].astype(BF16)], axis=0)
                    o = jnp.dot(lhs[h], rhs, preferred_element_type=F32)
                    s_cur[h] = cdec_ref[h] * s_cur[h] + kv[h]
                    vs = slice(h * HEAD_DV, (h + 1) * HEAD_DV)
                    bg = z_ref[rs, Z_BG + h * HEAD_DV:Z_BG + (h + 1) * HEAD_DV]
                    yb_wr[rs, vs] = _group_norm_gate(o, gn_ref[:, vs], bg).astype(BF16)
            for h in heads:
                nret_ref[h] = jnp.where(live, s_cur[h], nret_ref[h])
        assert slot[0] == n_slots

    @pl.when(lax.rem(s, 2) == 0)
    def _():
        step(ya1_ref, yb1_ref, ya0_ref, yb0_ref)

    @pl.when(lax.rem(s, 2) == 1)
    def _():
        step(ya0_ref, yb0_ref, ya1_ref, yb1_ref)

    @pl.when(c == tiles_per_seq - 1)
    def _():
        npool_ref[...] = ext_ref[HALO + rows - POOL_BUF:HALO + rows, :]

    ext_ref[0:HALO, :] = ext_ref[rows:rows + HALO, :]


def _prompt_mix_merge(z, batch, seq, tables, pool_w, pool_scale, gn_g, b_merge, w_a, w_b):
    rot, dmask, qdec, kdec, cdec = tables
    tps = seq // MIX_ROWS
    n_tiles = batch * tps
    cur = lambda s: jnp.minimum(s, n_tiles - 1)
    prev = lambda s: jnp.maximum(s - 1, 0)
    m = batch * seq
    return pl.pallas_call(
        functools.partial(_prompt_kernel, tiles_per_seq=tps),
        out_shape=(jax.ShapeDtypeStruct((m, D_MODEL), BF16),
                   jax.ShapeDtypeStruct((batch, POOL_BUF, D_POOL), F32),
                   jax.ShapeDtypeStruct((batch, N_HEADS, HEAD_DK, HEAD_DV), F32)),
        grid=(n_tiles + 1,),
        in_specs=[pl.BlockSpec(memory_space=pltpu.SMEM),
                  pl.BlockSpec(memory_space=pl.ANY),
                  pl.BlockSpec((MIX_ROWS, 2 * HEAD_DK), lambda s: (cur(s) % tps, 0)),
                  _resident(dmask.shape), _resident(qdec.shape), _resident(kdec.shape),
                  _resident(pool_w.shape), _resident(pool_scale.shape), _resident(gn_g.shape),
                  pl.BlockSpec((MIX_ROWS, 2 * D_MODEL), lambda s: (prev(s), D_IN // (2 * D_MODEL))),
                  _resident(b_merge.shape), _resident(w_a.shape), _resident(w_b.shape)],
        out_specs=(pl.BlockSpec((MIX_ROWS, D_MODEL), lambda s: (prev(s), 0)),
                   pl.BlockSpec((None, POOL_BUF, D_POOL), lambda s: (cur(s) // tps, 0, 0)),
                   pl.BlockSpec((None, N_HEADS, HEAD_DK, HEAD_DV), lambda s: (cur(s) // tps, 0, 0, 0))),
        scratch_shapes=[pltpu.VMEM((MIX_ROWS, D_POOL), BF16), pltpu.VMEM((MIX_ROWS, D_V), BF16),
                        pltpu.VMEM((MIX_ROWS, D_POOL), BF16), pltpu.VMEM((MIX_ROWS, D_V), BF16),
                        pltpu.VMEM((HALO + MIX_ROWS, D_POOL), F32),
                        pltpu.VMEM((RING_IN, MIX_ROWS, D_IN), F32), pltpu.SemaphoreType.DMA((RING_IN,))],
        compiler_params=_params("arbitrary", vmem_limit_bytes=MIX_VMEM_LIMIT_BYTES),
        name="prompt_mix_merge",
    )(cdec, z, rot, dmask, qdec, kdec, pool_w, pool_scale, gn_g, z, b_merge, w_a, w_b)


SAMPLE_TILE = 8
SAMPLE_HEADS = 4
TAIL_ROWS = 256


def _tail_kernel(dec_ref, m_ref, x_ref, gate_ref, wo_ref, gpost_ref,
                 zs_ref, rot_ref, pw_ref, ps_ref, gn_ref, spool_ref, sret_hbm,
                 y_ref, ya_ref, yb_ref, npool_ref, nret_hbm, o_ref, in_ring, out_ring, in_sem, out_sem,
                 *, inv_cnt, rows_per_mod, n_steps):
    bt = zs_ref.shape[0]
    n_hg = N_HEADS // SAMPLE_HEADS
    assert n_hg % RING_OUT == 0
    t = pl.program_id(0) * n_hg + pl.program_id(1)

    def state_block(hbm, step):
        seq0 = pl.multiple_of(lax.div(step, n_hg) * bt, bt)
        head0 = pl.multiple_of(lax.rem(step, n_hg) * SAMPLE_HEADS, SAMPLE_HEADS)
        return hbm.at[pl.ds(seq0, bt), pl.ds(head0, SAMPLE_HEADS)]

    def in_copy(step):
        return _ring_copy(state_block(sret_hbm, step), in_ring, in_sem, step)

    def out_copy(step):
        return _ring_copy(state_block(nret_hbm, step), out_ring, out_sem, step, to_hbm=True)

    _ring_begin(t, n_steps, in_copy, out_copy)
    s_in = in_ring.at[lax.rem(t, RING_IN)]

    def step(hg):
        s_out = out_ring.at[hg % RING_OUT]
        o = _post_norm(jnp.dot(m_ref[...], wo_ref[...], preferred_element_type=F32), gpost_ref)
        tile = pl.program_id(0) * n_hg + hg
        gate = gate_ref[pl.ds((tile * x_ref.shape[0]) // rows_per_mod, 1), :]
        y_ref[...] = x_ref[...] + gate * o

        if hg == 0:
            xa = zs_ref[:, Z_AX:Z_AX + D_POOL]
            run = xa
            wins = {1: xa}
            for j in range(1, POOL_BUF + 1):
                run = run + spool_ref[POOL_BUF - j]
                wins[j + 1] = run
            for g, w in enumerate(POOL_WINDOWS):
                lo, hi = g * POOL_GROUP_DIM, (g + 1) * POOL_GROUP_DIM
                pooled = wins[w][:, lo:hi] * inv_cnt[g] - xa[:, lo:hi]
                ya = _pool_project(pooled, pw_ref[g], ps_ref[:, lo:hi], zs_ref[:, Z_AG + lo:Z_AG + hi])
                ya_ref[:, lo:hi] = ya.astype(BF16)
            for j in range(POOL_BUF - 1):
                npool_ref[j] = spool_ref[j + 1]
            npool_ref[POOL_BUF - 1] = xa

        cos = rot_ref[:, :HEAD_DK]
        sin = rot_ref[:, HEAD_DK:]
        for hl in range(SAMPLE_HEADS):
            h = hg * SAMPLE_HEADS + hl
            vs = slice(h * HEAD_DV, (h + 1) * HEAD_DV)
            q = _rotate(zs_ref[:, Z_Q + h * HEAD_DK:Z_Q + (h + 1) * HEAD_DK], cos, sin)
            k = _rotate(zs_ref[:, Z_K + h * HEAD_DK:Z_K + (h + 1) * HEAD_DK], cos, sin) * K_SCALE
            v = zs_ref[:, Z_V + h * HEAD_DV:Z_V + (h + 1) * HEAD_DV]
            score = jnp.sum(q * k, axis=1, keepdims=True) * dec_ref[0, h]
            q_cols = jnp.transpose(q * dec_ref[1, h])
            k_cols = jnp.transpose(k * dec_ref[2, h])
            for r in range(bt):
                s_old = s_in[r, hl]
                v_row = v[r:r + 1, :]
                o_row = score[r:r + 1, :] * v_row + jnp.sum(q_cols[:, r:r + 1] * s_old, axis=0, keepdims=True)
                s_out[r, hl] = dec_ref[3, h] * s_old + k_cols[:, r:r + 1] * v_row
                o_ref[r:r + 1, vs] = o_row
        for hl in range(SAMPLE_HEADS):
            h = hg * SAMPLE_HEADS + hl
            vs = slice(h * HEAD_DV, (h + 1) * HEAD_DV)
            bg = zs_ref[:, Z_BG + h * HEAD_DV:Z_BG + (h + 1) * HEAD_DV]
            yb_ref[:, vs] = _group_norm_gate(o_ref[:, vs], gn_ref[:, vs], bg).astype(BF16)

    for hg in range(n_hg):
        pl.when(pl.program_id(1) == hg)(functools.partial(step, hg))

    _ring_end(t, n_steps, out_copy)


def _tail(merged, x, mod, w_o, g_post, zs, state_pool, state_ret, tables, inv_cnt, pool_w, pool_scale, gn_g,
          *, rows_per_mod, mod_row0):
    rot, dec = tables
    m, batch = x.shape[0], zs.shape[0]
    bt, hs = SAMPLE_TILE, SAMPLE_HEADS
    n_hg = N_HEADS // hs
    assert m == (batch // bt) * n_hg * TAIL_ROWS and mod_row0 % SUBLANES == 0 and m // rows_per_mod <= SUBLANES
    rows = lambda i, g: (i * n_hg + g, 0)
    per_tile = lambda width: pl.BlockSpec((bt, width), lambda i, g: (i, 0))
    return pl.pallas_call(
        functools.partial(_tail_kernel, inv_cnt=inv_cnt, rows_per_mod=rows_per_mod, n_steps=(batch // bt) * n_hg),
        out_shape=(jax.ShapeDtypeStruct((m, D_MODEL), F32),
                   jax.ShapeDtypeStruct((batch, D_POOL), BF16),
                   jax.ShapeDtypeStruct((batch, D_V), BF16),
                   jax.ShapeDtypeStruct(state_pool.shape, state_pool.dtype),
                   jax.ShapeDtypeStruct(state_ret.shape, state_ret.dtype)),
        grid=(batch // bt, n_hg),
        in_specs=[pl.BlockSpec(memory_space=pltpu.SMEM),
                  pl.BlockSpec((TAIL_ROWS, D_MODEL), rows), pl.BlockSpec((TAIL_ROWS, D_MODEL), rows),
                  pl.BlockSpec((SUBLANES, D_MODEL), lambda i, g: (mod_row0 // SUBLANES, 2)),
                  _resident(w_o.shape), _resident(g_post.shape),
                  per_tile(D_IN), _resident(rot.shape),
                  _resident(pool_w.shape), _resident(pool_scale.shape), _resident(gn_g.shape),
                  pl.BlockSpec((POOL_BUF, bt, D_POOL), lambda i, g: (0, i, 0)),
                  pl.BlockSpec(memory_space=pl.ANY)],
        out_specs=(pl.BlockSpec((TAIL_ROWS, D_MODEL), rows),
                   per_tile(D_POOL), per_tile(D_V),
                   pl.BlockSpec((POOL_BUF, bt, D_POOL), lambda i, g: (0, i, 0)),
                   pl.BlockSpec(memory_space=pl.ANY)),
        scratch_shapes=[pltpu.VMEM((bt, D_V), F32),
                        pltpu.VMEM((RING_IN, bt, hs, HEAD_DK, HEAD_DV), F32),
                        pltpu.VMEM((RING_OUT, bt, hs, HEAD_DK, HEAD_DV), F32),
                        pltpu.SemaphoreType.DMA((RING_IN,)), pltpu.SemaphoreType.DMA((RING_OUT,))],
        compiler_params=_params("arbitrary", "arbitrary"),
        name="tail",
    )(dec, merged, x, mod, w_o, g_post, zs, rot, pool_w, pool_scale, gn_g, state_pool, state_ret)


def _out_proj_kernel(ya_ref, yb_ref, gp_ref, x_ref, gate_ref, bm_ref, wa_ref, wb_ref, wo_ref, gpost_ref, y_ref,
                     *, rows_per_mod):
    o = _merge_project(ya_ref[...], yb_ref[...], gp_ref, bm_ref, wa_ref, wb_ref, wo_ref, gpost_ref)
    gate = _mod_rows(gate_ref, slice(None), x_ref.shape[0], rows_per_mod)
    y_ref[...] = x_ref[...] + gate * o


def _out_proj(ya, yb, z, x, mod, b_merge, w_a, w_b, w_o, g_post, *, tm, rows_per_mod, mod_row0):
    m = x.shape[0]
    return pl.pallas_call(
        functools.partial(_out_proj_kernel, rows_per_mod=rows_per_mod),
        out_shape=jax.ShapeDtypeStruct((m, D_MODEL), F32),
        grid=(m // tm,),
        in_specs=[pl.BlockSpec((tm, D_POOL), lambda i: (i, 0)),
                  pl.BlockSpec((tm, D_V), lambda i: (i, 0)),
                  pl.BlockSpec((tm, 2 * D_MODEL), lambda i: (i, D_IN // (2 * D_MODEL))),
                  pl.BlockSpec((tm, D_MODEL), lambda i: (i, 0)),
                  _mod_spec(tm, rows_per_mod, mod_row0, 2),
                  _resident(b_merge.shape), _resident(w_a.shape), _resident(w_b.shape), _resident(w_o.shape),
                  _resident(g_post.shape)],
        out_specs=pl.BlockSpec((tm, D_MODEL), lambda i: (i, 0)),
        compiler_params=_params("parallel"),
        name="out_proj",
    )(ya, yb, z, x, mod, b_merge, w_a, w_b, w_o, g_post)


def _rotary_tables(start, length):
    half = HEAD_DK // 2
    inv = ROPE_BASE ** (-np.arange(half, dtype=np.float64) / half)
    ang = (start + np.arange(length, dtype=np.float64))[:, None] * inv[None, :]
    cos, sin = np.cos(ang), np.sin(ang)
    return (np.concatenate([cos, cos], axis=-1).astype(np.float32),
            np.concatenate([-sin, sin], axis=-1).astype(np.float32))


def _decay_tables(c):
    lg = np.log1p(-np.power(2.0, -5.0 - np.arange(N_HEADS, dtype=np.float64)))
    idx = np.arange(c, dtype=np.float64)
    diff = idx[:, None] - idx[None, :]
    dmask = np.where(diff[None] >= 0, np.exp(np.maximum(diff, 0.0)[None] * lg[:, None, None]), 0.0)
    q_dec = np.exp((idx + 1.0)[None, :] * lg[:, None])
    k_dec = np.exp((c - 1.0 - idx)[None, :] * lg[:, None])
    chunk_dec = np.exp(c * lg)
    return tuple(a.astype(np.float32) for a in (dmask, q_dec, k_dec, chunk_dec))


def _layer(xp, xs, c_prompt, c_sample, state_pool, state_ret, ada_w, ada_b, g_pre, g_post, w_in, pool_w, pool_scale, gn_g,
           w_a_proj, w_b_proj, w_merge, b_merge, w_out):
    batch, seq, _ = xp.shape
    dec_batch, dec_seq, _ = xs.shape
    assert dec_seq == 1 and seq % CHUNK == 0

    row = lambda v: v.reshape(1, -1)
    g_pre, g_post, pool_scale, gn_g, b_merge = map(row, (g_pre, g_post, pool_scale, gn_g, b_merge))

    mod = _modulation(c_sample, c_prompt, ada_w, row(ada_b))

    x2 = xp.reshape(batch * seq, D_MODEL)
    xs2 = xs.reshape(dec_batch, D_MODEL)
    h = _norm(x2, mod, g_pre, tm=NORM_TM, rows_per_mod=seq, mod_row0=dec_batch)
    z, zs, w_a, w_b, w_o = _in_proj(h, xs2, mod, g_pre, w_in, w_merge, w_a_proj, w_b_proj, w_out)

    cos, sin = _rotary_tables(0, seq)
    dmask, q_dec, k_dec, chunk_dec = _decay_tables(CHUNK)
    wide = lambda d: np.ascontiguousarray(np.broadcast_to(d[:, :, None], (N_HEADS, CHUNK, HEAD_DK)))
    rot = np.concatenate([cos, sin], axis=1)
    merged, pool_p, ret_p = _prompt_mix_merge(z, batch, seq, (rot, dmask, wide(q_dec), wide(k_dec), chunk_dec),
                                              pool_w, pool_scale, gn_g, b_merge, w_a, w_b)

    cos_s, sin_s = _rotary_tables(PAST_LEN, 1)
    dmask_s, q_dec_s, k_dec_s, chunk_dec_s = _decay_tables(1)
    dec_s = np.stack([dmask_s[:, 0, 0], q_dec_s[:, 0], k_dec_s[:, 0], chunk_dec_s])
    inv_cnt = tuple(1.0 / min(PAST_LEN + 1, w) for w in POOL_WINDOWS)
    yp, ya_s, yb_s, pool_s, ret_s = _tail(merged, x2, mod, w_o, g_post, zs, jnp.transpose(state_pool, (1, 0, 2)),
                                          state_ret, (np.concatenate([cos_s, sin_s], axis=1), dec_s), inv_cnt,
                                          pool_w, pool_scale, gn_g,
                                          rows_per_mod=seq, mod_row0=dec_batch)
    pool_s = jnp.transpose(pool_s, (1, 0, 2))
    ys = _out_proj(ya_s, yb_s, zs, xs2, mod, b_merge, w_a, w_b, w_o, g_post, tm=dec_batch, rows_per_mod=1, mod_row0=0)

    return yp.reshape(xp.shape), ys.reshape(xs.shape), pool_p, ret_p, pool_s, ret_s


def kernel(x_prompt, x_sample, state_pool, state_ret, c_prompt, c_sample, ada_w, ada_b, g_pre, g_post,
           w_in, pool_w, pool_scale, gn_g, w_a_proj, w_b_proj, w_merge, b_merge, w_out):
    depth = ada_w.shape[0]
    xp, xs = x_prompt, x_sample
    pool_p, ret_p, pool_s, ret_s = [], [], [], []
    for l in range(depth):
        xp, xs, bp, sp, bs, ss = _layer(
            xp, xs, c_prompt, c_sample, state_pool[l], state_ret[l], ada_w[l], ada_b[l], g_pre[l], g_post[l], w_in[l],
            pool_w[l], pool_scale[l], gn_g[l].reshape(-1), w_a_proj[l], w_b_proj[l], w_merge[l], b_merge[l],
            w_out[l])
        pool_p.append(bp)
        ret_p.append(sp)
        pool_s.append(bs)
        ret_s.append(ss)
    return (xp, xs, jnp.stack(pool_p), jnp.stack(ret_p), jnp.stack(pool_s), jnp.stack(ret_s))
```

```python
import functools

import jax
import jax.numpy as jnp
import numpy as np
from jax import lax
from jax.experimental import pallas as pl
from jax.experimental.pallas import tpu as pltpu

F32 = jnp.float32
BF16 = jnp.bfloat16

D_MODEL = 2048
PAST_LEN = 16384
D_POOL = D_MODEL // 2
POOL_WINDOWS = (2, 4, 8, 16)
POOL_GROUP_DIM = D_POOL // len(POOL_WINDOWS)
POOL_BUF = max(POOL_WINDOWS) - 1
N_HEADS = 8
HEAD_DK = D_MODEL // 16
HEAD_DV = D_MODEL // 8
D_QK = N_HEADS * HEAD_DK
D_V = N_HEADS * HEAD_DV
CHUNK = 128
ROPE_BASE = 10000.0
EPS = 1e-6
D_IN = 2 * D_POOL + 2 * D_QK + 2 * D_V
D_Z = D_IN + 2 * D_MODEL
K_SCALE = HEAD_DK ** -0.5

VMEM_LIMIT_BYTES = 56 * 1024 * 1024
SUBLANES = 8
HALO = 16
CAST_STEPS = 32


def _params(*semantics):
    return pltpu.CompilerParams(dimension_semantics=semantics, vmem_limit_bytes=VMEM_LIMIT_BYTES)


def _cast_specs(w, step_of):
    rows = w.shape[0] // CAST_STEPS
    assert rows * CAST_STEPS == w.shape[0] and rows % (2 * SUBLANES) == 0
    spec = pl.BlockSpec((rows, w.shape[1]), lambda *idx: (jnp.minimum(step_of(*idx), CAST_STEPS - 1), 0))
    return spec, spec, jax.ShapeDtypeStruct(w.shape, BF16)


def _resident(shape):
    return pl.BlockSpec(shape, lambda *_: (0,) * len(shape), pipeline_mode=pl.Buffered(1))


def _silu(x):
    return x * jax.nn.sigmoid(x)


RING_IN, RING_OUT = 3, 2


def _ring_copy(hbm_block, ring, sem, step, *, to_hbm=False):
    slot = lax.rem(step, ring.shape[0])
    src, dst = (ring.at[slot], hbm_block) if to_hbm else (hbm_block, ring.at[slot])
    return pltpu.make_async_copy(src, dst, sem.at[slot])


def _ring_begin(t, n_steps, in_copy, out_copy=None):
    @pl.when(t == 0)
    def _():
        for k in range(RING_IN - 1):
            in_copy(jnp.int32(k)).start()

    in_copy(t).wait()

    @pl.when(t + RING_IN - 1 < n_steps)
    def _():
        in_copy(t + RING_IN - 1).start()

    if out_copy is not None:
        @pl.when(t >= RING_OUT)
        def _():
            out_copy(t - RING_OUT).wait()


def _ring_end(t, n_steps, out_copy):
    out_copy(t).start()

    @pl.when(t == n_steps - 1)
    def _():
        for k in reversed(range(RING_OUT)):
            out_copy(t - k).wait()


MOD_TN = 512


def _mod_kernel(cs_ref, cp_ref, w_hbm, b_ref, o_ref, w_ring, sem, *, n_steps):
    t = pl.program_id(0)
    tn = w_ring.shape[2]

    def in_copy(step):
        return _ring_copy(w_hbm.at[:, pl.ds(pl.multiple_of(step * tn, tn), tn)], w_ring, sem, step)

    _ring_begin(t, n_steps, in_copy)
    n_pad = o_ref.shape[0] - cs_ref.shape[0] - cp_ref.shape[0]
    c = jnp.concatenate([cs_ref[...], cp_ref[...], jnp.zeros((n_pad, D_MODEL), F32)], axis=0)
    w = w_ring[lax.rem(t, RING_IN)].astype(BF16)
    o_ref[...] = jnp.dot(_silu(c).astype(BF16), w, preferred_element_type=F32) + b_ref[...]


def _modulation(c_sample, c_prompt, ada_w, ada_b):
    ns, n_p = c_sample.shape[0], c_prompt.shape[0]
    assert ns % SUBLANES == 0
    rows = ns + n_p + (-n_p) % SUBLANES
    n_steps = 3 * D_MODEL // MOD_TN
    return pl.pallas_call(
        functools.partial(_mod_kernel, n_steps=n_steps),
        out_shape=jax.ShapeDtypeStruct((rows, 3 * D_MODEL), F32),
        grid=(n_steps,),
        in_specs=[pl.BlockSpec((ns, D_MODEL), lambda j: (0, 0)),
                  pl.BlockSpec((n_p, D_MODEL), lambda j: (0, 0)),
                  pl.BlockSpec(memory_space=pl.ANY),
                  pl.BlockSpec((1, MOD_TN), lambda j: (0, j))],
        out_specs=pl.BlockSpec((rows, MOD_TN), lambda j: (0, j)),
        scratch_shapes=[pltpu.VMEM((RING_IN, D_MODEL, MOD_TN), F32), pltpu.SemaphoreType.DMA((RING_IN,))],
        compiler_params=_params("arbitrary"),
        name="modulation",
    )(c_sample, c_prompt, ada_w, ada_b)


NORM_ROWS = 128
NORM_TM = 512


def _mod_rows(ref, rows, tm, rows_per_mod):
    if rows_per_mod == 1:
        return ref[rows, :]
    return ref[pl.ds((pl.program_id(0) * tm) // rows_per_mod, 1), :]


def _mod_spec(tm, rows_per_mod, mod_row0, col):
    if rows_per_mod == 1:
        assert mod_row0 % tm == 0
        return pl.BlockSpec((tm, D_MODEL), lambda i, *_: (mod_row0 // tm + i, col))
    assert mod_row0 % SUBLANES == 0
    return pl.BlockSpec((SUBLANES, D_MODEL), lambda i, *_: (mod_row0 // SUBLANES, col))


def _norm_mod(x, g, scale, shift):
    xn = x * lax.rsqrt(jnp.mean(x * x, axis=-1, keepdims=True) + EPS) * g
    return (xn * (1.0 + scale) + shift).astype(BF16)


def _norm_kernel(x_hbm, shift_ref, scale_ref, g_ref, h_hbm, x_ring, h_ring, in_sem, out_sem,
                 *, rows_per_mod, n_steps):
    tm = x_ring.shape[1]
    t = pl.program_id(0)

    def rows_of(step):
        return pl.ds(pl.multiple_of(step * tm, tm), tm)

    def in_copy(step):
        return _ring_copy(x_hbm.at[rows_of(step)], x_ring, in_sem, step)

    def out_copy(step):
        return _ring_copy(h_hbm.at[rows_of(step)], h_ring, out_sem, step, to_hbm=True)

    _ring_begin(t, n_steps, in_copy, out_copy)
    x_ref, h_ref = x_ring.at[lax.rem(t, RING_IN)], h_ring.at[lax.rem(t, RING_OUT)]

    def body(r, carry):
        rows = pl.ds(pl.multiple_of(r * NORM_ROWS, NORM_ROWS), NORM_ROWS)
        h_ref[rows, :] = _norm_mod(x_ref[rows, :], g_ref[...], _mod_rows(scale_ref, rows, tm, rows_per_mod),
                                   _mod_rows(shift_ref, rows, tm, rows_per_mod))
        return carry
    lax.fori_loop(0, tm // NORM_ROWS, body, 0)
    _ring_end(t, n_steps, out_copy)


def _norm(x, mod, g_pre, *, tm, rows_per_mod, mod_row0):
    m = x.shape[0]
    n_steps = m // tm
    assert (rows_per_mod == 1 or m // rows_per_mod <= SUBLANES) and n_steps >= 2
    mod_spec = lambda col: _mod_spec(tm, rows_per_mod, mod_row0, col)
    return pl.pallas_call(
        functools.partial(_norm_kernel, rows_per_mod=rows_per_mod, n_steps=n_steps),
        out_shape=jax.ShapeDtypeStruct((m, D_MODEL), BF16),
        grid=(n_steps,),
        in_specs=[pl.BlockSpec(memory_space=pl.ANY), mod_spec(0), mod_spec(1),
                  pl.BlockSpec((1, D_MODEL), lambda i: (0, 0))],
        out_specs=pl.BlockSpec(memory_space=pl.ANY),
        scratch_shapes=[pltpu.VMEM((RING_IN, tm, D_MODEL), F32), pltpu.VMEM((RING_OUT, tm, D_MODEL), BF16),
                        pltpu.SemaphoreType.DMA((RING_IN,)), pltpu.SemaphoreType.DMA((RING_OUT,))],
        compiler_params=_params("arbitrary"),
        name="norm",
    )(x, mod, mod, g_pre)


IN_TM = 1024
IN_TN = 2048
IN_DOT_COLS = 512


def _in_proj_kernel(h_ref, xs_ref, shift_s_ref, scale_s_ref, g_ref, wa_ref, wb_ref, w_in_hbm, w_mg_hbm,
                    z_ref, zs_ref, wa_bf_ref, wb_bf_ref, wbf0_ref, wbf1_ref, stage_ref, hs_ref, sem,
                    *, n_in_tiles, n_j, n_i):
    j, i = pl.program_id(0), pl.program_id(1)
    t = j * n_i + i
    total = n_j * n_i
    tn = wbf0_ref.shape[1]
    chunk = stage_ref.shape[1]

    def chunk_copy(w_hbm, col0, r, b):
        rows = pl.ds(pl.multiple_of(r * chunk, chunk), chunk)
        cols = pl.ds(col0 if isinstance(col0, int) else pl.multiple_of(col0, tn), tn)
        return pltpu.make_async_copy(w_hbm.at[rows, cols], stage_ref.at[b], sem.at[b])

    def start_chunk(g):
        g = lax.rem(jnp.asarray(g, jnp.int32), total)
        jt, r, b = lax.div(g, n_i), lax.rem(g, n_i), lax.rem(g, 2)

        @pl.when(jt < n_in_tiles)
        def _():
            chunk_copy(w_in_hbm, jt * tn, r, b).start()

        @pl.when(jt >= n_in_tiles)
        def _():
            chunk_copy(w_mg_hbm, (jt - n_in_tiles) * tn, r, b).start()

    def land_chunk(g, dst_ref):
        g = jnp.asarray(g, jnp.int32)
        r, b = lax.rem(g, n_i), lax.rem(g, 2)
        chunk_copy(w_in_hbm, 0, r, b).wait()
        dst_ref[pl.ds(pl.multiple_of(r * chunk, chunk), chunk), :] = stage_ref[b].astype(BF16)

    @pl.when(t == 0)
    def _():
        hs_ref[...] = _norm_mod(xs_ref[...], g_ref[...], scale_s_ref[...], shift_s_ref[...])
        start_chunk(0)

        def body(g, carry):
            start_chunk(g + 1)
            land_chunk(g, wbf0_ref)
            return carry
        lax.fori_loop(0, n_i, body, 0)

    @pl.when(t + 1 < total)
    def _():
        start_chunk(t + n_i + 1)

    def multiply(w_cur, w_nxt):
        land_chunk(t + n_i, w_nxt)
        wa_bf_ref[...] = wa_ref[...].astype(BF16)
        wb_bf_ref[...] = wb_ref[...].astype(BF16)
        for c0 in range(0, tn, IN_DOT_COLS):
            cols = slice(c0, c0 + IN_DOT_COLS)
            z_ref[:, cols] = jnp.dot(h_ref[...], w_cur[:, cols], preferred_element_type=F32)

        @pl.when(i == 0)
        def _():
            zs_ref[...] = jnp.dot(hs_ref[...], w_cur[...], preferred_element_type=F32)

    @pl.when(lax.rem(j, 2) == 0)
    def _():
        multiply(wbf0_ref, wbf1_ref)

    @pl.when(lax.rem(j, 2) == 1)
    def _():
        multiply(wbf1_ref, wbf0_ref)


def _in_proj(h, xs, mod, g_pre, w_in, w_mg, w_a, w_b):
    m, ms = h.shape[0], xs.shape[0]
    n_in, n_mg = w_in.shape[1] // IN_TN, w_mg.shape[1] // IN_TN
    n_i = m // IN_TM
    chunk = D_MODEL // n_i
    assert chunk * n_i == D_MODEL and chunk % 16 == 0 and (n_in + n_mg) * n_i >= CAST_STEPS
    const = lambda shape, col: pl.BlockSpec(shape, lambda j, i: (0, col), pipeline_mode=pl.Buffered(1))
    wa_in, wa_out, wa_shape = _cast_specs(w_a, lambda j, i: j * n_i + i)
    wb_in, wb_out, wb_shape = _cast_specs(w_b, lambda j, i: j * n_i + i)
    return pl.pallas_call(
        functools.partial(_in_proj_kernel, n_in_tiles=n_in, n_j=n_in + n_mg, n_i=n_i),
        out_shape=(jax.ShapeDtypeStruct((m, D_Z), F32), jax.ShapeDtypeStruct((ms, D_Z), F32), wa_shape, wb_shape),
        grid=(n_in + n_mg, n_i),
        in_specs=[pl.BlockSpec((IN_TM, D_MODEL), lambda j, i: (i, 0)),
                  const((ms, D_MODEL), 0), const((ms, D_MODEL), 0), const((ms, D_MODEL), 1),
                  const((1, D_MODEL), 0), wa_in, wb_in,
                  pl.BlockSpec(memory_space=pl.ANY), pl.BlockSpec(memory_space=pl.ANY)],
        out_specs=(pl.BlockSpec((IN_TM, IN_TN), lambda j, i: (i, j)),
                   pl.BlockSpec((ms, IN_TN), lambda j, i: (0, j)), wa_out, wb_out),
        scratch_shapes=[pltpu.VMEM((D_MODEL, IN_TN), BF16), pltpu.VMEM((D_MODEL, IN_TN), BF16),
                        pltpu.VMEM((2, chunk, IN_TN), F32),
                        pltpu.VMEM((ms, D_MODEL), BF16),
                        pltpu.SemaphoreType.DMA((2,))],
        compiler_params=_params("arbitrary", "arbitrary"),
        name="in_proj",
    )(h, xs, mod, mod, g_pre, w_a, w_b, w_in, w_mg)


def _rotate(x, cos, sin_signed):
    return x * cos + pltpu.roll(x, HEAD_DK // 2, 1) * sin_signed


def _group_norm_gate(o, gn, bg):
    mu = jnp.mean(o, axis=-1, keepdims=True)
    d = o - mu
    var = jnp.mean(d * d, axis=-1, keepdims=True)
    return d * lax.rsqrt(var + EPS) * gn * _silu(bg)


def _pool_project(pooled, pw, ps, ag):
    mixed = jnp.dot(pooled.astype(BF16), pw.astype(BF16), preferred_element_type=F32)
    return mixed * ps * _silu(ag)


PROJ_SPLIT = 8
PROJ_COLS = D_MODEL // PROJ_SPLIT


def _merge_chunk(y_a, y_b, j, gp_ref, bm_ref, wa_ref, wb_ref):
    lo, hi = j * PROJ_COLS, (j + 1) * PROJ_COLS
    ya = jnp.dot(y_a, wa_ref[:, lo:hi], preferred_element_type=F32)
    yb = jnp.dot(y_b, wb_ref[:, lo:hi], preferred_element_type=F32)
    g_a = jax.nn.sigmoid(gp_ref[:, lo:hi] + bm_ref[:, lo:hi])
    g_b = jax.nn.sigmoid(gp_ref[:, D_MODEL + lo:D_MODEL + hi] + bm_ref[:, D_MODEL + lo:D_MODEL + hi])
    return (g_a * ya + g_b * yb).astype(BF16)


def _out_chunk(merged, j, wo_ref):
    return jnp.dot(merged, wo_ref[:, j * PROJ_COLS:(j + 1) * PROJ_COLS], preferred_element_type=F32)


def _post_norm(o, gpost_ref):
    return o * lax.rsqrt(jnp.mean(o * o, axis=-1, keepdims=True) + EPS) * gpost_ref[...]


MIX_ROWS = 2 * CHUNK
MIX_HEADS = 4
Z_AX, Z_AG, Z_Q, Z_K, Z_V, Z_BG = 0, D_POOL, 2 * D_POOL, 2 * D_POOL + D_QK, 2 * D_POOL + 2 * D_QK, D_IN - D_V


def _prompt_kernel(cdec_ref, z_ref, rot_ref, dmask_ref, qdec_ref, kdec_ref, pw_ref, ps_ref, gn_ref,
                   gp_ref, bm_ref, wa_ref, wb_ref, wo_ref,
                   m_ref, npool_ref, nret_ref, wo_bf_ref, ya0_ref, yb0_ref, ya1_ref, yb1_ref, ext_ref, *, tiles_per_seq):
    s = pl.program_id(0)
    n_tiles = pl.num_programs(0) - 1
    live = s < n_tiles
    c = lax.rem(jnp.minimum(s, n_tiles - 1), tiles_per_seq)
    rows = z_ref.shape[0]

    @pl.when(s == 0)
    def _():
        for ref in (ya0_ref, yb0_ref, ya1_ref, yb1_ref):
            ref[...] = jnp.zeros(ref.shape, BF16)

    @pl.when(c == 0)
    def _():
        ext_ref[0:HALO, :] = jnp.zeros((HALO, D_POOL), F32)
        nret_ref[...] = jnp.zeros(nret_ref.shape, F32)

    def step(ya_rd, yb_rd, ya_wr, yb_wr):
        y_a, y_b = ya_rd[...], yb_rd[...]
        wo_bf_ref[...] = wo_ref[...].astype(BF16)
        xa = z_ref[:, Z_AX:Z_AX + D_POOL]
        ext_ref[HALO:HALO + rows, :] = xa
        pos = c * rows + lax.broadcasted_iota(jnp.int32, (rows, 1), 0)
        nt = (((1,), (1,)), ((), ()))
        tn = (((0,), (0,)), ((), ()))

        def merge_piece(j):
            m_ref[:, j * PROJ_COLS:(j + 1) * PROJ_COLS] = _merge_chunk(y_a, y_b, j, gp_ref, bm_ref, wa_ref, wb_ref)

        def pool_group(g):
            w = POOL_WINDOWS[g]
            lo, hi = g * POOL_GROUP_DIM, (g + 1) * POOL_GROUP_DIM
            acc = ext_ref[:, lo:hi]
            span = 1
            while span < w:
                acc = acc + pltpu.roll(acc, span, 0)
                span *= 2
            inv_cnt = 1.0 / jnp.minimum(pos + 1, w).astype(F32)
            pooled = acc[HALO:, :] * inv_cnt - xa[:, lo:hi]
            ya = _pool_project(pooled, pw_ref[g], ps_ref[:, lo:hi], z_ref[:, Z_AG + lo:Z_AG + hi])
            ya_wr[:, lo:hi] = ya.astype(BF16)

        per_pool = PROJ_SPLIT // len(POOL_WINDOWS)
        assert per_pool * len(POOL_WINDOWS) == PROJ_SPLIT
        fillers = [f for g in range(len(POOL_WINDOWS))
                   for f in [functools.partial(merge_piece, g * per_pool + j) for j in range(per_pool)]
                   + [functools.partial(pool_group, g)]]
        n_sub = rows // CHUNK
        n_slots = 2 * (N_HEADS // MIX_HEADS) * n_sub
        slot = [0]

        def fill():
            lo, hi = (slot[0] * len(fillers)) // n_slots, ((slot[0] + 1) * len(fillers)) // n_slots
            slot[0] += 1
            for f in fillers[lo:hi]:
                f()

        for h0 in range(0, N_HEADS, MIX_HEADS):
            heads = range(h0, h0 + MIX_HEADS)
            s_cur = {h: nret_ref[h] for h in heads}
            for ci in range(n_sub):
                rs = slice(ci * CHUNK, (ci + 1) * CHUNK)
                cos = rot_ref[rs, :HEAD_DK]
                sin = rot_ref[rs, HEAD_DK:]
                q = {h: _rotate(z_ref[rs, Z_Q + h * HEAD_DK:Z_Q + (h + 1) * HEAD_DK], cos, sin) for h in heads}
                k = {h: _rotate(z_ref[rs, Z_K + h * HEAD_DK:Z_K + (h + 1) * HEAD_DK], cos, sin) * K_SCALE
                     for h in heads}
                v = {h: z_ref[rs, Z_V + h * HEAD_DV:Z_V + (h + 1) * HEAD_DV].astype(BF16) for h in heads}
                fill()
                scores = {h: lax.dot_general(q[h].astype(BF16), k[h].astype(BF16), nt, preferred_element_type=F32)
                          for h in heads}
                kv = {h: lax.dot_general((k[h] * kdec_ref[h]).astype(BF16), v[h], tn, preferred_element_type=F32)
                      for h in heads}
                lhs = {h: jnp.concatenate([(scores[h] * dmask_ref[h]).astype(BF16),
                                           (q[h] * qdec_ref[h]).astype(BF16)], axis=1) for h in heads}
                fill()
                for h in heads:
                    rhs = jnp.concatenate([v[h], s_cur[h].astype(BF16)], axis=0)
                    o = jnp.dot(lhs[h], rhs, preferred_element_type=F32)
                    s_cur[h] = cdec_ref[h] * s_cur[h] + kv[h]
                    vs = slice(h * HEAD_DV, (h + 1) * HEAD_DV)
                    bg = z_ref[rs, Z_BG + h * HEAD_DV:Z_BG + (h + 1) * HEAD_DV]
                    yb_wr[rs, vs] = _group_norm_gate(o, gn_ref[:, vs], bg).astype(BF16)
            for h in heads:
                nret_ref[h] = jnp.where(live, s_cur[h], nret_ref[h])
        assert slot[0] == n_slots

    @pl.when(lax.rem(s, 2) == 0)
    def _():
        step(ya1_ref, yb1_ref, ya0_ref, yb0_ref)

    @pl.when(lax.rem(s, 2) == 1)
    def _():
        step(ya0_ref, yb0_ref, ya1_ref, yb1_ref)

    @pl.when(c == tiles_per_seq - 1)
    def _():
        npool_ref[...] = ext_ref[HALO + rows - POOL_BUF:HALO + rows, :]

    ext_ref[0:HALO, :] = ext_ref[rows:rows + HALO, :]


def _prompt_mix_merge(z, batch, seq, tables, pool_w, pool_scale, gn_g, b_merge, w_a, w_b, w_o):
    rot, dmask, qdec, kdec, cdec = tables
    tps = seq // MIX_ROWS
    n_tiles = batch * tps
    assert n_tiles >= CAST_STEPS
    cur = lambda s: jnp.minimum(s, n_tiles - 1)
    prev = lambda s: jnp.maximum(s - 1, 0)
    m = batch * seq
    wo_in, wo_out, wo_shape = _cast_specs(w_o, lambda s: s)
    return pl.pallas_call(
        functools.partial(_prompt_kernel, tiles_per_seq=tps),
        out_shape=(jax.ShapeDtypeStruct((m, D_MODEL), BF16),
                   jax.ShapeDtypeStruct((batch, POOL_BUF, D_POOL), F32),
                   jax.ShapeDtypeStruct((batch, N_HEADS, HEAD_DK, HEAD_DV), F32), wo_shape),
        grid=(n_tiles + 1,),
        in_specs=[pl.BlockSpec(memory_space=pltpu.SMEM),
                  pl.BlockSpec((MIX_ROWS, D_IN), lambda s: (cur(s), 0)),
                  pl.BlockSpec((MIX_ROWS, 2 * HEAD_DK), lambda s: (cur(s) % tps, 0)),
                  _resident(dmask.shape), _resident(qdec.shape), _resident(kdec.shape),
                  _resident(pool_w.shape), _resident(pool_scale.shape), _resident(gn_g.shape),
                  pl.BlockSpec((MIX_ROWS, 2 * D_MODEL), lambda s: (prev(s), D_IN // (2 * D_MODEL))),
                  _resident(b_merge.shape), _resident(w_a.shape), _resident(w_b.shape), wo_in],
        out_specs=(pl.BlockSpec((MIX_ROWS, D_MODEL), lambda s: (prev(s), 0)),
                   pl.BlockSpec((None, POOL_BUF, D_POOL), lambda s: (cur(s) // tps, 0, 0)),
                   pl.BlockSpec((None, N_HEADS, HEAD_DK, HEAD_DV), lambda s: (cur(s) // tps, 0, 0, 0)), wo_out),
        scratch_shapes=[pltpu.VMEM((MIX_ROWS, D_POOL), BF16), pltpu.VMEM((MIX_ROWS, D_V), BF16),
                        pltpu.VMEM((MIX_ROWS, D_POOL), BF16), pltpu.VMEM((MIX_ROWS, D_V), BF16),
                        pltpu.VMEM((HALO + MIX_ROWS, D_POOL), F32)],
        compiler_params=_params("arbitrary"),
        name="prompt_mix_merge",
    )(cdec, z, rot, dmask, qdec, kdec, pool_w, pool_scale, gn_g, z, b_merge, w_a, w_b, w_o)


SAMPLE_TILE = 8
SAMPLE_HEADS = 4
TAIL_ROWS = 256


def _tail_kernel(dec_ref, m_ref, x_ref, gate_ref, wo_ref, gpost_ref,
                 zs_ref, rot_ref, pw_ref, ps_ref, gn_ref, spool_ref, sret_hbm,
                 y_ref, ya_ref, yb_ref, npool_ref, nret_hbm, o_ref, in_ring, out_ring, in_sem, out_sem,
                 *, inv_cnt, rows_per_mod, n_steps):
    bt = zs_ref.shape[0]
    n_hg = N_HEADS // SAMPLE_HEADS
    assert n_hg % RING_OUT == 0
    t = pl.program_id(0) * n_hg + pl.program_id(1)

    def state_block(hbm, step):
        seq0 = pl.multiple_of(lax.div(step, n_hg) * bt, bt)
        head0 = pl.multiple_of(lax.rem(step, n_hg) * SAMPLE_HEADS, SAMPLE_HEADS)
        return hbm.at[pl.ds(seq0, bt), pl.ds(head0, SAMPLE_HEADS)]

    def in_copy(step):
        return _ring_copy(state_block(sret_hbm, step), in_ring, in_sem, step)

    def out_copy(step):
        return _ring_copy(state_block(nret_hbm, step), out_ring, out_sem, step, to_hbm=True)

    _ring_begin(t, n_steps, in_copy, out_copy)
    s_in = in_ring.at[lax.rem(t, RING_IN)]

    def step(hg):
        s_out = out_ring.at[hg % RING_OUT]
        o = _post_norm(jnp.dot(m_ref[...], wo_ref[...], preferred_element_type=F32), gpost_ref)
        tile = pl.program_id(0) * n_hg + hg
        gate = gate_ref[pl.ds((tile * x_ref.shape[0]) // rows_per_mod, 1), :]
        y_ref[...] = x_ref[...] + gate * o

        if hg == 0:
            xa = zs_ref[:, Z_AX:Z_AX + D_POOL]
            run = xa
            wins = {1: xa}
            for j in range(1, POOL_BUF + 1):
                run = run + spool_ref[POOL_BUF - j]
                wins[j + 1] = run
            for g, w in enumerate(POOL_WINDOWS):
                lo, hi = g * POOL_GROUP_DIM, (g + 1) * POOL_GROUP_DIM
                pooled = wins[w][:, lo:hi] * inv_cnt[g] - xa[:, lo:hi]
                ya = _pool_project(pooled, pw_ref[g], ps_ref[:, lo:hi], zs_ref[:, Z_AG + lo:Z_AG + hi])
                ya_ref[:, lo:hi] = ya.astype(BF16)
            for j in range(POOL_BUF - 1):
                npool_ref[j] = spool_ref[j + 1]
            npool_ref[POOL_BUF - 1] = xa

        cos = rot_ref[:, :HEAD_DK]
        sin = rot_ref[:, HEAD_DK:]
        for hl in range(SAMPLE_HEADS):
            h = hg * SAMPLE_HEADS + hl
            vs = slice(h * HEAD_DV, (h + 1) * HEAD_DV)
            q = _rotate(zs_ref[:, Z_Q + h * HEAD_DK:Z_Q + (h + 1) * HEAD_DK], cos, sin)
            k = _rotate(zs_ref[:, Z_K + h * HEAD_DK:Z_K + (h + 1) * HEAD_DK], cos, sin) * K_SCALE
            v = zs_ref[:, Z_V + h * HEAD_DV:Z_V + (h + 1) * HEAD_DV]
            score = jnp.sum(q * k, axis=1, keepdims=True) * dec_ref[0, h]
            q_cols = jnp.transpose(q * dec_ref[1, h])
            k_cols = jnp.transpose(k * dec_ref[2, h])
            for r in range(bt):
                s_old = s_in[r, hl]
                v_row = v[r:r + 1, :]
                o_row = score[r:r + 1, :] * v_row + jnp.sum(q_cols[:, r:r + 1] * s_old, axis=0, keepdims=True)
                s_out[r, hl] = dec_ref[3, h] * s_old + k_cols[:, r:r + 1] * v_row
                o_ref[r:r + 1, vs] = o_row
        for hl in range(SAMPLE_HEADS):
            h = hg * SAMPLE_HEADS + hl
            vs = slice(h * HEAD_DV, (h + 1) * HEAD_DV)
            bg = zs_ref[:, Z_BG + h * HEAD_DV:Z_BG + (h + 1) * HEAD_DV]
            yb_ref[:, vs] = _group_norm_gate(o_ref[:, vs], gn_ref[:, vs], bg).astype(BF16)

    for hg in range(n_hg):
        pl.when(pl.program_id(1) == hg)(functools.partial(step, hg))

    _ring_end(t, n_steps, out_copy)


def _tail(merged, x, mod, w_o, g_post, zs, state_pool, state_ret, tables, inv_cnt, pool_w, pool_scale, gn_g,
          *, rows_per_mod, mod_row0):
    rot, dec = tables
    m, batch = x.shape[0], zs.shape[0]
    bt, hs = SAMPLE_TILE, SAMPLE_HEADS
    n_hg = N_HEADS // hs
    assert m == (batch // bt) * n_hg * TAIL_ROWS and mod_row0 % SUBLANES == 0 and m // rows_per_mod <= SUBLANES
    rows = lambda i, g: (i * n_hg + g, 0)
    per_tile = lambda width: pl.BlockSpec((bt, width), lambda i, g: (i, 0))
    return pl.pallas_call(
        functools.partial(_tail_kernel, inv_cnt=inv_cnt, rows_per_mod=rows_per_mod, n_steps=(batch // bt) * n_hg),
        out_shape=(jax.ShapeDtypeStruct((m, D_MODEL), F32),
                   jax.ShapeDtypeStruct((batch, D_POOL), BF16),
                   jax.ShapeDtypeStruct((batch, D_V), BF16),
                   jax.ShapeDtypeStruct(state_pool.shape, state_pool.dtype),
                   jax.ShapeDtypeStruct(state_ret.shape, state_ret.dtype)),
        grid=(batch // bt, n_hg),
        in_specs=[pl.BlockSpec(memory_space=pltpu.SMEM),
                  pl.BlockSpec((TAIL_ROWS, D_MODEL), rows), pl.BlockSpec((TAIL_ROWS, D_MODEL), rows),
                  pl.BlockSpec((SUBLANES, D_MODEL), lambda i, g: (mod_row0 // SUBLANES, 2)),
                  _resident(w_o.shape), _resident(g_post.shape),
                  per_tile(D_IN), _resident(rot.shape),
                  _resident(pool_w.shape), _resident(pool_scale.shape), _resident(gn_g.shape),
                  pl.BlockSpec((POOL_BUF, bt, D_POOL), lambda i, g: (0, i, 0)),
                  pl.BlockSpec(memory_space=pl.ANY)],
        out_specs=(pl.BlockSpec((TAIL_ROWS, D_MODEL), rows),
                   per_tile(D_POOL), per_tile(D_V),
                   pl.BlockSpec((POOL_BUF, bt, D_POOL), lambda i, g: (0, i, 0)),
                   pl.BlockSpec(memory_space=pl.ANY)),
        scratch_shapes=[pltpu.VMEM((bt, D_V), F32),
                        pltpu.VMEM((RING_IN, bt, hs, HEAD_DK, HEAD_DV), F32),
                        pltpu.VMEM((RING_OUT, bt, hs, HEAD_DK, HEAD_DV), F32),
                        pltpu.SemaphoreType.DMA((RING_IN,)), pltpu.SemaphoreType.DMA((RING_OUT,))],
        compiler_params=_params("arbitrary", "arbitrary"),
        name="tail",
    )(dec, merged, x, mod, w_o, g_post, zs, rot, pool_w, pool_scale, gn_g, state_pool, state_ret)


def _out_proj_kernel(ya_ref, yb_ref, ga_ref, gb_ref, x_ref, gate_ref, bma_ref, bmb_ref, wa_ref, wb_ref, wo_ref,
                     gpost_ref, y_ref, acc_ref, *, rows_per_mod):
    j = pl.program_id(1)
    ya = jnp.dot(ya_ref[...], wa_ref[...], preferred_element_type=F32)
    yb = jnp.dot(yb_ref[...], wb_ref[...], preferred_element_type=F32)
    g_a = jax.nn.sigmoid(ga_ref[...] + bma_ref[...])
    g_b = jax.nn.sigmoid(gb_ref[...] + bmb_ref[...])
    part = jnp.dot((g_a * ya + g_b * yb).astype(BF16), wo_ref[...], preferred_element_type=F32)

    @pl.when(j == 0)
    def _():
        acc_ref[...] = part

    @pl.when(j > 0)
    def _():
        acc_ref[...] += part

    @pl.when(j == PROJ_SPLIT - 1)
    def _():
        gate = _mod_rows(gate_ref, slice(None), x_ref.shape[0], rows_per_mod)
        y_ref[...] = x_ref[...] + gate * _post_norm(acc_ref[...], gpost_ref)


def _out_proj(ya, yb, z, x, mod, b_merge, w_a, w_b, w_o, g_post, *, tm, rows_per_mod, mod_row0):
    m = x.shape[0]
    gate_cols = lambda first: pl.BlockSpec((tm, PROJ_COLS), lambda i, j: (i, first // PROJ_COLS + j))
    bias_cols = lambda first: pl.BlockSpec((1, PROJ_COLS), lambda i, j: (0, first // PROJ_COLS + j))
    return pl.pallas_call(
        functools.partial(_out_proj_kernel, rows_per_mod=rows_per_mod),
        out_shape=jax.ShapeDtypeStruct((m, D_MODEL), F32),
        grid=(m // tm, PROJ_SPLIT),
        in_specs=[pl.BlockSpec((tm, D_POOL), lambda i, j: (i, 0)),
                  pl.BlockSpec((tm, D_V), lambda i, j: (i, 0)),
                  gate_cols(D_IN), gate_cols(D_IN + D_MODEL),
                  pl.BlockSpec((tm, D_MODEL), lambda i, j: (i, 0)),
                  _mod_spec(tm, rows_per_mod, mod_row0, 2),
                  bias_cols(0), bias_cols(D_MODEL),
                  pl.BlockSpec((D_POOL, PROJ_COLS), lambda i, j: (0, j)),
                  pl.BlockSpec((D_V, PROJ_COLS), lambda i, j: (0, j)),
                  pl.BlockSpec((PROJ_COLS, D_MODEL), lambda i, j: (j, 0)),
                  _resident(g_post.shape)],
        out_specs=pl.BlockSpec((tm, D_MODEL), lambda i, j: (i, 0)),
        scratch_shapes=[pltpu.VMEM((tm, D_MODEL), F32)],
        compiler_params=_params("arbitrary", "arbitrary"),
        name="out_proj",
    )(ya, yb, z, z, x, mod, b_merge, b_merge, w_a, w_b, w_o, g_post)


def _rotary_tables(start, length):
    half = HEAD_DK // 2
    inv = ROPE_BASE ** (-np.arange(half, dtype=np.float64) / half)
    ang = (start + np.arange(length, dtype=np.float64))[:, None] * inv[None, :]
    cos, sin = np.cos(ang), np.sin(ang)
    return (np.concatenate([cos, cos], axis=-1).astype(np.float32),
            np.concatenate([-sin, sin], axis=-1).astype(np.float32))


def _decay_tables(c):
    lg = np.log1p(-np.power(2.0, -5.0 - np.arange(N_HEADS, dtype=np.float64)))
    idx = np.arange(c, dtype=np.float64)
    diff = idx[:, None] - idx[None, :]
    dmask = np.where(diff[None] >= 0, np.exp(np.maximum(diff, 0.0)[None] * lg[:, None, None]), 0.0)
    q_dec = np.exp((idx + 1.0)[None, :] * lg[:, None])
    k_dec = np.exp((c - 1.0 - idx)[None, :] * lg[:, None])
    chunk_dec = np.exp(c * lg)
    return tuple(a.astype(np.float32) for a in (dmask, q_dec, k_dec, chunk_dec))


def _layer(xp, xs, c_prompt, c_sample, state_pool, state_ret, ada_w, ada_b, g_pre, g_post, w_in, pool_w, pool_scale, gn_g,
           w_a_proj, w_b_proj, w_merge, b_merge, w_out):
    batch, seq, _ = xp.shape
    dec_batch, dec_seq, _ = xs.shape
    assert dec_seq == 1 and seq % CHUNK == 0

    row = lambda v: v.reshape(1, -1)
    g_pre, g_post, pool_scale, gn_g, b_merge = map(row, (g_pre, g_post, pool_scale, gn_g, b_merge))

    mod = _modulation(c_sample, c_prompt, ada_w, row(ada_b))

    x2 = xp.reshape(batch * seq, D_MODEL)
    xs2 = xs.reshape(dec_batch, D_MODEL)
    h = _norm(x2, mod, g_pre, tm=NORM_TM, rows_per_mod=seq, mod_row0=dec_batch)
    z, zs, w_a, w_b = _in_proj(h, xs2, mod, g_pre, w_in, w_merge, w_a_proj, w_b_proj)

    cos, sin = _rotary_tables(0, seq)
    dmask, q_dec, k_dec, chunk_dec = _decay_tables(CHUNK)
    wide = lambda d: np.ascontiguousarray(np.broadcast_to(d[:, :, None], (N_HEADS, CHUNK, HEAD_DK)))
    rot = np.concatenate([cos, sin], axis=1)
    merged, pool_p, ret_p, w_o = _prompt_mix_merge(z, batch, seq, (rot, dmask, wide(q_dec), wide(k_dec), chunk_dec),
                                                   pool_w, pool_scale, gn_g, b_merge, w_a, w_b, w_out)

    cos_s, sin_s = _rotary_tables(PAST_LEN, 1)
    dmask_s, q_dec_s, k_dec_s, chunk_dec_s = _decay_tables(1)
    dec_s = np.stack([dmask_s[:, 0, 0], q_dec_s[:, 0], k_dec_s[:, 0], chunk_dec_s])
    inv_cnt = tuple(1.0 / min(PAST_LEN + 1, w) for w in POOL_WINDOWS)
    yp, ya_s, yb_s, pool_s, ret_s = _tail(merged, x2, mod, w_o, g_post, zs, jnp.transpose(state_pool, (1, 0, 2)),
                                          state_ret, (np.concatenate([cos_s, sin_s], axis=1), dec_s), inv_cnt,
                                          pool_w, pool_scale, gn_g,
                                          rows_per_mod=seq, mod_row0=dec_batch)
    pool_s = jnp.transpose(pool_s, (1, 0, 2))
    ys = _out_proj(ya_s, yb_s, zs, xs2, mod, b_merge, w_a, w_b, w_o, g_post, tm=dec_batch, rows_per_mod=1, mod_row0=0)

    return yp.reshape(xp.shape), ys.reshape(xs.shape), pool_p, ret_p, pool_s, ret_s


def kernel(x_prompt, x_sample, state_pool, state_ret, c_prompt, c_sample, ada_w, ada_b, g_pre, g_post,
           w_in, pool_w, pool_scale, gn_g, w_a_proj, w_b_proj, w_merge, b_merge, w_out):
    depth = ada_w.shape[0]
    xp, xs = x_prompt, x_sample
    pool_p, ret_p, pool_s, ret_s = [], [], [], []
    for l in range(depth):
        xp, xs, bp, sp, bs, ss = _layer(
            xp, xs, c_prompt, c_sample, state_pool[l], state_ret[l], ada_w[l], ada_b[l], g_pre[l], g_post[l], w_in[l],
            pool_w[l], pool_scale[l], gn_g[l].reshape(-1), w_a_proj[l], w_b_proj[l], w_merge[l], b_merge[l],
            w_out[l])
        pool_p.append(bp)
        ret_p.append(sp)
        pool_s.append(bs)
        ret_s.append(ss)
    return (xp, xs, jnp.stack(pool_p), jnp.stack(ret_p), jnp.stack(pool_s), jnp.stack(ret_s))
```

```python
import functools

import jax
import jax.numpy as jnp
import numpy as np
from jax import lax
from jax.experimental import pallas as pl
from jax.experimental.pallas import tpu as pltpu

F32 = jnp.float32
BF16 = jnp.bfloat16

D_MODEL = 2048
PAST_LEN = 16384
D_POOL = D_MODEL // 2
POOL_WINDOWS = (2, 4, 8, 16)
POOL_GROUP_DIM = D_POOL // len(POOL_WINDOWS)
POOL_BUF = max(POOL_WINDOWS) - 1
N_HEADS = 8
HEAD_DK = D_MODEL // 16
HEAD_DV = D_MODEL // 8
D_QK = N_HEADS * HEAD_DK
D_V = N_HEADS * HEAD_DV
CHUNK = 128
ROPE_BASE = 10000.0
EPS = 1e-6
D_IN = 2 * D_POOL + 2 * D_QK + 2 * D_V
D_Z = D_IN + 2 * D_MODEL
K_SCALE = HEAD_DK ** -0.5

VMEM_LIMIT_BYTES = 56 * 1024 * 1024
SUBLANES = 8
HALO = 16
CAST_STEPS = 32


def _params(*semantics):
    return pltpu.CompilerParams(dimension_semantics=semantics, vmem_limit_bytes=VMEM_LIMIT_BYTES)


def _cast_specs(w, step_of):
    rows = w.shape[0] // CAST_STEPS
    assert rows * CAST_STEPS == w.shape[0] and rows % (2 * SUBLANES) == 0
    spec = pl.BlockSpec((rows, w.shape[1]), lambda *idx: (jnp.minimum(step_of(*idx), CAST_STEPS - 1), 0))
    return spec, spec, jax.ShapeDtypeStruct(w.shape, BF16)


def _resident(shape):
    return pl.BlockSpec(shape, lambda *_: (0,) * len(shape), pipeline_mode=pl.Buffered(1))


def _silu(x):
    return x * jax.nn.sigmoid(x)


RING_IN, RING_OUT = 3, 2


def _ring_copy(hbm_block, ring, sem, step, *, to_hbm=False):
    slot = lax.rem(step, ring.shape[0])
    src, dst = (ring.at[slot], hbm_block) if to_hbm else (hbm_block, ring.at[slot])
    return pltpu.make_async_copy(src, dst, sem.at[slot])


def _ring_begin(t, n_steps, in_copy, out_copy=None):
    @pl.when(t == 0)
    def _():
        for k in range(RING_IN - 1):
            in_copy(jnp.int32(k)).start()

    in_copy(t).wait()

    @pl.when(t + RING_IN - 1 < n_steps)
    def _():
        in_copy(t + RING_IN - 1).start()

    if out_copy is not None:
        @pl.when(t >= RING_OUT)
        def _():
            out_copy(t - RING_OUT).wait()


def _ring_end(t, n_steps, out_copy):
    out_copy(t).start()

    @pl.when(t == n_steps - 1)
    def _():
        for k in reversed(range(RING_OUT)):
            out_copy(t - k).wait()


MOD_TN = 512


def _mod_kernel(cs_ref, cp_ref, w_hbm, b_ref, o_ref, w_ring, sem, *, n_steps):
    t = pl.program_id(0)
    tn = w_ring.shape[2]

    def in_copy(step):
        return _ring_copy(w_hbm.at[:, pl.ds(pl.multiple_of(step * tn, tn), tn)], w_ring, sem, step)

    _ring_begin(t, n_steps, in_copy)
    n_pad = o_ref.shape[0] - cs_ref.shape[0] - cp_ref.shape[0]
    c = jnp.concatenate([cs_ref[...], cp_ref[...], jnp.zeros((n_pad, D_MODEL), F32)], axis=0)
    w = w_ring[lax.rem(t, RING_IN)].astype(BF16)
    o_ref[...] = jnp.dot(_silu(c).astype(BF16), w, preferred_element_type=F32) + b_ref[...]


def _modulation(c_sample, c_prompt, ada_w, ada_b):
    ns, n_p = c_sample.shape[0], c_prompt.shape[0]
    assert ns % SUBLANES == 0
    rows = ns + n_p + (-n_p) % SUBLANES
    n_steps = 3 * D_MODEL // MOD_TN
    return pl.pallas_call(
        functools.partial(_mod_kernel, n_steps=n_steps),
        out_shape=jax.ShapeDtypeStruct((rows, 3 * D_MODEL), F32),
        grid=(n_steps,),
        in_specs=[pl.BlockSpec((ns, D_MODEL), lambda j: (0, 0)),
                  pl.BlockSpec((n_p, D_MODEL), lambda j: (0, 0)),
                  pl.BlockSpec(memory_space=pl.ANY),
                  pl.BlockSpec((1, MOD_TN), lambda j: (0, j))],
        out_specs=pl.BlockSpec((rows, MOD_TN), lambda j: (0, j)),
        scratch_shapes=[pltpu.VMEM((RING_IN, D_MODEL, MOD_TN), F32), pltpu.SemaphoreType.DMA((RING_IN,))],
        compiler_params=_params("arbitrary"),
        name="modulation",
    )(c_sample, c_prompt, ada_w, ada_b)


NORM_ROWS = 128
NORM_TM = 512


def _mod_rows(ref, rows, tm, rows_per_mod):
    if rows_per_mod == 1:
        return ref[rows, :]
    return ref[pl.ds((pl.program_id(0) * tm) // rows_per_mod, 1), :]


def _mod_spec(tm, rows_per_mod, mod_row0, col):
    if rows_per_mod == 1:
        assert mod_row0 % tm == 0
        return pl.BlockSpec((tm, D_MODEL), lambda i, *_: (mod_row0 // tm + i, col))
    assert mod_row0 % SUBLANES == 0
    return pl.BlockSpec((SUBLANES, D_MODEL), lambda i, *_: (mod_row0 // SUBLANES, col))


def _norm_mod(x, g, scale, shift):
    xn = x * lax.rsqrt(jnp.mean(x * x, axis=-1, keepdims=True) + EPS) * g
    return (xn * (1.0 + scale) + shift).astype(BF16)


def _norm_kernel(x_hbm, shift_ref, scale_ref, g_ref, h_hbm, x_ring, h_ring, in_sem, out_sem,
                 *, rows_per_mod, n_steps):
    tm = x_ring.shape[1]
    t = pl.program_id(0)

    def rows_of(step):
        return pl.ds(pl.multiple_of(step * tm, tm), tm)

    def in_copy(step):
        return _ring_copy(x_hbm.at[rows_of(step)], x_ring, in_sem, step)

    def out_copy(step):
        return _ring_copy(h_hbm.at[rows_of(step)], h_ring, out_sem, step, to_hbm=True)

    _ring_begin(t, n_steps, in_copy, out_copy)
    x_ref, h_ref = x_ring.at[lax.rem(t, RING_IN)], h_ring.at[lax.rem(t, RING_OUT)]

    def body(r, carry):
        rows = pl.ds(pl.multiple_of(r * NORM_ROWS, NORM_ROWS), NORM_ROWS)
        h_ref[rows, :] = _norm_mod(x_ref[rows, :], g_ref[...], _mod_rows(scale_ref, rows, tm, rows_per_mod),
                                   _mod_rows(shift_ref, rows, tm, rows_per_mod))
        return carry
    lax.fori_loop(0, tm // NORM_ROWS, body, 0)
    _ring_end(t, n_steps, out_copy)


def _norm(x, mod, g_pre, *, tm, rows_per_mod, mod_row0):
    m = x.shape[0]
    n_steps = m // tm
    assert (rows_per_mod == 1 or m // rows_per_mod <= SUBLANES) and n_steps >= 2
    mod_spec = lambda col: _mod_spec(tm, rows_per_mod, mod_row0, col)
    return pl.pallas_call(
        functools.partial(_norm_kernel, rows_per_mod=rows_per_mod, n_steps=n_steps),
        out_shape=jax.ShapeDtypeStruct((m, D_MODEL), BF16),
        grid=(n_steps,),
        in_specs=[pl.BlockSpec(memory_space=pl.ANY), mod_spec(0), mod_spec(1),
                  pl.BlockSpec((1, D_MODEL), lambda i: (0, 0))],
        out_specs=pl.BlockSpec(memory_space=pl.ANY),
        scratch_shapes=[pltpu.VMEM((RING_IN, tm, D_MODEL), F32), pltpu.VMEM((RING_OUT, tm, D_MODEL), BF16),
                        pltpu.SemaphoreType.DMA((RING_IN,)), pltpu.SemaphoreType.DMA((RING_OUT,))],
        compiler_params=_params("arbitrary"),
        name="norm",
    )(x, mod, mod, g_pre)


IN_TM = 1024
IN_TN = 2048
IN_DOT_COLS = 512


def _in_proj_kernel(h_ref, xs_ref, shift_s_ref, scale_s_ref, g_ref, wa_ref, wb_ref, w_in_hbm, w_mg_hbm,
                    z_ref, zs_ref, wa_bf_ref, wb_bf_ref, wbf0_ref, wbf1_ref, stage_ref, hs_ref, sem,
                    *, n_in_tiles, n_j, n_i):
    j, i = pl.program_id(0), pl.program_id(1)
    t = j * n_i + i
    total = n_j * n_i
    tn = wbf0_ref.shape[1]
    chunk = stage_ref.shape[1]

    def chunk_copy(w_hbm, col0, r, b):
        rows = pl.ds(pl.multiple_of(r * chunk, chunk), chunk)
        cols = pl.ds(col0 if isinstance(col0, int) else pl.multiple_of(col0, tn), tn)
        return pltpu.make_async_copy(w_hbm.at[rows, cols], stage_ref.at[b], sem.at[b])

    def start_chunk(g):
        g = lax.rem(jnp.asarray(g, jnp.int32), total)
        jt, r, b = lax.div(g, n_i), lax.rem(g, n_i), lax.rem(g, 2)

        @pl.when(jt < n_in_tiles)
        def _():
            chunk_copy(w_in_hbm, jt * tn, r, b).start()

        @pl.when(jt >= n_in_tiles)
        def _():
            chunk_copy(w_mg_hbm, (jt - n_in_tiles) * tn, r, b).start()

    def land_chunk(g, dst_ref):
        g = jnp.asarray(g, jnp.int32)
        r, b = lax.rem(g, n_i), lax.rem(g, 2)
        chunk_copy(w_in_hbm, 0, r, b).wait()
        dst_ref[pl.ds(pl.multiple_of(r * chunk, chunk), chunk), :] = stage_ref[b].astype(BF16)

    @pl.when(t == 0)
    def _():
        hs_ref[...] = _norm_mod(xs_ref[...], g_ref[...], scale_s_ref[...], shift_s_ref[...])
        start_chunk(0)

        def body(g, carry):
            start_chunk(g + 1)
            land_chunk(g, wbf0_ref)
            return carry
        lax.fori_loop(0, n_i, body, 0)

    @pl.when(t + 1 < total)
    def _():
        start_chunk(t + n_i + 1)

    def multiply(w_cur, w_nxt):
        land_chunk(t + n_i, w_nxt)
        wa_bf_ref[...] = wa_ref[...].astype(BF16)
        wb_bf_ref[...] = wb_ref[...].astype(BF16)
        for c0 in range(0, tn, IN_DOT_COLS):
            cols = slice(c0, c0 + IN_DOT_COLS)
            z_ref[:, cols] = jnp.dot(h_ref[...], w_cur[:, cols], preferred_element_type=F32)

        @pl.when(i == 0)
        def _():
            zs_ref[...] = jnp.dot(hs_ref[...], w_cur[...], preferred_element_type=F32)

    @pl.when(lax.rem(j, 2) == 0)
    def _():
        multiply(wbf0_ref, wbf1_ref)

    @pl.when(lax.rem(j, 2) == 1)
    def _():
        multiply(wbf1_ref, wbf0_ref)


def _in_proj(h, xs, mod, g_pre, w_in, w_mg, w_a, w_b):
    m, ms = h.shape[0], xs.shape[0]
    n_in, n_mg = w_in.shape[1] // IN_TN, w_mg.shape[1] // IN_TN
    n_i = m // IN_TM
    chunk = D_MODEL // n_i
    assert chunk * n_i == D_MODEL and chunk % 16 == 0 and (n_in + n_mg) * n_i >= CAST_STEPS
    const = lambda shape, col: pl.BlockSpec(shape, lambda j, i: (0, col), pipeline_mode=pl.Buffered(1))
    wa_in, wa_out, wa_shape = _cast_specs(w_a, lambda j, i: j * n_i + i)
    wb_in, wb_out, wb_shape = _cast_specs(w_b, lambda j, i: j * n_i + i)
    return pl.pallas_call(
        functools.partial(_in_proj_kernel, n_in_tiles=n_in, n_j=n_in + n_mg, n_i=n_i),
        out_shape=(jax.ShapeDtypeStruct((m, D_Z), F32), jax.ShapeDtypeStruct((ms, D_Z), F32), wa_shape, wb_shape),
        grid=(n_in + n_mg, n_i),
        in_specs=[pl.BlockSpec((IN_TM, D_MODEL), lambda j, i: (i, 0)),
                  const((ms, D_MODEL), 0), const((ms, D_MODEL), 0), const((ms, D_MODEL), 1),
                  const((1, D_MODEL), 0), wa_in, wb_in,
                  pl.BlockSpec(memory_space=pl.ANY), pl.BlockSpec(memory_space=pl.ANY)],
        out_specs=(pl.BlockSpec((IN_TM, IN_TN), lambda j, i: (i, j)),
                   pl.BlockSpec((ms, IN_TN), lambda j, i: (0, j)), wa_out, wb_out),
        scratch_shapes=[pltpu.VMEM((D_MODEL, IN_TN), BF16), pltpu.VMEM((D_MODEL, IN_TN), BF16),
                        pltpu.VMEM((2, chunk, IN_TN), F32),
                        pltpu.VMEM((ms, D_MODEL), BF16),
                        pltpu.SemaphoreType.DMA((2,))],
        compiler_params=_params("arbitrary", "arbitrary"),
        name="in_proj",
    )(h, xs, mod, mod, g_pre, w_a, w_b, w_in, w_mg)


def _rotate(x, cos, sin_signed):
    return x * cos + pltpu.roll(x, HEAD_DK // 2, 1) * sin_signed


def _group_norm_gate(o, gn, bg):
    mu = jnp.mean(o, axis=-1, keepdims=True)
    d = o - mu
    var = jnp.mean(d * d, axis=-1, keepdims=True)
    return d * lax.rsqrt(var + EPS) * gn * _silu(bg)


def _pool_project(pooled, pw, ps, ag):
    mixed = jnp.dot(pooled.astype(BF16), pw.astype(BF16), preferred_element_type=F32)
    return mixed * ps * _silu(ag)


PROJ_SPLIT = 8
PROJ_COLS = D_MODEL // PROJ_SPLIT


def _merge_chunk(y_a, y_b, j, gp_ref, bm_ref, wa_ref, wb_ref):
    lo, hi = j * PROJ_COLS, (j + 1) * PROJ_COLS
    ya = jnp.dot(y_a, wa_ref[:, lo:hi], preferred_element_type=F32)
    yb = jnp.dot(y_b, wb_ref[:, lo:hi], preferred_element_type=F32)
    g_a = jax.nn.sigmoid(gp_ref[:, lo:hi] + bm_ref[:, lo:hi])
    g_b = jax.nn.sigmoid(gp_ref[:, D_MODEL + lo:D_MODEL + hi] + bm_ref[:, D_MODEL + lo:D_MODEL + hi])
    return (g_a * ya + g_b * yb).astype(BF16)


def _out_chunk(merged, j, wo_ref):
    return jnp.dot(merged, wo_ref[:, j * PROJ_COLS:(j + 1) * PROJ_COLS], preferred_element_type=F32)


def _post_norm(o, gpost_ref):
    return o * lax.rsqrt(jnp.mean(o * o, axis=-1, keepdims=True) + EPS) * gpost_ref[...]


MIX_ROWS = 2 * CHUNK
MIX_HEADS = 4
Z_AX, Z_AG, Z_Q, Z_K, Z_V, Z_BG = 0, D_POOL, 2 * D_POOL, 2 * D_POOL + D_QK, 2 * D_POOL + 2 * D_QK, D_IN - D_V


def _prompt_kernel(cdec_ref, z_ref, rot_ref, dmask_ref, qdec_ref, kdec_ref, pw_ref, ps_ref, gn_ref,
                   gp_ref, bm_ref, wa_ref, wb_ref, wo_ref,
                   m_ref, npool_ref, nret_ref, wo_bf_ref, ya0_ref, yb0_ref, ya1_ref, yb1_ref, ext_ref, *, tiles_per_seq):
    s = pl.program_id(0)
    n_tiles = pl.num_programs(0) - 1
    live = s < n_tiles
    c = lax.rem(jnp.minimum(s, n_tiles - 1), tiles_per_seq)
    rows = z_ref.shape[0]

    @pl.when(s == 0)
    def _():
        for ref in (ya0_ref, yb0_ref, ya1_ref, yb1_ref):
            ref[...] = jnp.zeros(ref.shape, BF16)

    @pl.when(c == 0)
    def _():
        ext_ref[0:HALO, :] = jnp.zeros((HALO, D_POOL), F32)
        nret_ref[...] = jnp.zeros(nret_ref.shape, F32)

    def step(ya_rd, yb_rd, ya_wr, yb_wr):
        y_a, y_b = ya_rd[...], yb_rd[...]
        wo_bf_ref[...] = wo_ref[...].astype(BF16)
        xa = z_ref[:, Z_AX:Z_AX + D_POOL]
        ext_ref[HALO:HALO + rows, :] = xa
        pos = c * rows + lax.broadcasted_iota(jnp.int32, (rows, 1), 0)
        nt = (((1,), (1,)), ((), ()))
        tn = (((0,), (0,)), ((), ()))

        def merge_piece(j):
            m_ref[:, j * PROJ_COLS:(j + 1) * PROJ_COLS] = _merge_chunk(y_a, y_b, j, gp_ref, bm_ref, wa_ref, wb_ref)

        def pool_group(g):
            w = POOL_WINDOWS[g]
            lo, hi = g * POOL_GROUP_DIM, (g + 1) * POOL_GROUP_DIM
            acc = ext_ref[:, lo:hi]
            span = 1
            while span < w:
                acc = acc + pltpu.roll(acc, span, 0)
                span *= 2
            inv_cnt = 1.0 / jnp.minimum(pos + 1, w).astype(F32)
            pooled = acc[HALO:, :] * inv_cnt - xa[:, lo:hi]
            ya = _pool_project(pooled, pw_ref[g], ps_ref[:, lo:hi], z_ref[:, Z_AG + lo:Z_AG + hi])
            ya_wr[:, lo:hi] = ya.astype(BF16)

        per_pool = PROJ_SPLIT // len(POOL_WINDOWS)
        assert per_pool * len(POOL_WINDOWS) == PROJ_SPLIT
        fillers = [f for g in range(len(POOL_WINDOWS))
                   for f in [functools.partial(merge_piece, g * per_pool + j) for j in range(per_pool)]
                   + [functools.partial(pool_group, g)]]
        n_sub = rows // CHUNK
        n_slots = 2 * (N_HEADS // MIX_HEADS) * n_sub
        slot = [0]

        def fill():
            lo, hi = (slot[0] * len(fillers)) // n_slots, ((slot[0] + 1) * len(fillers)) // n_slots
            slot[0] += 1
            for f in fillers[lo:hi]:
                f()

        for h0 in range(0, N_HEADS, MIX_HEADS):
            heads = range(h0, h0 + MIX_HEADS)
            s_cur = {h: nret_ref[h] for h in heads}
            for ci in range(n_sub):
                rs = slice(ci * CHUNK, (ci + 1) * CHUNK)
                cos = rot_ref[rs, :HEAD_DK]
                sin = rot_ref[rs, HEAD_DK:]
                q = {h: _rotate(z_ref[rs, Z_Q + h * HEAD_DK:Z_Q + (h + 1) * HEAD_DK], cos, sin) for h in heads}
                k = {h: _rotate(z_ref[rs, Z_K + h * HEAD_DK:Z_K + (h + 1) * HEAD_DK], cos, sin) * K_SCALE
                     for h in heads}
                v = {h: z_ref[rs, Z_V + h * HEAD_DV:Z_V + (h + 1) * HEAD_DV].astype(BF16) for h in heads}
                fill()
                scores = {h: lax.dot_general(q[h].astype(BF16), k[h].astype(BF16), nt, preferred_element_type=F32)
                          for h in heads}
                kv = {h: lax.dot_general((k[h] * kdec_ref[h]).astype(BF16), v[h], tn, preferred_element_type=F32)
                      for h in heads}
                lhs = {h: jnp.concatenate([(scores[h] * dmask_ref[h]).astype(BF16),
                                           (q[h] * qdec_ref[h]).astype(BF16)], axis=1) for h in heads}
                fill()
                for h in heads:
                    rhs = jnp.concatenate([v[h], s_cur[h].astype(BF16)], axis=0)
                    o = jnp.dot(lhs[h], rhs, preferred_element_type=F32)
                    s_cur[h] = cdec_ref[h] * s_cur[h] + kv[h]
                    vs = slice(h * HEAD_DV, (h + 1) * HEAD_DV)
                    bg = z_ref[rs, Z_BG + h * HEAD_DV:Z_BG + (h + 1) * HEAD_DV]
                    yb_wr[rs, vs] = _group_norm_gate(o, gn_ref[:, vs], bg).astype(BF16)
            for h in heads:
                nret_ref[h] = jnp.where(live, s_cur[h], nret_ref[h])
        assert slot[0] == n_slots

    @pl.when(lax.rem(s, 2) == 0)
    def _():
        step(ya1_ref, yb1_ref, ya0_ref, yb0_ref)

    @pl.when(lax.rem(s, 2) == 1)
    def _():
        step(ya0_ref, yb0_ref, ya1_ref, yb1_ref)

    @pl.when(c == tiles_per_seq - 1)
    def _():
        npool_ref[...] = ext_ref[HALO + rows - POOL_BUF:HALO + rows, :]

    ext_ref[0:HALO, :] = ext_ref[rows:rows + HALO, :]


def _prompt_mix_merge(z, batch, seq, tables, pool_w, pool_scale, gn_g, b_merge, w_a, w_b, w_o):
    rot, dmask, qdec, kdec, cdec = tables
    tps = seq // MIX_ROWS
    n_tiles = batch * tps
    assert n_tiles >= CAST_STEPS
    cur = lambda s: jnp.minimum(s, n_tiles - 1)
    prev = lambda s: jnp.maximum(s - 1, 0)
    m = batch * seq
    wo_in, wo_out, wo_shape = _cast_specs(w_o, lambda s: s)
    return pl.pallas_call(
        functools.partial(_prompt_kernel, tiles_per_seq=tps),
        out_shape=(jax.ShapeDtypeStruct((m, D_MODEL), BF16),
                   jax.ShapeDtypeStruct((batch, POOL_BUF, D_POOL), F32),
                   jax.ShapeDtypeStruct((batch, N_HEADS, HEAD_DK, HEAD_DV), F32), wo_shape),
        grid=(n_tiles + 1,),
        in_specs=[pl.BlockSpec(memory_space=pltpu.SMEM),
                  pl.BlockSpec((MIX_ROWS, D_IN), lambda s: (cur(s), 0)),
                  pl.BlockSpec((MIX_ROWS, 2 * HEAD_DK), lambda s: (cur(s) % tps, 0)),
                  _resident(dmask.shape), _resident(qdec.shape), _resident(kdec.shape),
                  _resident(pool_w.shape), _resident(pool_scale.shape), _resident(gn_g.shape),
                  pl.BlockSpec((MIX_ROWS, 2 * D_MODEL), lambda s: (prev(s), D_IN // (2 * D_MODEL))),
                  _resident(b_merge.shape), _resident(w_a.shape), _resident(w_b.shape), wo_in],
        out_specs=(pl.BlockSpec((MIX_ROWS, D_MODEL), lambda s: (prev(s), 0)),
                   pl.BlockSpec((None, POOL_BUF, D_POOL), lambda s: (cur(s) // tps, 0, 0)),
                   pl.BlockSpec((None, N_HEADS, HEAD_DK, HEAD_DV), lambda s: (cur(s) // tps, 0, 0, 0)), wo_out),
        scratch_shapes=[pltpu.VMEM((MIX_ROWS, D_POOL), BF16), pltpu.VMEM((MIX_ROWS, D_V), BF16),
                        pltpu.VMEM((MIX_ROWS, D_POOL), BF16), pltpu.VMEM((MIX_ROWS, D_V), BF16),
                        pltpu.VMEM((HALO + MIX_ROWS, D_POOL), F32)],
        compiler_params=_params("arbitrary"),
        name="prompt_mix_merge",
    )(cdec, z, rot, dmask, qdec, kdec, pool_w, pool_scale, gn_g, z, b_merge, w_a, w_b, w_o)


SAMPLE_TILE = 8
SAMPLE_HEADS = 4
TAIL_ROWS = 256


def _tail_kernel(dec_ref, m_ref, x_ref, gate_ref, wo_ref, gpost_ref,
                 zs_ref, rot_ref, pw_ref, ps_ref, gn_ref, spool_ref, sret_hbm,
                 y_ref, ya_ref, yb_ref, npool_ref, nret_hbm, o_ref, in_ring, out_ring, in_sem, out_sem,
                 *, inv_cnt, rows_per_mod, n_steps):
    bt = zs_ref.shape[0]
    n_hg = N_HEADS // SAMPLE_HEADS
    assert n_hg % RING_OUT == 0
    t = pl.program_id(0) * n_hg + pl.program_id(1)

    def state_block(hbm, step):
        seq0 = pl.multiple_of(lax.div(step, n_hg) * bt, bt)
        head0 = pl.multiple_of(lax.rem(step, n_hg) * SAMPLE_HEADS, SAMPLE_HEADS)
        return hbm.at[pl.ds(seq0, bt), pl.ds(head0, SAMPLE_HEADS)]

    def in_copy(step):
        return _ring_copy(state_block(sret_hbm, step), in_ring, in_sem, step)

    def out_copy(step):
        return _ring_copy(state_block(nret_hbm, step), out_ring, out_sem, step, to_hbm=True)

    _ring_begin(t, n_steps, in_copy, out_copy)
    s_in = in_ring.at[lax.rem(t, RING_IN)]

    def step(hg):
        s_out = out_ring.at[hg % RING_OUT]
        o = _post_norm(jnp.dot(m_ref[...], wo_ref[...], preferred_element_type=F32), gpost_ref)
        tile = pl.program_id(0) * n_hg + hg
        gate = gate_ref[pl.ds((tile * x_ref.shape[0]) // rows_per_mod, 1), :]
        y_ref[...] = x_ref[...] + gate * o

        if hg == 0:
            xa = zs_ref[:, Z_AX:Z_AX + D_POOL]
            run = xa
            wins = {1: xa}
            for j in range(1, POOL_BUF + 1):
                run = run + spool_ref[POOL_BUF - j]
                wins[j + 1] = run
            for g, w in enumerate(POOL_WINDOWS):
                lo, hi = g * POOL_GROUP_DIM, (g + 1) * POOL_GROUP_DIM
                pooled = wins[w][:, lo:hi] * inv_cnt[g] - xa[:, lo:hi]
                ya = _pool_project(pooled, pw_ref[g], ps_ref[:, lo:hi], zs_ref[:, Z_AG + lo:Z_AG + hi])
                ya_ref[:, lo:hi] = ya.astype(BF16)
            for j in range(POOL_BUF - 1):
                npool_ref[j] = spool_ref[j + 1]
            npool_ref[POOL_BUF - 1] = xa

        cos = rot_ref[:, :HEAD_DK]
        sin = rot_ref[:, HEAD_DK:]
        for hl in range(SAMPLE_HEADS):
            h = hg * SAMPLE_HEADS + hl
            vs = slice(h * HEAD_DV, (h + 1) * HEAD_DV)
            q = _rotate(zs_ref[:, Z_Q + h * HEAD_DK:Z_Q + (h + 1) * HEAD_DK], cos, sin)
            k = _rotate(zs_ref[:, Z_K + h * HEAD_DK:Z_K + (h + 1) * HEAD_DK], cos, sin) * K_SCALE
            v = zs_ref[:, Z_V + h * HEAD_DV:Z_V + (h + 1) * HEAD_DV]
            score = jnp.sum(q * k, axis=1, keepdims=True) * dec_ref[0, h]
            q_cols = jnp.transpose(q * dec_ref[1, h])
            k_cols = jnp.transpose(k * dec_ref[2, h])
            for r in range(bt):
                s_old = s_in[r, hl]
                v_row = v[r:r + 1, :]
                o_row = score[r:r + 1, :] * v_row + jnp.sum(q_cols[:, r:r + 1] * s_old, axis=0, keepdims=True)
                s_out[r, hl] = dec_ref[3, h] * s_old + k_cols[:, r:r + 1] * v_row
                o_ref[r:r + 1, vs] = o_row
        for hl in range(SAMPLE_HEADS):
            h = hg * SAMPLE_HEADS + hl
            vs = slice(h * HEAD_DV, (h + 1) * HEAD_DV)
            bg = zs_ref[:, Z_BG + h * HEAD_DV:Z_BG + (h + 1) * HEAD_DV]
            yb_ref[:, vs] = _group_norm_gate(o_ref[:, vs], gn_ref[:, vs], bg).astype(BF16)

    for hg in range(n_hg):
        pl.when(pl.program_id(1) == hg)(functools.partial(step, hg))

    _ring_end(t, n_steps, out_copy)


def _tail(merged, x, mod, w_o, g_post, zs, state_pool, state_ret, tables, inv_cnt, pool_w, pool_scale, gn_g,
          *, rows_per_mod, mod_row0):
    rot, dec = tables
    m, batch = x.shape[0], zs.shape[0]
    bt, hs = SAMPLE_TILE, SAMPLE_HEADS
    n_hg = N_HEADS // hs
    assert m == (batch // bt) * n_hg * TAIL_ROWS and mod_row0 % SUBLANES == 0 and m // rows_per_mod <= SUBLANES
    rows = lambda i, g: (i * n_hg + g, 0)
    per_tile = lambda width: pl.BlockSpec((bt, width), lambda i, g: (i, 0))
    return pl.pallas_call(
        functools.partial(_tail_kernel, inv_cnt=inv_cnt, rows_per_mod=rows_per_mod, n_steps=(batch // bt) * n_hg),
        out_shape=(jax.ShapeDtypeStruct((m, D_MODEL), F32),
                   jax.ShapeDtypeStruct((batch, D_POOL), BF16),
                   jax.ShapeDtypeStruct((batch, D_V), BF16),
                   jax.ShapeDtypeStruct(state_pool.shape, state_pool.dtype),
                   jax.ShapeDtypeStruct(state_ret.shape, state_ret.dtype)),
        grid=(batch // bt, n_hg),
        in_specs=[pl.BlockSpec(memory_space=pltpu.SMEM),
                  pl.BlockSpec((TAIL_ROWS, D_MODEL), rows), pl.BlockSpec((TAIL_ROWS, D_MODEL), rows),
                  pl.BlockSpec((SUBLANES, D_MODEL), lambda i, g: (mod_row0 // SUBLANES, 2)),
                  _resident(w_o.shape), _resident(g_post.shape),
                  per_tile(D_IN), _resident(rot.shape),
                  _resident(pool_w.shape), _resident(pool_scale.shape), _resident(gn_g.shape),
                  pl.BlockSpec((POOL_BUF, bt, D_POOL), lambda i, g: (0, i, 0)),
                  pl.BlockSpec(memory_space=pl.ANY)],
        out_specs=(pl.BlockSpec((TAIL_ROWS, D_MODEL), rows),
                   per_tile(D_POOL), per_tile(D_V),
                   pl.BlockSpec((POOL_BUF, bt, D_POOL), lambda i, g: (0, i, 0)),
                   pl.BlockSpec(memory_space=pl.ANY)),
        scratch_shapes=[pltpu.VMEM((bt, D_V), F32),
                        pltpu.VMEM((RING_IN, bt, hs, HEAD_DK, HEAD_DV), F32),
                        pltpu.VMEM((RING_OUT, bt, hs, HEAD_DK, HEAD_DV), F32),
                        pltpu.SemaphoreType.DMA((RING_IN,)), pltpu.SemaphoreType.DMA((RING_OUT,))],
        compiler_params=_params("arbitrary", "arbitrary"),
        name="tail",
    )(dec, merged, x, mod, w_o, g_post, zs, rot, pool_w, pool_scale, gn_g, state_pool, state_ret)


def _out_proj_kernel(ya_ref, yb_ref, gp_ref, x_ref, gate_ref, bm_ref, gpost_ref, wa_hbm, wb_hbm, wo_hbm,
                     y_ref, wa_buf, wb_buf, wo_buf, sem, *, rows_per_mod):
    copies, n = [], 0
    for hbm, buf in ((wa_hbm, wa_buf), (wb_hbm, wb_buf), (wo_hbm, wo_buf)):
        chunks = []
        for r0 in range(0, buf.shape[0], PROJ_COLS):
            rows = pl.ds(r0, PROJ_COLS)
            chunks.append((pltpu.make_async_copy(hbm.at[rows], buf.at[rows], sem.at[n]), r0))
            n += 1
        copies.append(chunks)
    for chunks in copies:
        for copy, _ in chunks:
            copy.start()

    def streamed_dot(lhs, buf, chunks):
        acc = None
        for copy, r0 in chunks:
            copy.wait()
            part = jnp.dot(lhs[:, r0:r0 + PROJ_COLS], buf[r0:r0 + PROJ_COLS, :], preferred_element_type=F32)
            acc = part if acc is None else acc + part
        return acc

    ya = streamed_dot(ya_ref, wa_buf, copies[0])
    yb = streamed_dot(yb_ref, wb_buf, copies[1])
    g_a = jax.nn.sigmoid(gp_ref[:, :D_MODEL] + bm_ref[:, :D_MODEL])
    g_b = jax.nn.sigmoid(gp_ref[:, D_MODEL:] + bm_ref[:, D_MODEL:])
    o = streamed_dot((g_a * ya + g_b * yb).astype(BF16), wo_buf, copies[2])
    gate = _mod_rows(gate_ref, slice(None), x_ref.shape[0], rows_per_mod)
    y_ref[...] = x_ref[...] + gate * _post_norm(o, gpost_ref)


def _out_proj(ya, yb, z, x, mod, b_merge, w_a, w_b, w_o, g_post, *, tm, rows_per_mod, mod_row0):
    m = x.shape[0]
    n_chunks = (w_a.shape[0] + w_b.shape[0] + w_o.shape[0]) // PROJ_COLS
    return pl.pallas_call(
        functools.partial(_out_proj_kernel, rows_per_mod=rows_per_mod),
        out_shape=jax.ShapeDtypeStruct((m, D_MODEL), F32),
        grid=(m // tm,),
        in_specs=[pl.BlockSpec((tm, D_POOL), lambda i: (i, 0)),
                  pl.BlockSpec((tm, D_V), lambda i: (i, 0)),
                  pl.BlockSpec((tm, 2 * D_MODEL), lambda i: (i, D_IN // (2 * D_MODEL))),
                  pl.BlockSpec((tm, D_MODEL), lambda i: (i, 0)),
                  _mod_spec(tm, rows_per_mod, mod_row0, 2),
                  _resident(b_merge.shape), _resident(g_post.shape),
                  pl.BlockSpec(memory_space=pl.ANY), pl.BlockSpec(memory_space=pl.ANY),
                  pl.BlockSpec(memory_space=pl.ANY)],
        out_specs=pl.BlockSpec((tm, D_MODEL), lambda i: (i, 0)),
        scratch_shapes=[pltpu.VMEM(w_a.shape, BF16), pltpu.VMEM(w_b.shape, BF16), pltpu.VMEM(w_o.shape, BF16),
                        pltpu.SemaphoreType.DMA((n_chunks,))],
        compiler_params=_params("arbitrary"),
        name="out_proj",
    )(ya, yb, z, x, mod, b_merge, g_post, w_a, w_b, w_o)


def _rotary_tables(start, length):
    half = HEAD_DK // 2
    inv = ROPE_BASE ** (-np.arange(half, dtype=np.float64) / half)
    ang = (start + np.arange(length, dtype=np.float64))[:, None] * inv[None, :]
    cos, sin = np.cos(ang), np.sin(ang)
    return (np.concatenate([cos, cos], axis=-1).astype(np.float32),
            np.concatenate([-sin, sin], axis=-1).astype(np.float32))


def _decay_tables(c):
    lg = np.log1p(-np.power(2.0, -5.0 - np.arange(N_HEADS, dtype=np.float64)))
    idx = np.arange(c, dtype=np.float64)
    diff = idx[:, None] - idx[None, :]
    dmask = np.where(diff[None] >= 0, np.exp(np.maximum(diff, 0.0)[None] * lg[:, None, None]), 0.0)
    q_dec = np.exp((idx + 1.0)[None, :] * lg[:, None])
    k_dec = np.exp((c - 1.0 - idx)[None, :] * lg[:, None])
    chunk_dec = np.exp(c * lg)
    return tuple(a.astype(np.float32) for a in (dmask, q_dec, k_dec, chunk_dec))


def _layer(xp, xs, c_prompt, c_sample, state_pool, state_ret, ada_w, ada_b, g_pre, g_post, w_in, pool_w, pool_scale, gn_g,
           w_a_proj, w_b_proj, w_merge, b_merge, w_out):
    batch, seq, _ = xp.shape
    dec_batch, dec_seq, _ = xs.shape
    assert dec_seq == 1 and seq % CHUNK == 0

    row = lambda v: v.reshape(1, -1)
    g_pre, g_post, pool_scale, gn_g, b_merge = map(row, (g_pre, g_post, pool_scale, gn_g, b_merge))

    mod = _modulation(c_sample, c_prompt, ada_w, row(ada_b))

    x2 = xp.reshape(batch * seq, D_MODEL)
    xs2 = xs.reshape(dec_batch, D_MODEL)
    h = _norm(x2, mod, g_pre, tm=NORM_TM, rows_per_mod=seq, mod_row0=dec_batch)
    z, zs, w_a, w_b = _in_proj(h, xs2, mod, g_pre, w_in, w_merge, w_a_proj, w_b_proj)

    cos, sin = _rotary_tables(0, seq)
    dmask, q_dec, k_dec, chunk_dec = _decay_tables(CHUNK)
    wide = lambda d: np.ascontiguousarray(np.broadcast_to(d[:, :, None], (N_HEADS, CHUNK, HEAD_DK)))
    rot = np.concatenate([cos, sin], axis=1)
    merged, pool_p, ret_p, w_o = _prompt_mix_merge(z, batch, seq, (rot, dmask, wide(q_dec), wide(k_dec), chunk_dec),
                                                   pool_w, pool_scale, gn_g, b_merge, w_a, w_b, w_out)

    cos_s, sin_s = _rotary_tables(PAST_LEN, 1)
    dmask_s, q_dec_s, k_dec_s, chunk_dec_s = _decay_tables(1)
    dec_s = np.stack([dmask_s[:, 0, 0], q_dec_s[:, 0], k_dec_s[:, 0], chunk_dec_s])
    inv_cnt = tuple(1.0 / min(PAST_LEN + 1, w) for w in POOL_WINDOWS)
    yp, ya_s, yb_s, pool_s, ret_s = _tail(merged, x2, mod, w_o, g_post, zs, jnp.transpose(state_pool, (1, 0, 2)),
                                          state_ret, (np.concatenate([cos_s, sin_s], axis=1), dec_s), inv_cnt,
                                          pool_w, pool_scale, gn_g,
                                          rows_per_mod=seq, mod_row0=dec_batch)
    pool_s = jnp.transpose(pool_s, (1, 0, 2))
    ys = _out_proj(ya_s, yb_s, zs, xs2, mod, b_merge, w_a, w_b, w_o, g_post, tm=dec_batch, rows_per_mod=1, mod_row0=0)

    return yp.reshape(xp.shape), ys.reshape(xs.shape), pool_p, ret_p, pool_s, ret_s


def kernel(x_prompt, x_sample, state_pool, state_ret, c_prompt, c_sample, ada_w, ada_b, g_pre, g_post,
           w_in, pool_w, pool_scale, gn_g, w_a_proj, w_b_proj, w_merge, b_merge, w_out):
    depth = ada_w.shape[0]
    xp, xs = x_prompt, x_sample
    pool_p, ret_p, pool_s, ret_s = [], [], [], []
    for l in range(depth):
        xp, xs, bp, sp, bs, ss = _layer(
            xp, xs, c_prompt, c_sample, state_pool[l], state_ret[l], ada_w[l], ada_b[l], g_pre[l], g_post[l], w_in[l],
            pool_w[l], pool_scale[l], gn_g[l].reshape(-1), w_a_proj[l], w_b_proj[l], w_merge[l], b_merge[l],
            w_out[l])
        pool_p.append(bp)
        ret_p.append(sp)
        pool_s.append(bs)
        ret_s.append(ss)
    return (xp, xs, jnp.stack(pool_p), jnp.stack(ret_p), jnp.stack(pool_s), jnp.stack(ret_s))
```

```python
import functools

import jax
import jax.numpy as jnp
import numpy as np
from jax import lax
from jax.experimental import pallas as pl
from jax.experimental.pallas import tpu as pltpu

F32 = jnp.float32
BF16 = jnp.bfloat16

D_MODEL = 2048
PAST_LEN = 16384
D_POOL = D_MODEL // 2
POOL_WINDOWS = (2, 4, 8, 16)
POOL_GROUP_DIM = D_POOL // len(POOL_WINDOWS)
POOL_BUF = max(POOL_WINDOWS) - 1
N_HEADS = 8
HEAD_DK = D_MODEL // 16
HEAD_DV = D_MODEL // 8
D_QK = N_HEADS * HEAD_DK
D_V = N_HEADS * HEAD_DV
CHUNK = 128
ROPE_BASE = 10000.0
EPS = 1e-6
D_IN = 2 * D_POOL + 2 * D_QK + 2 * D_V
D_Z = D_IN + 2 * D_MODEL
K_SCALE = HEAD_DK ** -0.5

VMEM_LIMIT_BYTES = 56 * 1024 * 1024
SUBLANES = 8
HALO = 16
CAST_STEPS = 32


def _params(*semantics):
    return pltpu.CompilerParams(dimension_semantics=semantics, vmem_limit_bytes=VMEM_LIMIT_BYTES)


def _cast_specs(w, step_of):
    rows = w.shape[0] // CAST_STEPS
    assert rows * CAST_STEPS == w.shape[0] and rows % (2 * SUBLANES) == 0
    spec = pl.BlockSpec((rows, w.shape[1]), lambda *idx: (jnp.minimum(step_of(*idx), CAST_STEPS - 1), 0))
    return spec, spec, jax.ShapeDtypeStruct(w.shape, BF16)


def _resident(shape):
    return pl.BlockSpec(shape, lambda *_: (0,) * len(shape), pipeline_mode=pl.Buffered(1))


def _silu(x):
    return x * jax.nn.sigmoid(x)


RING_IN, RING_OUT = 3, 2


def _ring_copy(hbm_block, ring, sem, step, *, to_hbm=False):
    slot = lax.rem(step, ring.shape[0])
    src, dst = (ring.at[slot], hbm_block) if to_hbm else (hbm_block, ring.at[slot])
    return pltpu.make_async_copy(src, dst, sem.at[slot])


def _ring_begin(t, n_steps, in_copy, out_copy=None):
    @pl.when(t == 0)
    def _():
        for k in range(RING_IN - 1):
            in_copy(jnp.int32(k)).start()

    in_copy(t).wait()

    @pl.when(t + RING_IN - 1 < n_steps)
    def _():
        in_copy(t + RING_IN - 1).start()

    if out_copy is not None:
        @pl.when(t >= RING_OUT)
        def _():
            out_copy(t - RING_OUT).wait()


def _ring_end(t, n_steps, out_copy):
    out_copy(t).start()

    @pl.when(t == n_steps - 1)
    def _():
        for k in reversed(range(RING_OUT)):
            out_copy(t - k).wait()


MOD_TN = 512


def _mod_kernel(cs_ref, cp_ref, w_hbm, b_ref, o_ref, w_ring, sem, *, n_steps):
    t = pl.program_id(0)
    tn = w_ring.shape[2]

    def in_copy(step):
        return _ring_copy(w_hbm.at[:, pl.ds(pl.multiple_of(step * tn, tn), tn)], w_ring, sem, step)

    _ring_begin(t, n_steps, in_copy)
    n_pad = o_ref.shape[0] - cs_ref.shape[0] - cp_ref.shape[0]
    c = jnp.concatenate([cs_ref[...], cp_ref[...], jnp.zeros((n_pad, D_MODEL), F32)], axis=0)
    w = w_ring[lax.rem(t, RING_IN)].astype(BF16)
    o_ref[...] = jnp.dot(_silu(c).astype(BF16), w, preferred_element_type=F32) + b_ref[...]


def _modulation(c_sample, c_prompt, ada_w, ada_b):
    ns, n_p = c_sample.shape[0], c_prompt.shape[0]
    assert ns % SUBLANES == 0
    rows = ns + n_p + (-n_p) % SUBLANES
    n_steps = 3 * D_MODEL // MOD_TN
    return pl.pallas_call(
        functools.partial(_mod_kernel, n_steps=n_steps),
        out_shape=jax.ShapeDtypeStruct((rows, 3 * D_MODEL), F32),
        grid=(n_steps,),
        in_specs=[pl.BlockSpec((ns, D_MODEL), lambda j: (0, 0)),
                  pl.BlockSpec((n_p, D_MODEL), lambda j: (0, 0)),
                  pl.BlockSpec(memory_space=pl.ANY),
                  pl.BlockSpec((1, MOD_TN), lambda j: (0, j))],
        out_specs=pl.BlockSpec((rows, MOD_TN), lambda j: (0, j)),
        scratch_shapes=[pltpu.VMEM((RING_IN, D_MODEL, MOD_TN), F32), pltpu.SemaphoreType.DMA((RING_IN,))],
        compiler_params=_params("arbitrary"),
        name="modulation",
    )(c_sample, c_prompt, ada_w, ada_b)


NORM_ROWS = 128
NORM_TM = 512


def _mod_rows(ref, rows, tm, rows_per_mod):
    if rows_per_mod == 1:
        return ref[rows, :]
    return ref[pl.ds((pl.program_id(0) * tm) // rows_per_mod, 1), :]


def _mod_spec(tm, rows_per_mod, mod_row0, col):
    if rows_per_mod == 1:
        assert mod_row0 % tm == 0
        return pl.BlockSpec((tm, D_MODEL), lambda i, *_: (mod_row0 // tm + i, col))
    assert mod_row0 % SUBLANES == 0
    return pl.BlockSpec((SUBLANES, D_MODEL), lambda i, *_: (mod_row0 // SUBLANES, col))


def _norm_mod(x, g, scale, shift):
    xn = x * lax.rsqrt(jnp.mean(x * x, axis=-1, keepdims=True) + EPS) * g
    return (xn * (1.0 + scale) + shift).astype(BF16)


def _norm_kernel(x_hbm, shift_ref, scale_ref, g_ref, h_hbm, x_ring, h_ring, in_sem, out_sem,
                 *, rows_per_mod, n_steps):
    tm = x_ring.shape[1]
    t = pl.program_id(0)

    def rows_of(step):
        return pl.ds(pl.multiple_of(step * tm, tm), tm)

    def in_copy(step):
        return _ring_copy(x_hbm.at[rows_of(step)], x_ring, in_sem, step)

    def out_copy(step):
        return _ring_copy(h_hbm.at[rows_of(step)], h_ring, out_sem, step, to_hbm=True)

    _ring_begin(t, n_steps, in_copy, out_copy)
    x_ref, h_ref = x_ring.at[lax.rem(t, RING_IN)], h_ring.at[lax.rem(t, RING_OUT)]

    def body(r, carry):
        rows = pl.ds(pl.multiple_of(r * NORM_ROWS, NORM_ROWS), NORM_ROWS)
        h_ref[rows, :] = _norm_mod(x_ref[rows, :], g_ref[...], _mod_rows(scale_ref, rows, tm, rows_per_mod),
                                   _mod_rows(shift_ref, rows, tm, rows_per_mod))
        return carry
    lax.fori_loop(0, tm // NORM_ROWS, body, 0)
    _ring_end(t, n_steps, out_copy)


def _norm(x, mod, g_pre, *, tm, rows_per_mod, mod_row0):
    m = x.shape[0]
    n_steps = m // tm
    assert (rows_per_mod == 1 or m // rows_per_mod <= SUBLANES) and n_steps >= 2
    mod_spec = lambda col: _mod_spec(tm, rows_per_mod, mod_row0, col)
    return pl.pallas_call(
        functools.partial(_norm_kernel, rows_per_mod=rows_per_mod, n_steps=n_steps),
        out_shape=jax.ShapeDtypeStruct((m, D_MODEL), BF16),
        grid=(n_steps,),
        in_specs=[pl.BlockSpec(memory_space=pl.ANY), mod_spec(0), mod_spec(1),
                  pl.BlockSpec((1, D_MODEL), lambda i: (0, 0))],
        out_specs=pl.BlockSpec(memory_space=pl.ANY),
        scratch_shapes=[pltpu.VMEM((RING_IN, tm, D_MODEL), F32), pltpu.VMEM((RING_OUT, tm, D_MODEL), BF16),
                        pltpu.SemaphoreType.DMA((RING_IN,)), pltpu.SemaphoreType.DMA((RING_OUT,))],
        compiler_params=_params("arbitrary"),
        name="norm",
    )(x, mod, mod, g_pre)


IN_TM = 1024
IN_TN = 2048
IN_DOT_COLS = 512


def _in_proj_kernel(h_ref, xs_ref, shift_s_ref, scale_s_ref, g_ref, wa_ref, wb_ref, w_in_hbm, w_mg_hbm,
                    z_ref, zs_ref, wa_bf_ref, wb_bf_ref, wbf0_ref, wbf1_ref, stage_ref, hs_ref, sem,
                    *, n_in_tiles, n_j, n_i):
    j, i = pl.program_id(0), pl.program_id(1)
    t = j * n_i + i
    total = n_j * n_i
    tn = wbf0_ref.shape[1]
    chunk = stage_ref.shape[1]

    def chunk_copy(w_hbm, col0, r, b):
        rows = pl.ds(pl.multiple_of(r * chunk, chunk), chunk)
        cols = pl.ds(col0 if isinstance(col0, int) else pl.multiple_of(col0, tn), tn)
        return pltpu.make_async_copy(w_hbm.at[rows, cols], stage_ref.at[b], sem.at[b])

    def start_chunk(g):
        g = lax.rem(jnp.asarray(g, jnp.int32), total)
        jt, r, b = lax.div(g, n_i), lax.rem(g, n_i), lax.rem(g, 2)

        @pl.when(jt < n_in_tiles)
        def _():
            chunk_copy(w_in_hbm, jt * tn, r, b).start()

        @pl.when(jt >= n_in_tiles)
        def _():
            chunk_copy(w_mg_hbm, (jt - n_in_tiles) * tn, r, b).start()

    def land_chunk(g, dst_ref):
        g = jnp.asarray(g, jnp.int32)
        r, b = lax.rem(g, n_i), lax.rem(g, 2)
        chunk_copy(w_in_hbm, 0, r, b).wait()
        dst_ref[pl.ds(pl.multiple_of(r * chunk, chunk), chunk), :] = stage_ref[b].astype(BF16)

    @pl.when(t == 0)
    def _():
        hs_ref[...] = _norm_mod(xs_ref[...], g_ref[...], scale_s_ref[...], shift_s_ref[...])
        start_chunk(0)

        def body(g, carry):
            start_chunk(g + 1)
            land_chunk(g, wbf0_ref)
            return carry
        lax.fori_loop(0, n_i, body, 0)

    @pl.when(t + 1 < total)
    def _():
        start_chunk(t + n_i + 1)

    def multiply(w_cur, w_nxt):
        land_chunk(t + n_i, w_nxt)
        wa_bf_ref[...] = wa_ref[...].astype(BF16)
        wb_bf_ref[...] = wb_ref[...].astype(BF16)
        for c0 in range(0, tn, IN_DOT_COLS):
            cols = slice(c0, c0 + IN_DOT_COLS)
            z_ref[:, cols] = jnp.dot(h_ref[...], w_cur[:, cols], preferred_element_type=F32)

        @pl.when(i == 0)
        def _():
            zs_ref[...] = jnp.dot(hs_ref[...], w_cur[...], preferred_element_type=F32)

    @pl.when(lax.rem(j, 2) == 0)
    def _():
        multiply(wbf0_ref, wbf1_ref)

    @pl.when(lax.rem(j, 2) == 1)
    def _():
        multiply(wbf1_ref, wbf0_ref)


def _in_proj(h, xs, mod, g_pre, w_in, w_mg, w_a, w_b):
    m, ms = h.shape[0], xs.shape[0]
    n_in, n_mg = w_in.shape[1] // IN_TN, w_mg.shape[1] // IN_TN
    n_i = m // IN_TM
    chunk = D_MODEL // n_i
    assert chunk * n_i == D_MODEL and chunk % 16 == 0 and (n_in + n_mg) * n_i >= CAST_STEPS
    const = lambda shape, col: pl.BlockSpec(shape, lambda j, i: (0, col), pipeline_mode=pl.Buffered(1))
    wa_in, wa_out, wa_shape = _cast_specs(w_a, lambda j, i: j * n_i + i)
    wb_in, wb_out, wb_shape = _cast_specs(w_b, lambda j, i: j * n_i + i)
    return pl.pallas_call(
        functools.partial(_in_proj_kernel, n_in_tiles=n_in, n_j=n_in + n_mg, n_i=n_i),
        out_shape=(jax.ShapeDtypeStruct((m, D_Z), F32), jax.ShapeDtypeStruct((ms, D_Z), F32), wa_shape, wb_shape),
        grid=(n_in + n_mg, n_i),
        in_specs=[pl.BlockSpec((IN_TM, D_MODEL), lambda j, i: (i, 0)),
                  const((ms, D_MODEL), 0), const((ms, D_MODEL), 0), const((ms, D_MODEL), 1),
                  const((1, D_MODEL), 0), wa_in, wb_in,
                  pl.BlockSpec(memory_space=pl.ANY), pl.BlockSpec(memory_space=pl.ANY)],
        out_specs=(pl.BlockSpec((IN_TM, IN_TN), lambda j, i: (i, j)),
                   pl.BlockSpec((ms, IN_TN), lambda j, i: (0, j)), wa_out, wb_out),
        scratch_shapes=[pltpu.VMEM((D_MODEL, IN_TN), BF16), pltpu.VMEM((D_MODEL, IN_TN), BF16),
                        pltpu.VMEM((2, chunk, IN_TN), F32),
                        pltpu.VMEM((ms, D_MODEL), BF16),
                        pltpu.SemaphoreType.DMA((2,))],
        compiler_params=_params("arbitrary", "arbitrary"),
        name="in_proj",
    )(h, xs, mod, mod, g_pre, w_a, w_b, w_in, w_mg)


def _rotate(x, cos, sin_signed):
    return x * cos + pltpu.roll(x, HEAD_DK // 2, 1) * sin_signed


def _group_norm_gate(o, gn, bg):
    mu = jnp.mean(o, axis=-1, keepdims=True)
    d = o - mu
    var = jnp.mean(d * d, axis=-1, keepdims=True)
    return d * lax.rsqrt(var + EPS) * gn * _silu(bg)


def _pool_project(pooled, pw, ps, ag):
    mixed = jnp.dot(pooled.astype(BF16), pw.astype(BF16), preferred_element_type=F32)
    return mixed * ps * _silu(ag)


PROJ_SPLIT = 8
PROJ_COLS = D_MODEL // PROJ_SPLIT


def _merge_chunk(y_a, y_b, j, gp_ref, bm_ref, wa_ref, wb_ref):
    lo, hi = j * PROJ_COLS, (j + 1) * PROJ_COLS
    ya = jnp.dot(y_a, wa_ref[:, lo:hi], preferred_element_type=F32)
    yb = jnp.dot(y_b, wb_ref[:, lo:hi], preferred_element_type=F32)
    g_a = jax.nn.sigmoid(gp_ref[:, lo:hi] + bm_ref[:, lo:hi])
    g_b = jax.nn.sigmoid(gp_ref[:, D_MODEL + lo:D_MODEL + hi] + bm_ref[:, D_MODEL + lo:D_MODEL + hi])
    return (g_a * ya + g_b * yb).astype(BF16)


def _out_chunk(merged, j, wo_ref):
    return jnp.dot(merged, wo_ref[:, j * PROJ_COLS:(j + 1) * PROJ_COLS], preferred_element_type=F32)


def _post_norm(o, gpost_ref):
    return o * lax.rsqrt(jnp.mean(o * o, axis=-1, keepdims=True) + EPS) * gpost_ref[...]


def _merge_project(y_a, y_b, gp_ref, bm_ref, wa_ref, wb_ref, wo_ref, gpost_ref):
    merged = jnp.concatenate([_merge_chunk(y_a, y_b, j, gp_ref, bm_ref, wa_ref, wb_ref)
                              for j in range(PROJ_SPLIT)], axis=1)
    o = jnp.concatenate([_out_chunk(merged, j, wo_ref) for j in range(PROJ_SPLIT)], axis=1)
    return _post_norm(o, gpost_ref)


MIX_ROWS = 2 * CHUNK
MIX_HEADS = 4
Z_AX, Z_AG, Z_Q, Z_K, Z_V, Z_BG = 0, D_POOL, 2 * D_POOL, 2 * D_POOL + D_QK, 2 * D_POOL + 2 * D_QK, D_IN - D_V


def _prompt_kernel(cdec_ref, z_ref, rot_ref, dmask_ref, qdec_ref, kdec_ref, pw_ref, ps_ref, gn_ref,
                   gp_ref, bm_ref, wa_ref, wb_ref, wo_ref,
                   m_ref, npool_ref, nret_ref, wo_bf_ref, ya0_ref, yb0_ref, ya1_ref, yb1_ref, ext_ref, *, tiles_per_seq):
    s = pl.program_id(0)
    n_tiles = pl.num_programs(0) - 1
    live = s < n_tiles
    c = lax.rem(jnp.minimum(s, n_tiles - 1), tiles_per_seq)
    rows = z_ref.shape[0]

    @pl.when(s == 0)
    def _():
        for ref in (ya0_ref, yb0_ref, ya1_ref, yb1_ref):
            ref[...] = jnp.zeros(ref.shape, BF16)

    @pl.when(c == 0)
    def _():
        ext_ref[0:HALO, :] = jnp.zeros((HALO, D_POOL), F32)
        nret_ref[...] = jnp.zeros(nret_ref.shape, F32)

    def step(ya_rd, yb_rd, ya_wr, yb_wr):
        y_a, y_b = ya_rd[...], yb_rd[...]
        wo_bf_ref[...] = wo_ref[...].astype(BF16)
        xa = z_ref[:, Z_AX:Z_AX + D_POOL]
        ext_ref[HALO:HALO + rows, :] = xa
        pos = c * rows + lax.broadcasted_iota(jnp.int32, (rows, 1), 0)
        nt = (((1,), (1,)), ((), ()))
        tn = (((0,), (0,)), ((), ()))

        def merge_piece(j):
            m_ref[:, j * PROJ_COLS:(j + 1) * PROJ_COLS] = _merge_chunk(y_a, y_b, j, gp_ref, bm_ref, wa_ref, wb_ref)

        def pool_group(g):
            w = POOL_WINDOWS[g]
            lo, hi = g * POOL_GROUP_DIM, (g + 1) * POOL_GROUP_DIM
            acc = ext_ref[:, lo:hi]
            span = 1
            while span < w:
                acc = acc + pltpu.roll(acc, span, 0)
                span *= 2
            inv_cnt = 1.0 / jnp.minimum(pos + 1, w).astype(F32)
            pooled = acc[HALO:, :] * inv_cnt - xa[:, lo:hi]
            ya = _pool_project(pooled, pw_ref[g], ps_ref[:, lo:hi], z_ref[:, Z_AG + lo:Z_AG + hi])
            ya_wr[:, lo:hi] = ya.astype(BF16)

        per_pool = PROJ_SPLIT // len(POOL_WINDOWS)
        assert per_pool * len(POOL_WINDOWS) == PROJ_SPLIT
        fillers = [f for g in range(len(POOL_WINDOWS))
                   for f in [functools.partial(merge_piece, g * per_pool + j) for j in range(per_pool)]
                   + [functools.partial(pool_group, g)]]
        n_sub = rows // CHUNK
        n_slots = 2 * (N_HEADS // MIX_HEADS) * n_sub
        slot = [0]

        def fill():
            lo, hi = (slot[0] * len(fillers)) // n_slots, ((slot[0] + 1) * len(fillers)) // n_slots
            slot[0] += 1
            for f in fillers[lo:hi]:
                f()

        for h0 in range(0, N_HEADS, MIX_HEADS):
            heads = range(h0, h0 + MIX_HEADS)
            s_cur = {h: nret_ref[h] for h in heads}
            for ci in range(n_sub):
                rs = slice(ci * CHUNK, (ci + 1) * CHUNK)
                cos = rot_ref[rs, :HEAD_DK]
                sin = rot_ref[rs, HEAD_DK:]
                q = {h: _rotate(z_ref[rs, Z_Q + h * HEAD_DK:Z_Q + (h + 1) * HEAD_DK], cos, sin) for h in heads}
                k = {h: _rotate(z_ref[rs, Z_K + h * HEAD_DK:Z_K + (h + 1) * HEAD_DK], cos, sin) * K_SCALE
                     for h in heads}
                v = {h: z_ref[rs, Z_V + h * HEAD_DV:Z_V + (h + 1) * HEAD_DV].astype(BF16) for h in heads}
                fill()
                scores = {h: lax.dot_general(q[h].astype(BF16), k[h].astype(BF16), nt, preferred_element_type=F32)
                          for h in heads}
                kv = {h: lax.dot_general((k[h] * kdec_ref[h]).astype(BF16), v[h], tn, preferred_element_type=F32)
                      for h in heads}
                lhs = {h: jnp.concatenate([(scores[h] * dmask_ref[h]).astype(BF16),
                                           (q[h] * qdec_ref[h]).astype(BF16)], axis=1) for h in heads}
                fill()
                for h in heads:
                    rhs = jnp.concatenate([v[h], s_cur[h].astype(BF16)], axis=0)
                    o = jnp.dot(lhs[h], rhs, preferred_element_type=F32)
                    s_cur[h] = cdec_ref[h] * s_cur[h] + kv[h]
                    vs = slice(h * HEAD_DV, (h + 1) * HEAD_DV)
                    bg = z_ref[rs, Z_BG + h * HEAD_DV:Z_BG + (h + 1) * HEAD_DV]
                    yb_wr[rs, vs] = _group_norm_gate(o, gn_ref[h:h + 1, :], bg).astype(BF16)
            for h in heads:
                nret_ref[h] = jnp.where(live, s_cur[h], nret_ref[h])
        assert slot[0] == n_slots

    @pl.when(lax.rem(s, 2) == 0)
    def _():
        step(ya1_ref, yb1_ref, ya0_ref, yb0_ref)

    @pl.when(lax.rem(s, 2) == 1)
    def _():
        step(ya0_ref, yb0_ref, ya1_ref, yb1_ref)

    @pl.when(c == tiles_per_seq - 1)
    def _():
        npool_ref[...] = ext_ref[HALO + rows - POOL_BUF:HALO + rows, :]

    ext_ref[0:HALO, :] = ext_ref[rows:rows + HALO, :]


def _prompt_mix_merge(z, batch, seq, tables, pool_w, pool_scale, gn_g, b_merge, w_a, w_b, w_o):
    rot, dmask, qdec, kdec, cdec = tables
    tps = seq // MIX_ROWS
    n_tiles = batch * tps
    assert n_tiles >= CAST_STEPS
    cur = lambda s: jnp.minimum(s, n_tiles - 1)
    prev = lambda s: jnp.maximum(s - 1, 0)
    m = batch * seq
    wo_in, wo_out, wo_shape = _cast_specs(w_o, lambda s: s)
    return pl.pallas_call(
        functools.partial(_prompt_kernel, tiles_per_seq=tps),
        out_shape=(jax.ShapeDtypeStruct((m, D_MODEL), BF16),
                   jax.ShapeDtypeStruct((batch, POOL_BUF, D_POOL), F32),
                   jax.ShapeDtypeStruct((batch, N_HEADS, HEAD_DK, HEAD_DV), F32), wo_shape),
        grid=(n_tiles + 1,),
        in_specs=[pl.BlockSpec(memory_space=pltpu.SMEM),
                  pl.BlockSpec((MIX_ROWS, D_IN), lambda s: (cur(s), 0)),
                  pl.BlockSpec((MIX_ROWS, 2 * HEAD_DK), lambda s: (cur(s) % tps, 0)),
                  _resident(dmask.shape), _resident(qdec.shape), _resident(kdec.shape),
                  _resident(pool_w.shape), _resident(pool_scale.shape), _resident(gn_g.shape),
                  pl.BlockSpec((MIX_ROWS, 2 * D_MODEL), lambda s: (prev(s), D_IN // (2 * D_MODEL))),
                  _resident(b_merge.shape), _resident(w_a.shape), _resident(w_b.shape), wo_in],
        out_specs=(pl.BlockSpec((MIX_ROWS, D_MODEL), lambda s: (prev(s), 0)),
                   pl.BlockSpec((None, POOL_BUF, D_POOL), lambda s: (cur(s) // tps, 0, 0)),
                   pl.BlockSpec((None, N_HEADS, HEAD_DK, HEAD_DV), lambda s: (cur(s) // tps, 0, 0, 0)), wo_out),
        scratch_shapes=[pltpu.VMEM((MIX_ROWS, D_POOL), BF16), pltpu.VMEM((MIX_ROWS, D_V), BF16),
                        pltpu.VMEM((MIX_ROWS, D_POOL), BF16), pltpu.VMEM((MIX_ROWS, D_V), BF16),
                        pltpu.VMEM((HALO + MIX_ROWS, D_POOL), F32)],
        compiler_params=_params("arbitrary"),
        name="prompt_mix_merge",
    )(cdec, z, rot, dmask, qdec, kdec, pool_w, pool_scale, gn_g, z, b_merge, w_a, w_b, w_o)


SAMPLE_TILE = 8
SAMPLE_HEADS = 4
TAIL_ROWS = 256


def _tail_kernel(dec_ref, m_ref, x_ref, gate_ref, wo_ref, gpost_ref,
                 zs_ref, rot_ref, pw_ref, ps_ref, gn_ref, spool_ref, sret_hbm,
                 y_ref, ya_ref, yb_ref, npool_ref, nret_hbm, o_ref, in_ring, out_ring, in_sem, out_sem,
                 *, inv_cnt, rows_per_mod, n_steps):
    bt = zs_ref.shape[0]
    n_hg = N_HEADS // SAMPLE_HEADS
    assert n_hg % RING_OUT == 0
    t = pl.program_id(0) * n_hg + pl.program_id(1)

    def state_block(hbm, step):
        seq0 = pl.multiple_of(lax.div(step, n_hg) * bt, bt)
        head0 = pl.multiple_of(lax.rem(step, n_hg) * SAMPLE_HEADS, SAMPLE_HEADS)
        return hbm.at[pl.ds(seq0, bt), pl.ds(head0, SAMPLE_HEADS)]

    def in_copy(step):
        return _ring_copy(state_block(sret_hbm, step), in_ring, in_sem, step)

    def out_copy(step):
        return _ring_copy(state_block(nret_hbm, step), out_ring, out_sem, step, to_hbm=True)

    _ring_begin(t, n_steps, in_copy, out_copy)
    s_in = in_ring.at[lax.rem(t, RING_IN)]

    def step(hg):
        s_out = out_ring.at[hg % RING_OUT]
        o = _post_norm(jnp.dot(m_ref[...], wo_ref[...], preferred_element_type=F32), gpost_ref)
        tile = pl.program_id(0) * n_hg + hg
        gate = gate_ref[pl.ds((tile * x_ref.shape[0]) // rows_per_mod, 1), :]
        y_ref[...] = x_ref[...] + gate * o

        if hg == 0:
            xa = zs_ref[:, Z_AX:Z_AX + D_POOL]
            run = xa
            wins = {1: xa}
            for j in range(1, POOL_BUF + 1):
                run = run + spool_ref[POOL_BUF - j]
                wins[j + 1] = run
            for g, w in enumerate(POOL_WINDOWS):
                lo, hi = g * POOL_GROUP_DIM, (g + 1) * POOL_GROUP_DIM
                pooled = wins[w][:, lo:hi] * inv_cnt[g] - xa[:, lo:hi]
                ya = _pool_project(pooled, pw_ref[g], ps_ref[:, lo:hi], zs_ref[:, Z_AG + lo:Z_AG + hi])
                ya_ref[:, lo:hi] = ya.astype(BF16)
            for j in range(POOL_BUF - 1):
                npool_ref[j] = spool_ref[j + 1]
            npool_ref[POOL_BUF - 1] = xa

        cos = rot_ref[:, :HEAD_DK]
        sin = rot_ref[:, HEAD_DK:]
        for hl in range(SAMPLE_HEADS):
            h = hg * SAMPLE_HEADS + hl
            vs = slice(h * HEAD_DV, (h + 1) * HEAD_DV)
            q = _rotate(zs_ref[:, Z_Q + h * HEAD_DK:Z_Q + (h + 1) * HEAD_DK], cos, sin)
            k = _rotate(zs_ref[:, Z_K + h * HEAD_DK:Z_K + (h + 1) * HEAD_DK], cos, sin) * K_SCALE
            v = zs_ref[:, Z_V + h * HEAD_DV:Z_V + (h + 1) * HEAD_DV]
            score = jnp.sum(q * k, axis=1, keepdims=True) * dec_ref[0, h]
            q_cols = jnp.transpose(q * dec_ref[1, h])
            k_cols = jnp.transpose(k * dec_ref[2, h])
            for r in range(bt):
                s_old = s_in[r, hl]
                v_row = v[r:r + 1, :]
                o_row = score[r:r + 1, :] * v_row + jnp.sum(q_cols[:, r:r + 1] * s_old, axis=0, keepdims=True)
                s_out[r, hl] = dec_ref[3, h] * s_old + k_cols[:, r:r + 1] * v_row
                o_ref[r:r + 1, vs] = o_row
        for hl in range(SAMPLE_HEADS):
            h = hg * SAMPLE_HEADS + hl
            vs = slice(h * HEAD_DV, (h + 1) * HEAD_DV)
            bg = zs_ref[:, Z_BG + h * HEAD_DV:Z_BG + (h + 1) * HEAD_DV]
            yb_ref[:, vs] = _group_norm_gate(o_ref[:, vs], gn_ref[h:h + 1, :], bg).astype(BF16)

    for hg in range(n_hg):
        pl.when(pl.program_id(1) == hg)(functools.partial(step, hg))

    _ring_end(t, n_steps, out_copy)


def _tail(merged, x, mod, w_o, g_post, zs, state_pool, state_ret, tables, inv_cnt, pool_w, pool_scale, gn_g,
          *, rows_per_mod, mod_row0):
    rot, dec = tables
    m, batch = x.shape[0], zs.shape[0]
    bt, hs = SAMPLE_TILE, SAMPLE_HEADS
    n_hg = N_HEADS // hs
    assert m == (batch // bt) * n_hg * TAIL_ROWS and mod_row0 % SUBLANES == 0 and m // rows_per_mod <= SUBLANES
    rows = lambda i, g: (i * n_hg + g, 0)
    per_tile = lambda width: pl.BlockSpec((bt, width), lambda i, g: (i, 0))
    return pl.pallas_call(
        functools.partial(_tail_kernel, inv_cnt=inv_cnt, rows_per_mod=rows_per_mod, n_steps=(batch // bt) * n_hg),
        out_shape=(jax.ShapeDtypeStruct((m, D_MODEL), F32),
                   jax.ShapeDtypeStruct((batch, D_POOL), BF16),
                   jax.ShapeDtypeStruct((batch, D_V), BF16),
                   jax.ShapeDtypeStruct(state_pool.shape, state_pool.dtype),
                   jax.ShapeDtypeStruct(state_ret.shape, state_ret.dtype)),
        grid=(batch // bt, n_hg),
        in_specs=[pl.BlockSpec(memory_space=pltpu.SMEM),
                  pl.BlockSpec((TAIL_ROWS, D_MODEL), rows), pl.BlockSpec((TAIL_ROWS, D_MODEL), rows),
                  pl.BlockSpec((SUBLANES, D_MODEL), lambda i, g: (mod_row0 // SUBLANES, 2)),
                  _resident(w_o.shape), _resident(g_post.shape),
                  per_tile(D_IN), _resident(rot.shape),
                  _resident(pool_w.shape), _resident(pool_scale.shape), _resident(gn_g.shape),
                  pl.BlockSpec((POOL_BUF, bt, D_POOL), lambda i, g: (0, i, 0)),
                  pl.BlockSpec(memory_space=pl.ANY)],
        out_specs=(pl.BlockSpec((TAIL_ROWS, D_MODEL), rows),
                   per_tile(D_POOL), per_tile(D_V),
                   pl.BlockSpec((POOL_BUF, bt, D_POOL), lambda i, g: (0, i, 0)),
                   pl.BlockSpec(memory_space=pl.ANY)),
        scratch_shapes=[pltpu.VMEM((bt, D_V), F32),
                        pltpu.VMEM((RING_IN, bt, hs, HEAD_DK, HEAD_DV), F32),
                        pltpu.VMEM((RING_OUT, bt, hs, HEAD_DK, HEAD_DV), F32),
                        pltpu.SemaphoreType.DMA((RING_IN,)), pltpu.SemaphoreType.DMA((RING_OUT,))],
        compiler_params=_params("arbitrary", "arbitrary"),
        name="tail",
    )(dec, merged, x, mod, w_o, g_post, zs, rot, pool_w, pool_scale, gn_g, state_pool, state_ret)


def _out_proj_kernel(ya_ref, yb_ref, gp_ref, x_ref, gate_ref, bm_ref, wa_ref, wb_ref, wo_ref, gpost_ref, y_ref,
                     *, rows_per_mod):
    o = _merge_project(ya_ref[...], yb_ref[...], gp_ref, bm_ref, wa_ref, wb_ref, wo_ref, gpost_ref)
    gate = _mod_rows(gate_ref, slice(None), x_ref.shape[0], rows_per_mod)
    y_ref[...] = x_ref[...] + gate * o


def _out_proj(ya, yb, z, x, mod, b_merge, w_a, w_b, w_o, g_post, *, tm, rows_per_mod, mod_row0):
    m = x.shape[0]
    return pl.pallas_call(
        functools.partial(_out_proj_kernel, rows_per_mod=rows_per_mod),
        out_shape=jax.ShapeDtypeStruct((m, D_MODEL), F32),
        grid=(m // tm,),
        in_specs=[pl.BlockSpec((tm, D_POOL), lambda i: (i, 0)),
                  pl.BlockSpec((tm, D_V), lambda i: (i, 0)),
                  pl.BlockSpec((tm, 2 * D_MODEL), lambda i: (i, D_IN // (2 * D_MODEL))),
                  pl.BlockSpec((tm, D_MODEL), lambda i: (i, 0)),
                  _mod_spec(tm, rows_per_mod, mod_row0, 2),
                  _resident(b_merge.shape), _resident(w_a.shape), _resident(w_b.shape), _resident(w_o.shape),
                  _resident(g_post.shape)],
        out_specs=pl.BlockSpec((tm, D_MODEL), lambda i: (i, 0)),
        compiler_params=_params("parallel"),
        name="out_proj",
    )(ya, yb, z, x, mod, b_merge, w_a, w_b, w_o, g_post)


def _rotary_tables(start, length):
    half = HEAD_DK // 2
    inv = ROPE_BASE ** (-np.arange(half, dtype=np.float64) / half)
    ang = (start + np.arange(length, dtype=np.float64))[:, None] * inv[None, :]
    cos, sin = np.cos(ang), np.sin(ang)
    return (np.concatenate([cos, cos], axis=-1).astype(np.float32),
            np.concatenate([-sin, sin], axis=-1).astype(np.float32))


def _decay_tables(c):
    lg = np.log1p(-np.power(2.0, -5.0 - np.arange(N_HEADS, dtype=np.float64)))
    idx = np.arange(c, dtype=np.float64)
    diff = idx[:, None] - idx[None, :]
    dmask = np.where(diff[None] >= 0, np.exp(np.maximum(diff, 0.0)[None] * lg[:, None, None]), 0.0)
    q_dec = np.exp((idx + 1.0)[None, :] * lg[:, None])
    k_dec = np.exp((c - 1.0 - idx)[None, :] * lg[:, None])
    chunk_dec = np.exp(c * lg)
    return tuple(a.astype(np.float32) for a in (dmask, q_dec, k_dec, chunk_dec))


def _layer(xp, xs, c_prompt, c_sample, state_pool, state_ret, ada_w, ada_b, g_pre, g_post, w_in, pool_w, pool_scale, gn_g,
           w_a_proj, w_b_proj, w_merge, b_merge, w_out):
    batch, seq, _ = xp.shape
    dec_batch, dec_seq, _ = xs.shape
    assert dec_seq == 1 and seq % CHUNK == 0

    row = lambda v: v.reshape(1, -1)
    g_pre, g_post, pool_scale, b_merge = map(row, (g_pre, g_post, pool_scale, b_merge))
    gn_g = gn_g.reshape(N_HEADS, HEAD_DV)

    mod = _modulation(c_sample, c_prompt, ada_w, row(ada_b))

    x2 = xp.reshape(batch * seq, D_MODEL)
    xs2 = xs.reshape(dec_batch, D_MODEL)
    h = _norm(x2, mod, g_pre, tm=NORM_TM, rows_per_mod=seq, mod_row0=dec_batch)
    z, zs, w_a, w_b = _in_proj(h, xs2, mod, g_pre, w_in, w_merge, w_a_proj, w_b_proj)

    cos, sin = _rotary_tables(0, seq)
    dmask, q_dec, k_dec, chunk_dec = _decay_tables(CHUNK)
    wide = lambda d: np.ascontiguousarray(np.broadcast_to(d[:, :, None], (N_HEADS, CHUNK, HEAD_DK)))
    rot = np.concatenate([cos, sin], axis=1)
    merged, pool_p, ret_p, w_o = _prompt_mix_merge(z, batch, seq, (rot, dmask, wide(q_dec), wide(k_dec), chunk_dec),
                                                   pool_w, pool_scale, gn_g, b_merge, w_a, w_b, w_out)

    cos_s, sin_s = _rotary_tables(PAST_LEN, 1)
    dmask_s, q_dec_s, k_dec_s, chunk_dec_s = _decay_tables(1)
    dec_s = np.stack([dmask_s[:, 0, 0], q_dec_s[:, 0], k_dec_s[:, 0], chunk_dec_s])
    inv_cnt = tuple(1.0 / min(PAST_LEN + 1, w) for w in POOL_WINDOWS)
    yp, ya_s, yb_s, pool_s, ret_s = _tail(merged, x2, mod, w_o, g_post, zs, jnp.transpose(state_pool, (1, 0, 2)),
                                          state_ret, (np.concatenate([cos_s, sin_s], axis=1), dec_s), inv_cnt,
                                          pool_w, pool_scale, gn_g,
                                          rows_per_mod=seq, mod_row0=dec_batch)
    pool_s = jnp.transpose(pool_s, (1, 0, 2))
    ys = _out_proj(ya_s, yb_s, zs, xs2, mod, b_merge, w_a, w_b, w_o, g_post, tm=dec_batch, rows_per_mod=1, mod_row0=0)

    return yp.reshape(xp.shape), ys.reshape(xs.shape), pool_p, ret_p, pool_s, ret_s


def kernel(x_prompt, x_sample, state_pool, state_ret, c_prompt, c_sample, ada_w, ada_b, g_pre, g_post,
           w_in, pool_w, pool_scale, gn_g, w_a_proj, w_b_proj, w_merge, b_merge, w_out):
    depth = ada_w.shape[0]
    xp, xs = x_prompt, x_sample
    pool_p, ret_p, pool_s, ret_s = [], [], [], []
    for l in range(depth):
        xp, xs, bp, sp, bs, ss = _layer(
            xp, xs, c_prompt, c_sample, state_pool[l], state_ret[l], ada_w[l], ada_b[l], g_pre[l], g_post[l], w_in[l],
            pool_w[l], pool_scale[l], gn_g[l], w_a_proj[l], w_b_proj[l], w_merge[l], b_merge[l],
            w_out[l])
        pool_p.append(bp)
        ret_p.append(sp)
        pool_s.append(bs)
        ret_s.append(ss)
    return (xp, xs, jnp.stack(pool_p), jnp.stack(ret_p), jnp.stack(pool_s), jnp.stack(ret_s))
```

```python
import functools

import jax
import jax.numpy as jnp
import numpy as np
from jax import lax
from jax.experimental import pallas as pl
from jax.experimental.pallas import tpu as pltpu

F32 = jnp.float32
BF16 = jnp.bfloat16

D_MODEL = 2048
PAST_LEN = 16384
D_POOL = D_MODEL // 2
POOL_WINDOWS = (2, 4, 8, 16)
POOL_GROUP_DIM = D_POOL // len(POOL_WINDOWS)
POOL_BUF = max(POOL_WINDOWS) - 1
N_HEADS = 8
HEAD_DK = D_MODEL // 16
HEAD_DV = D_MODEL // 8
D_QK = N_HEADS * HEAD_DK
D_V = N_HEADS * HEAD_DV
CHUNK = 128
ROPE_BASE = 10000.0
EPS = 1e-6
D_IN = 2 * D_POOL + 2 * D_QK + 2 * D_V
D_Z = D_IN + 2 * D_MODEL
K_SCALE = HEAD_DK ** -0.5

VMEM_LIMIT_BYTES = 56 * 1024 * 1024
SUBLANES = 8
HALO = 16
CAST_STEPS = 32


def _params(*semantics):
    return pltpu.CompilerParams(dimension_semantics=semantics, vmem_limit_bytes=VMEM_LIMIT_BYTES)


def _cast_specs(w, step_of):
    rows = w.shape[0] // CAST_STEPS
    assert rows * CAST_STEPS == w.shape[0] and rows % (2 * SUBLANES) == 0
    spec = pl.BlockSpec((rows, w.shape[1]), lambda *idx: (jnp.minimum(step_of(*idx), CAST_STEPS - 1), 0))
    return spec, spec, jax.ShapeDtypeStruct(w.shape, BF16)


def _resident(shape):
    return pl.BlockSpec(shape, lambda *_: (0,) * len(shape), pipeline_mode=pl.Buffered(1))


def _silu(x):
    return x * jax.nn.sigmoid(x)


RING_IN, RING_OUT = 3, 2


def _ring_copy(hbm_block, ring, sem, step, *, to_hbm=False):
    slot = lax.rem(step, ring.shape[0])
    src, dst = (ring.at[slot], hbm_block) if to_hbm else (hbm_block, ring.at[slot])
    return pltpu.make_async_copy(src, dst, sem.at[slot])


def _ring_begin(t, n_steps, in_copy, out_copy=None):
    @pl.when(t == 0)
    def _():
        for k in range(RING_IN - 1):
            in_copy(jnp.int32(k)).start()

    in_copy(t).wait()

    @pl.when(t + RING_IN - 1 < n_steps)
    def _():
        in_copy(t + RING_IN - 1).start()

    if out_copy is not None:
        @pl.when(t >= RING_OUT)
        def _():
            out_copy(t - RING_OUT).wait()


def _ring_end(t, n_steps, out_copy):
    out_copy(t).start()

    @pl.when(t == n_steps - 1)
    def _():
        for k in reversed(range(RING_OUT)):
            out_copy(t - k).wait()


MOD_TN = 512


def _mod_kernel(cs_ref, cp_ref, w_hbm, b_ref, o_ref, w_ring, sem, *, n_steps):
    t = pl.program_id(0)
    tn = w_ring.shape[2]

    def in_copy(step):
        return _ring_copy(w_hbm.at[:, pl.ds(pl.multiple_of(step * tn, tn), tn)], w_ring, sem, step)

    _ring_begin(t, n_steps, in_copy)
    n_pad = o_ref.shape[0] - cs_ref.shape[0] - cp_ref.shape[0]
    c = jnp.concatenate([cs_ref[...], cp_ref[...], jnp.zeros((n_pad, D_MODEL), F32)], axis=0)
    w = w_ring[lax.rem(t, RING_IN)].astype(BF16)
    o_ref[...] = jnp.dot(_silu(c).astype(BF16), w, preferred_element_type=F32) + b_ref[...]


def _modulation(c_sample, c_prompt, ada_w, ada_b):
    ns, n_p = c_sample.shape[0], c_prompt.shape[0]
    assert ns % SUBLANES == 0
    rows = ns + n_p + (-n_p) % SUBLANES
    n_steps = 3 * D_MODEL // MOD_TN
    return pl.pallas_call(
        functools.partial(_mod_kernel, n_steps=n_steps),
        out_shape=jax.ShapeDtypeStruct((rows, 3 * D_MODEL), F32),
        grid=(n_steps,),
        in_specs=[pl.BlockSpec((ns, D_MODEL), lambda j: (0, 0)),
                  pl.BlockSpec((n_p, D_MODEL), lambda j: (0, 0)),
                  pl.BlockSpec(memory_space=pl.ANY),
                  pl.BlockSpec((1, MOD_TN), lambda j: (0, j))],
        out_specs=pl.BlockSpec((rows, MOD_TN), lambda j: (0, j)),
        scratch_shapes=[pltpu.VMEM((RING_IN, D_MODEL, MOD_TN), F32), pltpu.SemaphoreType.DMA((RING_IN,))],
        compiler_params=_params("arbitrary"),
        name="modulation",
    )(c_sample, c_prompt, ada_w, ada_b)


NORM_ROWS = 128
NORM_TM = 512


def _mod_rows(ref, rows, tm, rows_per_mod):
    if rows_per_mod == 1:
        return ref[rows, :]
    return ref[pl.ds((pl.program_id(0) * tm) // rows_per_mod, 1), :]


def _mod_spec(tm, rows_per_mod, mod_row0, col):
    if rows_per_mod == 1:
        assert mod_row0 % tm == 0
        return pl.BlockSpec((tm, D_MODEL), lambda i, *_: (mod_row0 // tm + i, col))
    assert mod_row0 % SUBLANES == 0
    return pl.BlockSpec((SUBLANES, D_MODEL), lambda i, *_: (mod_row0 // SUBLANES, col))


def _norm_mod(x, g, scale, shift):
    xn = x * lax.rsqrt(jnp.mean(x * x, axis=-1, keepdims=True) + EPS) * g
    return (xn * (1.0 + scale) + shift).astype(BF16)


def _norm_kernel(x_hbm, shift_ref, scale_ref, g_ref, h_hbm, x_ring, h_ring, in_sem, out_sem,
                 *, rows_per_mod, n_steps):
    tm = x_ring.shape[1]
    t = pl.program_id(0)

    def rows_of(step):
        return pl.ds(pl.multiple_of(step * tm, tm), tm)

    def in_copy(step):
        return _ring_copy(x_hbm.at[rows_of(step)], x_ring, in_sem, step)

    def out_copy(step):
        return _ring_copy(h_hbm.at[rows_of(step)], h_ring, out_sem, step, to_hbm=True)

    _ring_begin(t, n_steps, in_copy, out_copy)
    x_ref, h_ref = x_ring.at[lax.rem(t, RING_IN)], h_ring.at[lax.rem(t, RING_OUT)]

    def body(r, carry):
        rows = pl.ds(pl.multiple_of(r * NORM_ROWS, NORM_ROWS), NORM_ROWS)
        h_ref[rows, :] = _norm_mod(x_ref[rows, :], g_ref[...], _mod_rows(scale_ref, rows, tm, rows_per_mod),
                                   _mod_rows(shift_ref, rows, tm, rows_per_mod))
        return carry
    lax.fori_loop(0, tm // NORM_ROWS, body, 0)
    _ring_end(t, n_steps, out_copy)


def _norm(x, mod, g_pre, *, tm, rows_per_mod, mod_row0):
    m = x.shape[0]
    n_steps = m // tm
    assert (rows_per_mod == 1 or m // rows_per_mod <= SUBLANES) and n_steps >= 2
    mod_spec = lambda col: _mod_spec(tm, rows_per_mod, mod_row0, col)
    return pl.pallas_call(
        functools.partial(_norm_kernel, rows_per_mod=rows_per_mod, n_steps=n_steps),
        out_shape=jax.ShapeDtypeStruct((m, D_MODEL), BF16),
        grid=(n_steps,),
        in_specs=[pl.BlockSpec(memory_space=pl.ANY), mod_spec(0), mod_spec(1),
                  pl.BlockSpec((1, D_MODEL), lambda i: (0, 0))],
        out_specs=pl.BlockSpec(memory_space=pl.ANY),
        scratch_shapes=[pltpu.VMEM((RING_IN, tm, D_MODEL), F32), pltpu.VMEM((RING_OUT, tm, D_MODEL), BF16),
                        pltpu.SemaphoreType.DMA((RING_IN,)), pltpu.SemaphoreType.DMA((RING_OUT,))],
        compiler_params=_params("arbitrary"),
        name="norm",
    )(x, mod, mod, g_pre)


IN_TM = 1024
IN_TN = 2048
IN_DOT_COLS = 512


def _in_proj_kernel(h_ref, xs_ref, shift_s_ref, scale_s_ref, g_ref, wa_ref, wb_ref, w_in_hbm, w_mg_hbm,
                    z_ref, zs_ref, wa_bf_ref, wb_bf_ref, wbf0_ref, wbf1_ref, stage_ref, hs_ref, sem,
                    *, n_in_tiles, n_j, n_i):
    j, i = pl.program_id(0), pl.program_id(1)
    t = j * n_i + i
    total = n_j * n_i
    tn = wbf0_ref.shape[1]
    chunk = stage_ref.shape[1]

    def chunk_copy(w_hbm, col0, r, b):
        rows = pl.ds(pl.multiple_of(r * chunk, chunk), chunk)
        cols = pl.ds(col0 if isinstance(col0, int) else pl.multiple_of(col0, tn), tn)
        return pltpu.make_async_copy(w_hbm.at[rows, cols], stage_ref.at[b], sem.at[b])

    def start_chunk(g):
        g = lax.rem(jnp.asarray(g, jnp.int32), total)
        jt, r, b = lax.div(g, n_i), lax.rem(g, n_i), lax.rem(g, 2)

        @pl.when(jt < n_in_tiles)
        def _():
            chunk_copy(w_in_hbm, jt * tn, r, b).start()

        @pl.when(jt >= n_in_tiles)
        def _():
            chunk_copy(w_mg_hbm, (jt - n_in_tiles) * tn, r, b).start()

    def land_chunk(g, dst_ref):
        g = jnp.asarray(g, jnp.int32)
        r, b = lax.rem(g, n_i), lax.rem(g, 2)
        chunk_copy(w_in_hbm, 0, r, b).wait()
        dst_ref[pl.ds(pl.multiple_of(r * chunk, chunk), chunk), :] = stage_ref[b].astype(BF16)

    @pl.when(t == 0)
    def _():
        hs_ref[...] = _norm_mod(xs_ref[...], g_ref[...], scale_s_ref[...], shift_s_ref[...])
        start_chunk(0)

        def body(g, carry):
            start_chunk(g + 1)
            land_chunk(g, wbf0_ref)
            return carry
        lax.fori_loop(0, n_i, body, 0)

    @pl.when(t + 1 < total)
    def _():
        start_chunk(t + n_i + 1)

    def multiply(w_cur, w_nxt):
        land_chunk(t + n_i, w_nxt)
        wa_bf_ref[...] = wa_ref[...].astype(BF16)
        wb_bf_ref[...] = wb_ref[...].astype(BF16)
        for c0 in range(0, tn, IN_DOT_COLS):
            cols = slice(c0, c0 + IN_DOT_COLS)
            z_ref[:, cols] = jnp.dot(h_ref[...], w_cur[:, cols], preferred_element_type=F32)

        @pl.when(i == 0)
        def _():
            zs_ref[...] = jnp.dot(hs_ref[...], w_cur[...], preferred_element_type=F32)

    @pl.when(lax.rem(j, 2) == 0)
    def _():
        multiply(wbf0_ref, wbf1_ref)

    @pl.when(lax.rem(j, 2) == 1)
    def _():
        multiply(wbf1_ref, wbf0_ref)


def _in_proj(h, xs, mod, g_pre, w_in, w_mg, w_a, w_b):
    m, ms = h.shape[0], xs.shape[0]
    n_in, n_mg = w_in.shape[1] // IN_TN, w_mg.shape[1] // IN_TN
    n_i = m // IN_TM
    chunk = D_MODEL // n_i
    assert chunk * n_i == D_MODEL and chunk % 16 == 0 and (n_in + n_mg) * n_i >= CAST_STEPS
    const = lambda shape, col: pl.BlockSpec(shape, lambda j, i: (0, col), pipeline_mode=pl.Buffered(1))
    wa_in, wa_out, wa_shape = _cast_specs(w_a, lambda j, i: j * n_i + i)
    wb_in, wb_out, wb_shape = _cast_specs(w_b, lambda j, i: j * n_i + i)
    return pl.pallas_call(
        functools.partial(_in_proj_kernel, n_in_tiles=n_in, n_j=n_in + n_mg, n_i=n_i),
        out_shape=(jax.ShapeDtypeStruct((m, D_Z), F32), jax.ShapeDtypeStruct((ms, D_Z), F32), wa_shape, wb_shape),
        grid=(n_in + n_mg, n_i),
        in_specs=[pl.BlockSpec((IN_TM, D_MODEL), lambda j, i: (i, 0)),
                  const((ms, D_MODEL), 0), const((ms, D_MODEL), 0), const((ms, D_MODEL), 1),
                  const((1, D_MODEL), 0), wa_in, wb_in,
                  pl.BlockSpec(memory_space=pl.ANY), pl.BlockSpec(memory_space=pl.ANY)],
        out_specs=(pl.BlockSpec((IN_TM, IN_TN), lambda j, i: (i, j)),
                   pl.BlockSpec((ms, IN_TN), lambda j, i: (0, j)), wa_out, wb_out),
        scratch_shapes=[pltpu.VMEM((D_MODEL, IN_TN), BF16), pltpu.VMEM((D_MODEL, IN_TN), BF16),
                        pltpu.VMEM((2, chunk, IN_TN), F32),
                        pltpu.VMEM((ms, D_MODEL), BF16),
                        pltpu.SemaphoreType.DMA((2,))],
        compiler_params=_params("arbitrary", "arbitrary"),
        name="in_proj",
    )(h, xs, mod, mod, g_pre, w_a, w_b, w_in, w_mg)


def _rotate(x, cos, sin_signed):
    return x * cos + pltpu.roll(x, HEAD_DK // 2, 1) * sin_signed


def _group_norm_gate(o, gn, bg):
    mu = jnp.mean(o, axis=-1, keepdims=True)
    d = o - mu
    var = jnp.mean(d * d, axis=-1, keepdims=True)
    return d * lax.rsqrt(var + EPS) * gn * _silu(bg)


def _pool_project(pooled, pw, ps, ag):
    mixed = jnp.dot(pooled.astype(BF16), pw.astype(BF16), preferred_element_type=F32)
    return mixed * ps * _silu(ag)


PROJ_SPLIT = 8
PROJ_COLS = D_MODEL // PROJ_SPLIT


def _merge_chunk(y_a, y_b, j, gp_ref, bm_ref, wa_ref, wb_ref):
    lo, hi = j * PROJ_COLS, (j + 1) * PROJ_COLS
    ya = jnp.dot(y_a, wa_ref[:, lo:hi], preferred_element_type=F32)
    yb = jnp.dot(y_b, wb_ref[:, lo:hi], preferred_element_type=F32)
    g_a = jax.nn.sigmoid(gp_ref[:, lo:hi] + bm_ref[:, lo:hi])
    g_b = jax.nn.sigmoid(gp_ref[:, D_MODEL + lo:D_MODEL + hi] + bm_ref[:, D_MODEL + lo:D_MODEL + hi])
    return (g_a * ya + g_b * yb).astype(BF16)


def _out_chunk(merged, j, wo_ref):
    return jnp.dot(merged, wo_ref[:, j * PROJ_COLS:(j + 1) * PROJ_COLS], preferred_element_type=F32)


def _post_norm(o, gpost_ref):
    return o * lax.rsqrt(jnp.mean(o * o, axis=-1, keepdims=True) + EPS) * gpost_ref[...]


def _merge_project(y_a, y_b, gp_ref, bm_ref, wa_ref, wb_ref, wo_ref, gpost_ref):
    merged = jnp.concatenate([_merge_chunk(y_a, y_b, j, gp_ref, bm_ref, wa_ref, wb_ref)
                              for j in range(PROJ_SPLIT)], axis=1)
    o = jnp.concatenate([_out_chunk(merged, j, wo_ref) for j in range(PROJ_SPLIT)], axis=1)
    return _post_norm(o, gpost_ref)


MIX_ROWS = 2 * CHUNK
MIX_HEADS = 4
Z_AX, Z_AG, Z_Q, Z_K, Z_V, Z_BG = 0, D_POOL, 2 * D_POOL, 2 * D_POOL + D_QK, 2 * D_POOL + 2 * D_QK, D_IN - D_V


def _prompt_kernel(cdec_ref, z_ref, rot_ref, dmask_ref, qdec_ref, kdec_ref, pw_ref, ps_ref, gn_ref,
                   gp_ref, bm_ref, wa_ref, wb_ref, wo_ref,
                   m_ref, npool_ref, nret_ref, wo_bf_ref, ya0_ref, yb0_ref, ya1_ref, yb1_ref, ext_ref, *, tiles_per_seq):
    s = pl.program_id(0)
    n_tiles = pl.num_programs(0) - 1
    live = s < n_tiles
    c = lax.rem(jnp.minimum(s, n_tiles - 1), tiles_per_seq)
    rows = z_ref.shape[0]

    @pl.when(s == 0)
    def _():
        for ref in (ya0_ref, yb0_ref, ya1_ref, yb1_ref):
            ref[...] = jnp.zeros(ref.shape, BF16)

    @pl.when(c == 0)
    def _():
        ext_ref[0:HALO, :] = jnp.zeros((HALO, D_POOL), F32)
        nret_ref[...] = jnp.zeros(nret_ref.shape, F32)

    def step(ya_rd, yb_rd, ya_wr, yb_wr):
        y_a, y_b = ya_rd[...], yb_rd[...]
        wo_bf_ref[...] = wo_ref[...].astype(BF16)
        xa = z_ref[:, Z_AX:Z_AX + D_POOL]
        ext_ref[HALO:HALO + rows, :] = xa
        pos = c * rows + lax.broadcasted_iota(jnp.int32, (rows, 1), 0)
        nt = (((1,), (1,)), ((), ()))
        tn = (((0,), (0,)), ((), ()))

        def merge_piece(j):
            m_ref[:, j * PROJ_COLS:(j + 1) * PROJ_COLS] = _merge_chunk(y_a, y_b, j, gp_ref, bm_ref, wa_ref, wb_ref)

        def pool_group(g):
            w = POOL_WINDOWS[g]
            lo, hi = g * POOL_GROUP_DIM, (g + 1) * POOL_GROUP_DIM
            acc = ext_ref[:, lo:hi]
            span = 1
            while span < w:
                acc = acc + pltpu.roll(acc, span, 0)
                span *= 2
            inv_cnt = 1.0 / jnp.minimum(pos + 1, w).astype(F32)
            pooled = acc[HALO:, :] * inv_cnt - xa[:, lo:hi]
            ya = _pool_project(pooled, pw_ref[g], ps_ref[:, lo:hi], z_ref[:, Z_AG + lo:Z_AG + hi])
            ya_wr[:, lo:hi] = ya.astype(BF16)

        per_pool = PROJ_SPLIT // len(POOL_WINDOWS)
        assert per_pool * len(POOL_WINDOWS) == PROJ_SPLIT
        fillers = [f for g in range(len(POOL_WINDOWS))
                   for f in [functools.partial(merge_piece, g * per_pool + j) for j in range(per_pool)]
                   + [functools.partial(pool_group, g)]]
        n_sub = rows // CHUNK
        n_slots = 2 * (N_HEADS // MIX_HEADS) * n_sub
        slot = [0]

        def fill():
            lo, hi = (slot[0] * len(fillers)) // n_slots, ((slot[0] + 1) * len(fillers)) // n_slots
            slot[0] += 1
            for f in fillers[lo:hi]:
                f()

        for h0 in range(0, N_HEADS, MIX_HEADS):
            heads = range(h0, h0 + MIX_HEADS)
            s_cur = {h: nret_ref[h] for h in heads}
            for ci in range(n_sub):
                rs = slice(ci * CHUNK, (ci + 1) * CHUNK)
                cos = rot_ref[rs, :HEAD_DK]
                sin = rot_ref[rs, HEAD_DK:]
                q = {h: _rotate(z_ref[rs, Z_Q + h * HEAD_DK:Z_Q + (h + 1) * HEAD_DK], cos, sin) for h in heads}
                k = {h: _rotate(z_ref[rs, Z_K + h * HEAD_DK:Z_K + (h + 1) * HEAD_DK], cos, sin) * K_SCALE
                     for h in heads}
                v = {h: z_ref[rs, Z_V + h * HEAD_DV:Z_V + (h + 1) * HEAD_DV].astype(BF16) for h in heads}
                fill()
                scores = {h: lax.dot_general(q[h].astype(BF16), k[h].astype(BF16), nt, preferred_element_type=F32)
                          for h in heads}
                kv = {h: lax.dot_general((k[h] * kdec_ref[h]).astype(BF16), v[h], tn, preferred_element_type=F32)
                      for h in heads}
                lhs = {h: jnp.concatenate([(scores[h] * dmask_ref[h]).astype(BF16),
                                           (q[h] * qdec_ref[h]).astype(BF16)], axis=1) for h in heads}
                fill()
                for h in heads:
                    rhs = jnp.concatenate([v[h], s_cur[h].astype(BF16)], axis=0)
                    o = jnp.dot(lhs[h], rhs, preferred_element_type=F32)
                    s_cur[h] = cdec_ref[h] * s_cur[h] + kv[h]
                    vs = slice(h * HEAD_DV, (h + 1) * HEAD_DV)
                    bg = z_ref[rs, Z_BG + h * HEAD_DV:Z_BG + (h + 1) * HEAD_DV]
                    yb_wr[rs, vs] = _group_norm_gate(o, gn_ref[h:h + 1, :], bg).astype(BF16)
            for h in heads:
                nret_ref[h] = jnp.where(live, s_cur[h], nret_ref[h])
        assert slot[0] == n_slots

    @pl.when(lax.rem(s, 2) == 0)
    def _():
        step(ya1_ref, yb1_ref, ya0_ref, yb0_ref)

    @pl.when(lax.rem(s, 2) == 1)
    def _():
        step(ya0_ref, yb0_ref, ya1_ref, yb1_ref)

    seq_id = lax.div(jnp.minimum(s, n_tiles - 1), tiles_per_seq)
    for b in range(npool_ref.shape[1]):
        @pl.when((c == tiles_per_seq - 1) & (seq_id == b))
        def _():
            npool_ref[:, b, :] = ext_ref[HALO + rows - POOL_BUF:HALO + rows, :]

    ext_ref[0:HALO, :] = ext_ref[rows:rows + HALO, :]


def _prompt_mix_merge(z, batch, seq, tables, pool_w, pool_scale, gn_g, b_merge, w_a, w_b, w_o):
    rot, dmask, qdec, kdec, cdec = tables
    tps = seq // MIX_ROWS
    n_tiles = batch * tps
    assert n_tiles >= CAST_STEPS
    cur = lambda s: jnp.minimum(s, n_tiles - 1)
    prev = lambda s: jnp.maximum(s - 1, 0)
    m = batch * seq
    wo_in, wo_out, wo_shape = _cast_specs(w_o, lambda s: s)
    return pl.pallas_call(
        functools.partial(_prompt_kernel, tiles_per_seq=tps),
        out_shape=(jax.ShapeDtypeStruct((m, D_MODEL), BF16),
                   jax.ShapeDtypeStruct((POOL_BUF, batch, D_POOL), F32),
                   jax.ShapeDtypeStruct((batch, N_HEADS, HEAD_DK, HEAD_DV), F32), wo_shape),
        grid=(n_tiles + 1,),
        in_specs=[pl.BlockSpec(memory_space=pltpu.SMEM),
                  pl.BlockSpec((MIX_ROWS, D_IN), lambda s: (cur(s), 0)),
                  pl.BlockSpec((MIX_ROWS, 2 * HEAD_DK), lambda s: (cur(s) % tps, 0)),
                  _resident(dmask.shape), _resident(qdec.shape), _resident(kdec.shape),
                  _resident(pool_w.shape), _resident(pool_scale.shape), _resident(gn_g.shape),
                  pl.BlockSpec((MIX_ROWS, 2 * D_MODEL), lambda s: (prev(s), D_IN // (2 * D_MODEL))),
                  _resident(b_merge.shape), _resident(w_a.shape), _resident(w_b.shape), wo_in],
        out_specs=(pl.BlockSpec((MIX_ROWS, D_MODEL), lambda s: (prev(s), 0)),
                   pl.BlockSpec((POOL_BUF, batch, D_POOL), lambda s: (0, 0, 0)),
                   pl.BlockSpec((None, N_HEADS, HEAD_DK, HEAD_DV), lambda s: (cur(s) // tps, 0, 0, 0)), wo_out),
        scratch_shapes=[pltpu.VMEM((MIX_ROWS, D_POOL), BF16), pltpu.VMEM((MIX_ROWS, D_V), BF16),
                        pltpu.VMEM((MIX_ROWS, D_POOL), BF16), pltpu.VMEM((MIX_ROWS, D_V), BF16),
                        pltpu.VMEM((HALO + MIX_ROWS, D_POOL), F32)],
        compiler_params=_params("arbitrary"),
        name="prompt_mix_merge",
    )(cdec, z, rot, dmask, qdec, kdec, pool_w, pool_scale, gn_g, z, b_merge, w_a, w_b, w_o)


SAMPLE_TILE = 8
SAMPLE_HEADS = 4
TAIL_ROWS = 256


def _tail_kernel(dec_ref, m_ref, x_ref, gate_ref, wo_ref, gpost_ref,
                 zs_ref, rot_ref, pw_ref, ps_ref, gn_ref, spool_ref, sret_hbm,
                 y_ref, ya_ref, yb_ref, npool_ref, nret_hbm, o_ref, in_ring, out_ring, in_sem, out_sem,
                 *, inv_cnt, rows_per_mod, n_steps):
    bt = zs_ref.shape[0]
    n_hg = N_HEADS // SAMPLE_HEADS
    assert n_hg % RING_OUT == 0
    t = pl.program_id(0) * n_hg + pl.program_id(1)

    def state_block(hbm, step):
        seq0 = pl.multiple_of(lax.div(step, n_hg) * bt, bt)
        head0 = pl.multiple_of(lax.rem(step, n_hg) * SAMPLE_HEADS, SAMPLE_HEADS)
        return hbm.at[pl.ds(seq0, bt), pl.ds(head0, SAMPLE_HEADS)]

    def in_copy(step):
        return _ring_copy(state_block(sret_hbm, step), in_ring, in_sem, step)

    def out_copy(step):
        return _ring_copy(state_block(nret_hbm, step), out_ring, out_sem, step, to_hbm=True)

    _ring_begin(t, n_steps, in_copy, out_copy)
    s_in = in_ring.at[lax.rem(t, RING_IN)]

    def step(hg):
        s_out = out_ring.at[hg % RING_OUT]
        o = _post_norm(jnp.dot(m_ref[...], wo_ref[...], preferred_element_type=F32), gpost_ref)
        tile = pl.program_id(0) * n_hg + hg
        gate = gate_ref[pl.ds((tile * x_ref.shape[0]) // rows_per_mod, 1), :]
        y_ref[...] = x_ref[...] + gate * o

        if hg == 0:
            xa = zs_ref[:, Z_AX:Z_AX + D_POOL]
            run = xa
            wins = {1: xa}
            for j in range(1, POOL_BUF + 1):
                run = run + spool_ref[POOL_BUF - j]
                wins[j + 1] = run
            for g, w in enumerate(POOL_WINDOWS):
                lo, hi = g * POOL_GROUP_DIM, (g + 1) * POOL_GROUP_DIM
                pooled = wins[w][:, lo:hi] * inv_cnt[g] - xa[:, lo:hi]
                ya = _pool_project(pooled, pw_ref[g], ps_ref[:, lo:hi], zs_ref[:, Z_AG + lo:Z_AG + hi])
                ya_ref[:, lo:hi] = ya.astype(BF16)
            for j in range(POOL_BUF - 1):
                npool_ref[j] = spool_ref[j + 1]
            npool_ref[POOL_BUF - 1] = xa

        cos = rot_ref[:, :HEAD_DK]
        sin = rot_ref[:, HEAD_DK:]
        for hl in range(SAMPLE_HEADS):
            h = hg * SAMPLE_HEADS + hl
            vs = slice(h * HEAD_DV, (h + 1) * HEAD_DV)
            q = _rotate(zs_ref[:, Z_Q + h * HEAD_DK:Z_Q + (h + 1) * HEAD_DK], cos, sin)
            k = _rotate(zs_ref[:, Z_K + h * HEAD_DK:Z_K + (h + 1) * HEAD_DK], cos, sin) * K_SCALE
            v = zs_ref[:, Z_V + h * HEAD_DV:Z_V + (h + 1) * HEAD_DV]
            score = jnp.sum(q * k, axis=1, keepdims=True) * dec_ref[0, h]
            q_cols = jnp.transpose(q * dec_ref[1, h])
            k_cols = jnp.transpose(k * dec_ref[2, h])
            for r in range(bt):
                s_old = s_in[r, hl]
                v_row = v[r:r + 1, :]
                o_row = score[r:r + 1, :] * v_row + jnp.sum(q_cols[:, r:r + 1] * s_old, axis=0, keepdims=True)
                s_out[r, hl] = dec_ref[3, h] * s_old + k_cols[:, r:r + 1] * v_row
                o_ref[r:r + 1, vs] = o_row
        for hl in range(SAMPLE_HEADS):
            h = hg * SAMPLE_HEADS + hl
            vs = slice(h * HEAD_DV, (h + 1) * HEAD_DV)
            bg = zs_ref[:, Z_BG + h * HEAD_DV:Z_BG + (h + 1) * HEAD_DV]
            yb_ref[:, vs] = _group_norm_gate(o_ref[:, vs], gn_ref[h:h + 1, :], bg).astype(BF16)

    for hg in range(n_hg):
        pl.when(pl.program_id(1) == hg)(functools.partial(step, hg))

    _ring_end(t, n_steps, out_copy)


def _tail(merged, x, mod, w_o, g_post, zs, state_pool, state_ret, tables, inv_cnt, pool_w, pool_scale, gn_g,
          *, rows_per_mod, mod_row0):
    rot, dec = tables
    m, batch = x.shape[0], zs.shape[0]
    bt, hs = SAMPLE_TILE, SAMPLE_HEADS
    n_hg = N_HEADS // hs
    assert m == (batch // bt) * n_hg * TAIL_ROWS and mod_row0 % SUBLANES == 0 and m // rows_per_mod <= SUBLANES
    rows = lambda i, g: (i * n_hg + g, 0)
    per_tile = lambda width: pl.BlockSpec((bt, width), lambda i, g: (i, 0))
    return pl.pallas_call(
        functools.partial(_tail_kernel, inv_cnt=inv_cnt, rows_per_mod=rows_per_mod, n_steps=(batch // bt) * n_hg),
        out_shape=(jax.ShapeDtypeStruct((m, D_MODEL), F32),
                   jax.ShapeDtypeStruct((batch, D_POOL), BF16),
                   jax.ShapeDtypeStruct((batch, D_V), BF16),
                   jax.ShapeDtypeStruct(state_pool.shape, state_pool.dtype),
                   jax.ShapeDtypeStruct(state_ret.shape, state_ret.dtype)),
        grid=(batch // bt, n_hg),
        in_specs=[pl.BlockSpec(memory_space=pltpu.SMEM),
                  pl.BlockSpec((TAIL_ROWS, D_MODEL), rows), pl.BlockSpec((TAIL_ROWS, D_MODEL), rows),
                  pl.BlockSpec((SUBLANES, D_MODEL), lambda i, g: (mod_row0 // SUBLANES, 2)),
                  _resident(w_o.shape), _resident(g_post.shape),
                  per_tile(D_IN), _resident(rot.shape),
                  _resident(pool_w.shape), _resident(pool_scale.shape), _resident(gn_g.shape),
                  pl.BlockSpec((POOL_BUF, bt, D_POOL), lambda i, g: (0, i, 0)),
                  pl.BlockSpec(memory_space=pl.ANY)],
        out_specs=(pl.BlockSpec((TAIL_ROWS, D_MODEL), rows),
                   per_tile(D_POOL), per_tile(D_V),
                   pl.BlockSpec((POOL_BUF, bt, D_POOL), lambda i, g: (0, i, 0)),
                   pl.BlockSpec(memory_space=pl.ANY)),
        scratch_shapes=[pltpu.VMEM((bt, D_V), F32),
                        pltpu.VMEM((RING_IN, bt, hs, HEAD_DK, HEAD_DV), F32),
                        pltpu.VMEM((RING_OUT, bt, hs, HEAD_DK, HEAD_DV), F32),
                        pltpu.SemaphoreType.DMA((RING_IN,)), pltpu.SemaphoreType.DMA((RING_OUT,))],
        compiler_params=_params("arbitrary", "arbitrary"),
        name="tail",
    )(dec, merged, x, mod, w_o, g_post, zs, rot, pool_w, pool_scale, gn_g, state_pool, state_ret)


def _out_proj_kernel(ya_ref, yb_ref, gp_ref, x_ref, gate_ref, bm_ref, wa_ref, wb_ref, wo_ref, gpost_ref, y_ref,
                     *, rows_per_mod):
    o = _merge_project(ya_ref[...], yb_ref[...], gp_ref, bm_ref, wa_ref, wb_ref, wo_ref, gpost_ref)
    gate = _mod_rows(gate_ref, slice(None), x_ref.shape[0], rows_per_mod)
    y_ref[...] = x_ref[...] + gate * o


def _out_proj(ya, yb, z, x, mod, b_merge, w_a, w_b, w_o, g_post, *, tm, rows_per_mod, mod_row0):
    m = x.shape[0]
    return pl.pallas_call(
        functools.partial(_out_proj_kernel, rows_per_mod=rows_per_mod),
        out_shape=jax.ShapeDtypeStruct((m, D_MODEL), F32),
        grid=(m // tm,),
        in_specs=[pl.BlockSpec((tm, D_POOL), lambda i: (i, 0)),
                  pl.BlockSpec((tm, D_V), lambda i: (i, 0)),
                  pl.BlockSpec((tm, 2 * D_MODEL), lambda i: (i, D_IN // (2 * D_MODEL))),
                  pl.BlockSpec((tm, D_MODEL), lambda i: (i, 0)),
                  _mod_spec(tm, rows_per_mod, mod_row0, 2),
                  _resident(b_merge.shape), _resident(w_a.shape), _resident(w_b.shape), _resident(w_o.shape),
                  _resident(g_post.shape)],
        out_specs=pl.BlockSpec((tm, D_MODEL), lambda i: (i, 0)),
        compiler_params=_params("parallel"),
        name="out_proj",
    )(ya, yb, z, x, mod, b_merge, w_a, w_b, w_o, g_post)


def _rotary_tables(start, length):
    half = HEAD_DK // 2
    inv = ROPE_BASE ** (-np.arange(half, dtype=np.float64) / half)
    ang = (start + np.arange(length, dtype=np.float64))[:, None] * inv[None, :]
    cos, sin = np.cos(ang), np.sin(ang)
    return (np.concatenate([cos, cos], axis=-1).astype(np.float32),
            np.concatenate([-sin, sin], axis=-1).astype(np.float32))


def _decay_tables(c):
    lg = np.log1p(-np.power(2.0, -5.0 - np.arange(N_HEADS, dtype=np.float64)))
    idx = np.arange(c, dtype=np.float64)
    diff = idx[:, None] - idx[None, :]
    dmask = np.where(diff[None] >= 0, np.exp(np.maximum(diff, 0.0)[None] * lg[:, None, None]), 0.0)
    q_dec = np.exp((idx + 1.0)[None, :] * lg[:, None])
    k_dec = np.exp((c - 1.0 - idx)[None, :] * lg[:, None])
    chunk_dec = np.exp(c * lg)
    return tuple(a.astype(np.float32) for a in (dmask, q_dec, k_dec, chunk_dec))


def _layer(xp, xs, c_prompt, c_sample, state_pool, state_ret, ada_w, ada_b, g_pre, g_post, w_in, pool_w, pool_scale, gn_g,
           w_a_proj, w_b_proj, w_merge, b_merge, w_out):
    batch, seq, _ = xp.shape
    dec_batch, dec_seq, _ = xs.shape
    assert dec_seq == 1 and seq % CHUNK == 0

    row = lambda v: v.reshape(1, -1)
    g_pre, g_post, pool_scale, b_merge = map(row, (g_pre, g_post, pool_scale, b_merge))
    gn_g = gn_g.reshape(N_HEADS, HEAD_DV)

    mod = _modulation(c_sample, c_prompt, ada_w, row(ada_b))

    x2 = xp.reshape(batch * seq, D_MODEL)
    xs2 = xs.reshape(dec_batch, D_MODEL)
    h = _norm(x2, mod, g_pre, tm=NORM_TM, rows_per_mod=seq, mod_row0=dec_batch)
    z, zs, w_a, w_b = _in_proj(h, xs2, mod, g_pre, w_in, w_merge, w_a_proj, w_b_proj)

    cos, sin = _rotary_tables(0, seq)
    dmask, q_dec, k_dec, chunk_dec = _decay_tables(CHUNK)
    wide = lambda d: np.ascontiguousarray(np.broadcast_to(d[:, :, None], (N_HEADS, CHUNK, HEAD_DK)))
    rot = np.concatenate([cos, sin], axis=1)
    merged, pool_p, ret_p, w_o = _prompt_mix_merge(z, batch, seq, (rot, dmask, wide(q_dec), wide(k_dec), chunk_dec),
                                                   pool_w, pool_scale, gn_g, b_merge, w_a, w_b, w_out)
    pool_p = jnp.transpose(pool_p, (1, 0, 2))

    cos_s, sin_s = _rotary_tables(PAST_LEN, 1)
    dmask_s, q_dec_s, k_dec_s, chunk_dec_s = _decay_tables(1)
    dec_s = np.stack([dmask_s[:, 0, 0], q_dec_s[:, 0], k_dec_s[:, 0], chunk_dec_s])
    inv_cnt = tuple(1.0 / min(PAST_LEN + 1, w) for w in POOL_WINDOWS)
    yp, ya_s, yb_s, pool_s, ret_s = _tail(merged, x2, mod, w_o, g_post, zs, jnp.transpose(state_pool, (1, 0, 2)),
                                          state_ret, (np.concatenate([cos_s, sin_s], axis=1), dec_s), inv_cnt,
                                          pool_w, pool_scale, gn_g,
                                          rows_per_mod=seq, mod_row0=dec_batch)
    pool_s = jnp.transpose(pool_s, (1, 0, 2))
    ys = _out_proj(ya_s, yb_s, zs, xs2, mod, b_merge, w_a, w_b, w_o, g_post, tm=dec_batch, rows_per_mod=1, mod_row0=0)

    return yp.reshape(xp.shape), ys.reshape(xs.shape), pool_p, ret_p, pool_s, ret_s


def kernel(x_prompt, x_sample, state_pool, state_ret, c_prompt, c_sample, ada_w, ada_b, g_pre, g_post,
           w_in, pool_w, pool_scale, gn_g, w_a_proj, w_b_proj, w_merge, b_merge, w_out):
    depth = ada_w.shape[0]
    xp, xs = x_prompt, x_sample
    pool_p, ret_p, pool_s, ret_s = [], [], [], []
    for l in range(depth):
        xp, xs, bp, sp, bs, ss = _layer(
            xp, xs, c_prompt, c_sample, state_pool[l], state_ret[l], ada_w[l], ada_b[l], g_pre[l], g_post[l], w_in[l],
            pool_w[l], pool_scale[l], gn_g[l], w_a_proj[l], w_b_proj[l], w_merge[l], b_merge[l],
            w_out[l])
        pool_p.append(bp)
        ret_p.append(sp)
        pool_s.append(bs)
        ret_s.append(ss)
    return (xp, xs, jnp.stack(pool_p), jnp.stack(ret_p), jnp.stack(pool_s), jnp.stack(ret_s))
```

```python
import functools

import jax
import jax.numpy as jnp
import numpy as np
from jax import lax
from jax.experimental import pallas as pl
from jax.experimental.pallas import tpu as pltpu

F32 = jnp.float32
BF16 = jnp.bfloat16

D_MODEL = 2048
PAST_LEN = 16384
D_POOL = D_MODEL // 2
POOL_WINDOWS = (2, 4, 8, 16)
POOL_GROUP_DIM = D_POOL // len(POOL_WINDOWS)
POOL_BUF = max(POOL_WINDOWS) - 1
N_HEADS = 8
HEAD_DK = D_MODEL // 16
HEAD_DV = D_MODEL // 8
D_QK = N_HEADS * HEAD_DK
D_V = N_HEADS * HEAD_DV
CHUNK = 128
ROPE_BASE = 10000.0
EPS = 1e-6
D_IN = 2 * D_POOL + 2 * D_QK + 2 * D_V
D_Z = D_IN + 2 * D_MODEL
K_SCALE = HEAD_DK ** -0.5

VMEM_LIMIT_BYTES = 56 * 1024 * 1024
SUBLANES = 8
HALO = 16
CAST_STEPS = 32


def _params(*semantics):
    return pltpu.CompilerParams(dimension_semantics=semantics, vmem_limit_bytes=VMEM_LIMIT_BYTES)


def _cast_specs(w, step_of):
    rows = w.shape[0] // CAST_STEPS
    assert rows * CAST_STEPS == w.shape[0] and rows % (2 * SUBLANES) == 0
    spec = pl.BlockSpec((rows, w.shape[1]), lambda *idx: (jnp.minimum(step_of(*idx), CAST_STEPS - 1), 0))
    return spec, spec, jax.ShapeDtypeStruct(w.shape, BF16)


def _resident(shape):
    return pl.BlockSpec(shape, lambda *_: (0,) * len(shape), pipeline_mode=pl.Buffered(1))


def _silu(x):
    return x * jax.nn.sigmoid(x)


RING_IN, RING_OUT = 3, 2


def _ring_copy(hbm_block, ring, sem, step, *, to_hbm=False):
    slot = lax.rem(step, ring.shape[0])
    src, dst = (ring.at[slot], hbm_block) if to_hbm else (hbm_block, ring.at[slot])
    return pltpu.make_async_copy(src, dst, sem.at[slot])


def _ring_begin(t, n_steps, in_copy, out_copy=None):
    @pl.when(t == 0)
    def _():
        for k in range(RING_IN - 1):
            in_copy(jnp.int32(k)).start()

    in_copy(t).wait()

    @pl.when(t + RING_IN - 1 < n_steps)
    def _():
        in_copy(t + RING_IN - 1).start()

    if out_copy is not None:
        @pl.when(t >= RING_OUT)
        def _():
            out_copy(t - RING_OUT).wait()


def _ring_end(t, n_steps, out_copy):
    out_copy(t).start()

    @pl.when(t == n_steps - 1)
    def _():
        for k in reversed(range(RING_OUT)):
            out_copy(t - k).wait()


MOD_TN = 512


def _mod_kernel(cs_ref, cp_ref, w_hbm, b_ref, o_ref, w_ring, sem, *, n_steps):
    t = pl.program_id(0)
    tn = w_ring.shape[2]

    def in_copy(step):
        return _ring_copy(w_hbm.at[:, pl.ds(pl.multiple_of(step * tn, tn), tn)], w_ring, sem, step)

    _ring_begin(t, n_steps, in_copy)
    n_pad = o_ref.shape[0] - cs_ref.shape[0] - cp_ref.shape[0]
    c = jnp.concatenate([cs_ref[...], cp_ref[...], jnp.zeros((n_pad, D_MODEL), F32)], axis=0)
    w = w_ring[lax.rem(t, RING_IN)].astype(BF16)
    o_ref[...] = jnp.dot(_silu(c).astype(BF16), w, preferred_element_type=F32) + b_ref[...]


def _modulation(c_sample, c_prompt, ada_w, ada_b):
    ns, n_p = c_sample.shape[0], c_prompt.shape[0]
    assert ns % SUBLANES == 0
    rows = ns + n_p + (-n_p) % SUBLANES
    n_steps = 3 * D_MODEL // MOD_TN
    return pl.pallas_call(
        functools.partial(_mod_kernel, n_steps=n_steps),
        out_shape=jax.ShapeDtypeStruct((rows, 3 * D_MODEL), F32),
        grid=(n_steps,),
        in_specs=[pl.BlockSpec((ns, D_MODEL), lambda j: (0, 0)),
                  pl.BlockSpec((n_p, D_MODEL), lambda j: (0, 0)),
                  pl.BlockSpec(memory_space=pl.ANY),
                  pl.BlockSpec((1, MOD_TN), lambda j: (0, j))],
        out_specs=pl.BlockSpec((rows, MOD_TN), lambda j: (0, j)),
        scratch_shapes=[pltpu.VMEM((RING_IN, D_MODEL, MOD_TN), F32), pltpu.SemaphoreType.DMA((RING_IN,))],
        compiler_params=_params("arbitrary"),
        name="modulation",
    )(c_sample, c_prompt, ada_w, ada_b)


NORM_ROWS = 128
NORM_TM = 512


def _mod_rows(ref, rows, tm, rows_per_mod):
    if rows_per_mod == 1:
        return ref[rows, :]
    return ref[pl.ds((pl.program_id(0) * tm) // rows_per_mod, 1), :]


def _mod_spec(tm, rows_per_mod, mod_row0, col):
    if rows_per_mod == 1:
        assert mod_row0 % tm == 0
        return pl.BlockSpec((tm, D_MODEL), lambda i, *_: (mod_row0 // tm + i, col))
    assert mod_row0 % SUBLANES == 0
    return pl.BlockSpec((SUBLANES, D_MODEL), lambda i, *_: (mod_row0 // SUBLANES, col))


def _norm_mod(x, g, scale, shift):
    xn = x * lax.rsqrt(jnp.mean(x * x, axis=-1, keepdims=True) + EPS) * g
    return (xn * (1.0 + scale) + shift).astype(BF16)


def _norm_kernel(x_hbm, shift_ref, scale_ref, g_ref, h_hbm, x_ring, h_ring, in_sem, out_sem,
                 *, rows_per_mod, n_steps):
    tm = x_ring.shape[1]
    t = pl.program_id(0)

    def rows_of(step):
        return pl.ds(pl.multiple_of(step * tm, tm), tm)

    def in_copy(step):
        return _ring_copy(x_hbm.at[rows_of(step)], x_ring, in_sem, step)

    def out_copy(step):
        return _ring_copy(h_hbm.at[rows_of(step)], h_ring, out_sem, step, to_hbm=True)

    _ring_begin(t, n_steps, in_copy, out_copy)
    x_ref, h_ref = x_ring.at[lax.rem(t, RING_IN)], h_ring.at[lax.rem(t, RING_OUT)]

    def body(r, carry):
        rows = pl.ds(pl.multiple_of(r * NORM_ROWS, NORM_ROWS), NORM_ROWS)
        h_ref[rows, :] = _norm_mod(x_ref[rows, :], g_ref[...], _mod_rows(scale_ref, rows, tm, rows_per_mod),
                                   _mod_rows(shift_ref, rows, tm, rows_per_mod))
        return carry
    lax.fori_loop(0, tm // NORM_ROWS, body, 0)
    _ring_end(t, n_steps, out_copy)


def _norm(x, mod, g_pre, *, tm, rows_per_mod, mod_row0):
    m = x.shape[0]
    n_steps = m // tm
    assert (rows_per_mod == 1 or m // rows_per_mod <= SUBLANES) and n_steps >= 2
    mod_spec = lambda col: _mod_spec(tm, rows_per_mod, mod_row0, col)
    return pl.pallas_call(
        functools.partial(_norm_kernel, rows_per_mod=rows_per_mod, n_steps=n_steps),
        out_shape=jax.ShapeDtypeStruct((m, D_MODEL), BF16),
        grid=(n_steps,),
        in_specs=[pl.BlockSpec(memory_space=pl.ANY), mod_spec(0), mod_spec(1),
                  pl.BlockSpec((1, D_MODEL), lambda i: (0, 0))],
        out_specs=pl.BlockSpec(memory_space=pl.ANY),
        scratch_shapes=[pltpu.VMEM((RING_IN, tm, D_MODEL), F32), pltpu.VMEM((RING_OUT, tm, D_MODEL), BF16),
                        pltpu.SemaphoreType.DMA((RING_IN,)), pltpu.SemaphoreType.DMA((RING_OUT,))],
        compiler_params=_params("arbitrary"),
        name="norm",
    )(x, mod, mod, g_pre)


IN_TM = 1024
IN_TN = 2048
IN_DOT_COLS = 512


def _in_proj_kernel(h_ref, xs_ref, shift_s_ref, scale_s_ref, g_ref, wa_ref, wb_ref, w_in_hbm, w_mg_hbm,
                    z_ref, zs_ref, wa_bf_ref, wb_bf_ref, wbf0_ref, wbf1_ref, stage_ref, hs_ref, sem,
                    *, n_in_tiles, n_j, n_i):
    j, i = pl.program_id(0), pl.program_id(1)
    t = j * n_i + i
    total = n_j * n_i
    tn = wbf0_ref.shape[1]
    chunk = stage_ref.shape[1]

    def chunk_copy(w_hbm, col0, r, b):
        rows = pl.ds(pl.multiple_of(r * chunk, chunk), chunk)
        cols = pl.ds(col0 if isinstance(col0, int) else pl.multiple_of(col0, tn), tn)
        return pltpu.make_async_copy(w_hbm.at[rows, cols], stage_ref.at[b], sem.at[b])

    def start_chunk(g):
        g = lax.rem(jnp.asarray(g, jnp.int32), total)
        jt, r, b = lax.div(g, n_i), lax.rem(g, n_i), lax.rem(g, 2)

        @pl.when(jt < n_in_tiles)
        def _():
            chunk_copy(w_in_hbm, jt * tn, r, b).start()

        @pl.when(jt >= n_in_tiles)
        def _():
            chunk_copy(w_mg_hbm, (jt - n_in_tiles) * tn, r, b).start()

    def land_chunk(g, dst_ref):
        g = jnp.asarray(g, jnp.int32)
        r, b = lax.rem(g, n_i), lax.rem(g, 2)
        chunk_copy(w_in_hbm, 0, r, b).wait()
        dst_ref[pl.ds(pl.multiple_of(r * chunk, chunk), chunk), :] = stage_ref[b].astype(BF16)

    @pl.when(t == 0)
    def _():
        hs_ref[...] = _norm_mod(xs_ref[...], g_ref[...], scale_s_ref[...], shift_s_ref[...])
        start_chunk(0)

        def body(g, carry):
            start_chunk(g + 1)
            land_chunk(g, wbf0_ref)
            return carry
        lax.fori_loop(0, n_i, body, 0)

    @pl.when(t + 1 < total)
    def _():
        start_chunk(t + n_i + 1)

    def multiply(w_cur, w_nxt):
        land_chunk(t + n_i, w_nxt)
        wa_bf_ref[...] = wa_ref[...].astype(BF16)
        wb_bf_ref[...] = wb_ref[...].astype(BF16)
        for c0 in range(0, tn, IN_DOT_COLS):
            cols = slice(c0, c0 + IN_DOT_COLS)
            z_ref[:, cols] = jnp.dot(h_ref[...], w_cur[:, cols], preferred_element_type=F32)

        @pl.when(i == 0)
        def _():
            zs_ref[...] = jnp.dot(hs_ref[...], w_cur[...], preferred_element_type=F32)

    @pl.when(lax.rem(j, 2) == 0)
    def _():
        multiply(wbf0_ref, wbf1_ref)

    @pl.when(lax.rem(j, 2) == 1)
    def _():
        multiply(wbf1_ref, wbf0_ref)


def _in_proj(h, xs, mod, g_pre, w_in, w_mg, w_a, w_b):
    m, ms = h.shape[0], xs.shape[0]
    n_in, n_mg = w_in.shape[1] // IN_TN, w_mg.shape[1] // IN_TN
    n_i = m // IN_TM
    chunk = D_MODEL // n_i
    assert chunk * n_i == D_MODEL and chunk % 16 == 0 and (n_in + n_mg) * n_i >= CAST_STEPS
    const = lambda shape, col: pl.BlockSpec(shape, lambda j, i: (0, col), pipeline_mode=pl.Buffered(1))
    wa_in, wa_out, wa_shape = _cast_specs(w_a, lambda j, i: j * n_i + i)
    wb_in, wb_out, wb_shape = _cast_specs(w_b, lambda j, i: j * n_i + i)
    return pl.pallas_call(
        functools.partial(_in_proj_kernel, n_in_tiles=n_in, n_j=n_in + n_mg, n_i=n_i),
        out_shape=(jax.ShapeDtypeStruct((m, D_Z), F32), jax.ShapeDtypeStruct((ms, D_Z), F32), wa_shape, wb_shape),
        grid=(n_in + n_mg, n_i),
        in_specs=[pl.BlockSpec((IN_TM, D_MODEL), lambda j, i: (i, 0)),
                  const((ms, D_MODEL), 0), const((ms, D_MODEL), 0), const((ms, D_MODEL), 1),
                  const((1, D_MODEL), 0), wa_in, wb_in,
                  pl.BlockSpec(memory_space=pl.ANY), pl.BlockSpec(memory_space=pl.ANY)],
        out_specs=(pl.BlockSpec((IN_TM, IN_TN), lambda j, i: (i, j)),
                   pl.BlockSpec((ms, IN_TN), lambda j, i: (0, j)), wa_out, wb_out),
        scratch_shapes=[pltpu.VMEM((D_MODEL, IN_TN), BF16), pltpu.VMEM((D_MODEL, IN_TN), BF16),
                        pltpu.VMEM((2, chunk, IN_TN), F32),
                        pltpu.VMEM((ms, D_MODEL), BF16),
                        pltpu.SemaphoreType.DMA((2,))],
        compiler_params=_params("arbitrary", "arbitrary"),
        name="in_proj",
    )(h, xs, mod, mod, g_pre, w_a, w_b, w_in, w_mg)


def _rotate(x, cos, sin_signed):
    return x * cos + pltpu.roll(x, HEAD_DK // 2, 1) * sin_signed


def _group_norm_gate(o, gn, bg):
    mu = jnp.mean(o, axis=-1, keepdims=True)
    d = o - mu
    var = jnp.mean(d * d, axis=-1, keepdims=True)
    return d * lax.rsqrt(var + EPS) * gn * _silu(bg)


def _pool_project(pooled, pw, ps, ag):
    mixed = jnp.dot(pooled.astype(BF16), pw.astype(BF16), preferred_element_type=F32)
    return mixed * ps * _silu(ag)


PROJ_SPLIT = 8
PROJ_COLS = D_MODEL // PROJ_SPLIT


def _merge_chunk(y_a, y_b, j, gp_ref, bm_ref, wa_ref, wb_ref):
    lo, hi = j * PROJ_COLS, (j + 1) * PROJ_COLS
    ya = jnp.dot(y_a, wa_ref[:, lo:hi], preferred_element_type=F32)
    yb = jnp.dot(y_b, wb_ref[:, lo:hi], preferred_element_type=F32)
    g_a = jax.nn.sigmoid(gp_ref[:, lo:hi] + bm_ref[:, lo:hi])
    g_b = jax.nn.sigmoid(gp_ref[:, D_MODEL + lo:D_MODEL + hi] + bm_ref[:, D_MODEL + lo:D_MODEL + hi])
    return (g_a * ya + g_b * yb).astype(BF16)


def _out_chunk(merged, j, wo_ref):
    return jnp.dot(merged, wo_ref[:, j * PROJ_COLS:(j + 1) * PROJ_COLS], preferred_element_type=F32)


def _post_norm(o, gpost_ref):
    return o * lax.rsqrt(jnp.mean(o * o, axis=-1, keepdims=True) + EPS) * gpost_ref[...]


def _merge_project(y_a, y_b, gp_ref, bm_ref, wa_ref, wb_ref, wo_ref, gpost_ref):
    merged = jnp.concatenate([_merge_chunk(y_a, y_b, j, gp_ref, bm_ref, wa_ref, wb_ref)
                              for j in range(PROJ_SPLIT)], axis=1)
    o = jnp.concatenate([_out_chunk(merged, j, wo_ref) for j in range(PROJ_SPLIT)], axis=1)
    return _post_norm(o, gpost_ref)


MIX_ROWS = 2 * CHUNK
MIX_HEADS = 4
Z_AX, Z_AG, Z_Q, Z_K, Z_V, Z_BG = 0, D_POOL, 2 * D_POOL, 2 * D_POOL + D_QK, 2 * D_POOL + 2 * D_QK, D_IN - D_V


def _prompt_kernel(cdec_ref, z_ref, rot_ref, dmask_ref, qdec_ref, kdec_ref, pw_ref, ps_ref, gn_ref,
                   gp_ref, bm_ref, wa_ref, wb_ref, wo_ref,
                   m_ref, npool_ref, nret_ref, wo_bf_ref, ya0_ref, yb0_ref, ya1_ref, yb1_ref, ext_ref, *, tiles_per_seq):
    s = pl.program_id(0)
    n_tiles = pl.num_programs(0) - 1
    live = s < n_tiles
    c = lax.rem(jnp.minimum(s, n_tiles - 1), tiles_per_seq)
    rows = z_ref.shape[0]

    @pl.when(s == 0)
    def _():
        for ref in (ya0_ref, yb0_ref, ya1_ref, yb1_ref):
            ref[...] = jnp.zeros(ref.shape, BF16)

    @pl.when(c == 0)
    def _():
        ext_ref[0:HALO, :] = jnp.zeros((HALO, D_POOL), F32)
        nret_ref[...] = jnp.zeros(nret_ref.shape, F32)

    def step(ya_rd, yb_rd, ya_wr, yb_wr):
        y_a, y_b = ya_rd[...], yb_rd[...]
        wo_bf_ref[...] = wo_ref[...].astype(BF16)
        xa = z_ref[:, Z_AX:Z_AX + D_POOL]
        ext_ref[HALO:HALO + rows, :] = xa
        pos = c * rows + lax.broadcasted_iota(jnp.int32, (rows, 1), 0)
        nt = (((1,), (1,)), ((), ()))
        tn = (((0,), (0,)), ((), ()))

        def merge_piece(j):
            m_ref[:, j * PROJ_COLS:(j + 1) * PROJ_COLS] = _merge_chunk(y_a, y_b, j, gp_ref, bm_ref, wa_ref, wb_ref)

        def pool_group(g):
            w = POOL_WINDOWS[g]
            lo, hi = g * POOL_GROUP_DIM, (g + 1) * POOL_GROUP_DIM
            acc = ext_ref[:, lo:hi]
            span = 1
            while span < w:
                acc = acc + pltpu.roll(acc, span, 0)
                span *= 2
            inv_cnt = 1.0 / jnp.minimum(pos + 1, w).astype(F32)
            pooled = acc[HALO:, :] * inv_cnt - xa[:, lo:hi]
            ya = _pool_project(pooled, pw_ref[g], ps_ref[:, lo:hi], z_ref[:, Z_AG + lo:Z_AG + hi])
            ya_wr[:, lo:hi] = ya.astype(BF16)

        per_pool = PROJ_SPLIT // len(POOL_WINDOWS)
        assert per_pool * len(POOL_WINDOWS) == PROJ_SPLIT
        fillers = [f for g in range(len(POOL_WINDOWS))
                   for f in [functools.partial(merge_piece, g * per_pool + j) for j in range(per_pool)]
                   + [functools.partial(pool_group, g)]]
        n_sub = rows // CHUNK
        n_slots = 2 * (N_HEADS // MIX_HEADS) * n_sub
        slot = [0]

        def fill():
            lo, hi = (slot[0] * len(fillers)) // n_slots, ((slot[0] + 1) * len(fillers)) // n_slots
            slot[0] += 1
            for f in fillers[lo:hi]:
                f()

        for h0 in range(0, N_HEADS, MIX_HEADS):
            heads = range(h0, h0 + MIX_HEADS)
            s_cur = {h: nret_ref[h] for h in heads}
            for ci in range(n_sub):
                rs = slice(ci * CHUNK, (ci + 1) * CHUNK)
                cos = rot_ref[rs, :HEAD_DK]
                sin = rot_ref[rs, HEAD_DK:]
                q = {h: _rotate(z_ref[rs, Z_Q + h * HEAD_DK:Z_Q + (h + 1) * HEAD_DK], cos, sin) for h in heads}
                k = {h: _rotate(z_ref[rs, Z_K + h * HEAD_DK:Z_K + (h + 1) * HEAD_DK], cos, sin) * K_SCALE
                     for h in heads}
                v = {h: z_ref[rs, Z_V + h * HEAD_DV:Z_V + (h + 1) * HEAD_DV].astype(BF16) for h in heads}
                fill()
                scores = {h: lax.dot_general(q[h].astype(BF16), k[h].astype(BF16), nt, preferred_element_type=F32)
                          for h in heads}
                kv = {h: lax.dot_general((k[h] * kdec_ref[h]).astype(BF16), v[h], tn, preferred_element_type=F32)
                      for h in heads}
                lhs = {h: jnp.concatenate([(scores[h] * dmask_ref[h]).astype(BF16),
                                           (q[h] * qdec_ref[h]).astype(BF16)], axis=1) for h in heads}
                fill()
                for h in heads:
                    rhs = jnp.concatenate([v[h], s_cur[h].astype(BF16)], axis=0)
                    o = jnp.dot(lhs[h], rhs, preferred_element_type=F32)
                    s_cur[h] = cdec_ref[h] * s_cur[h] + kv[h]
                    vs = slice(h * HEAD_DV, (h + 1) * HEAD_DV)
                    bg = z_ref[rs, Z_BG + h * HEAD_DV:Z_BG + (h + 1) * HEAD_DV]
                    yb_wr[rs, vs] = _group_norm_gate(o, gn_ref[h:h + 1, :], bg).astype(BF16)
            for h in heads:
                nret_ref[h] = jnp.where(live, s_cur[h], nret_ref[h])
        assert slot[0] == n_slots

    @pl.when(lax.rem(s, 2) == 0)
    def _():
        step(ya1_ref, yb1_ref, ya0_ref, yb0_ref)

    @pl.when(lax.rem(s, 2) == 1)
    def _():
        step(ya0_ref, yb0_ref, ya1_ref, yb1_ref)

    @pl.when(c == tiles_per_seq - 1)
    def _():
        npool_ref[...] = ext_ref[HALO + rows - POOL_BUF:HALO + rows, :]

    ext_ref[0:HALO, :] = ext_ref[rows:rows + HALO, :]


def _prompt_mix_merge(z, batch, seq, tables, pool_w, pool_scale, gn_g, b_merge, w_a, w_b, w_o):
    rot, dmask, qdec, kdec, cdec = tables
    tps = seq // MIX_ROWS
    n_tiles = batch * tps
    assert n_tiles >= CAST_STEPS
    cur = lambda s: jnp.minimum(s, n_tiles - 1)
    prev = lambda s: jnp.maximum(s - 1, 0)
    m = batch * seq
    wo_in, wo_out, wo_shape = _cast_specs(w_o, lambda s: s)
    return pl.pallas_call(
        functools.partial(_prompt_kernel, tiles_per_seq=tps),
        out_shape=(jax.ShapeDtypeStruct((m, D_MODEL), BF16),
                   jax.ShapeDtypeStruct((batch, POOL_BUF, D_POOL), F32),
                   jax.ShapeDtypeStruct((batch, N_HEADS, HEAD_DK, HEAD_DV), F32), wo_shape),
        grid=(n_tiles + 1,),
        in_specs=[pl.BlockSpec(memory_space=pltpu.SMEM),
                  pl.BlockSpec((MIX_ROWS, D_IN), lambda s: (cur(s), 0)),
                  pl.BlockSpec((MIX_ROWS, 2 * HEAD_DK), lambda s: (cur(s) % tps, 0)),
                  _resident(dmask.shape), _resident(qdec.shape), _resident(kdec.shape),
                  _resident(pool_w.shape), _resident(pool_scale.shape), _resident(gn_g.shape),
                  pl.BlockSpec((MIX_ROWS, 2 * D_MODEL), lambda s: (prev(s), D_IN // (2 * D_MODEL))),
                  _resident(b_merge.shape), _resident(w_a.shape), _resident(w_b.shape), wo_in],
        out_specs=(pl.BlockSpec((MIX_ROWS, D_MODEL), lambda s: (prev(s), 0)),
                   pl.BlockSpec((None, POOL_BUF, D_POOL), lambda s: (cur(s) // tps, 0, 0)),
                   pl.BlockSpec((None, N_HEADS, HEAD_DK, HEAD_DV), lambda s: (cur(s) // tps, 0, 0, 0)), wo_out),
        scratch_shapes=[pltpu.VMEM((MIX_ROWS, D_POOL), BF16), pltpu.VMEM((MIX_ROWS, D_V), BF16),
                        pltpu.VMEM((MIX_ROWS, D_POOL), BF16), pltpu.VMEM((MIX_ROWS, D_V), BF16),
                        pltpu.VMEM((HALO + MIX_ROWS, D_POOL), F32)],
        compiler_params=_params("arbitrary"),
        name="prompt_mix_merge",
    )(cdec, z, rot, dmask, qdec, kdec, pool_w, pool_scale, gn_g, z, b_merge, w_a, w_b, w_o)


SAMPLE_TILE = 8
SAMPLE_HEADS = 4
TAIL_ROWS = 256


def _tail_kernel(dec_ref, m_ref, x_ref, gate_ref, wo_ref, gpost_ref,
                 zs_ref, rot_ref, pw_ref, ps_ref, gn_ref, spool_ref, sret_hbm,
                 y_ref, ya_ref, yb_ref, npool_ref, nret_hbm, o_ref, in_ring, out_ring, in_sem, out_sem,
                 *, inv_cnt, rows_per_mod, n_steps):
    bt = zs_ref.shape[0]
    n_hg = N_HEADS // SAMPLE_HEADS
    assert n_hg % RING_OUT == 0
    t = pl.program_id(0) * n_hg + pl.program_id(1)

    def state_block(hbm, step):
        seq0 = pl.multiple_of(lax.div(step, n_hg) * bt, bt)
        head0 = pl.multiple_of(lax.rem(step, n_hg) * SAMPLE_HEADS, SAMPLE_HEADS)
        return hbm.at[pl.ds(seq0, bt), pl.ds(head0, SAMPLE_HEADS)]

    def in_copy(step):
        return _ring_copy(state_block(sret_hbm, step), in_ring, in_sem, step)

    def out_copy(step):
        return _ring_copy(state_block(nret_hbm, step), out_ring, out_sem, step, to_hbm=True)

    _ring_begin(t, n_steps, in_copy, out_copy)
    s_in = in_ring.at[lax.rem(t, RING_IN)]

    def step(hg):
        s_out = out_ring.at[hg % RING_OUT]
        o = _post_norm(jnp.dot(m_ref[...], wo_ref[...], preferred_element_type=F32), gpost_ref)
        tile = pl.program_id(0) * n_hg + hg
        gate = gate_ref[pl.ds((tile * x_ref.shape[0]) // rows_per_mod, 1), :]
        y_ref[...] = x_ref[...] + gate * o

        if hg == 0:
            xa = zs_ref[:, Z_AX:Z_AX + D_POOL]
            run = xa
            wins = {1: xa}
            for j in range(1, POOL_BUF + 1):
                run = run + spool_ref[POOL_BUF - j]
                wins[j + 1] = run
            for g, w in enumerate(POOL_WINDOWS):
                lo, hi = g * POOL_GROUP_DIM, (g + 1) * POOL_GROUP_DIM
                pooled = wins[w][:, lo:hi] * inv_cnt[g] - xa[:, lo:hi]
                ya = _pool_project(pooled, pw_ref[g], ps_ref[:, lo:hi], zs_ref[:, Z_AG + lo:Z_AG + hi])
                ya_ref[:, lo:hi] = ya.astype(BF16)
            for j in range(POOL_BUF - 1):
                npool_ref[j] = spool_ref[j + 1]
            npool_ref[POOL_BUF - 1] = xa

        cos = rot_ref[:, :HEAD_DK]
        sin = rot_ref[:, HEAD_DK:]
        for hl in range(SAMPLE_HEADS):
            h = hg * SAMPLE_HEADS + hl
            vs = slice(h * HEAD_DV, (h + 1) * HEAD_DV)
            q = _rotate(zs_ref[:, Z_Q + h * HEAD_DK:Z_Q + (h + 1) * HEAD_DK], cos, sin)
            k = _rotate(zs_ref[:, Z_K + h * HEAD_DK:Z_K + (h + 1) * HEAD_DK], cos, sin) * K_SCALE
            v = zs_ref[:, Z_V + h * HEAD_DV:Z_V + (h + 1) * HEAD_DV]
            score = jnp.sum(q * k, axis=1, keepdims=True) * dec_ref[0, h]
            q_cols = jnp.transpose(q * dec_ref[1, h])
            k_cols = jnp.transpose(k * dec_ref[2, h])
            for r in range(bt):
                s_old = s_in[r, hl]
                v_row = v[r:r + 1, :]
                o_row = score[r:r + 1, :] * v_row + jnp.sum(q_cols[:, r:r + 1] * s_old, axis=0, keepdims=True)
                s_out[r, hl] = dec_ref[3, h] * s_old + k_cols[:, r:r + 1] * v_row
                o_ref[r:r + 1, vs] = o_row
        for hl in range(SAMPLE_HEADS):
            h = hg * SAMPLE_HEADS + hl
            vs = slice(h * HEAD_DV, (h + 1) * HEAD_DV)
            bg = zs_ref[:, Z_BG + h * HEAD_DV:Z_BG + (h + 1) * HEAD_DV]
            yb_ref[:, vs] = _group_norm_gate(o_ref[:, vs], gn_ref[h:h + 1, :], bg).astype(BF16)

    for hg in range(n_hg):
        pl.when(pl.program_id(1) == hg)(functools.partial(step, hg))

    _ring_end(t, n_steps, out_copy)


def _tail(merged, x, mod, w_o, g_post, zs, state_pool, state_ret, tables, inv_cnt, pool_w, pool_scale, gn_g,
          *, rows_per_mod, mod_row0):
    rot, dec = tables
    m, batch = x.shape[0], zs.shape[0]
    bt, hs = SAMPLE_TILE, SAMPLE_HEADS
    n_hg = N_HEADS // hs
    assert m == (batch // bt) * n_hg * TAIL_ROWS and mod_row0 % SUBLANES == 0 and m // rows_per_mod <= SUBLANES
    rows = lambda i, g: (i * n_hg + g, 0)
    per_tile = lambda width: pl.BlockSpec((bt, width), lambda i, g: (i, 0))
    return pl.pallas_call(
        functools.partial(_tail_kernel, inv_cnt=inv_cnt, rows_per_mod=rows_per_mod, n_steps=(batch // bt) * n_hg),
        out_shape=(jax.ShapeDtypeStruct((m, D_MODEL), F32),
                   jax.ShapeDtypeStruct((batch, D_POOL), BF16),
                   jax.ShapeDtypeStruct((batch, D_V), BF16),
                   jax.ShapeDtypeStruct(state_pool.shape, state_pool.dtype),
                   jax.ShapeDtypeStruct(state_ret.shape, state_ret.dtype)),
        grid=(batch // bt, n_hg),
        in_specs=[pl.BlockSpec(memory_space=pltpu.SMEM),
                  pl.BlockSpec((TAIL_ROWS, D_MODEL), rows), pl.BlockSpec((TAIL_ROWS, D_MODEL), rows),
                  pl.BlockSpec((SUBLANES, D_MODEL), lambda i, g: (mod_row0 // SUBLANES, 2)),
                  _resident(w_o.shape), _resident(g_post.shape),
                  per_tile(D_IN), _resident(rot.shape),
                  _resident(pool_w.shape), _resident(pool_scale.shape), _resident(gn_g.shape),
                  pl.BlockSpec((POOL_BUF, bt, D_POOL), lambda i, g: (0, i, 0)),
                  pl.BlockSpec(memory_space=pl.ANY)],
        out_specs=(pl.BlockSpec((TAIL_ROWS, D_MODEL), rows),
                   per_tile(D_POOL), per_tile(D_V),
                   pl.BlockSpec((POOL_BUF, bt, D_POOL), lambda i, g: (0, i, 0)),
                   pl.BlockSpec(memory_space=pl.ANY)),
        scratch_shapes=[pltpu.VMEM((bt, D_V), F32),
                        pltpu.VMEM((RING_IN, bt, hs, HEAD_DK, HEAD_DV), F32),
                        pltpu.VMEM((RING_OUT, bt, hs, HEAD_DK, HEAD_DV), F32),
                        pltpu.SemaphoreType.DMA((RING_IN,)), pltpu.SemaphoreType.DMA((RING_OUT,))],
        compiler_params=_params("arbitrary", "arbitrary"),
        name="tail",
    )(dec, merged, x, mod, w_o, g_post, zs, rot, pool_w, pool_scale, gn_g, state_pool, state_ret)


def _out_proj_kernel(ya_ref, yb_ref, gp_ref, x_ref, gate_ref, bm_ref, wa_ref, wb_ref, wo_ref, gpost_ref, y_ref,
                     *, rows_per_mod):
    o = _merge_project(ya_ref[...], yb_ref[...], gp_ref, bm_ref, wa_ref, wb_ref, wo_ref, gpost_ref)
    gate = _mod_rows(gate_ref, slice(None), x_ref.shape[0], rows_per_mod)
    y_ref[...] = (x_ref[...] + gate * o)[:, None, :]


def _out_proj(ya, yb, z, x, mod, b_merge, w_a, w_b, w_o, g_post, *, tm, rows_per_mod, mod_row0):
    m = x.shape[0]
    return pl.pallas_call(
        functools.partial(_out_proj_kernel, rows_per_mod=rows_per_mod),
        out_shape=jax.ShapeDtypeStruct((m, 1, D_MODEL), F32),
        grid=(m // tm,),
        in_specs=[pl.BlockSpec((tm, D_POOL), lambda i: (i, 0)),
                  pl.BlockSpec((tm, D_V), lambda i: (i, 0)),
                  pl.BlockSpec((tm, 2 * D_MODEL), lambda i: (i, D_IN // (2 * D_MODEL))),
                  pl.BlockSpec((tm, D_MODEL), lambda i: (i, 0)),
                  _mod_spec(tm, rows_per_mod, mod_row0, 2),
                  _resident(b_merge.shape), _resident(w_a.shape), _resident(w_b.shape), _resident(w_o.shape),
                  _resident(g_post.shape)],
        out_specs=pl.BlockSpec((tm, 1, D_MODEL), lambda i: (i, 0, 0)),
        compiler_params=_params("parallel"),
        name="out_proj",
    )(ya, yb, z, x, mod, b_merge, w_a, w_b, w_o, g_post)


def _rotary_tables(start, length):
    half = HEAD_DK // 2
    inv = ROPE_BASE ** (-np.arange(half, dtype=np.float64) / half)
    ang = (start + np.arange(length, dtype=np.float64))[:, None] * inv[None, :]
    cos, sin = np.cos(ang), np.sin(ang)
    return (np.concatenate([cos, cos], axis=-1).astype(np.float32),
            np.concatenate([-sin, sin], axis=-1).astype(np.float32))


def _decay_tables(c):
    lg = np.log1p(-np.power(2.0, -5.0 - np.arange(N_HEADS, dtype=np.float64)))
    idx = np.arange(c, dtype=np.float64)
    diff = idx[:, None] - idx[None, :]
    dmask = np.where(diff[None] >= 0, np.exp(np.maximum(diff, 0.0)[None] * lg[:, None, None]), 0.0)
    q_dec = np.exp((idx + 1.0)[None, :] * lg[:, None])
    k_dec = np.exp((c - 1.0 - idx)[None, :] * lg[:, None])
    chunk_dec = np.exp(c * lg)
    return tuple(a.astype(np.float32) for a in (dmask, q_dec, k_dec, chunk_dec))


def _layer(xp, xs, c_prompt, c_sample, state_pool, state_ret, ada_w, ada_b, g_pre, g_post, w_in, pool_w, pool_scale, gn_g,
           w_a_proj, w_b_proj, w_merge, b_merge, w_out):
    batch, seq, _ = xp.shape
    dec_batch, dec_seq, _ = xs.shape
    assert dec_seq == 1 and seq % CHUNK == 0

    row = lambda v: v.reshape(1, -1)
    g_pre, g_post, pool_scale, b_merge = map(row, (g_pre, g_post, pool_scale, b_merge))
    gn_g = gn_g.reshape(N_HEADS, HEAD_DV)

    mod = _modulation(c_sample, c_prompt, ada_w, row(ada_b))

    x2 = xp.reshape(batch * seq, D_MODEL)
    xs2 = xs.reshape(dec_batch, D_MODEL)
    h = _norm(x2, mod, g_pre, tm=NORM_TM, rows_per_mod=seq, mod_row0=dec_batch)
    z, zs, w_a, w_b = _in_proj(h, xs2, mod, g_pre, w_in, w_merge, w_a_proj, w_b_proj)

    cos, sin = _rotary_tables(0, seq)
    dmask, q_dec, k_dec, chunk_dec = _decay_tables(CHUNK)
    wide = lambda d: np.ascontiguousarray(np.broadcast_to(d[:, :, None], (N_HEADS, CHUNK, HEAD_DK)))
    rot = np.concatenate([cos, sin], axis=1)
    merged, pool_p, ret_p, w_o = _prompt_mix_merge(z, batch, seq, (rot, dmask, wide(q_dec), wide(k_dec), chunk_dec),
                                                   pool_w, pool_scale, gn_g, b_merge, w_a, w_b, w_out)

    cos_s, sin_s = _rotary_tables(PAST_LEN, 1)
    dmask_s, q_dec_s, k_dec_s, chunk_dec_s = _decay_tables(1)
    dec_s = np.stack([dmask_s[:, 0, 0], q_dec_s[:, 0], k_dec_s[:, 0], chunk_dec_s])
    inv_cnt = tuple(1.0 / min(PAST_LEN + 1, w) for w in POOL_WINDOWS)
    yp, ya_s, yb_s, pool_s, ret_s = _tail(merged, x2, mod, w_o, g_post, zs, jnp.transpose(state_pool, (1, 0, 2)),
                                          state_ret, (np.concatenate([cos_s, sin_s], axis=1), dec_s), inv_cnt,
                                          pool_w, pool_scale, gn_g,
                                          rows_per_mod=seq, mod_row0=dec_batch)
    pool_s = jnp.transpose(pool_s, (1, 0, 2))
    ys = _out_proj(ya_s, yb_s, zs, xs2, mod, b_merge, w_a, w_b, w_o, g_post, tm=dec_batch, rows_per_mod=1, mod_row0=0)

    return yp.reshape(xp.shape), ys.reshape(xs.shape), pool_p, ret_p, pool_s, ret_s


def kernel(x_prompt, x_sample, state_pool, state_ret, c_prompt, c_sample, ada_w, ada_b, g_pre, g_post,
           w_in, pool_w, pool_scale, gn_g, w_a_proj, w_b_proj, w_merge, b_merge, w_out):
    depth = ada_w.shape[0]
    xp, xs = x_prompt, x_sample
    pool_p, ret_p, pool_s, ret_s = [], [], [], []
    for l in range(depth):
        xp, xs, bp, sp, bs, ss = _layer(
            xp, xs, c_prompt, c_sample, state_pool[l], state_ret[l], ada_w[l], ada_b[l], g_pre[l], g_post[l], w_in[l],
            pool_w[l], pool_scale[l], gn_g[l], w_a_proj[l], w_b_proj[l], w_merge[l], b_merge[l],
            w_out[l])
        pool_p.append(bp)
        ret_p.append(sp)
        pool_s.append(bs)
        ret_s.append(ss)
    return (xp, xs, jnp.stack(pool_p), jnp.stack(ret_p), jnp.stack(pool_s), jnp.stack(ret_s))
```

```python
import functools

import jax
import jax.numpy as jnp
import numpy as np
from jax import lax
from jax.experimental import pallas as pl
from jax.experimental.pallas import tpu as pltpu

F32 = jnp.float32
BF16 = jnp.bfloat16

D_MODEL = 2048
PAST_LEN = 16384
D_POOL = D_MODEL // 2
POOL_WINDOWS = (2, 4, 8, 16)
POOL_GROUP_DIM = D_POOL // len(POOL_WINDOWS)
POOL_BUF = max(POOL_WINDOWS) - 1
N_HEADS = 8
HEAD_DK = D_MODEL // 16
HEAD_DV = D_MODEL // 8
D_QK = N_HEADS * HEAD_DK
D_V = N_HEADS * HEAD_DV
CHUNK = 128
ROPE_BASE = 10000.0
EPS = 1e-6
D_IN = 2 * D_POOL + 2 * D_QK + 2 * D_V
D_Z = D_IN + 2 * D_MODEL
K_SCALE = HEAD_DK ** -0.5

VMEM_LIMIT_BYTES = 56 * 1024 * 1024
SUBLANES = 8
HALO = 16
CAST_STEPS = 32


def _params(*semantics):
    return pltpu.CompilerParams(dimension_semantics=semantics, vmem_limit_bytes=VMEM_LIMIT_BYTES)


def _cast_specs(w, step_of):
    rows = w.shape[0] // CAST_STEPS
    assert rows * CAST_STEPS == w.shape[0] and rows % (2 * SUBLANES) == 0
    spec = pl.BlockSpec((rows, w.shape[1]), lambda *idx: (jnp.minimum(step_of(*idx), CAST_STEPS - 1), 0))
    return spec, spec, jax.ShapeDtypeStruct(w.shape, BF16)


def _resident(shape):
    return pl.BlockSpec(shape, lambda *_: (0,) * len(shape), pipeline_mode=pl.Buffered(1))


def _silu(x):
    return x * jax.nn.sigmoid(x)


RING_IN, RING_OUT = 3, 2


def _ring_copy(hbm_block, ring, sem, step, *, to_hbm=False):
    slot = lax.rem(step, ring.shape[0])
    src, dst = (ring.at[slot], hbm_block) if to_hbm else (hbm_block, ring.at[slot])
    return pltpu.make_async_copy(src, dst, sem.at[slot])


def _ring_begin(t, n_steps, in_copy, out_copy=None):
    @pl.when(t == 0)
    def _():
        for k in range(RING_IN - 1):
            in_copy(jnp.int32(k)).start()

    in_copy(t).wait()

    @pl.when(t + RING_IN - 1 < n_steps)
    def _():
        in_copy(t + RING_IN - 1).start()

    if out_copy is not None:
        @pl.when(t >= RING_OUT)
        def _():
            out_copy(t - RING_OUT).wait()


def _ring_end(t, n_steps, out_copy):
    out_copy(t).start()

    @pl.when(t == n_steps - 1)
    def _():
        for k in reversed(range(RING_OUT)):
            out_copy(t - k).wait()


MOD_TN = 512


def _mod_kernel(cs_ref, cp_ref, w_hbm, b_ref, o_ref, w_ring, sem, *, n_steps):
    t = pl.program_id(0)
    tn = w_ring.shape[2]

    def in_copy(step):
        return _ring_copy(w_hbm.at[:, pl.ds(pl.multiple_of(step * tn, tn), tn)], w_ring, sem, step)

    _ring_begin(t, n_steps, in_copy)
    n_pad = o_ref.shape[0] - cs_ref.shape[0] - cp_ref.shape[0]
    c = jnp.concatenate([cs_ref[...], cp_ref[...], jnp.zeros((n_pad, D_MODEL), F32)], axis=0)
    w = w_ring[lax.rem(t, RING_IN)].astype(BF16)
    o_ref[...] = jnp.dot(_silu(c).astype(BF16), w, preferred_element_type=F32) + b_ref[...]


def _modulation(c_sample, c_prompt, ada_w, ada_b):
    ns, n_p = c_sample.shape[0], c_prompt.shape[0]
    assert ns % SUBLANES == 0
    rows = ns + n_p + (-n_p) % SUBLANES
    n_steps = 3 * D_MODEL // MOD_TN
    return pl.pallas_call(
        functools.partial(_mod_kernel, n_steps=n_steps),
        out_shape=jax.ShapeDtypeStruct((rows, 3 * D_MODEL), F32),
        grid=(n_steps,),
        in_specs=[pl.BlockSpec((ns, D_MODEL), lambda j: (0, 0)),
                  pl.BlockSpec((n_p, D_MODEL), lambda j: (0, 0)),
                  pl.BlockSpec(memory_space=pl.ANY),
                  pl.BlockSpec((1, MOD_TN), lambda j: (0, j))],
        out_specs=pl.BlockSpec((rows, MOD_TN), lambda j: (0, j)),
        scratch_shapes=[pltpu.VMEM((RING_IN, D_MODEL, MOD_TN), F32), pltpu.SemaphoreType.DMA((RING_IN,))],
        compiler_params=_params("arbitrary"),
        name="modulation",
    )(c_sample, c_prompt, ada_w, ada_b)


NORM_ROWS = 128
NORM_TM = 512


def _mod_rows(ref, rows, tm, rows_per_mod):
    if rows_per_mod == 1:
        return ref[rows, :]
    return ref[pl.ds((pl.program_id(0) * tm) // rows_per_mod, 1), :]


def _mod_spec(tm, rows_per_mod, mod_row0, col):
    if rows_per_mod == 1:
        assert mod_row0 % tm == 0
        return pl.BlockSpec((tm, D_MODEL), lambda i, *_: (mod_row0 // tm + i, col))
    assert mod_row0 % SUBLANES == 0
    return pl.BlockSpec((SUBLANES, D_MODEL), lambda i, *_: (mod_row0 // SUBLANES, col))


def _norm_mod(x, g, scale, shift):
    xn = x * lax.rsqrt(jnp.mean(x * x, axis=-1, keepdims=True) + EPS) * g
    return (xn * (1.0 + scale) + shift).astype(BF16)


def _norm_kernel(x_hbm, shift_ref, scale_ref, g_ref, h_hbm, x_ring, h_ring, in_sem, out_sem,
                 *, rows_per_mod, n_steps):
    tm = x_ring.shape[1]
    t = pl.program_id(0)

    def rows_of(step):
        return pl.ds(pl.multiple_of(step * tm, tm), tm)

    def in_copy(step):
        return _ring_copy(x_hbm.at[rows_of(step)], x_ring, in_sem, step)

    def out_copy(step):
        return _ring_copy(h_hbm.at[rows_of(step)], h_ring, out_sem, step, to_hbm=True)

    _ring_begin(t, n_steps, in_copy, out_copy)
    x_ref, h_ref = x_ring.at[lax.rem(t, RING_IN)], h_ring.at[lax.rem(t, RING_OUT)]

    def body(r, carry):
        rows = pl.ds(pl.multiple_of(r * NORM_ROWS, NORM_ROWS), NORM_ROWS)
        h_ref[rows, :] = _norm_mod(x_ref[rows, :], g_ref[...], _mod_rows(scale_ref, rows, tm, rows_per_mod),
                                   _mod_rows(shift_ref, rows, tm, rows_per_mod))
        return carry
    lax.fori_loop(0, tm // NORM_ROWS, body, 0)
    _ring_end(t, n_steps, out_copy)


def _norm(x, mod, g_pre, *, tm, rows_per_mod, mod_row0):
    m = x.shape[0]
    n_steps = m // tm
    assert (rows_per_mod == 1 or m // rows_per_mod <= SUBLANES) and n_steps >= 2
    mod_spec = lambda col: _mod_spec(tm, rows_per_mod, mod_row0, col)
    return pl.pallas_call(
        functools.partial(_norm_kernel, rows_per_mod=rows_per_mod, n_steps=n_steps),
        out_shape=jax.ShapeDtypeStruct((m, D_MODEL), BF16),
        grid=(n_steps,),
        in_specs=[pl.BlockSpec(memory_space=pl.ANY), mod_spec(0), mod_spec(1),
                  pl.BlockSpec((1, D_MODEL), lambda i: (0, 0))],
        out_specs=pl.BlockSpec(memory_space=pl.ANY),
        scratch_shapes=[pltpu.VMEM((RING_IN, tm, D_MODEL), F32), pltpu.VMEM((RING_OUT, tm, D_MODEL), BF16),
                        pltpu.SemaphoreType.DMA((RING_IN,)), pltpu.SemaphoreType.DMA((RING_OUT,))],
        compiler_params=_params("arbitrary"),
        name="norm",
    )(x, mod, mod, g_pre)


IN_TM = 1024
IN_TN = 2048
IN_DOT_COLS = 512


def _in_proj_kernel(h_ref, xs_ref, shift_s_ref, scale_s_ref, g_ref, wa_ref, wb_ref, w_in_hbm, w_mg_hbm,
                    z_ref, zs_ref, wa_bf_ref, wb_bf_ref, wbf0_ref, wbf1_ref, stage_ref, hs_ref, sem,
                    *, n_in_tiles, n_j, n_i):
    j, i = pl.program_id(0), pl.program_id(1)
    t = j * n_i + i
    total = n_j * n_i
    tn = wbf0_ref.shape[1]
    chunk = stage_ref.shape[1]

    def chunk_copy(w_hbm, col0, r, b):
        rows = pl.ds(pl.multiple_of(r * chunk, chunk), chunk)
        cols = pl.ds(col0 if isinstance(col0, int) else pl.multiple_of(col0, tn), tn)
        return pltpu.make_async_copy(w_hbm.at[rows, cols], stage_ref.at[b], sem.at[b])

    def start_chunk(g):
        g = lax.rem(jnp.asarray(g, jnp.int32), total)
        jt, r, b = lax.div(g, n_i), lax.rem(g, n_i), lax.rem(g, 2)

        @pl.when(jt < n_in_tiles)
        def _():
            chunk_copy(w_in_hbm, jt * tn, r, b).start()

        @pl.when(jt >= n_in_tiles)
        def _():
            chunk_copy(w_mg_hbm, (jt - n_in_tiles) * tn, r, b).start()

    def land_chunk(g, dst_ref):
        g = jnp.asarray(g, jnp.int32)
        r, b = lax.rem(g, n_i), lax.rem(g, 2)
        chunk_copy(w_in_hbm, 0, r, b).wait()
        dst_ref[pl.ds(pl.multiple_of(r * chunk, chunk), chunk), :] = stage_ref[b].astype(BF16)

    @pl.when(t == 0)
    def _():
        hs_ref[...] = _norm_mod(xs_ref[...], g_ref[...], scale_s_ref[...], shift_s_ref[...])
        start_chunk(0)

        def body(g, carry):
            start_chunk(g + 1)
            land_chunk(g, wbf0_ref)
            return carry
        lax.fori_loop(0, n_i, body, 0)

    @pl.when(t + 1 < total)
    def _():
        start_chunk(t + n_i + 1)

    def multiply(w_cur, w_nxt):
        land_chunk(t + n_i, w_nxt)
        wa_bf_ref[...] = wa_ref[...].astype(BF16)
        wb_bf_ref[...] = wb_ref[...].astype(BF16)
        for c0 in range(0, tn, IN_DOT_COLS):
            cols = slice(c0, c0 + IN_DOT_COLS)
            z_ref[:, cols] = jnp.dot(h_ref[...], w_cur[:, cols], preferred_element_type=F32)

        @pl.when(i == 0)
        def _():
            zs_ref[...] = jnp.dot(hs_ref[...], w_cur[...], preferred_element_type=F32)

    @pl.when(lax.rem(j, 2) == 0)
    def _():
        multiply(wbf0_ref, wbf1_ref)

    @pl.when(lax.rem(j, 2) == 1)
    def _():
        multiply(wbf1_ref, wbf0_ref)


def _in_proj(h, xs, mod, g_pre, w_in, w_mg, w_a, w_b):
    m, ms = h.shape[0], xs.shape[0]
    n_in, n_mg = w_in.shape[1] // IN_TN, w_mg.shape[1] // IN_TN
    n_i = m // IN_TM
    chunk = D_MODEL // n_i
    assert chunk * n_i == D_MODEL and chunk % 16 == 0 and (n_in + n_mg) * n_i >= CAST_STEPS
    const = lambda shape, col: pl.BlockSpec(shape, lambda j, i: (0, col), pipeline_mode=pl.Buffered(1))
    wa_in, wa_out, wa_shape = _cast_specs(w_a, lambda j, i: j * n_i + i)
    wb_in, wb_out, wb_shape = _cast_specs(w_b, lambda j, i: j * n_i + i)
    return pl.pallas_call(
        functools.partial(_in_proj_kernel, n_in_tiles=n_in, n_j=n_in + n_mg, n_i=n_i),
        out_shape=(jax.ShapeDtypeStruct((m, D_Z), F32), jax.ShapeDtypeStruct((ms, D_Z), F32), wa_shape, wb_shape),
        grid=(n_in + n_mg, n_i),
        in_specs=[pl.BlockSpec((IN_TM, D_MODEL), lambda j, i: (i, 0)),
                  const((ms, D_MODEL), 0), const((ms, D_MODEL), 0), const((ms, D_MODEL), 1),
                  const((1, D_MODEL), 0), wa_in, wb_in,
                  pl.BlockSpec(memory_space=pl.ANY), pl.BlockSpec(memory_space=pl.ANY)],
        out_specs=(pl.BlockSpec((IN_TM, IN_TN), lambda j, i: (i, j)),
                   pl.BlockSpec((ms, IN_TN), lambda j, i: (0, j)), wa_out, wb_out),
        scratch_shapes=[pltpu.VMEM((D_MODEL, IN_TN), BF16), pltpu.VMEM((D_MODEL, IN_TN), BF16),
                        pltpu.VMEM((2, chunk, IN_TN), F32),
                        pltpu.VMEM((ms, D_MODEL), BF16),
                        pltpu.SemaphoreType.DMA((2,))],
        compiler_params=_params("arbitrary", "arbitrary"),
        name="in_proj",
    )(h, xs, mod, mod, g_pre, w_a, w_b, w_in, w_mg)


def _rotate(x, cos, sin_signed):
    return x * cos + pltpu.roll(x, HEAD_DK // 2, 1) * sin_signed


def _group_norm_gate(o, gn, bg):
    mu = jnp.mean(o, axis=-1, keepdims=True)
    d = o - mu
    var = jnp.mean(d * d, axis=-1, keepdims=True)
    return d * lax.rsqrt(var + EPS) * gn * _silu(bg)


def _pool_project(pooled, pw, ps, ag):
    mixed = jnp.dot(pooled.astype(BF16), pw.astype(BF16), preferred_element_type=F32)
    return mixed * ps * _silu(ag)


PROJ_SPLIT = 8
PROJ_COLS = D_MODEL // PROJ_SPLIT


def _merge_chunk(y_a, y_b, j, gp_ref, bm_ref, wa_ref, wb_ref):
    lo, hi = j * PROJ_COLS, (j + 1) * PROJ_COLS
    ya = jnp.dot(y_a, wa_ref[:, lo:hi], preferred_element_type=F32)
    yb = jnp.dot(y_b, wb_ref[:, lo:hi], preferred_element_type=F32)
    g_a = jax.nn.sigmoid(gp_ref[:, lo:hi] + bm_ref[:, lo:hi])
    g_b = jax.nn.sigmoid(gp_ref[:, D_MODEL + lo:D_MODEL + hi] + bm_ref[:, D_MODEL + lo:D_MODEL + hi])
    return (g_a * ya + g_b * yb).astype(BF16)


def _out_chunk(merged, j, wo_ref):
    return jnp.dot(merged, wo_ref[:, j * PROJ_COLS:(j + 1) * PROJ_COLS], preferred_element_type=F32)


def _post_norm(o, gpost_ref):
    return o * lax.rsqrt(jnp.mean(o * o, axis=-1, keepdims=True) + EPS) * gpost_ref[...]


MIX_ROWS = 2 * CHUNK
MIX_HEADS = 4
Z_AX, Z_AG, Z_Q, Z_K, Z_V, Z_BG = 0, D_POOL, 2 * D_POOL, 2 * D_POOL + D_QK, 2 * D_POOL + 2 * D_QK, D_IN - D_V


def _prompt_kernel(cdec_ref, z_ref, rot_ref, dmask_ref, qdec_ref, kdec_ref, pw_ref, ps_ref, gn_ref,
                   gp_ref, bm_ref, wa_ref, wb_ref, wo_ref,
                   m_ref, npool_ref, nret_ref, wo_bf_ref, ya0_ref, yb0_ref, ya1_ref, yb1_ref, ext_ref, *, tiles_per_seq):
    s = pl.program_id(0)
    n_tiles = pl.num_programs(0) - 1
    live = s < n_tiles
    c = lax.rem(jnp.minimum(s, n_tiles - 1), tiles_per_seq)
    rows = z_ref.shape[0]

    @pl.when(s == 0)
    def _():
        for ref in (ya0_ref, yb0_ref, ya1_ref, yb1_ref):
            ref[...] = jnp.zeros(ref.shape, BF16)

    @pl.when(c == 0)
    def _():
        ext_ref[0:HALO, :] = jnp.zeros((HALO, D_POOL), F32)
        nret_ref[...] = jnp.zeros(nret_ref.shape, F32)

    def step(ya_rd, yb_rd, ya_wr, yb_wr):
        y_a, y_b = ya_rd[...], yb_rd[...]
        wo_bf_ref[...] = wo_ref[...].astype(BF16)
        xa = z_ref[:, Z_AX:Z_AX + D_POOL]
        ext_ref[HALO:HALO + rows, :] = xa
        pos = c * rows + lax.broadcasted_iota(jnp.int32, (rows, 1), 0)
        nt = (((1,), (1,)), ((), ()))
        tn = (((0,), (0,)), ((), ()))

        def merge_piece(j):
            m_ref[:, j * PROJ_COLS:(j + 1) * PROJ_COLS] = _merge_chunk(y_a, y_b, j, gp_ref, bm_ref, wa_ref, wb_ref)

        def pool_group(g):
            w = POOL_WINDOWS[g]
            lo, hi = g * POOL_GROUP_DIM, (g + 1) * POOL_GROUP_DIM
            acc = ext_ref[:, lo:hi]
            span = 1
            while span < w:
                acc = acc + pltpu.roll(acc, span, 0)
                span *= 2
            inv_cnt = 1.0 / jnp.minimum(pos + 1, w).astype(F32)
            pooled = acc[HALO:, :] * inv_cnt - xa[:, lo:hi]
            ya = _pool_project(pooled, pw_ref[g], ps_ref[:, lo:hi], z_ref[:, Z_AG + lo:Z_AG + hi])
            ya_wr[:, lo:hi] = ya.astype(BF16)

        per_pool = PROJ_SPLIT // len(POOL_WINDOWS)
        assert per_pool * len(POOL_WINDOWS) == PROJ_SPLIT
        fillers = [f for g in range(len(POOL_WINDOWS))
                   for f in [functools.partial(merge_piece, g * per_pool + j) for j in range(per_pool)]
                   + [functools.partial(pool_group, g)]]
        n_sub = rows // CHUNK
        n_slots = 2 * (N_HEADS // MIX_HEADS) * n_sub
        slot = [0]

        def fill():
            lo, hi = (slot[0] * len(fillers)) // n_slots, ((slot[0] + 1) * len(fillers)) // n_slots
            slot[0] += 1
            for f in fillers[lo:hi]:
                f()

        for h0 in range(0, N_HEADS, MIX_HEADS):
            heads = range(h0, h0 + MIX_HEADS)
            s_cur = {h: nret_ref[h] for h in heads}
            for ci in range(n_sub):
                rs = slice(ci * CHUNK, (ci + 1) * CHUNK)
                cos = rot_ref[rs, :HEAD_DK]
                sin = rot_ref[rs, HEAD_DK:]
                q = {h: _rotate(z_ref[rs, Z_Q + h * HEAD_DK:Z_Q + (h + 1) * HEAD_DK], cos, sin) for h in heads}
                k = {h: _rotate(z_ref[rs, Z_K + h * HEAD_DK:Z_K + (h + 1) * HEAD_DK], cos, sin) * K_SCALE
                     for h in heads}
                v = {h: z_ref[rs, Z_V + h * HEAD_DV:Z_V + (h + 1) * HEAD_DV].astype(BF16) for h in heads}
                fill()
                scores = {h: lax.dot_general(q[h].astype(BF16), k[h].astype(BF16), nt, preferred_element_type=F32)
                          for h in heads}
                kv = {h: lax.dot_general((k[h] * kdec_ref[h]).astype(BF16), v[h], tn, preferred_element_type=F32)
                      for h in heads}
                lhs = {h: jnp.concatenate([(scores[h] * dmask_ref[h]).astype(BF16),
                                           (q[h] * qdec_ref[h]).astype(BF16)], axis=1) for h in heads}
                fill()
                for h in heads:
                    rhs = jnp.concatenate([v[h], s_cur[h].astype(BF16)], axis=0)
                    o = jnp.dot(lhs[h], rhs, preferred_element_type=F32)
                    s_cur[h] = cdec_ref[h] * s_cur[h] + kv[h]
                    vs = slice(h * HEAD_DV, (h + 1) * HEAD_DV)
                    bg = z_ref[rs, Z_BG + h * HEAD_DV:Z_BG + (h + 1) * HEAD_DV]
                    yb_wr[rs, vs] = _group_norm_gate(o, gn_ref[h:h + 1, :], bg).astype(BF16)
            for h in heads:
                nret_ref[h] = jnp.where(live, s_cur[h], nret_ref[h])
        assert slot[0] == n_slots

    @pl.when(lax.rem(s, 2) == 0)
    def _():
        step(ya1_ref, yb1_ref, ya0_ref, yb0_ref)

    @pl.when(lax.rem(s, 2) == 1)
    def _():
        step(ya0_ref, yb0_ref, ya1_ref, yb1_ref)

    @pl.when(c == tiles_per_seq - 1)
    def _():
        npool_ref[...] = ext_ref[HALO + rows - POOL_BUF:HALO + rows, :]

    ext_ref[0:HALO, :] = ext_ref[rows:rows + HALO, :]


def _prompt_mix_merge(z, batch, seq, tables, pool_w, pool_scale, gn_g, b_merge, w_a, w_b, w_o):
    rot, dmask, qdec, kdec, cdec = tables
    tps = seq // MIX_ROWS
    n_tiles = batch * tps
    assert n_tiles >= CAST_STEPS
    cur = lambda s: jnp.minimum(s, n_tiles - 1)
    prev = lambda s: jnp.maximum(s - 1, 0)
    m = batch * seq
    wo_in, wo_out, wo_shape = _cast_specs(w_o, lambda s: s)
    return pl.pallas_call(
        functools.partial(_prompt_kernel, tiles_per_seq=tps),
        out_shape=(jax.ShapeDtypeStruct((m, D_MODEL), BF16),
                   jax.ShapeDtypeStruct((batch, POOL_BUF, D_POOL), F32),
                   jax.ShapeDtypeStruct((batch, N_HEADS, HEAD_DK, HEAD_DV), F32), wo_shape),
        grid=(n_tiles + 1,),
        in_specs=[pl.BlockSpec(memory_space=pltpu.SMEM),
                  pl.BlockSpec((MIX_ROWS, D_IN), lambda s: (cur(s), 0)),
                  pl.BlockSpec((MIX_ROWS, 2 * HEAD_DK), lambda s: (cur(s) % tps, 0)),
                  _resident(dmask.shape), _resident(qdec.shape), _resident(kdec.shape),
                  _resident(pool_w.shape), _resident(pool_scale.shape), _resident(gn_g.shape),
                  pl.BlockSpec((MIX_ROWS, 2 * D_MODEL), lambda s: (prev(s), D_IN // (2 * D_MODEL))),
                  _resident(b_merge.shape), _resident(w_a.shape), _resident(w_b.shape), wo_in],
        out_specs=(pl.BlockSpec((MIX_ROWS, D_MODEL), lambda s: (prev(s), 0)),
                   pl.BlockSpec((None, POOL_BUF, D_POOL), lambda s: (cur(s) // tps, 0, 0)),
                   pl.BlockSpec((None, N_HEADS, HEAD_DK, HEAD_DV), lambda s: (cur(s) // tps, 0, 0, 0)), wo_out),
        scratch_shapes=[pltpu.VMEM((MIX_ROWS, D_POOL), BF16), pltpu.VMEM((MIX_ROWS, D_V), BF16),
                        pltpu.VMEM((MIX_ROWS, D_POOL), BF16), pltpu.VMEM((MIX_ROWS, D_V), BF16),
                        pltpu.VMEM((HALO + MIX_ROWS, D_POOL), F32)],
        compiler_params=_params("arbitrary"),
        name="prompt_mix_merge",
    )(cdec, z, rot, dmask, qdec, kdec, pool_w, pool_scale, gn_g, z, b_merge, w_a, w_b, w_o)


SAMPLE_TILE = 8
SAMPLE_HEADS = 4
TAIL_ROWS = 256


def _tail_kernel(dec_ref, m_ref, x_ref, gate_ref, wo_ref, gpost_ref,
                 zs_ref, rot_ref, pw_ref, ps_ref, gn_ref, spool_ref, sret_hbm,
                 y_ref, ya_ref, yb_ref, npool_ref, nret_hbm, o_ref, in_ring, out_ring, in_sem, out_sem,
                 *, inv_cnt, rows_per_mod, n_steps):
    bt = zs_ref.shape[0]
    n_hg = N_HEADS // SAMPLE_HEADS
    assert n_hg % RING_OUT == 0
    t = pl.program_id(0) * n_hg + pl.program_id(1)

    def state_block(hbm, step):
        seq0 = pl.multiple_of(lax.div(step, n_hg) * bt, bt)
        head0 = pl.multiple_of(lax.rem(step, n_hg) * SAMPLE_HEADS, SAMPLE_HEADS)
        return hbm.at[pl.ds(seq0, bt), pl.ds(head0, SAMPLE_HEADS)]

    def in_copy(step):
        return _ring_copy(state_block(sret_hbm, step), in_ring, in_sem, step)

    def out_copy(step):
        return _ring_copy(state_block(nret_hbm, step), out_ring, out_sem, step, to_hbm=True)

    _ring_begin(t, n_steps, in_copy, out_copy)
    s_in = in_ring.at[lax.rem(t, RING_IN)]

    def step(hg):
        s_out = out_ring.at[hg % RING_OUT]
        o = _post_norm(jnp.dot(m_ref[...], wo_ref[...], preferred_element_type=F32), gpost_ref)
        tile = pl.program_id(0) * n_hg + hg
        gate = gate_ref[pl.ds((tile * x_ref.shape[0]) // rows_per_mod, 1), :]
        y_ref[...] = x_ref[...] + gate * o

        if hg == 0:
            xa = zs_ref[:, Z_AX:Z_AX + D_POOL]
            run = xa
            wins = {1: xa}
            for j in range(1, POOL_BUF + 1):
                run = run + spool_ref[POOL_BUF - j]
                wins[j + 1] = run
            for g, w in enumerate(POOL_WINDOWS):
                lo, hi = g * POOL_GROUP_DIM, (g + 1) * POOL_GROUP_DIM
                pooled = wins[w][:, lo:hi] * inv_cnt[g] - xa[:, lo:hi]
                ya = _pool_project(pooled, pw_ref[g], ps_ref[:, lo:hi], zs_ref[:, Z_AG + lo:Z_AG + hi])
                ya_ref[:, lo:hi] = ya.astype(BF16)
            for j in range(POOL_BUF - 1):
                npool_ref[j] = spool_ref[j + 1]
            npool_ref[POOL_BUF - 1] = xa

        cos = rot_ref[:, :HEAD_DK]
        sin = rot_ref[:, HEAD_DK:]
        for hl in range(SAMPLE_HEADS):
            h = hg * SAMPLE_HEADS + hl
            vs = slice(h * HEAD_DV, (h + 1) * HEAD_DV)
            q = _rotate(zs_ref[:, Z_Q + h * HEAD_DK:Z_Q + (h + 1) * HEAD_DK], cos, sin)
            k = _rotate(zs_ref[:, Z_K + h * HEAD_DK:Z_K + (h + 1) * HEAD_DK], cos, sin) * K_SCALE
            v = zs_ref[:, Z_V + h * HEAD_DV:Z_V + (h + 1) * HEAD_DV]
            score = jnp.sum(q * k, axis=1, keepdims=True) * dec_ref[0, h]
            q_cols = jnp.transpose(q * dec_ref[1, h])
            k_cols = jnp.transpose(k * dec_ref[2, h])
            for r in range(bt):
                s_old = s_in[r, hl]
                v_row = v[r:r + 1, :]
                o_row = score[r:r + 1, :] * v_row + jnp.sum(q_cols[:, r:r + 1] * s_old, axis=0, keepdims=True)
                s_out[r, hl] = dec_ref[3, h] * s_old + k_cols[:, r:r + 1] * v_row
                o_ref[r:r + 1, vs] = o_row
        for hl in range(SAMPLE_HEADS):
            h = hg * SAMPLE_HEADS + hl
            vs = slice(h * HEAD_DV, (h + 1) * HEAD_DV)
            bg = zs_ref[:, Z_BG + h * HEAD_DV:Z_BG + (h + 1) * HEAD_DV]
            yb_ref[:, vs] = _group_norm_gate(o_ref[:, vs], gn_ref[h:h + 1, :], bg).astype(BF16)

    for hg in range(n_hg):
        pl.when(pl.program_id(1) == hg)(functools.partial(step, hg))

    _ring_end(t, n_steps, out_copy)


def _tail(merged, x, mod, w_o, g_post, zs, state_pool, state_ret, tables, inv_cnt, pool_w, pool_scale, gn_g,
          *, rows_per_mod, mod_row0):
    rot, dec = tables
    m, batch = x.shape[0], zs.shape[0]
    bt, hs = SAMPLE_TILE, SAMPLE_HEADS
    n_hg = N_HEADS // hs
    assert m == (batch // bt) * n_hg * TAIL_ROWS and mod_row0 % SUBLANES == 0 and m // rows_per_mod <= SUBLANES
    rows = lambda i, g: (i * n_hg + g, 0)
    per_tile = lambda width: pl.BlockSpec((bt, width), lambda i, g: (i, 0))
    return pl.pallas_call(
        functools.partial(_tail_kernel, inv_cnt=inv_cnt, rows_per_mod=rows_per_mod, n_steps=(batch // bt) * n_hg),
        out_shape=(jax.ShapeDtypeStruct((m, D_MODEL), F32),
                   jax.ShapeDtypeStruct((batch, D_POOL), BF16),
                   jax.ShapeDtypeStruct((batch, D_V), BF16),
                   jax.ShapeDtypeStruct(state_pool.shape, state_pool.dtype),
                   jax.ShapeDtypeStruct(state_ret.shape, state_ret.dtype)),
        grid=(batch // bt, n_hg),
        in_specs=[pl.BlockSpec(memory_space=pltpu.SMEM),
                  pl.BlockSpec((TAIL_ROWS, D_MODEL), rows), pl.BlockSpec((TAIL_ROWS, D_MODEL), rows),
                  pl.BlockSpec((SUBLANES, D_MODEL), lambda i, g: (mod_row0 // SUBLANES, 2)),
                  _resident(w_o.shape), _resident(g_post.shape),
                  per_tile(D_IN), _resident(rot.shape),
                  _resident(pool_w.shape), _resident(pool_scale.shape), _resident(gn_g.shape),
                  pl.BlockSpec((POOL_BUF, bt, D_POOL), lambda i, g: (0, i, 0)),
                  pl.BlockSpec(memory_space=pl.ANY)],
        out_specs=(pl.BlockSpec((TAIL_ROWS, D_MODEL), rows),
                   per_tile(D_POOL), per_tile(D_V),
                   pl.BlockSpec((POOL_BUF, bt, D_POOL), lambda i, g: (0, i, 0)),
                   pl.BlockSpec(memory_space=pl.ANY)),
        scratch_shapes=[pltpu.VMEM((bt, D_V), F32),
                        pltpu.VMEM((RING_IN, bt, hs, HEAD_DK, HEAD_DV), F32),
                        pltpu.VMEM((RING_OUT, bt, hs, HEAD_DK, HEAD_DV), F32),
                        pltpu.SemaphoreType.DMA((RING_IN,)), pltpu.SemaphoreType.DMA((RING_OUT,))],
        compiler_params=_params("arbitrary", "arbitrary"),
        name="tail",
    )(dec, merged, x, mod, w_o, g_post, zs, rot, pool_w, pool_scale, gn_g, state_pool, state_ret)


def _out_proj_kernel(ya_ref, yb_ref, gp_ref, x_ref, gate_ref, bm_ref, gpost_ref, wa_hbm, wb_hbm, wo_hbm,
                     y_ref, wa_buf, wb_buf, wo_buf, sem, *, rows_per_mod):
    copies, n = [], 0
    for hbm, buf in ((wa_hbm, wa_buf), (wb_hbm, wb_buf), (wo_hbm, wo_buf)):
        chunks = []
        for r0 in range(0, buf.shape[0], PROJ_COLS):
            rows = pl.ds(r0, PROJ_COLS)
            chunks.append((pltpu.make_async_copy(hbm.at[rows], buf.at[rows], sem.at[n]), r0))
            n += 1
        copies.append(chunks)
    for chunks in copies:
        for copy, _ in chunks:
            copy.start()

    def streamed_dot(lhs, buf, chunks):
        acc = None
        for copy, r0 in chunks:
            copy.wait()
            part = jnp.dot(lhs[:, r0:r0 + PROJ_COLS], buf[r0:r0 + PROJ_COLS, :], preferred_element_type=F32)
            acc = part if acc is None else acc + part
        return acc

    ya = streamed_dot(ya_ref, wa_buf, copies[0])
    yb = streamed_dot(yb_ref, wb_buf, copies[1])
    g_a = jax.nn.sigmoid(gp_ref[:, :D_MODEL] + bm_ref[:, :D_MODEL])
    g_b = jax.nn.sigmoid(gp_ref[:, D_MODEL:] + bm_ref[:, D_MODEL:])
    o = _post_norm(streamed_dot((g_a * ya + g_b * yb).astype(BF16), wo_buf, copies[2]), gpost_ref)
    gate = _mod_rows(gate_ref, slice(None), x_ref.shape[0], rows_per_mod)
    y_ref[...] = (x_ref[...] + gate * o)[:, None, :]


def _out_proj(ya, yb, z, x, mod, b_merge, w_a, w_b, w_o, g_post, *, tm, rows_per_mod, mod_row0):
    m = x.shape[0]
    n_chunks = (w_a.shape[0] + w_b.shape[0] + w_o.shape[0]) // PROJ_COLS
    return pl.pallas_call(
        functools.partial(_out_proj_kernel, rows_per_mod=rows_per_mod),
        out_shape=jax.ShapeDtypeStruct((m, 1, D_MODEL), F32),
        grid=(m // tm,),
        in_specs=[pl.BlockSpec((tm, D_POOL), lambda i: (i, 0)),
                  pl.BlockSpec((tm, D_V), lambda i: (i, 0)),
                  pl.BlockSpec((tm, 2 * D_MODEL), lambda i: (i, D_IN // (2 * D_MODEL))),
                  pl.BlockSpec((tm, D_MODEL), lambda i: (i, 0)),
                  _mod_spec(tm, rows_per_mod, mod_row0, 2),
                  _resident(b_merge.shape), _resident(g_post.shape),
                  pl.BlockSpec(memory_space=pl.ANY), pl.BlockSpec(memory_space=pl.ANY),
                  pl.BlockSpec(memory_space=pl.ANY)],
        out_specs=pl.BlockSpec((tm, 1, D_MODEL), lambda i: (i, 0, 0)),
        scratch_shapes=[pltpu.VMEM(w_a.shape, BF16), pltpu.VMEM(w_b.shape, BF16), pltpu.VMEM(w_o.shape, BF16),
                        pltpu.SemaphoreType.DMA((n_chunks,))],
        compiler_params=_params("arbitrary"),
        name="out_proj",
    )(ya, yb, z, x, mod, b_merge, g_post, w_a, w_b, w_o)


def _rotary_tables(start, length):
    half = HEAD_DK // 2
    inv = ROPE_BASE ** (-np.arange(half, dtype=np.float64) / half)
    ang = (start + np.arange(length, dtype=np.float64))[:, None] * inv[None, :]
    cos, sin = np.cos(ang), np.sin(ang)
    return (np.concatenate([cos, cos], axis=-1).astype(np.float32),
            np.concatenate([-sin, sin], axis=-1).astype(np.float32))


def _decay_tables(c):
    lg = np.log1p(-np.power(2.0, -5.0 - np.arange(N_HEADS, dtype=np.float64)))
    idx = np.arange(c, dtype=np.float64)
    diff = idx[:, None] - idx[None, :]
    dmask = np.where(diff[None] >= 0, np.exp(np.maximum(diff, 0.0)[None] * lg[:, None, None]), 0.0)
    q_dec = np.exp((idx + 1.0)[None, :] * lg[:, None])
    k_dec = np.exp((c - 1.0 - idx)[None, :] * lg[:, None])
    chunk_dec = np.exp(c * lg)
    return tuple(a.astype(np.float32) for a in (dmask, q_dec, k_dec, chunk_dec))


def _layer(xp, xs, c_prompt, c_sample, state_pool, state_ret, ada_w, ada_b, g_pre, g_post, w_in, pool_w, pool_scale, gn_g,
           w_a_proj, w_b_proj, w_merge, b_merge, w_out):
    batch, seq, _ = xp.shape
    dec_batch, dec_seq, _ = xs.shape
    assert dec_seq == 1 and seq % CHUNK == 0

    row = lambda v: v.reshape(1, -1)
    g_pre, g_post, pool_scale, b_merge = map(row, (g_pre, g_post, pool_scale, b_merge))
    gn_g = gn_g.reshape(N_HEADS, HEAD_DV)

    mod = _modulation(c_sample, c_prompt, ada_w, row(ada_b))

    x2 = xp.reshape(batch * seq, D_MODEL)
    xs2 = xs.reshape(dec_batch, D_MODEL)
    h = _norm(x2, mod, g_pre, tm=NORM_TM, rows_per_mod=seq, mod_row0=dec_batch)
    z, zs, w_a, w_b = _in_proj(h, xs2, mod, g_pre, w_in, w_merge, w_a_proj, w_b_proj)

    cos, sin = _rotary_tables(0, seq)
    dmask, q_dec, k_dec, chunk_dec = _decay_tables(CHUNK)
    wide = lambda d: np.ascontiguousarray(np.broadcast_to(d[:, :, None], (N_HEADS, CHUNK, HEAD_DK)))
    rot = np.concatenate([cos, sin], axis=1)
    merged, pool_p, ret_p, w_o = _prompt_mix_merge(z, batch, seq, (rot, dmask, wide(q_dec), wide(k_dec), chunk_dec),
                                                   pool_w, pool_scale, gn_g, b_merge, w_a, w_b, w_out)

    cos_s, sin_s = _rotary_tables(PAST_LEN, 1)
    dmask_s, q_dec_s, k_dec_s, chunk_dec_s = _decay_tables(1)
    dec_s = np.stack([dmask_s[:, 0, 0], q_dec_s[:, 0], k_dec_s[:, 0], chunk_dec_s])
    inv_cnt = tuple(1.0 / min(PAST_LEN + 1, w) for w in POOL_WINDOWS)
    yp, ya_s, yb_s, pool_s, ret_s = _tail(merged, x2, mod, w_o, g_post, zs, jnp.transpose(state_pool, (1, 0, 2)),
                                          state_ret, (np.concatenate([cos_s, sin_s], axis=1), dec_s), inv_cnt,
                                          pool_w, pool_scale, gn_g,
                                          rows_per_mod=seq, mod_row0=dec_batch)
    pool_s = jnp.transpose(pool_s, (1, 0, 2))
    ys = _out_proj(ya_s, yb_s, zs, xs2, mod, b_merge, w_a, w_b, w_o, g_post, tm=dec_batch, rows_per_mod=1, mod_row0=0)

    return yp.reshape(xp.shape), ys.reshape(xs.shape), pool_p, ret_p, pool_s, ret_s


def kernel(x_prompt, x_sample, state_pool, state_ret, c_prompt, c_sample, ada_w, ada_b, g_pre, g_post,
           w_in, pool_w, pool_scale, gn_g, w_a_proj, w_b_proj, w_merge, b_merge, w_out):
    depth = ada_w.shape[0]
    xp, xs = x_prompt, x_sample
    pool_p, ret_p, pool_s, ret_s = [], [], [], []
    for l in range(depth):
        xp, xs, bp, sp, bs, ss = _layer(
            xp, xs, c_prompt, c_sample, state_pool[l], state_ret[l], ada_w[l], ada_b[l], g_pre[l], g_post[l], w_in[l],
            pool_w[l], pool_scale[l], gn_g[l], w_a_proj[l], w_b_proj[l], w_merge[l], b_merge[l],
            w_out[l])
        pool_p.append(bp)
        ret_p.append(sp)
        pool_s.append(bs)
        ret_s.append(ss)
    return (xp, xs, jnp.stack(pool_p), jnp.stack(ret_p), jnp.stack(pool_s), jnp.stack(ret_s))
```
